```python
import jax, jax.numpy as jnp
from jax import lax
import numpy as np

D_MODEL = 2048
BATCH = 8
SEQ = 4096
DEPTH = 2

N_MIXERS = 2
N_HEADS = 16
HEAD_DIM = D_MODEL // N_HEADS
Q_BLOCK = 128
CONV_WIDTH = 3
D_FF = 4 * D_MODEL
N_MOD = 6
RMS_EPS = 1e-6
FORGET_BIAS_CENTER = 3.0

kernel_name = 'fox_shortconv_hybrid_adaln'


def rmsnorm(x, gain):
    xf = x.astype(jnp.float32)
    inv = lax.rsqrt(jnp.mean(xf * xf, axis=-1, keepdims=True) + RMS_EPS)
    return (xf * inv).astype(x.dtype) * gain


def modulate(h, shift, scale):
    return h * (1.0 + scale[:, None, :]) + shift[:, None, :]


def fox_attention(q, k, v, log_f):
    B, H, S, Dh = q.shape
    nb = S // Q_BLOCK
    F = jnp.cumsum(log_f.astype(jnp.float32), axis=-1)
    q_blocks = q.reshape(B, H, nb, Q_BLOCK, Dh).transpose(2, 0, 1, 3, 4)
    F_blocks = F.reshape(B, H, nb, Q_BLOCK).transpose(2, 0, 1, 3)
    k_pos = jnp.arange(S)
    scale = HEAD_DIM ** -0.5

    def one_block(args):
        blk, q_i, F_i = args
        s = jnp.einsum('bhqd,bhkd->bhqk', q_i, k).astype(jnp.float32) * scale
        s = s + F_i[..., :, None] - F[..., None, :]
        q_pos = blk * Q_BLOCK + jnp.arange(Q_BLOCK)
        causal = k_pos[None, :] <= q_pos[:, None]
        s = jnp.where(causal, s, -jnp.inf)
        p = jax.nn.softmax(s, axis=-1)
        return jnp.einsum('bhqk,bhkd->bhqd', p.astype(v.dtype), v)

    out = lax.map(one_block, (jnp.arange(nb), q_blocks, F_blocks))
    return out.transpose(1, 2, 0, 3, 4).reshape(B, H, S, Dh)


def fox_mixer(h, w_in, b_f, w_out):
    B, S, D = h.shape
    proj = h @ w_in
    q, k, v, f_logit = jnp.split(proj, [D, 2 * D, 3 * D], axis=-1)
    to_heads = lambda t: t.reshape(B, S, N_HEADS, HEAD_DIM).transpose(0, 2, 1, 3)
    log_f = jax.nn.log_sigmoid((f_logit + b_f).astype(jnp.float32)).transpose(0, 2, 1)
    o = fox_attention(to_heads(q), to_heads(k), to_heads(v), log_f)
    o = o.transpose(0, 2, 1, 3).reshape(B, S, D)
    return o @ w_out


def short_conv_mixer(h, w_in, conv_w, w_out):
    S = h.shape[1]
    proj = h @ w_in
    b_gate, c_gate, u = jnp.split(proj, 3, axis=-1)
    u = c_gate * u
    u_pad = jnp.pad(u, ((0, 0), (CONV_WIDTH - 1, 0), (0, 0)))
    y = sum(conv_w[tap] * u_pad[:, tap:tap + S, :] for tap in range(CONV_WIDTH))
    return (b_gate * y) @ w_out


def sq_relu_mlp(h, w_up, w_down):
    return jnp.square(jax.nn.relu(h @ w_up)) @ w_down


def _fwd_setup_inputs(seed: int = 0) -> dict:
    key = jax.random.key(seed)
    ks = jax.random.split(key, 16)
    D = D_MODEL
    n_fox = (DEPTH + 1) // 2
    n_conv = DEPTH // 2
    nrm = lambda k, shape, fan_in, mult=1.0: jax.random.normal(k, shape, jnp.float32) * (mult * fan_in ** -0.5)
    return {
        'x': jax.random.normal(ks[0], (BATCH, SEQ, D), jnp.float32),
        'c': jax.random.normal(ks[1], (BATCH, D), jnp.float32),
        'ada_w': nrm(ks[2], (DEPTH, D, N_MOD * D), D, 0.5),
        'ada_b': 0.01 * jax.random.normal(ks[3], (DEPTH, N_MOD * D), jnp.float32),
        'norm_mix': 1.0 + 0.02 * jax.random.normal(ks[4], (DEPTH, D), jnp.float32),
        'norm_mlp': 1.0 + 0.02 * jax.random.normal(ks[5], (DEPTH, D), jnp.float32),
        'fox_w_in': nrm(ks[6], (n_fox, D, 3 * D + N_HEADS), D),
        'fox_b_f': FORGET_BIAS_CENTER + 0.1 * jax.random.normal(ks[7], (n_fox, N_HEADS), jnp.float32),
        'fox_w_out': nrm(ks[8], (n_fox, D, D), D),
        'conv_w_in': nrm(ks[9], (n_conv, D, 3 * D), D),
        'conv_w': nrm(ks[10], (n_conv, CONV_WIDTH, D), CONV_WIDTH),
        'conv_w_out': nrm(ks[11], (n_conv, D, D), D),
        'mlp_w_up': nrm(ks[12], (DEPTH, D, D_FF), D),
        'mlp_w_down': nrm(ks[13], (DEPTH, D_FF, D), D_FF),
        'final_norm': 1.0 + 0.02 * jax.random.normal(ks[14], (D,), jnp.float32),
    }


def _fwd_reference(x, c, ada_w, ada_b, norm_mix, norm_mlp, fox_w_in, fox_b_f, fox_w_out,
              conv_w_in, conv_w, conv_w_out, mlp_w_up, mlp_w_down, final_norm):
    c_act = jax.nn.silu(c)
    for i in range(DEPTH):
        mod = c_act @ ada_w[i] + ada_b[i]
        sh_mix, sc_mix, g_mix, sh_mlp, sc_mlp, g_mlp = jnp.split(mod, N_MOD, axis=-1)
        h = modulate(rmsnorm(x, norm_mix[i]), sh_mix, sc_mix)
        j = i // N_MIXERS
        if i % N_MIXERS == 0:
            mix = fox_mixer(h, fox_w_in[j], fox_b_f[j], fox_w_out[j])
        else:
            mix = short_conv_mixer(h, conv_w_in[j], conv_w[j], conv_w_out[j])
        x = x + g_mix[:, None, :] * mix
        h = modulate(rmsnorm(x, norm_mlp[i]), sh_mlp, sc_mlp)
        x = x + g_mlp[:, None, :] * sq_relu_mlp(h, mlp_w_up[i], mlp_w_down[i])
    return rmsnorm(x, final_norm)


import jax as _jax
import jax.numpy as _jnp

TWIN_FORMAT = 'train_step'
FWD_PARAMS = ['x', 'c', 'ada_w', 'ada_b', 'norm_mix', 'norm_mlp', 'fox_w_in', 'fox_b_f', 'fox_w_out', 'conv_w_in', 'conv_w', 'conv_w_out', 'mlp_w_up', 'mlp_w_down', 'final_norm']
TWIN_WEIGHTS = ['ada_w', 'ada_b', 'norm_mix', 'norm_mlp', 'fox_w_in', 'fox_b_f', 'fox_w_out', 'conv_w_in', 'conv_w', 'conv_w_out', 'mlp_w_up', 'mlp_w_down', 'final_norm']
TWIN_DIFF_INPUT = 'x'
TWIN_INPUTS = ['x', 'c', 'ada_w', 'ada_b', 'norm_mix', 'norm_mlp', 'fox_w_in', 'fox_b_f', 'fox_w_out', 'conv_w_in', 'conv_w', 'conv_w_out', 'mlp_w_up', 'mlp_w_down', 'final_norm', 'loss_target', 'm_ada_w', 'm_ada_b', 'm_norm_mix', 'm_norm_mlp', 'm_fox_w_in', 'm_fox_b_f', 'm_fox_w_out', 'm_conv_w_in', 'm_conv_w', 'm_conv_w_out', 'm_mlp_w_up', 'm_mlp_w_down', 'm_final_norm', 'v_ada_w', 'v_ada_b', 'v_norm_mix', 'v_norm_mlp', 'v_fox_w_in', 'v_fox_b_f', 'v_fox_w_out', 'v_conv_w_in', 'v_conv_w', 'v_conv_w_out', 'v_mlp_w_up', 'v_mlp_w_down', 'v_final_norm']
TWIN_OUTPUTS = ['loss', 'grad_x', 'grad_ada_w', 'grad_ada_b', 'grad_norm_mix', 'grad_norm_mlp', 'grad_fox_w_in', 'grad_fox_b_f', 'grad_fox_w_out', 'grad_conv_w_in', 'grad_conv_w', 'grad_conv_w_out', 'grad_mlp_w_up', 'grad_mlp_w_down', 'grad_final_norm', 'delta_ada_w', 'delta_ada_b', 'delta_norm_mix', 'delta_norm_mlp', 'delta_fox_w_in', 'delta_fox_b_f', 'delta_fox_w_out', 'delta_conv_w_in', 'delta_conv_w', 'delta_conv_w_out', 'delta_mlp_w_up', 'delta_mlp_w_down', 'delta_final_norm', 'new_m_ada_w', 'new_m_ada_b', 'new_m_norm_mix', 'new_m_norm_mlp', 'new_m_fox_w_in', 'new_m_fox_b_f', 'new_m_fox_w_out', 'new_m_conv_w_in', 'new_m_conv_w', 'new_m_conv_w_out', 'new_m_mlp_w_up', 'new_m_mlp_w_down', 'new_m_final_norm', 'new_v_ada_w', 'new_v_ada_b', 'new_v_norm_mix', 'new_v_norm_mlp', 'new_v_fox_w_in', 'new_v_fox_b_f', 'new_v_fox_w_out', 'new_v_conv_w_in', 'new_v_conv_w', 'new_v_conv_w_out', 'new_v_mlp_w_up', 'new_v_mlp_w_down', 'new_v_final_norm']
TWIN_LEAF_KINDS = {'loss': 'loss', 'grad_x': 'grad_x', 'grad_ada_w': 'grad_w', 'grad_ada_b': 'grad_w', 'grad_norm_mix': 'grad_w', 'grad_norm_mlp': 'grad_w', 'grad_fox_w_in': 'grad_w', 'grad_fox_b_f': 'grad_w', 'grad_fox_w_out': 'grad_w', 'grad_conv_w_in': 'grad_w', 'grad_conv_w': 'grad_w', 'grad_conv_w_out': 'grad_w', 'grad_mlp_w_up': 'grad_w', 'grad_mlp_w_down': 'grad_w', 'grad_final_norm': 'grad_w', 'delta_ada_w': 'delta_w', 'delta_ada_b': 'delta_w', 'delta_norm_mix': 'delta_w', 'delta_norm_mlp': 'delta_w', 'delta_fox_w_in': 'delta_w', 'delta_fox_b_f': 'delta_w', 'delta_fox_w_out': 'delta_w', 'delta_conv_w_in': 'delta_w', 'delta_conv_w': 'delta_w', 'delta_conv_w_out': 'delta_w', 'delta_mlp_w_up': 'delta_w', 'delta_mlp_w_down': 'delta_w', 'delta_final_norm': 'delta_w', 'new_m_ada_w': 'new_m', 'new_m_ada_b': 'new_m', 'new_m_norm_mix': 'new_m', 'new_m_norm_mlp': 'new_m', 'new_m_fox_w_in': 'new_m', 'new_m_fox_b_f': 'new_m', 'new_m_fox_w_out': 'new_m', 'new_m_conv_w_in': 'new_m', 'new_m_conv_w': 'new_m', 'new_m_conv_w_out': 'new_m', 'new_m_mlp_w_up': 'new_m', 'new_m_mlp_w_down': 'new_m', 'new_m_final_norm': 'new_m', 'new_v_ada_w': 'new_v', 'new_v_ada_b': 'new_v', 'new_v_norm_mix': 'new_v', 'new_v_norm_mlp': 'new_v', 'new_v_fox_w_in': 'new_v', 'new_v_fox_b_f': 'new_v', 'new_v_fox_w_out': 'new_v', 'new_v_conv_w_in': 'new_v', 'new_v_conv_w': 'new_v', 'new_v_conv_w_out': 'new_v', 'new_v_mlp_w_up': 'new_v', 'new_v_mlp_w_down': 'new_v', 'new_v_final_norm': 'new_v'}


def _forward(args):
    return _fwd_reference(*[args[k] for k in FWD_PARAMS])


def _output_shape():
    def fwd():
        inp = _fwd_setup_inputs(0)
        return _fwd_reference(*[inp[k] for k in FWD_PARAMS])
    out = _jax.eval_shape(fwd)
    return out.shape, out.dtype

N_MICROBATCH = 1
ADAM_LR = 0.001
ADAM_B1 = 0.9
ADAM_B2 = 0.999
ADAM_EPS = 1e-08
ADAM_WD = 0.01
ADAM_STEP = 10
PER_EXAMPLE_BATCH_AXIS = {'x': 0, 'c': 0, 'loss_target': 0}
SHARED_INPUTS = []
_WEIGHT_DTYPES = {'ada_w': _jnp.float32, 'ada_b': _jnp.float32, 'norm_mix': _jnp.float32, 'norm_mlp': _jnp.float32, 'fox_w_in': _jnp.float32, 'fox_b_f': _jnp.float32, 'fox_w_out': _jnp.float32, 'conv_w_in': _jnp.float32, 'conv_w': _jnp.float32, 'conv_w_out': _jnp.float32, 'mlp_w_up': _jnp.float32, 'mlp_w_down': _jnp.float32, 'final_norm': _jnp.float32}
MOMENT_SCALE = {'ada_w': 4.350286e-02, 'ada_b': 7.849968e-02, 'norm_mix': 3.708520e-02, 'norm_mlp': 4.037901e-02, 'fox_w_in': 1.123044e-02, 'fox_b_f': 6.633740e-02, 'fox_w_out': 1.386466e-02, 'conv_w_in': 2.960626e-02, 'conv_w': 2.964673e-02, 'conv_w_out': 2.961503e-02, 'mlp_w_up': 2.036720e-02, 'mlp_w_down': 3.685130e-02, 'final_norm': 1.609033e+01}


def _to_microbatches(a, axis):
    t = _jnp.moveaxis(a, axis, 0)
    t = t.reshape((N_MICROBATCH, t.shape[0] // N_MICROBATCH) + t.shape[1:])
    return _jnp.moveaxis(t, 1, axis + 1)


def setup_inputs(seed: int = 0) -> dict:
    inp = _fwd_setup_inputs(seed)
    key = _jax.random.fold_in(_jax.random.key(seed), 7919)
    shape, _ = _output_shape()
    out = dict(inp)
    out["loss_target"] = _jax.random.normal(_jax.random.fold_in(key, 0), shape, _jnp.float32)
    for i, name in enumerate(TWIN_WEIGHTS):
        w = inp[name].astype(_jnp.float32)
        if MOMENT_SCALE is None:
            s = _jnp.sqrt(_jnp.mean(_jnp.square(w)) + 1e-30)
        else:
            s = MOMENT_SCALE[name]
        km, kv = _jax.random.split(_jax.random.fold_in(key, i + 1))
        out[name] = w
        out["m_" + name] = s * _jax.random.normal(km, w.shape, _jnp.float32)
        out["v_" + name] = (s * s) * _jax.random.uniform(kv, w.shape, _jnp.float32, 0.5, 1.5)
    if N_MICROBATCH > 1:
        for name, axis in PER_EXAMPLE_BATCH_AXIS.items():
            out[name] = _to_microbatches(out[name], axis)
    return {'x': out['x'], 'c': out['c'], 'ada_w': out['ada_w'], 'ada_b': out['ada_b'], 'norm_mix': out['norm_mix'], 'norm_mlp': out['norm_mlp'], 'fox_w_in': out['fox_w_in'], 'fox_b_f': out['fox_b_f'], 'fox_w_out': out['fox_w_out'], 'conv_w_in': out['conv_w_in'], 'conv_w': out['conv_w'], 'conv_w_out': out['conv_w_out'], 'mlp_w_up': out['mlp_w_up'], 'mlp_w_down': out['mlp_w_down'], 'final_norm': out['final_norm'], 'loss_target': out['loss_target'], 'm_ada_w': out['m_ada_w'], 'm_ada_b': out['m_ada_b'], 'm_norm_mix': out['m_norm_mix'], 'm_norm_mlp': out['m_norm_mlp'], 'm_fox_w_in': out['m_fox_w_in'], 'm_fox_b_f': out['m_fox_b_f'], 'm_fox_w_out': out['m_fox_w_out'], 'm_conv_w_in': out['m_conv_w_in'], 'm_conv_w': out['m_conv_w'], 'm_conv_w_out': out['m_conv_w_out'], 'm_mlp_w_up': out['m_mlp_w_up'], 'm_mlp_w_down': out['m_mlp_w_down'], 'm_final_norm': out['m_final_norm'], 'v_ada_w': out['v_ada_w'], 'v_ada_b': out['v_ada_b'], 'v_norm_mix': out['v_norm_mix'], 'v_norm_mlp': out['v_norm_mlp'], 'v_fox_w_in': out['v_fox_w_in'], 'v_fox_b_f': out['v_fox_b_f'], 'v_fox_w_out': out['v_fox_w_out'], 'v_conv_w_in': out['v_conv_w_in'], 'v_conv_w': out['v_conv_w'], 'v_conv_w_out': out['v_conv_w_out'], 'v_mlp_w_up': out['v_mlp_w_up'], 'v_mlp_w_down': out['v_mlp_w_down'], 'v_final_norm': out['v_final_norm']}


def _loss(weights, diff, rest, loss_target):
    with _jax.named_scope("forward"):
        args = {**rest, TWIN_DIFF_INPUT: diff, **{k: w.astype(_WEIGHT_DTYPES[k]) for k, w in weights.items()}}
        y = _forward(args)
    with _jax.named_scope("loss_head"):
        err = _jnp.square(y.astype(_jnp.float32) - loss_target)
        return 0.5 * _jnp.sum(_jnp.mean(err, axis=-1)) if err.ndim else 0.5 * err


def _adamw(w, g, m, v):
    m = ADAM_B1 * m + (1.0 - ADAM_B1) * g
    v = ADAM_B2 * v + (1.0 - ADAM_B2) * _jnp.square(g)
    m_hat = m / (1.0 - ADAM_B1 ** ADAM_STEP)
    v_hat = v / (1.0 - ADAM_B2 ** ADAM_STEP)
    delta = -ADAM_LR * (m_hat / (_jnp.sqrt(v_hat) + ADAM_EPS) + ADAM_WD * w)
    return delta, m, v


def reference(x, c, ada_w, ada_b, norm_mix, norm_mlp, fox_w_in, fox_b_f, fox_w_out, conv_w_in, conv_w, conv_w_out, mlp_w_up, mlp_w_down, final_norm, loss_target, m_ada_w, m_ada_b, m_norm_mix, m_norm_mlp, m_fox_w_in, m_fox_b_f, m_fox_w_out, m_conv_w_in, m_conv_w, m_conv_w_out, m_mlp_w_up, m_mlp_w_down, m_final_norm, v_ada_w, v_ada_b, v_norm_mix, v_norm_mlp, v_fox_w_in, v_fox_b_f, v_fox_w_out, v_conv_w_in, v_conv_w, v_conv_w_out, v_mlp_w_up, v_mlp_w_down, v_final_norm):
    given = dict(x=x, c=c, ada_w=ada_w, ada_b=ada_b, norm_mix=norm_mix, norm_mlp=norm_mlp, fox_w_in=fox_w_in, fox_b_f=fox_b_f, fox_w_out=fox_w_out, conv_w_in=conv_w_in, conv_w=conv_w, conv_w_out=conv_w_out, mlp_w_up=mlp_w_up, mlp_w_down=mlp_w_down, final_norm=final_norm, loss_target=loss_target, m_ada_w=m_ada_w, m_ada_b=m_ada_b, m_norm_mix=m_norm_mix, m_norm_mlp=m_norm_mlp, m_fox_w_in=m_fox_w_in, m_fox_b_f=m_fox_b_f, m_fox_w_out=m_fox_w_out, m_conv_w_in=m_conv_w_in, m_conv_w=m_conv_w, m_conv_w_out=m_conv_w_out, m_mlp_w_up=m_mlp_w_up, m_mlp_w_down=m_mlp_w_down, m_final_norm=m_final_norm, v_ada_w=v_ada_w, v_ada_b=v_ada_b, v_norm_mix=v_norm_mix, v_norm_mlp=v_norm_mlp, v_fox_w_in=v_fox_w_in, v_fox_b_f=v_fox_b_f, v_fox_w_out=v_fox_w_out, v_conv_w_in=v_conv_w_in, v_conv_w=v_conv_w, v_conv_w_out=v_conv_w_out, v_mlp_w_up=v_mlp_w_up, v_mlp_w_down=v_mlp_w_down, v_final_norm=v_final_norm)
    weights = {n: given[n] for n in TWIN_WEIGHTS}
    shared = {n: given[n] for n in SHARED_INPUTS}
    per_example = {n: given[n] for n in ['x', 'c']}
    grad_fn = _jax.value_and_grad(_loss, argnums=(0, 1))

    def one_microbatch(ex, loss_target):
        ex = dict(ex)
        diff = ex.pop(TWIN_DIFF_INPUT)
        return grad_fn(weights, diff, {**shared, **ex}, loss_target)

    if N_MICROBATCH == 1:
        loss, (grad_w, grad_x) = one_microbatch(per_example, given["loss_target"])
    else:
        def body(carry, xs):
            loss_sum, grad_sum = carry
            l_k, (gw_k, gx_k) = one_microbatch(xs[0], xs[1])
            with _jax.named_scope("update"):
                return (loss_sum + l_k, _jax.tree.map(_jnp.add, grad_sum, gw_k)), gx_k

        init = (_jnp.zeros((), _jnp.float32), _jax.tree.map(_jnp.zeros_like, weights))
        (loss, grad_w), grad_x = _jax.lax.scan(body, init, (per_example, given["loss_target"]))
    with _jax.named_scope("update"):
        delta_w, new_m, new_v = {}, {}, {}
        for n in TWIN_WEIGHTS:
            delta_w[n], new_m[n], new_v[n] = _adamw(weights[n], grad_w[n], given["m_" + n], given["v_" + n])
    return (loss, grad_x, *[grad_w[n] for n in TWIN_WEIGHTS], *[delta_w[n] for n in TWIN_WEIGHTS],
            *[new_m[n] for n in TWIN_WEIGHTS], *[new_v[n] for n in TWIN_WEIGHTS])
```

```python
import functools

import jax
import jax.numpy as jnp
from jax import lax
from jax.experimental import pallas as pl
from jax.experimental.pallas import tpu as pltpu

F32 = jnp.float32
BF16 = jnp.bfloat16
MESH = pl.DeviceIdType.MESH
ANY = pl.BlockSpec(memory_space=pl.ANY)

RMS_EPS = 1e-6
ADAM_LR = 0.001
ADAM_B1 = 0.9
ADAM_B2 = 0.999
ADAM_EPS = 1e-08
ADAM_WD = 0.01
ADAM_STEP = 10
N_CHIPS = 4
N_DEV = 8
LANES = 128
VMEM_LIMIT = 56 * 1024 * 1024
NEG = -1e30


def _params(sems=None, vmem=VMEM_LIMIT):
    return pltpu.CompilerParams(dimension_semantics=sems, vmem_limit_bytes=vmem)


def _tile(n, pref, unit=LANES):
    if n <= pref:
        return n
    t = (pref // unit) * unit
    while n % t:
        t -= unit
    return t


def _me():
    return lax.axis_index("x"), lax.axis_index("y"), lax.axis_index("c")


def _allgather8(v, *, name):
    R, C = v.shape

    def body(v_ref, out_ref, send_sems, recv_sems):
        x, y, c = _me()
        me = 4 * x + 2 * y + c
        out_ref[me] = v_ref[...]
        copies = []
        for k in range(1, N_DEV):
            px, py, pc = (x + (k >> 2)) % 2, (y + ((k >> 1) & 1)) % 2, (c + (k & 1)) % 2
            copies.append(pltpu.make_async_remote_copy(
                src_ref=v_ref, dst_ref=out_ref.at[me], send_sem=send_sems.at[k - 1], recv_sem=recv_sems.at[k - 1],
                device_id=(px, py, pc), device_id_type=MESH))
        for cp in copies:
            cp.start()
        for k in range(1, N_DEV):
            px, py, pc = (x + (k >> 2)) % 2, (y + ((k >> 1) & 1)) % 2, (c + (k & 1)) % 2
            peer = 4 * px + 2 * py + pc
            pltpu.make_async_remote_copy(
                src_ref=v_ref, dst_ref=out_ref.at[peer], send_sem=send_sems.at[k - 1], recv_sem=recv_sems.at[k - 1],
                device_id=(px, py, pc), device_id_type=MESH).wait_recv()
        for cp in copies:
            cp.wait_send()

    return pl.pallas_call(
        body, name=name,
        out_shape=jax.ShapeDtypeStruct((N_DEV, R, C), v.dtype),
        in_specs=[pl.BlockSpec(memory_space=pltpu.VMEM)],
        out_specs=pl.BlockSpec(memory_space=pltpu.VMEM),
        scratch_shapes=[pltpu.SemaphoreType.DMA((N_DEV - 1,)), pltpu.SemaphoreType.DMA((N_DEV - 1,))],
        compiler_params=_params(),
    )(v)


def _chip_peers(x, y):
    return [((x + (k >> 1)) % 2, (y + (k & 1)) % 2) for k in range(1, N_CHIPS)]


def _allgather_chips(shards, *, name):
    n = len(shards)

    def body(*refs):
        ins, outs = refs[:n], refs[n:2 * n]
        send_sems, recv_sems, local_sems = refs[2 * n:]
        x, y, c = _me()
        j = 2 * x + y
        started = []
        for a in range(n):
            loc = pltpu.make_async_copy(ins[a], outs[a].at[j], local_sems.at[a])
            loc.start()
            started.append(loc)
        sends = []
        for a in range(n):
            for k, (px, py) in enumerate(_chip_peers(x, y)):
                cp = pltpu.make_async_remote_copy(
                    src_ref=ins[a], dst_ref=outs[a].at[j], send_sem=send_sems.at[3 * a + k], recv_sem=recv_sems.at[3 * a + k],
                    device_id=(px, py, c), device_id_type=MESH)
                cp.start()
                sends.append(cp)
        for a in range(n):
            for k, (px, py) in enumerate(_chip_peers(x, y)):
                pltpu.make_async_remote_copy(
                    src_ref=ins[a], dst_ref=outs[a].at[2 * px + py], send_sem=send_sems.at[3 * a + k],
                    recv_sem=recv_sems.at[3 * a + k], device_id=(px, py, c), device_id_type=MESH).wait_recv()
        for cp in sends:
            cp.wait_send()
        for loc in started:
            loc.wait()

    return pl.pallas_call(
        body, name=name,
        out_shape=[jax.ShapeDtypeStruct((N_CHIPS,) + s.shape, s.dtype) for s in shards],
        in_specs=[ANY] * n, out_specs=[ANY] * n,
        scratch_shapes=[pltpu.SemaphoreType.DMA((3 * n,)), pltpu.SemaphoreType.DMA((3 * n,)), pltpu.SemaphoreType.DMA((n,))],
        compiler_params=_params(),
    )(*shards)


def _exchange_chips(parts, *, name):
    n = len(parts)

    def body(*refs):
        ins, outs = refs[:n], refs[n:2 * n]
        send_sems, recv_sems, local_sems = refs[2 * n:]
        x, y, c = _me()
        j = 2 * x + y
        started = []
        for a in range(n):
            loc = pltpu.make_async_copy(ins[a].at[j], outs[a].at[j], local_sems.at[a])
            loc.start()
            started.append(loc)
        sends = []
        for a in range(n):
            for k, (px, py) in enumerate(_chip_peers(x, y)):
                cp = pltpu.make_async_remote_copy(
                    src_ref=ins[a].at[2 * px + py], dst_ref=outs[a].at[j], send_sem=send_sems.at[3 * a + k],
                    recv_sem=recv_sems.at[3 * a + k], device_id=(px, py, c), device_id_type=MESH)
                cp.start()
                sends.append(cp)
        for a in range(n):
            for k, (px, py) in enumerate(_chip_peers(x, y)):
                pltpu.make_async_remote_copy(
                    src_ref=ins[a].at[j], dst_ref=outs[a].at[2 * px + py], send_sem=send_sems.at[3 * a + k],
                    recv_sem=recv_sems.at[3 * a + k], device_id=(px, py, c), device_id_type=MESH).wait_recv()
        for cp in sends:
            cp.wait_send()
        for loc in started:
            loc.wait()

    return pl.pallas_call(
        body, name=name,
        out_shape=[jax.ShapeDtypeStruct(p.shape, p.dtype) for p in parts],
        in_specs=[ANY] * n, out_specs=[ANY] * n,
        scratch_shapes=[pltpu.SemaphoreType.DMA((3 * n,)), pltpu.SemaphoreType.DMA((3 * n,)), pltpu.SemaphoreType.DMA((n,))],
        compiler_params=_params(),
    )(*parts)


def _exchange_sibling(arrs, *, name):
    n = len(arrs)

    def body(*refs):
        ins, outs = refs[:n], refs[n:2 * n]
        send_sems, recv_sems = refs[2 * n:]
        x, y, c = _me()
        copies = [pltpu.make_async_remote_copy(
            src_ref=ins[a], dst_ref=outs[a], send_sem=send_sems.at[a], recv_sem=recv_sems.at[a],
            device_id=(x, y, 1 - c), device_id_type=MESH) for a in range(n)]
        for cp in copies:
            cp.start()
        for cp in copies:
            cp.wait_recv()
        for cp in copies:
            cp.wait_send()

    return pl.pallas_call(
        body, name=name,
        out_shape=[jax.ShapeDtypeStruct(a.shape, a.dtype) for a in arrs],
        in_specs=[ANY] * n, out_specs=[ANY] * n,
        scratch_shapes=[pltpu.SemaphoreType.DMA((n,)), pltpu.SemaphoreType.DMA((n,))],
        compiler_params=_params(),
    )(*arrs)


def _accumulate(part, acc_ref, nk, finalize):
    if nk == 1:
        finalize(part)
        return
    k = pl.program_id(2)

    @pl.when(k == 0)
    def _():
        acc_ref[...] = part

    @pl.when(k > 0)
    def _():
        acc_ref[...] += part

    @pl.when(k == nk - 1)
    def _():
        finalize(acc_ref[...])


def _mm_nn(a, b, *, name, epilogue="plain", res=None, gate=None, out_dtype=BF16, tm=1024, tn=1024, tk=2048):
    M, K = a.shape
    tm, tk = _tile(M, tm, 16), _tile(K, tk)
    if b.ndim == 3:
        P, _, Ns = b.shape
        N = P * Ns
        tn = _tile(Ns, tn)
        per = Ns // tn
        b_spec = pl.BlockSpec((None, tk, tn), lambda i, j, k: (j // per, k, j % per))
    else:
        N = b.shape[1]
        tn = _tile(N, tn)
        b_spec = pl.BlockSpec((tk, tn), lambda i, j, k: (k, j))
    nk = K // tk
    tile = pl.BlockSpec((tm, tn), lambda i, j, k: (i, j))

    def body(*refs):
        acc_ref = refs[-1] if nk > 1 else None
        a_ref, b_ref = refs[0], refs[1]
        part = jnp.dot(a_ref[...], b_ref[...], preferred_element_type=F32)
        if epilogue == "plain":
            def fin(acc):
                refs[2][...] = acc.astype(out_dtype)
        elif epilogue == "relu2":
            def fin(acc):
                refs[2][...] = acc.astype(BF16)
                refs[3][...] = jnp.square(jnp.maximum(acc, 0.0)).astype(BF16)
        else:
            def fin(acc):
                refs[4][...] = refs[2][...] + refs[3][...] * acc
                refs[5][...] = acc.astype(BF16)
        _accumulate(part, acc_ref, nk, fin)

    in_specs = [pl.BlockSpec((tm, tk), lambda i, j, k: (i, k)), b_spec]
    args = [a, b]
    if epilogue == "plain":
        out_shape, out_specs = jax.ShapeDtypeStruct((M, N), out_dtype), tile
    elif epilogue == "relu2":
        out_shape, out_specs = [jax.ShapeDtypeStruct((M, N), BF16)] * 2, [tile, tile]
    else:
        in_specs += [tile, pl.BlockSpec((1, tn), lambda i, j, k: (0, j))]
        args += [res, gate]
        out_shape, out_specs = [jax.ShapeDtypeStruct((M, N), F32), jax.ShapeDtypeStruct((M, N), BF16)], [tile, tile]
    return pl.pallas_call(
        body, name=name, grid=(M // tm, N // tn, nk), in_specs=in_specs, out_specs=out_specs, out_shape=out_shape,
        scratch_shapes=[pltpu.VMEM((tm, tn), F32)] if nk > 1 else [],
        compiler_params=_params(("parallel", "parallel", "arbitrary")),
    )(*args)


def _mm_nt(a, b, *, name, epilogue="plain", extra=None, out_dtype=F32, tm=1024, tn=1024, tk=2048):
    if a.ndim == 3:
        Q, M, Kq = a.shape
        K = Q * Kq
    else:
        (M, K), Kq = a.shape, a.shape[1]
    if b.ndim == 3:
        P, N, Ks = b.shape
    else:
        N, Ks = b.shape
    tm, tn, tk = _tile(M, tm, 16), _tile(N, tn), _tile(min(Kq, Ks), tk)
    nk = K // tk
    if a.ndim == 3:
        pa = Kq // tk
        a_spec = pl.BlockSpec((None, tm, tk), lambda i, j, k: (k // pa, i, k % pa))
    else:
        a_spec = pl.BlockSpec((tm, tk), lambda i, j, k: (i, k))
    if b.ndim == 3:
        pb = Ks // tk
        b_spec = pl.BlockSpec((None, tn, tk), lambda i, j, k: (k // pb, j, k % pb))
    else:
        b_spec = pl.BlockSpec((tn, tk), lambda i, j, k: (j, k))
    tile = pl.BlockSpec((tm, tn), lambda i, j, k: (i, j))

    def body(*refs):
        acc_ref = refs[-1] if nk > 1 else None
        part = lax.dot_general(refs[0][...], refs[1][...], (((1,), (1,)), ((), ())), preferred_element_type=F32)
        if epilogue == "plain":
            def fin(acc):
                refs[2][...] = acc.astype(out_dtype)
        elif epilogue == "add":
            def fin(acc):
                refs[3][...] = (acc + refs[2][...]).astype(out_dtype)
        else:
            def fin(acc):
                refs[3][...] = (acc * (2.0 * jnp.maximum(refs[2][...].astype(F32), 0.0))).astype(out_dtype)
        _accumulate(part, acc_ref, nk, fin)

    in_specs, args = [a_spec, b_spec], [a, b]
    if epilogue != "plain":
        in_specs.append(tile)
        args.append(extra)
    return pl.pallas_call(
        body, name=name, grid=(M // tm, N // tn, nk), in_specs=in_specs, out_specs=tile,
        out_shape=jax.ShapeDtypeStruct((M, N), out_dtype),
        scratch_shapes=[pltpu.VMEM((tm, tn), F32)] if nk > 1 else [],
        compiler_params=_params(("parallel", "parallel", "arbitrary")),
    )(*args)


def _mm_tn(a, b, *, name, out_parts=1, tm=1024, tn=1024, tk=2048):
    M, Kd = a.shape
    if b.ndim == 3:
        Q, _, Nq = b.shape
        N = Q * Nq
    else:
        N, Nq = b.shape[1], b.shape[1]
    Ns = N // out_parts
    tn = _tile(Ns, tn)
    while Nq % tn or Ns % tn:
        tn -= LANES
    tm, tk = _tile(Kd, tm), _tile(M, tk, 16)
    nk = M // tk
    if b.ndim == 3:
        pb = Nq // tn
        b_spec = pl.BlockSpec((None, tk, tn), lambda i, j, k: (j // pb, k, j % pb))
    else:
        b_spec = pl.BlockSpec((tk, tn), lambda i, j, k: (k, j))
    if out_parts > 1:
        po = Ns // tn
        o_spec = pl.BlockSpec((None, tm, tn), lambda i, j, k: (j // po, i, j % po))
        out_shape = jax.ShapeDtypeStruct((out_parts, Kd, Ns), BF16)
    else:
        o_spec = pl.BlockSpec((tm, tn), lambda i, j, k: (i, j))
        out_shape = jax.ShapeDtypeStruct((Kd, N), BF16)

    def body(*refs):
        acc_ref = refs[-1] if nk > 1 else None
        part = lax.dot_general(refs[0][...], refs[1][...], (((0,), (0,)), ((), ())), preferred_element_type=F32)

        def fin(acc):
            refs[2][...] = acc.astype(BF16)
        _accumulate(part, acc_ref, nk, fin)

    return pl.pallas_call(
        body, name=name, grid=(Kd // tm, N // tn, nk),
        in_specs=[pl.BlockSpec((tk, tm), lambda i, j, k: (k, i)), b_spec], out_specs=o_spec, out_shape=out_shape,
        scratch_shapes=[pltpu.VMEM((tm, tn), F32)] if nk > 1 else [],
        compiler_params=_params(("parallel", "parallel", "arbitrary")),
    )(a, b)


def _rows(S, D, i_map=lambda i: (i, 0), ts=512):
    return pl.BlockSpec((ts, D), i_map)


def _norm_fwd(x, gain, sc, sh, *, name):
    S, D = x.shape
    ts = _tile(S, 512, 16)
    vec = pl.BlockSpec((1, D), lambda i: (0, 0))

    def body(x_ref, g_ref, sc_ref, sh_ref, h_ref):
        xv = x_ref[...]
        r = lax.rsqrt(jnp.mean(xv * xv, axis=-1, keepdims=True) + RMS_EPS)
        h = (xv * r) * g_ref[...]
        h_ref[...] = (h * (1.0 + sc_ref[...]) + sh_ref[...]).astype(BF16)

    return pl.pallas_call(
        body, name=name, grid=(S // ts,), in_specs=[_rows(S, D, ts=ts), vec, vec, vec], out_specs=_rows(S, D, ts=ts),
        out_shape=jax.ShapeDtypeStruct((S, D), BF16), compiler_params=_params(("parallel",)),
    )(x, gain, sc, sh)


def _loss_bwd(x, target, gain, gate_prev, *, name):
    S, D = x.shape
    ts = _tile(S, 256, 16)
    vec = pl.BlockSpec((1, D), lambda i: (0, 0))

    def body(x_ref, t_ref, g_ref, gp_ref, dx_ref, dp_ref, sums_ref):
        @pl.when(pl.program_id(0) == 0)
        def _():
            sums_ref[...] = jnp.zeros_like(sums_ref)
        xv = x_ref[...]
        r = lax.rsqrt(jnp.mean(xv * xv, axis=-1, keepdims=True) + RMS_EPS)
        xn = xv * r
        err = xn * g_ref[...] - t_ref[...]
        loss = 0.5 * jnp.sum(jnp.mean(err * err, axis=-1, keepdims=True), axis=0, keepdims=True)
        dy = err * (1.0 / D)
        dxn = dy * g_ref[...]
        dx = r * (dxn - xn * jnp.mean(dxn * xn, axis=-1, keepdims=True))
        dx_ref[...] = dx
        dp_ref[...] = (gp_ref[...] * dx).astype(BF16)
        sums_ref[0:1, :] += jnp.sum(dy * xn, axis=0, keepdims=True)
        sums_ref[1:2, :] += jnp.broadcast_to(loss, (1, D))

    return pl.pallas_call(
        body, name=name, grid=(S // ts,),
        in_specs=[_rows(S, D, ts=ts), _rows(S, D, ts=ts), vec, vec],
        out_specs=[_rows(S, D, ts=ts), _rows(S, D, ts=ts), pl.BlockSpec((8, D), lambda i: (0, 0))],
        out_shape=[jax.ShapeDtypeStruct((S, D), F32), jax.ShapeDtypeStruct((S, D), BF16), jax.ShapeDtypeStruct((8, D), F32)],
        compiler_params=_params(("arbitrary",)),
    )(x, target, gain, gate_prev)


def _norm_bwd(x, dh, dxp, mix, gain, sc, gate_prev, *, name):
    S, D = x.shape
    ts = _tile(S, 256, 16)
    vec = pl.BlockSpec((1, D), lambda i: (0, 0))
    with_prev = gate_prev is not None

    def body(*refs):
        x_ref, dh_ref, dxp_ref, mix_ref, g_ref, sc_ref = refs[:6]
        outs = refs[7:] if with_prev else refs[6:]
        sums_ref = outs[-1]

        @pl.when(pl.program_id(0) == 0)
        def _():
            sums_ref[...] = jnp.zeros_like(sums_ref)
        xv, dhv, dxpv = x_ref[...], dh_ref[...], dxp_ref[...]
        r = lax.rsqrt(jnp.mean(xv * xv, axis=-1, keepdims=True) + RMS_EPS)
        xn = xv * r
        hn = xn * g_ref[...]
        dhn = dhv * (1.0 + sc_ref[...])
        dxn = dhn * g_ref[...]
        dx = dxpv + r * (dxn - xn * jnp.mean(dxn * xn, axis=-1, keepdims=True))
        outs[0][...] = dx
        if with_prev:
            outs[1][...] = (refs[6][...] * dx).astype(BF16)
        sums_ref[0:1, :] += jnp.sum(dhv, axis=0, keepdims=True)
        sums_ref[1:2, :] += jnp.sum(dhv * hn, axis=0, keepdims=True)
        sums_ref[2:3, :] += jnp.sum(dhn * xn, axis=0, keepdims=True)
        sums_ref[3:4, :] += jnp.sum(dxpv * mix_ref[...].astype(F32), axis=0, keepdims=True)

    tile = _rows(S, D, ts=ts)
    in_specs = [tile, tile, tile, tile, vec, vec] + ([vec] if with_prev else [])
    args = [x, dh, dxp, mix, gain, sc] + ([gate_prev] if with_prev else [])
    out_specs = [tile] + ([tile] if with_prev else []) + [pl.BlockSpec((8, D), lambda i: (0, 0))]
    out_shape = ([jax.ShapeDtypeStruct((S, D), F32)] + ([jax.ShapeDtypeStruct((S, D), BF16)] if with_prev else [])
                 + [jax.ShapeDtypeStruct((8, D), F32)])
    outs = pl.pallas_call(
        body, name=name, grid=(S // ts,), in_specs=in_specs, out_specs=out_specs, out_shape=out_shape,
        compiler_params=_params(("arbitrary",)),
    )(*args)
    return (outs[0], outs[1], outs[2]) if with_prev else (outs[0], None, outs[1])


def _fgate_fwd(h, wf, bf, *, name):
    S, D = h.shape
    ts = _tile(S, 256, 16)

    def body(h_ref, w_ref, b_ref, z_ref, f_ref, carry):
        @pl.when(pl.program_id(0) == 0)
        def _():
            carry[...] = jnp.zeros_like(carry)
        z = jnp.dot(h_ref[...], w_ref[...], preferred_element_type=F32) + b_ref[...]
        logf = jnp.minimum(z, 0.0) - jnp.log(1.0 + jnp.exp(-jnp.abs(z)))
        row = lax.broadcasted_iota(jnp.int32, (ts, ts), 0)
        col = lax.broadcasted_iota(jnp.int32, (ts, ts), 1)
        tril = (col <= row).astype(F32)
        run = jnp.dot(tril, logf, preferred_element_type=F32, precision=lax.Precision.HIGHEST) + carry[0:1, :]
        z_ref[...] = z
        f_ref[...] = run
        carry[0:1, :] = run[ts - 1:ts, :]

    return pl.pallas_call(
        body, name=name, grid=(S // ts,),
        in_specs=[pl.BlockSpec((ts, D), lambda i: (i, 0)), pl.BlockSpec((D, LANES), lambda i: (0, 0)),
                  pl.BlockSpec((1, LANES), lambda i: (0, 0))],
        out_specs=[pl.BlockSpec((ts, LANES), lambda i: (i, 0))] * 2,
        out_shape=[jax.ShapeDtypeStruct((S, LANES), F32)] * 2,
        scratch_shapes=[pltpu.VMEM((8, LANES), F32)],
        compiler_params=_params(("arbitrary",)),
    )(h, wf, bf)


def _fgate_bwd(dfq, dfk, z, *, name):
    S = z.shape[0]
    ts = _tile(S, 256, 16)
    n = S // ts

    def body(dq_ref, dk_ref, z_ref, dz_ref, sums_ref, carry):
        @pl.when(pl.program_id(0) == 0)
        def _():
            carry[...] = jnp.zeros_like(carry)
            sums_ref[...] = jnp.zeros_like(sums_ref)
        df = dq_ref[...] - dk_ref[...]
        row = lax.broadcasted_iota(jnp.int32, (ts, ts), 0)
        col = lax.broadcasted_iota(jnp.int32, (ts, ts), 1)
        triu = (col >= row).astype(F32)
        run = jnp.dot(triu, df, preferred_element_type=F32, precision=lax.Precision.HIGHEST) + carry[0:1, :]
        zv = z_ref[...]
        dz = run * (1.0 / (1.0 + jnp.exp(zv)))
        dz_ref[...] = dz.astype(BF16)
        sums_ref[0:1, :] += jnp.sum(dz, axis=0, keepdims=True)
        carry[0:1, :] = run[0:1, :]

    rev = pl.BlockSpec((ts, LANES), lambda i: (n - 1 - i, 0))
    return pl.pallas_call(
        body, name=name, grid=(n,), in_specs=[rev, rev, rev],
        out_specs=[rev, pl.BlockSpec((8, LANES), lambda i: (0, 0))],
        out_shape=[jax.ShapeDtypeStruct((S, LANES), BF16), jax.ShapeDtypeStruct((8, LANES), F32)],
        scratch_shapes=[pltpu.VMEM((8, LANES), F32)],
        compiler_params=_params(("arbitrary",)),
    )(dfq, dfk, z)


def _head_col(ref, rows, lane_mask):
    return jnp.sum(jnp.where(lane_mask, ref[rows, :], 0.0), axis=1, keepdims=True)


def _attn_fwd(qkv, fq, fk, *, heads, name, T=256):
    S, D3 = qkv.shape
    D = D3 // 3
    dh = D // heads
    T = _tile(S, T, 16)
    nq = S // T
    scale = dh ** -0.5
    hp = fk.shape[1]

    def body(q_ref, k_ref, v_ref, fq_ref, fk_ref, o_ref, lse_ref):
        h = pl.program_id(0)

        @pl.when(h == 0)
        def _():
            lse_ref[...] = jnp.zeros_like(lse_ref)
        lane = lax.broadcasted_iota(jnp.int32, (1, LANES), 1) == h
        row = lax.broadcasted_iota(jnp.int32, (T, T), 0)
        col = lax.broadcasted_iota(jnp.int32, (T, T), 1)

        def q_block(qi, _):
            rows = pl.ds(pl.multiple_of(qi * T, T), T)
            q = q_ref[rows, :]
            fq_col = _head_col(fq_ref, rows, lane)

            def kv_block(kj, carry, diag):
                m, l, acc = carry
                cols = pl.ds(pl.multiple_of(kj * T, T), T)
                s = lax.dot_general(q, k_ref[cols, :], (((1,), (1,)), ((), ())), preferred_element_type=F32) * scale
                s = s + (fq_col - fk_ref[kj, pl.ds(h, 1), :])
                if diag:
                    s = jnp.where(col <= row, s, NEG)
                m_new = jnp.maximum(m, jnp.max(s, axis=1, keepdims=True))
                p = jnp.exp(s - m_new)
                alpha = jnp.exp(m - m_new)
                l = alpha * l + jnp.sum(p, axis=1, keepdims=True)
                acc = alpha * acc + jnp.dot(p.astype(BF16), v_ref[cols, :], preferred_element_type=F32)
                return m_new, l, acc

            init = (jnp.full((T, 1), NEG, F32), jnp.zeros((T, 1), F32), jnp.zeros((T, dh), F32))
            carry = lax.fori_loop(0, qi, lambda kj, cr: kv_block(kj, cr, False), init)
            m, l, acc = kv_block(qi, carry, True)
            o_ref[rows, :] = (acc / l).astype(BF16)
            lse_ref[rows, :] = jnp.where(lane, m + jnp.log(l), lse_ref[rows, :])
            return 0

        lax.fori_loop(0, nq, q_block, 0)

    head = lambda part: pl.BlockSpec((S, dh), lambda h: (0, part * heads + h))
    return pl.pallas_call(
        body, name=name, grid=(heads,),
        in_specs=[head(0), head(1), head(2), pl.BlockSpec((S, LANES), lambda h: (0, 0)),
                  pl.BlockSpec((nq, hp, T), lambda h: (0, 0, 0))],
        out_specs=[pl.BlockSpec((S, dh), lambda h: (0, h)), pl.BlockSpec((S, LANES), lambda h: (0, 0))],
        out_shape=[jax.ShapeDtypeStruct((S, D), BF16), jax.ShapeDtypeStruct((S, LANES), F32)],
        compiler_params=_params(("arbitrary",)),
    )(qkv, qkv, qkv, fq, fk)


def _attn_bwd(qkv, o, do, fq, fk, lse, *, heads, name, T=256):
    S, D3 = qkv.shape
    D = D3 // 3
    dh = D // heads
    T = _tile(S, T, 16)
    nq = S // T
    scale = dh ** -0.5
    hp = fk.shape[1]

    def body(q_ref, k_ref, v_ref, o_ref, do_ref, fq_ref, fk_ref, lse_ref, dqkv_ref, dfq_ref, dfk_ref,
             dq_acc, fq_col, lse_col, delta_col, dfq_col):
        h = pl.program_id(0)

        @pl.when(h == 0)
        def _():
            dfq_ref[...] = jnp.zeros_like(dfq_ref)
            dfk_ref[...] = jnp.zeros_like(dfk_ref)
        lane = lax.broadcasted_iota(jnp.int32, (1, LANES), 1) == h
        row = lax.broadcasted_iota(jnp.int32, (T, T), 0)
        col = lax.broadcasted_iota(jnp.int32, (T, T), 1)
        dq_acc[...] = jnp.zeros_like(dq_acc)
        dfq_col[...] = jnp.zeros_like(dfq_col)

        def prep(qi, _):
            rows = pl.ds(pl.multiple_of(qi * T, T), T)
            fq_col[rows, :] = _head_col(fq_ref, rows, lane)
            lse_col[rows, :] = _head_col(lse_ref, rows, lane)
            delta_col[rows, :] = jnp.sum(do_ref[rows, :].astype(F32) * o_ref[rows, :].astype(F32), axis=1, keepdims=True)
            return 0

        lax.fori_loop(0, nq, prep, 0)

        def kv_block(kj, _):
            cols = pl.ds(pl.multiple_of(kj * T, T), T)
            k, v = k_ref[cols, :], v_ref[cols, :]
            fk_row = fk_ref[kj, pl.ds(h, 1), :]

            def q_block(qi, carry, diag):
                dk, dv, dfk = carry
                rows = pl.ds(pl.multiple_of(qi * T, T), T)
                q, dov = q_ref[rows, :], do_ref[rows, :]
                s = lax.dot_general(q, k, (((1,), (1,)), ((), ())), preferred_element_type=F32) * scale
                s = s + (fq_col[rows, :] - fk_row)
                p = jnp.exp(s - lse_col[rows, :])
                if diag:
                    p = jnp.where(col <= row, p, 0.0)
                dp = lax.dot_general(dov, v, (((1,), (1,)), ((), ())), preferred_element_type=F32)
                ds = p * (dp - delta_col[rows, :])
                dsb = ds.astype(BF16)
                dv = dv + lax.dot_general(p.astype(BF16), dov, (((0,), (0,)), ((), ())), preferred_element_type=F32)
                dk = dk + lax.dot_general(dsb, q, (((0,), (0,)), ((), ())), preferred_element_type=F32)
                dq_acc[rows, :] += jnp.dot(dsb, k, preferred_element_type=F32)
                dfq_col[rows, :] += jnp.sum(ds, axis=1, keepdims=True)
                dfk = dfk + jnp.sum(ds, axis=0, keepdims=True)
                return dk, dv, dfk

            init = (jnp.zeros((T, dh), F32), jnp.zeros((T, dh), F32), jnp.zeros((1, T), F32))
            carry = q_block(kj, init, True)
            dk, dv, dfk = lax.fori_loop(kj + 1, nq, lambda qi, cr: q_block(qi, cr, False), carry)
            dqkv_ref[1, cols, :] = (dk * scale).astype(BF16)
            dqkv_ref[2, cols, :] = dv.astype(BF16)
            dfk_ref[kj, pl.ds(h, 1), :] = dfk
            return 0

        lax.fori_loop(0, nq, kv_block, 0)

        def finish(qi, _):
            rows = pl.ds(pl.multiple_of(qi * T, T), T)
            dqkv_ref[0, rows, :] = (dq_acc[rows, :] * scale).astype(BF16)
            dfq_ref[rows, :] = jnp.where(lane, dfq_col[rows, :], dfq_ref[rows, :])
            return 0

        lax.fori_loop(0, nq, finish, 0)

    head = lambda part: pl.BlockSpec((S, dh), lambda h: (0, part * heads + h))
    own = pl.BlockSpec((S, dh), lambda h: (0, h))
    full = pl.BlockSpec((S, LANES), lambda h: (0, 0))
    krow = pl.BlockSpec((nq, hp, T), lambda h: (0, 0, 0))
    return pl.pallas_call(
        body, name=name, grid=(heads,),
        in_specs=[head(0), head(1), head(2), own, own, full, krow, full],
        out_specs=[pl.BlockSpec((3, S, dh), lambda h: (0, 0, h)), full, krow],
        out_shape=[jax.ShapeDtypeStruct((3, S, D), BF16), jax.ShapeDtypeStruct((S, LANES), F32),
                   jax.ShapeDtypeStruct((nq, hp, T), F32)],
        scratch_shapes=[pltpu.VMEM((S, dh), F32)] + [pltpu.VMEM((S, 1), F32)] * 4,
        compiler_params=_params(("arbitrary",)),
    )(qkv, qkv, qkv, o, do, fq, fk, lse)


def _shift_down(v, n):
    rows = lax.broadcasted_iota(jnp.int32, v.shape, 0)
    return jnp.where(rows >= n, pltpu.roll(v, n, axis=0), 0.0)


def _shift_up(v, n):
    S = v.shape[0]
    rows = lax.broadcasted_iota(jnp.int32, v.shape, 0)
    return jnp.where(rows < S - n, pltpu.roll(v, S - n, axis=0), 0.0)


def _conv_fwd(proj, conv_w, *, name, cb=LANES):
    S, D3 = proj.shape
    D = D3 // 3
    nb = D // cb

    def body(bg_ref, cg_ref, u_ref, w_ref, z_ref):
        uc = cg_ref[...].astype(F32) * u_ref[...].astype(F32)
        w = w_ref[...]
        y = w[2:3, :] * uc + w[1:2, :] * _shift_down(uc, 1) + w[0:1, :] * _shift_down(uc, 2)
        z_ref[...] = (bg_ref[...].astype(F32) * y).astype(BF16)

    part = lambda g: pl.BlockSpec((S, cb), lambda j: (0, g * nb + j))
    return pl.pallas_call(
        body, name=name, grid=(nb,),
        in_specs=[part(0), part(1), part(2), pl.BlockSpec((3, cb), lambda j: (0, j))],
        out_specs=pl.BlockSpec((S, cb), lambda j: (0, j)),
        out_shape=jax.ShapeDtypeStruct((S, D), BF16), compiler_params=_params(("parallel",)),
    )(proj, proj, proj, conv_w)


def _conv_bwd(proj, conv_w, dz, *, name, cb=LANES):
    S, D3 = proj.shape
    D = D3 // 3
    nb = D // cb

    def body(bg_ref, cg_ref, u_ref, w_ref, dz_ref, dp_ref, dw_ref):
        cg, u = cg_ref[...].astype(F32), u_ref[...].astype(F32)
        uc = cg * u
        w = w_ref[...]
        uc1, uc2 = _shift_down(uc, 1), _shift_down(uc, 2)
        y = w[2:3, :] * uc + w[1:2, :] * uc1 + w[0:1, :] * uc2
        dz = dz_ref[...].astype(F32)
        dp_ref[0] = (dz * y).astype(BF16)
        dy = dz * bg_ref[...].astype(F32)
        duc = w[2:3, :] * dy + w[1:2, :] * _shift_up(dy, 1) + w[0:1, :] * _shift_up(dy, 2)
        dp_ref[1] = (duc * u).astype(BF16)
        dp_ref[2] = (duc * cg).astype(BF16)
        dw_ref[...] = jnp.zeros_like(dw_ref)
        dw_ref[0:1, :] = jnp.sum(dy * uc2, axis=0, keepdims=True)
        dw_ref[1:2, :] = jnp.sum(dy * uc1, axis=0, keepdims=True)
        dw_ref[2:3, :] = jnp.sum(dy * uc, axis=0, keepdims=True)

    part = lambda g: pl.BlockSpec((S, cb), lambda j: (0, g * nb + j))
    return pl.pallas_call(
        body, name=name, grid=(nb,),
        in_specs=[part(0), part(1), part(2), pl.BlockSpec((3, cb), lambda j: (0, j)), pl.BlockSpec((S, cb), lambda j: (0, j))],
        out_specs=[pl.BlockSpec((3, S, cb), lambda j: (0, 0, j)), pl.BlockSpec((8, cb), lambda j: (0, j))],
        out_shape=[jax.ShapeDtypeStruct((3, S, D), BF16), jax.ShapeDtypeStruct((8, D), F32)],
        compiler_params=_params(("parallel",)),
    )(proj, proj, proj, conv_w, dz)


def _ada_fwd(c_all, ada_w, *, name):
    L, D, Ns = ada_w.shape
    tn = _tile(Ns, 512)

    def body(c_ref, w_ref, o_ref, act_ref):
        cv = c_ref[...]
        act = cv * (1.0 / (1.0 + jnp.exp(-cv)))
        act_ref[...] = act
        o_ref[...] = jnp.dot(act.astype(BF16), w_ref[...].astype(BF16), preferred_element_type=F32)

    return pl.pallas_call(
        body, name=name, grid=(L, Ns // tn),
        in_specs=[pl.BlockSpec((N_DEV, D), lambda l, j: (0, 0)), pl.BlockSpec((None, D, tn), lambda l, j: (l, 0, j))],
        out_specs=[pl.BlockSpec((None, N_DEV, tn), lambda l, j: (l, 0, j)), pl.BlockSpec((N_DEV, D), lambda l, j: (0, 0))],
        out_shape=[jax.ShapeDtypeStruct((L, N_DEV, Ns), F32), jax.ShapeDtypeStruct((N_DEV, D), F32)],
        compiler_params=_params(("arbitrary", "arbitrary")),
    )(c_all, ada_w)


def _select_mod(gathered, *, name):
    _, LB, Ns = gathered.shape
    L = LB // N_DEV

    def body(g_ref, o_ref):
        x, y, c = _me()
        b = 4 * x + 2 * y + c
        for j in range(N_CHIPS):
            for l in range(L):
                o_ref[j, pl.ds(l, 1), :] = g_ref[2 * j + c, pl.ds(l * N_DEV + b, 1), :]

    return pl.pallas_call(
        body, name=name, out_shape=jax.ShapeDtypeStruct((N_CHIPS, L, Ns), F32),
        in_specs=[pl.BlockSpec(memory_space=pltpu.VMEM)], out_specs=pl.BlockSpec(memory_space=pltpu.VMEM),
        compiler_params=_params(),
    )(gathered)


def _adamw_math(w, g, m, v):
    m = ADAM_B1 * m + (1.0 - ADAM_B1) * g
    v = ADAM_B2 * v + (1.0 - ADAM_B2) * jnp.square(g)
    m_hat = m / (1.0 - ADAM_B1 ** ADAM_STEP)
    v_hat = v / (1.0 - ADAM_B2 ** ADAM_STEP)
    delta = -ADAM_LR * (m_hat / (jnp.sqrt(v_hat) + ADAM_EPS) + ADAM_WD * w)
    return delta, m, v


def _adamw_shards(w, m, v, lands_a, lands_b, *, name, tr=128):
    L, R, C = w.shape
    tr = _tile(R, tr, 16)
    nr = R // tr

    def body(*refs):
        w_ref, m_ref, v_ref = refs[:3]
        la, lb = refs[3:3 + L], refs[3 + L:3 + 2 * L]
        g_ref, d_ref, mo_ref, vo_ref = refs[3 + 2 * L:]
        for l in range(L):
            @pl.when(pl.program_id(0) == l)
            def _():
                sa = la[l][0].astype(F32)
                sb = lb[l][0].astype(F32)
                for j in range(1, N_CHIPS):
                    sa = sa + la[l][j].astype(F32)
                    sb = sb + lb[l][j].astype(F32)
                g = sa + sb
                delta, mn, vn = _adamw_math(w_ref[...], g, m_ref[...], v_ref[...])
                g_ref[...] = g
                d_ref[...] = delta
                mo_ref[...] = mn
                vo_ref[...] = vn

    tile = pl.BlockSpec((None, tr, C), lambda l, i: (l, i, 0))

    def land(layer):
        return pl.BlockSpec((N_CHIPS, tr, C), lambda l, i: (0, jnp.where(l == layer, i, 0 if layer > 0 else nr - 1), 0))

    return pl.pallas_call(
        body, name=name, grid=(L, nr),
        in_specs=[tile] * 3 + [land(l) for l in range(L)] * 2, out_specs=[tile] * 4,
        out_shape=[jax.ShapeDtypeStruct((L, R, C), F32)] * 4, compiler_params=_params(("arbitrary", "arbitrary")),
    )(w, m, v, *lands_a, *lands_b)


def _adamw_ada(w, m, v, act_t, dmod, *, name, tr=256):
    L, D, Ns = w.shape
    tr = _tile(D, tr, 8)

    def body(w_ref, m_ref, v_ref, a_ref, d_ref, g_ref, dl_ref, mo_ref, vo_ref):
        x, y, _ = _me()
        g = jnp.dot(a_ref[...], d_ref[2 * x + y], preferred_element_type=F32, precision=lax.Precision.HIGHEST)
        delta, mn, vn = _adamw_math(w_ref[...], g, m_ref[...], v_ref[...])
        g_ref[...] = g
        dl_ref[...] = delta
        mo_ref[...] = mn
        vo_ref[...] = vn

    tile = pl.BlockSpec((None, tr, Ns), lambda l, i: (l, i, 0))
    return pl.pallas_call(
        body, name=name, grid=(L, D // tr),
        in_specs=[tile] * 3 + [pl.BlockSpec((tr, N_DEV), lambda l, i: (i, 0)),
                               pl.BlockSpec((N_CHIPS, None, N_DEV, Ns), lambda l, i: (0, l, 0, 0))],
        out_specs=[tile] * 4, out_shape=[jax.ShapeDtypeStruct((L, D, Ns), F32)] * 4,
        compiler_params=_params(("parallel", "parallel")),
    )(w, m, v, act_t, dmod)


def _adamw_small(w, m, v, gathered, *, rows, name):
    n, D = w.shape

    def body(w_ref, m_ref, v_ref, s_ref, g_ref, d_ref, mo_ref, vo_ref):
        for r, src in enumerate(rows):
            g = s_ref[0, src:src + 1, :]
            for d in range(1, N_DEV):
                g = g + s_ref[d, src:src + 1, :]
            g_ref[r:r + 1, :] = g
        g = g_ref[...]
        delta, mn, vn = _adamw_math(w_ref[...], g, m_ref[...], v_ref[...])
        d_ref[...] = delta
        mo_ref[...] = mn
        vo_ref[...] = vn

    vm = pl.BlockSpec(memory_space=pltpu.VMEM)
    return pl.pallas_call(
        body, name=name, in_specs=[vm] * 4, out_specs=[vm] * 4,
        out_shape=[jax.ShapeDtypeStruct((n, D), F32)] * 4, compiler_params=_params(),
    )(w, m, v, gathered)


def _adamw_conv_w(w, m, v, gathered4, *, name):
    Cs = w.shape[1]

    def body(w_ref, m_ref, v_ref, s_ref, g_ref, d_ref, mo_ref, vo_ref):
        x, y, _ = _me()
        j = 2 * x + y
        g = s_ref[j, 0]
        for d in range(1, N_DEV):
            g = g + s_ref[j, d]
        delta, mn, vn = _adamw_math(w_ref[...], g, m_ref[...], v_ref[...])
        g_ref[...] = g
        d_ref[...] = delta
        mo_ref[...] = mn
        vo_ref[...] = vn

    vm = pl.BlockSpec(memory_space=pltpu.VMEM)
    return pl.pallas_call(
        body, name=name, in_specs=[vm] * 4, out_specs=[vm] * 4,
        out_shape=[jax.ShapeDtypeStruct((8, Cs), F32)] * 4, compiler_params=_params(),
    )(w, m, v, gathered4)


def _loss_sum(gathered, *, row, name):
    _, _, D = gathered.shape

    def body(s_ref, o_ref):
        t = s_ref[0, row:row + 1, :]
        for d in range(1, N_DEV):
            t = t + s_ref[d, row:row + 1, :]
        o_ref[...] = jnp.broadcast_to(t, (8, D))

    vm = pl.BlockSpec(memory_space=pltpu.VMEM)
    return pl.pallas_call(body, name=name, in_specs=[vm], out_specs=vm, out_shape=jax.ShapeDtypeStruct((8, D), F32),
                          compiler_params=_params())(gathered)


def _pad_rows(a, n):
    return jnp.pad(a, ((0, n - a.shape[0]), (0, 0)))


def kernel(x, c, ada_w, ada_b, norm_mix, norm_mlp, fox_w_in, fox_b_f, fox_w_out, conv_w_in, conv_w, conv_w_out, mlp_w_up, mlp_w_down, final_norm, loss_target, m_ada_w, m_ada_b, m_norm_mix, m_norm_mlp, m_fox_w_in, m_fox_b_f, m_fox_w_out, m_conv_w_in, m_conv_w, m_conv_w_out, m_mlp_w_up, m_mlp_w_down, m_final_norm, v_ada_w, v_ada_b, v_norm_mix, v_norm_mlp, v_fox_w_in, v_fox_b_f, v_fox_w_out, v_conv_w_in, v_conv_w, v_conv_w_out, v_mlp_w_up, v_mlp_w_down, v_final_norm):
    S, D = x.shape[1], x.shape[2]
    H = fox_b_f.shape[-1]
    L = ada_w.shape[0]
    NM = ada_b.shape[1] // D
    Ns_ada = ada_w.shape[2]
    Cs_fox = fox_w_in.shape[2]
    Cs_conv = conv_w.shape[2]
    x0 = x[0]
    target = loss_target[0]

    shards = [fox_w_in[0].astype(BF16), fox_w_out[0].astype(BF16), conv_w_in[0].astype(BF16), conv_w_out[0].astype(BF16),
              mlp_w_up[0].astype(BF16), mlp_w_up[1].astype(BF16), mlp_w_down[0].astype(BF16), mlp_w_down[1].astype(BF16)]
    g_fin, g_fout, g_cin, g_cout, g_up0, g_up1, g_dn0, g_dn1 = _allgather_chips(shards, name="gather_weights")
    w_fin = jnp.transpose(g_fin, (1, 0, 2)).reshape(D, N_CHIPS * Cs_fox)
    w_qkv = w_fin[:, :3 * D]
    w_f = jnp.pad(w_fin[:, 3 * D:], ((0, 0), (0, LANES - H)))
    w_fout = g_fout.reshape(D, D)
    w_cin = jnp.transpose(g_cin, (1, 0, 2)).reshape(D, 3 * D)
    w_cout = g_cout.reshape(D, D)
    w_up = [g_up0, g_up1]
    w_dn = [g_dn0.reshape(-1, D), g_dn1.reshape(-1, D)]

    c_all = _allgather8(_pad_rows(c, 8), name="gather_c")[:, 0, :]
    mod_part, c_act = _ada_fwd(c_all, ada_w, name="ada_fwd")
    mod_all = _allgather8(mod_part.reshape(L * N_DEV, Ns_ada), name="gather_mod")
    mod = _select_mod(mod_all, name="select_mod")
    mod = jnp.transpose(mod, (1, 0, 2)).reshape(L, NM, 1, D) + ada_b.reshape(L, NM, 1, D)
    conv_w_all = _allgather8(_pad_rows(conv_w[0], 8), name="gather_conv_w")
    conv_w_full = jnp.transpose(conv_w_all[0::2, :3, :], (1, 0, 2)).reshape(3, D)

    def vec(a):
        return a.reshape(1, D)

    h0 = _norm_fwd(x0, vec(norm_mix[0]), mod[0, 1], mod[0, 0], name="norm_mix0")
    qkv = _mm_nn(h0, w_qkv, name="fox_in")
    b_f = jnp.pad(fox_b_f, ((0, 0), (0, LANES - H)))
    z_f, F_col = _fgate_fwd(h0, w_f, b_f, name="fgate_fwd")
    hp = max(8, H)
    at = _tile(S, 256, 16)
    F_row = jnp.transpose(_pad_rows(jnp.transpose(F_col[:, :H]), hp).reshape(hp, S // at, at), (1, 0, 2))
    o, lse = _attn_fwd(qkv, F_col, F_row, heads=H, name="attn_fwd", T=at)
    x1, mix0 = _mm_nn(o, w_fout, name="fox_out", epilogue="resid", res=x0, gate=mod[0, 2], tm=512)
    h1 = _norm_fwd(x1, vec(norm_mlp[0]), mod[0, 4], mod[0, 3], name="norm_mlp0")
    u0, a0 = _mm_nn(h1, w_up[0], name="mlp_up0", epilogue="relu2")
    x2, y0 = _mm_nn(a0, w_dn[0], name="mlp_down0", epilogue="resid", res=x1, gate=mod[0, 5], tm=512)
    h2 = _norm_fwd(x2, vec(norm_mix[1]), mod[1, 1], mod[1, 0], name="norm_mix1")
    proj = _mm_nn(h2, w_cin, name="conv_in")
    zc = _conv_fwd(proj, conv_w_full, name="conv_fwd")
    x3, mix1 = _mm_nn(zc, w_cout, name="conv_out", epilogue="resid", res=x2, gate=mod[1, 2], tm=512)
    h3 = _norm_fwd(x3, vec(norm_mlp[1]), mod[1, 4], mod[1, 3], name="norm_mlp1")
    u1, a1 = _mm_nn(h3, w_up[1], name="mlp_up1", epilogue="relu2")
    x4, y1 = _mm_nn(a1, w_dn[1], name="mlp_down1", epilogue="resid", res=x3, gate=mod[1, 5], tm=512)

    dx4, dy1, sums_f = _loss_bwd(x4, target, vec(final_norm), mod[1, 5], name="loss_bwd")
    du1 = _mm_nt(dy1, w_dn[1], name="mlp_down1_dx", epilogue="drelu2", extra=u1, out_dtype=BF16)
    gw_dn1 = _mm_tn(a1, dy1, name="mlp_down1_dw")
    gw_up1 = _mm_tn(h3, du1, name="mlp_up1_dw", out_parts=N_CHIPS)
    dh3 = _mm_nt(du1, w_up[1], name="mlp_up1_dx")
    dx3, dmix1, sums_mlp1 = _norm_bwd(x3, dh3, dx4, y1, vec(norm_mlp[1]), mod[1, 4], mod[1, 2], name="norm_mlp1_bwd")
    dzc = _mm_nt(dmix1, w_cout, name="conv_out_dx", out_dtype=BF16)
    gw_cout = _mm_tn(zc, dmix1, name="conv_out_dw")
    dproj, dconv_w = _conv_bwd(proj, conv_w_full, dzc, name="conv_bwd")
    gw_cin = _mm_tn(h2, dproj, name="conv_in_dw", out_parts=N_CHIPS, tn=512)
    dh2 = _mm_nt(dproj, w_cin, name="conv_in_dx")
    dx2, dy0, sums_mix1 = _norm_bwd(x2, dh2, dx3, mix1, vec(norm_mix[1]), mod[1, 1], mod[0, 5], name="norm_mix1_bwd")
    du0 = _mm_nt(dy0, w_dn[0], name="mlp_down0_dx", epilogue="drelu2", extra=u0, out_dtype=BF16)
    gw_dn0 = _mm_tn(a0, dy0, name="mlp_down0_dw")
    gw_up0 = _mm_tn(h1, du0, name="mlp_up0_dw", out_parts=N_CHIPS)
    dh1 = _mm_nt(du0, w_up[0], name="mlp_up0_dx")
    dx1, dmix0, sums_mlp0 = _norm_bwd(x1, dh1, dx2, y0, vec(norm_mlp[0]), mod[0, 4], mod[0, 2], name="norm_mlp0_bwd")
    do = _mm_nt(dmix0, w_fout, name="fox_out_dx", out_dtype=BF16)
    gw_fout = _mm_tn(o, dmix0, name="fox_out_dw")
    dqkv, dfq, dfk = _attn_bwd(qkv, o, do, F_col, F_row, lse, heads=H, name="attn_bwd", T=at)
    dfk_col = jnp.pad(jnp.transpose(jnp.transpose(dfk, (1, 0, 2)).reshape(hp, S)[:H]), ((0, 0), (0, LANES - H)))
    dz_f, sums_bf = _fgate_bwd(dfq, dfk_col, z_f, name="fgate_bwd")
    gw_qkv = _mm_tn(h0, dqkv, name="fox_in_dw")
    gw_f = _mm_tn(h0, dz_f, name="fox_gate_dw")
    dh0_f = _mm_nt(dz_f, w_f, name="fox_gate_dx")
    dh0 = _mm_nt(dqkv, w_qkv, name="fox_in_dx", epilogue="add", extra=dh0_f)
    grad_x, _, sums_mix0 = _norm_bwd(x0, dh0, dx1, mix0, vec(norm_mix[0]), mod[0, 1], None, name="norm_mix0_bwd")

    gw_fin = jnp.concatenate([gw_qkv, gw_f[:, :H]], axis=1).reshape(D, N_CHIPS, Cs_fox)
    parts = [jnp.transpose(gw_fin, (1, 0, 2)), gw_fout.reshape(N_CHIPS, -1, D), gw_cin, gw_cout.reshape(N_CHIPS, -1, D),
             gw_up0, gw_up1, gw_dn0.reshape(N_CHIPS, -1, D), gw_dn1.reshape(N_CHIPS, -1, D)]
    lands = _exchange_chips(parts, name="scatter_grads")
    lands_sib = _exchange_sibling(lands, name="sibling_grads")

    outs = {}

    def put(name_, res, shape):
        for kind, r in zip(("grad", "delta", "new_m", "new_v"), res):
            outs[kind + "_" + name_] = r.reshape(shape)

    put("fox_w_in", _adamw_shards(fox_w_in, m_fox_w_in, v_fox_w_in, lands[0:1], lands_sib[0:1], name="adamw_fox_in"), fox_w_in.shape)
    put("fox_w_out", _adamw_shards(fox_w_out, m_fox_w_out, v_fox_w_out, lands[1:2], lands_sib[1:2], name="adamw_fox_out"), fox_w_out.shape)
    put("conv_w_in", _adamw_shards(conv_w_in, m_conv_w_in, v_conv_w_in, lands[2:3], lands_sib[2:3], name="adamw_conv_in"), conv_w_in.shape)
    put("conv_w_out", _adamw_shards(conv_w_out, m_conv_w_out, v_conv_w_out, lands[3:4], lands_sib[3:4], name="adamw_conv_out"), conv_w_out.shape)
    put("mlp_w_up", _adamw_shards(mlp_w_up, m_mlp_w_up, v_mlp_w_up, lands[4:6], lands_sib[4:6], name="adamw_mlp_up"), mlp_w_up.shape)
    put("mlp_w_down", _adamw_shards(mlp_w_down, m_mlp_w_down, v_mlp_w_down, lands[6:8], lands_sib[6:8], name="adamw_mlp_down"), mlp_w_down.shape)

    dmod_rows = []
    for sm, sl in ((sums_mix0, sums_mlp0), (sums_mix1, sums_mlp1)):
        dmod_rows += [sm[0:1], sm[1:2], sm[3:4], sl[0:1], sl[1:2], sl[3:4]]
    bf_row = jnp.pad(sums_bf[0:1], ((0, 0), (0, D - LANES)))
    small = jnp.concatenate([sums_mix0[2:3], sums_mix1[2:3], sums_mlp0[2:3], sums_mlp1[2:3], sums_f[0:1], sums_f[1:2], bf_row,
                             jnp.zeros((1, D), F32)] + dmod_rows, axis=0)
    small_all = _allgather8(_pad_rows(small, -(-small.shape[0] // 8) * 8), name="gather_small")
    loss = _loss_sum(small_all, row=5, name="loss_sum")[0, 0]

    def rows_of(a_mix, a_mlp, a_fin, a_bf, a_ada):
        return jnp.concatenate([a_mix, a_mlp, a_fin.reshape(1, D), jnp.pad(a_bf, ((0, 0), (0, D - H))),
                                a_ada.reshape(L * NM, D)], axis=0)
    n_small = 2 * L + 2 + L * NM
    rw = -(-n_small // 8) * 8
    w_s = _pad_rows(rows_of(norm_mix, norm_mlp, final_norm, fox_b_f, ada_b), rw)
    m_s = _pad_rows(rows_of(m_norm_mix, m_norm_mlp, m_final_norm, m_fox_b_f, m_ada_b), rw)
    v_s = _pad_rows(rows_of(v_norm_mix, v_norm_mlp, v_final_norm, v_fox_b_f, v_ada_b), rw)
    src_rows = [0, 1, 2, 3, 4, 6] + [8 + r for r in range(L * NM)] + [7] * (rw - n_small)
    res_s = _adamw_small(w_s, m_s, v_s, small_all, rows=tuple(src_rows), name="adamw_small")
    for kind, r in zip(("grad", "delta", "new_m", "new_v"), res_s):
        outs[kind + "_norm_mix"] = r[0:L]
        outs[kind + "_norm_mlp"] = r[L:2 * L]
        outs[kind + "_final_norm"] = r[2 * L]
        outs[kind + "_fox_b_f"] = r[2 * L + 1:2 * L + 2, :H]
        outs[kind + "_ada_b"] = r[2 * L + 2:n_small].reshape(L, NM * D)

    dmod_all = small_all[:, 8:8 + L * NM, :].reshape(N_DEV, L, N_CHIPS, Ns_ada)
    dmod4 = jnp.transpose(dmod_all, (2, 1, 0, 3))
    act_t = jnp.transpose(c_act)
    put("ada_w", _adamw_ada(ada_w, m_ada_w, v_ada_w, act_t, dmod4, name="adamw_ada"), ada_w.shape)

    dconv_all = _allgather8(dconv_w, name="gather_dconv")
    dconv4 = jnp.transpose(dconv_all.reshape(N_DEV, 8, N_CHIPS, Cs_conv), (2, 0, 1, 3))
    res_c = _adamw_conv_w(_pad_rows(conv_w[0], 8), _pad_rows(m_conv_w[0], 8), _pad_rows(v_conv_w[0], 8), dconv4,
                          name="adamw_conv_w")
    for kind, r in zip(("grad", "delta", "new_m", "new_v"), res_c):
        outs[kind + "_conv_w"] = r[:3].reshape(conv_w.shape)

    names = ["ada_w", "ada_b", "norm_mix", "norm_mlp", "fox_w_in", "fox_b_f", "fox_w_out", "conv_w_in", "conv_w", "conv_w_out",
             "mlp_w_up", "mlp_w_down", "final_norm"]
    return (loss, grad_x[None], *[outs["grad_" + n] for n in names], *[outs["delta_" + n] for n in names],
            *[outs["new_m_" + n] for n in names], *[outs["new_v_" + n] for n in names])
```

```python
import functools

import jax
import jax.numpy as jnp
from jax import lax
from jax.experimental import pallas as pl
from jax.experimental.pallas import tpu as pltpu

F32 = jnp.float32
BF16 = jnp.bfloat16
MESH = pl.DeviceIdType.MESH
ANY = pl.BlockSpec(memory_space=pl.ANY)
HBM = pl.BlockSpec(memory_space=pltpu.HBM)
SEM = pl.BlockSpec(memory_space=pltpu.SEMAPHORE)
EFFECT = pltpu.SideEffectType.DATAFLOW_SIDE_EFFECTING

RMS_EPS = 1e-6
ADAM_LR = 0.001
ADAM_B1 = 0.9
ADAM_B2 = 0.999
ADAM_EPS = 1e-08
ADAM_WD = 0.01
ADAM_STEP = 10
N_CHIPS = 4
N_DEV = 8
LANES = 128
VMEM_LIMIT = 56 * 1024 * 1024
NEG = -1e30


def _params(sems=None, vmem=VMEM_LIMIT):
    return pltpu.CompilerParams(dimension_semantics=sems, vmem_limit_bytes=vmem)


def _tile(n, pref, unit=LANES):
    if n <= pref:
        return n
    t = (pref // unit) * unit
    while n % t:
        t -= unit
    return t


def _me():
    return lax.axis_index("x"), lax.axis_index("y"), lax.axis_index("c")


def _call(body, deps, **kw):
    nd = len(deps)

    def wrapped(*refs):
        body(*refs[nd:])

    kw["in_specs"] = [ANY] * nd + list(kw["in_specs"])
    fn = pl.pallas_call(wrapped, **kw)
    return lambda *args: fn(*deps, *args)


def _allgather8(v, *, name, deps=()):
    R, C = v.shape

    def body(v_ref, out_ref, send_sems, recv_sems):
        x, y, c = _me()
        me = 4 * x + 2 * y + c
        out_ref[me] = v_ref[...]
        copies = []
        for k in range(1, N_DEV):
            px, py, pc = (x + (k >> 2)) % 2, (y + ((k >> 1) & 1)) % 2, (c + (k & 1)) % 2
            copies.append(pltpu.make_async_remote_copy(
                src_ref=v_ref, dst_ref=out_ref.at[me], send_sem=send_sems.at[k - 1], recv_sem=recv_sems.at[k - 1],
                device_id=(px, py, pc), device_id_type=MESH))
        for cp in copies:
            cp.start()
        for k in range(1, N_DEV):
            px, py, pc = (x + (k >> 2)) % 2, (y + ((k >> 1) & 1)) % 2, (c + (k & 1)) % 2
            peer = 4 * px + 2 * py + pc
            pltpu.make_async_remote_copy(
                src_ref=v_ref, dst_ref=out_ref.at[peer], send_sem=send_sems.at[k - 1], recv_sem=recv_sems.at[k - 1],
                device_id=(px, py, pc), device_id_type=MESH).wait_recv()
        for cp in copies:
            cp.wait_send()

    return _call(
        body, deps, name=name,
        out_shape=jax.ShapeDtypeStruct((N_DEV, R, C), v.dtype),
        in_specs=[pl.BlockSpec(memory_space=pltpu.VMEM)],
        out_specs=pl.BlockSpec(memory_space=pltpu.VMEM),
        scratch_shapes=[pltpu.SemaphoreType.DMA((N_DEV - 1,)), pltpu.SemaphoreType.DMA((N_DEV - 1,))],
        compiler_params=_params(),
    )(v)


def _chip_peers(x, y):
    return [((x + (k >> 1)) % 2, (y + (k & 1)) % 2) for k in range(1, N_CHIPS)]


def _slot(x, y, k):
    return 2 * ((x + (k >> 1)) % 2) + (y + (k & 1)) % 2


def _split_copies(kind, groups, send_sems, recv_sems):
    x, y, c = _me()
    j = 2 * x + y
    copies = []
    for a, g in enumerate(groups):
        if kind == "sibling":
            parts, land, sib = g
            for k in range(N_CHIPS):
                s = _slot(x, y, k)
                copies.append(pltpu.make_async_remote_copy(
                    src_ref=(parts if k == 0 else land).at[s], dst_ref=sib.at[s], send_sem=send_sems.at[N_CHIPS * a + k],
                    recv_sem=recv_sems.at[N_CHIPS * a + k], device_id=(x, y, 1 - c), device_id_type=MESH))
        else:
            land = g[-1]
            for k, (px, py) in enumerate(_chip_peers(x, y)):
                src = land.at[j] if kind == "gather" else g[0].at[2 * px + py]
                copies.append(pltpu.make_async_remote_copy(
                    src_ref=src, dst_ref=land.at[j], send_sem=send_sems.at[3 * a + k], recv_sem=recv_sems.at[3 * a + k],
                    device_id=(px, py, c), device_id_type=MESH))
    return copies


def _split_start(kind, groups, *, name, dep=None):
    flat = [a for g in groups for a in g]
    nf, per = len(flat), len(groups[0])
    ncp = len(groups) * (N_CHIPS if kind == "sibling" else 3)
    nd = 0 if dep is None else 1

    def body(*refs):
        ins = refs[nd:nd + nf]
        send_sems, recv_sems, token = refs[nd + nf], refs[nd + nf + 1], refs[-1]
        for cp in _split_copies(kind, [ins[i:i + per] for i in range(0, nf, per)], send_sems, recv_sems):
            cp.start()
        token[...] = jnp.zeros_like(token)

    outs = pl.pallas_call(
        body, name=name,
        out_shape=(pltpu.SemaphoreType.DMA((ncp,)), pltpu.SemaphoreType.DMA((ncp,)), *[pltpu.HBM(a.shape, a.dtype) for a in flat],
                   jax.ShapeDtypeStruct((8, LANES), F32)),
        in_specs=[ANY] * nd + [HBM] * nf,
        out_specs=(SEM, SEM, *[HBM] * nf, pl.BlockSpec(memory_space=pltpu.VMEM)),
        input_output_aliases={nd + i: 2 + i for i in range(nf)},
        compiler_params=pltpu.CompilerParams(has_side_effects=EFFECT),
    )(*([dep] if nd else []), *[pltpu.with_memory_space_constraint(a, pltpu.HBM) for a in flat])
    thru = list(outs[2:2 + nf])
    return outs[0], outs[1], [tuple(thru[i:i + per]) for i in range(0, nf, per)], outs[-1]


def _split_wait(kind, started, after, *, name):
    send_sems, recv_sems, groups, _ = started
    flat = [a for g in groups for a in g]
    nf, per = len(flat), len(groups[0])

    def body(*refs):
        ins = refs[:nf]
        for cp in _split_copies(kind, [ins[i:i + per] for i in range(0, nf, per)], refs[nf], refs[nf + 1]):
            cp.wait_send()
            cp.wait_recv()

    outs = pl.pallas_call(
        body, name=name,
        out_shape=tuple(pltpu.HBM(a.shape, a.dtype) for a in flat),
        in_specs=[HBM] * nf + [SEM, SEM] + [ANY] * len(after), out_specs=tuple([HBM] * nf),
        input_output_aliases={i: i for i in range(nf)},
        compiler_params=pltpu.CompilerParams(has_side_effects=EFFECT),
    )(*flat, send_sems, recv_sems, *after)
    outs = list(outs)
    return [tuple(outs[i:i + per]) for i in range(0, nf, per)]


def _place_cast(shard, chip, *, name):
    R, C = shard.shape
    if R % 16 == 0:
        tr, tc = _tile(R, 512, 16), C
    else:
        tr, tc = R, _tile(C, 256)

    def body(chip_ref, x_ref, o_ref):
        o_ref[...] = x_ref[...].astype(BF16)

    return pl.pallas_call(
        body, name=name,
        grid_spec=pltpu.PrefetchScalarGridSpec(
            num_scalar_prefetch=1, grid=(R // tr, C // tc),
            in_specs=[pl.BlockSpec((tr, tc), lambda i, j, chip_ref: (i, j))],
            out_specs=pl.BlockSpec((None, tr, tc), lambda i, j, chip_ref: (chip_ref[0], i, j))),
        out_shape=jax.ShapeDtypeStruct((N_CHIPS, R, C), BF16), compiler_params=_params(("parallel", "parallel")),
    )(chip, shard)


def _accumulate(part, acc_ref, nk, finalize):
    if nk == 1:
        finalize(part)
        return
    k = pl.program_id(2)

    @pl.when(k == 0)
    def _():
        acc_ref[...] = part

    @pl.when(k > 0)
    def _():
        acc_ref[...] += part

    @pl.when(k == nk - 1)
    def _():
        finalize(acc_ref[...])


def _mm_nn(a, b, *, name, epilogue="plain", res=None, gate=None, out_dtype=BF16, tm=1024, tn=1024, tk=2048, deps=()):
    if a.ndim == 3:
        Q, M, Kq = a.shape
        K = Q * Kq
    else:
        (M, K), Kq = a.shape, a.shape[1]
    tm, tk = _tile(M, tm, 16), _tile(Kq, tk)
    if a.ndim == 3:
        pa = Kq // tk
        a_spec = pl.BlockSpec((None, tm, tk), lambda i, j, k: (k // pa, i, k % pa))
    else:
        a_spec = pl.BlockSpec((tm, tk), lambda i, j, k: (i, k))
    if b.ndim == 3:
        P, _, Ns = b.shape
        N = P * Ns
        tn = _tile(Ns, tn)
        per = Ns // tn
        b_spec = pl.BlockSpec((None, tk, tn), lambda i, j, k: (j // per, k, j % per))
    else:
        N = b.shape[1]
        tn = _tile(N, tn)
        b_spec = pl.BlockSpec((tk, tn), lambda i, j, k: (k, j))
    nk = K // tk
    tile = pl.BlockSpec((tm, tn), lambda i, j, k: (i, j))

    def body(*refs):
        acc_ref = refs[-1] if nk > 1 else None
        a_ref, b_ref = refs[0], refs[1]
        part = jnp.dot(a_ref[...], b_ref[...], preferred_element_type=F32)
        if epilogue == "plain":
            def fin(acc):
                refs[2][...] = acc.astype(out_dtype)
        elif epilogue == "relu2":
            def fin(acc):
                refs[2][...] = acc.astype(BF16)
                refs[3][...] = jnp.square(jnp.maximum(acc, 0.0)).astype(BF16)
        elif epilogue == "add":
            def fin(acc):
                refs[3][...] = (acc + refs[2][...]).astype(out_dtype)
        else:
            def fin(acc):
                refs[4][...] = refs[2][...] + refs[3][...] * acc
                refs[5][...] = acc.astype(BF16)
        _accumulate(part, acc_ref, nk, fin)

    in_specs = [a_spec, b_spec]
    args = [a, b]
    if epilogue == "plain":
        out_shape, out_specs = jax.ShapeDtypeStruct((M, N), out_dtype), tile
    elif epilogue == "relu2":
        out_shape, out_specs = [jax.ShapeDtypeStruct((M, N), BF16)] * 2, [tile, tile]
    elif epilogue == "add":
        in_specs.append(tile)
        args.append(res)
        out_shape, out_specs = jax.ShapeDtypeStruct((M, N), out_dtype), tile
    else:
        in_specs += [tile, pl.BlockSpec((1, tn), lambda i, j, k: (0, j))]
        args += [res, gate]
        out_shape, out_specs = [jax.ShapeDtypeStruct((M, N), F32), jax.ShapeDtypeStruct((M, N), BF16)], [tile, tile]
    return _call(
        body, deps, name=name, grid=(M // tm, N // tn, nk), in_specs=in_specs, out_specs=out_specs, out_shape=out_shape,
        scratch_shapes=[pltpu.VMEM((tm, tn), F32)] if nk > 1 else [],
        compiler_params=_params(("parallel", "parallel", "arbitrary")),
    )(*args)


def _mm_nt(a, b, *, name, n=None, epilogue="plain", extra=None, out_dtype=F32, tm=1024, tn=1024, tk=2048, deps=()):
    if a.ndim == 3:
        Q, M, Kq = a.shape
        K = Q * Kq
    else:
        (M, K), Kq = a.shape, a.shape[1]
    if b.ndim == 3:
        P, N, Ks = b.shape
    else:
        N, Ks = b.shape
    N = n or N
    tm, tn, tk = _tile(M, tm, 16), _tile(N, tn), _tile(min(Kq, Ks), tk)
    nk = K // tk
    if a.ndim == 3:
        pa = Kq // tk
        a_spec = pl.BlockSpec((None, tm, tk), lambda i, j, k: (k // pa, i, k % pa))
    else:
        a_spec = pl.BlockSpec((tm, tk), lambda i, j, k: (i, k))
    if b.ndim == 3:
        pb = Ks // tk
        b_spec = pl.BlockSpec((None, tn, tk), lambda i, j, k: (k // pb, j, k % pb))
    else:
        b_spec = pl.BlockSpec((tn, tk), lambda i, j, k: (j, k))
    tile = pl.BlockSpec((tm, tn), lambda i, j, k: (i, j))

    def body(*refs):
        acc_ref = refs[-1] if nk > 1 else None
        part = lax.dot_general(refs[0][...], refs[1][...], (((1,), (1,)), ((), ())), preferred_element_type=F32)
        if epilogue == "plain":
            def fin(acc):
                refs[2][...] = acc.astype(out_dtype)
        elif epilogue == "add":
            def fin(acc):
                refs[3][...] = (acc + refs[2][...]).astype(out_dtype)
        else:
            def fin(acc):
                refs[3][...] = (acc * (2.0 * jnp.maximum(refs[2][...].astype(F32), 0.0))).astype(out_dtype)
        _accumulate(part, acc_ref, nk, fin)

    in_specs, args = [a_spec, b_spec], [a, b]
    if epilogue != "plain":
        in_specs.append(tile)
        args.append(extra)
    return _call(
        body, deps, name=name, grid=(M // tm, N // tn, nk), in_specs=in_specs, out_specs=tile,
        out_shape=jax.ShapeDtypeStruct((M, N), out_dtype),
        scratch_shapes=[pltpu.VMEM((tm, tn), F32)] if nk > 1 else [],
        compiler_params=_params(("parallel", "parallel", "arbitrary")),
    )(*args)


def _mm_tn(a, b, *, name, out_parts=1, tm=1024, tn=1024, tk=2048, deps=()):
    if a.ndim == 3:
        Qa, M, Kq = a.shape
        Kd = Qa * Kq
    else:
        (M, Kd), Kq = a.shape, a.shape[1]
    if b.ndim == 3:
        Q, _, Nq = b.shape
        N = Q * Nq
    else:
        N, Nq = b.shape[1], b.shape[1]
    Ns = N // out_parts
    tn = _tile(Ns, tn)
    while Nq % tn or Ns % tn:
        tn -= LANES
    tm, tk = _tile(Kq, tm), _tile(M, tk, 16)
    nk = M // tk
    if a.ndim == 3:
        pa = Kq // tm
        a_spec = pl.BlockSpec((None, tk, tm), lambda i, j, k: (i // pa, k, i % pa))
    else:
        a_spec = pl.BlockSpec((tk, tm), lambda i, j, k: (k, i))
    if b.ndim == 3:
        pb = Nq // tn
        b_spec = pl.BlockSpec((None, tk, tn), lambda i, j, k: (j // pb, k, j % pb))
    else:
        b_spec = pl.BlockSpec((tk, tn), lambda i, j, k: (k, j))
    if out_parts > 1:
        po = Ns // tn
        o_spec = pl.BlockSpec((None, tm, tn), lambda i, j, k: (j // po, i, j % po))
        out_shape = jax.ShapeDtypeStruct((out_parts, Kd, Ns), BF16)
    else:
        o_spec = pl.BlockSpec((tm, tn), lambda i, j, k: (i, j))
        out_shape = jax.ShapeDtypeStruct((Kd, N), BF16)

    def body(*refs):
        acc_ref = refs[-1] if nk > 1 else None
        part = lax.dot_general(refs[0][...], refs[1][...], (((0,), (0,)), ((), ())), preferred_element_type=F32)

        def fin(acc):
            refs[2][...] = acc.astype(BF16)
        _accumulate(part, acc_ref, nk, fin)

    return _call(
        body, deps, name=name, grid=(Kd // tm, N // tn, nk),
        in_specs=[a_spec, b_spec], out_specs=o_spec, out_shape=out_shape,
        scratch_shapes=[pltpu.VMEM((tm, tn), F32)] if nk > 1 else [],
        compiler_params=_params(("parallel", "parallel", "arbitrary")),
    )(a, b)


def _rows(S, D, i_map=lambda i: (i, 0), ts=512):
    return pl.BlockSpec((ts, D), i_map)


def _norm_fwd(x, gain, sc, sh, *, name, deps=()):
    S, D = x.shape
    ts = _tile(S, 512, 16)
    vec = pl.BlockSpec((1, D), lambda i: (0, 0))

    def body(x_ref, g_ref, sc_ref, sh_ref, h_ref):
        xv = x_ref[...]
        r = lax.rsqrt(jnp.mean(xv * xv, axis=-1, keepdims=True) + RMS_EPS)
        h = (xv * r) * g_ref[...]
        h_ref[...] = (h * (1.0 + sc_ref[...]) + sh_ref[...]).astype(BF16)

    return _call(
        body, deps, name=name, grid=(S // ts,), in_specs=[_rows(S, D, ts=ts), vec, vec, vec], out_specs=_rows(S, D, ts=ts),
        out_shape=jax.ShapeDtypeStruct((S, D), BF16), compiler_params=_params(("parallel",)),
    )(x, gain, sc, sh)


def _loss_bwd(x, target, gain, gate_prev, *, name, deps=()):
    S, D = x.shape
    ts = _tile(S, 256, 16)
    vec = pl.BlockSpec((1, D), lambda i: (0, 0))

    def body(x_ref, t_ref, g_ref, gp_ref, dx_ref, dp_ref, sums_ref):
        @pl.when(pl.program_id(0) == 0)
        def _():
            sums_ref[...] = jnp.zeros_like(sums_ref)
        xv = x_ref[...]
        r = lax.rsqrt(jnp.mean(xv * xv, axis=-1, keepdims=True) + RMS_EPS)
        xn = xv * r
        err = xn * g_ref[...] - t_ref[...]
        loss = 0.5 * jnp.sum(jnp.mean(err * err, axis=-1, keepdims=True), axis=0, keepdims=True)
        dy = err * (1.0 / D)
        dxn = dy * g_ref[...]
        dx = r * (dxn - xn * jnp.mean(dxn * xn, axis=-1, keepdims=True))
        dx_ref[...] = dx
        dp_ref[...] = (gp_ref[...] * dx).astype(BF16)
        sums_ref[0:1, :] += jnp.sum(dy * xn, axis=0, keepdims=True)
        sums_ref[1:2, :] += jnp.broadcast_to(loss, (1, D))

    return _call(
        body, deps, name=name, grid=(S // ts,),
        in_specs=[_rows(S, D, ts=ts), _rows(S, D, ts=ts), vec, vec],
        out_specs=[_rows(S, D, ts=ts), _rows(S, D, ts=ts), pl.BlockSpec((8, D), lambda i: (0, 0))],
        out_shape=[jax.ShapeDtypeStruct((S, D), F32), jax.ShapeDtypeStruct((S, D), BF16), jax.ShapeDtypeStruct((8, D), F32)],
        compiler_params=_params(("arbitrary",)),
    )(x, target, gain, gate_prev)


def _norm_bwd(x, dh, dxp, mix, gain, sc, gate_prev, *, name, deps=()):
    S, D = x.shape
    ts = _tile(S, 256, 16)
    vec = pl.BlockSpec((1, D), lambda i: (0, 0))
    with_prev = gate_prev is not None

    def body(*refs):
        x_ref, dh_ref, dxp_ref, mix_ref, g_ref, sc_ref = refs[:6]
        outs = refs[7:] if with_prev else refs[6:]
        sums_ref = outs[-1]

        @pl.when(pl.program_id(0) == 0)
        def _():
            sums_ref[...] = jnp.zeros_like(sums_ref)
        xv, dhv, dxpv = x_ref[...], dh_ref[...], dxp_ref[...]
        r = lax.rsqrt(jnp.mean(xv * xv, axis=-1, keepdims=True) + RMS_EPS)
        xn = xv * r
        hn = xn * g_ref[...]
        dhn = dhv * (1.0 + sc_ref[...])
        dxn = dhn * g_ref[...]
        dx = dxpv + r * (dxn - xn * jnp.mean(dxn * xn, axis=-1, keepdims=True))
        outs[0][...] = dx
        if with_prev:
            outs[1][...] = (refs[6][...] * dx).astype(BF16)
        sums_ref[0:1, :] += jnp.sum(dhv, axis=0, keepdims=True)
        sums_ref[1:2, :] += jnp.sum(dhv * hn, axis=0, keepdims=True)
        sums_ref[2:3, :] += jnp.sum(dhn * xn, axis=0, keepdims=True)
        sums_ref[3:4, :] += jnp.sum(dxpv * mix_ref[...].astype(F32), axis=0, keepdims=True)

    tile = _rows(S, D, ts=ts)
    in_specs = [tile, tile, tile, tile, vec, vec] + ([vec] if with_prev else [])
    args = [x, dh, dxp, mix, gain, sc] + ([gate_prev] if with_prev else [])
    out_specs = [tile] + ([tile] if with_prev else []) + [pl.BlockSpec((8, D), lambda i: (0, 0))]
    out_shape = ([jax.ShapeDtypeStruct((S, D), F32)] + ([jax.ShapeDtypeStruct((S, D), BF16)] if with_prev else [])
                 + [jax.ShapeDtypeStruct((8, D), F32)])
    outs = _call(
        body, deps, name=name, grid=(S // ts,), in_specs=in_specs, out_specs=out_specs, out_shape=out_shape,
        compiler_params=_params(("arbitrary",)),
    )(*args)
    return (outs[0], outs[1], outs[2]) if with_prev else (outs[0], None, outs[1])


def _fgate_fwd(h, wf, bf, *, name, deps=()):
    S, D = h.shape
    ts = _tile(S, 256, 16)

    def body(h_ref, w_ref, b_ref, z_ref, f_ref, carry):
        @pl.when(pl.program_id(0) == 0)
        def _():
            carry[...] = jnp.zeros_like(carry)
        z = lax.dot_general(h_ref[...], w_ref[...], (((1,), (1,)), ((), ())), preferred_element_type=F32) + b_ref[...]
        logf = jnp.minimum(z, 0.0) - jnp.log(1.0 + jnp.exp(-jnp.abs(z)))
        row = lax.broadcasted_iota(jnp.int32, (ts, ts), 0)
        col = lax.broadcasted_iota(jnp.int32, (ts, ts), 1)
        tril = (col <= row).astype(F32)
        run = jnp.dot(tril, logf, preferred_element_type=F32, precision=lax.Precision.HIGHEST) + carry[0:1, :]
        z_ref[...] = z
        f_ref[...] = run
        carry[0:1, :] = run[ts - 1:ts, :]

    return _call(
        body, deps, name=name, grid=(S // ts,),
        in_specs=[pl.BlockSpec((ts, D), lambda i: (i, 0)), pl.BlockSpec((LANES, D), lambda i: (0, 0)),
                  pl.BlockSpec((1, LANES), lambda i: (0, 0))],
        out_specs=[pl.BlockSpec((ts, LANES), lambda i: (i, 0))] * 2,
        out_shape=[jax.ShapeDtypeStruct((S, LANES), F32)] * 2,
        scratch_shapes=[pltpu.VMEM((8, LANES), F32)],
        compiler_params=_params(("arbitrary",)),
    )(h, wf, bf)


def _fgate_bwd(dfq, dfk, z, *, name, deps=()):
    S = z.shape[0]
    ts = _tile(S, 256, 16)
    n = S // ts

    def body(dq_ref, dk_ref, z_ref, dz_ref, sums_ref, carry):
        @pl.when(pl.program_id(0) == 0)
        def _():
            carry[...] = jnp.zeros_like(carry)
            sums_ref[...] = jnp.zeros_like(sums_ref)
        df = dq_ref[...] - dk_ref[...]
        row = lax.broadcasted_iota(jnp.int32, (ts, ts), 0)
        col = lax.broadcasted_iota(jnp.int32, (ts, ts), 1)
        triu = (col >= row).astype(F32)
        run = jnp.dot(triu, df, preferred_element_type=F32, precision=lax.Precision.HIGHEST) + carry[0:1, :]
        zv = z_ref[...]
        dz = run * (1.0 / (1.0 + jnp.exp(zv)))
        dz_ref[...] = dz.astype(BF16)
        sums_ref[0:1, :] += jnp.sum(dz, axis=0, keepdims=True)
        carry[0:1, :] = run[0:1, :]

    rev = pl.BlockSpec((ts, LANES), lambda i: (n - 1 - i, 0))
    return _call(
        body, deps, name=name, grid=(n,), in_specs=[rev, rev, rev],
        out_specs=[rev, pl.BlockSpec((8, LANES), lambda i: (0, 0))],
        out_shape=[jax.ShapeDtypeStruct((S, LANES), BF16), jax.ShapeDtypeStruct((8, LANES), F32)],
        scratch_shapes=[pltpu.VMEM((8, LANES), F32)],
        compiler_params=_params(("arbitrary",)),
    )(dfq, dfk, z)


def _head_col(ref, rows, lane_mask):
    return jnp.sum(jnp.where(lane_mask, ref[rows, :], 0.0), axis=1, keepdims=True)


def _attn_fwd(qkv, fq, fk, *, heads, name, T=256, deps=()):
    S, D3 = qkv.shape
    D = D3 // 3
    dh = D // heads
    T = _tile(S, T, 16)
    nq = S // T
    scale = dh ** -0.5
    hp = fk.shape[1]

    def body(q_ref, k_ref, v_ref, fq_ref, fk_ref, o_ref, lse_ref):
        h = pl.program_id(0)

        @pl.when(h == 0)
        def _():
            lse_ref[...] = jnp.zeros_like(lse_ref)
        lane = lax.broadcasted_iota(jnp.int32, (1, LANES), 1) == h
        row = lax.broadcasted_iota(jnp.int32, (T, T), 0)
        col = lax.broadcasted_iota(jnp.int32, (T, T), 1)

        def q_block(qi, _):
            rows = pl.ds(pl.multiple_of(qi * T, T), T)
            q = q_ref[rows, :]
            fq_col = _head_col(fq_ref, rows, lane)

            def kv_block(kj, carry, diag):
                m, l, acc = carry
                cols = pl.ds(pl.multiple_of(kj * T, T), T)
                s = lax.dot_general(q, k_ref[cols, :], (((1,), (1,)), ((), ())), preferred_element_type=F32) * scale
                s = s + (fq_col - fk_ref[kj, pl.ds(h, 1), :])
                if diag:
                    s = jnp.where(col <= row, s, NEG)
                m_new = jnp.maximum(m, jnp.max(s, axis=1, keepdims=True))
                p = jnp.exp(s - m_new)
                alpha = jnp.exp(m - m_new)
                l = alpha * l + jnp.sum(p, axis=1, keepdims=True)
                acc = alpha * acc + jnp.dot(p.astype(BF16), v_ref[cols, :], preferred_element_type=F32)
                return m_new, l, acc

            init = (jnp.full((T, 1), NEG, F32), jnp.zeros((T, 1), F32), jnp.zeros((T, dh), F32))
            carry = lax.fori_loop(0, qi, lambda kj, cr: kv_block(kj, cr, False), init)
            m, l, acc = kv_block(qi, carry, True)
            o_ref[rows, :] = (acc / l).astype(BF16)
            lse_ref[rows, :] = jnp.where(lane, m + jnp.log(l), lse_ref[rows, :])
            return 0

        lax.fori_loop(0, nq, q_block, 0)

    head = lambda part: pl.BlockSpec((S, dh), lambda h: (0, part * heads + h))
    return _call(
        body, deps, name=name, grid=(heads,),
        in_specs=[head(0), head(1), head(2), pl.BlockSpec((S, LANES), lambda h: (0, 0)),
                  pl.BlockSpec((nq, hp, T), lambda h: (0, 0, 0))],
        out_specs=[pl.BlockSpec((S, dh), lambda h: (0, h)), pl.BlockSpec((S, LANES), lambda h: (0, 0))],
        out_shape=[jax.ShapeDtypeStruct((S, D), BF16), jax.ShapeDtypeStruct((S, LANES), F32)],
        compiler_params=_params(("arbitrary",)),
    )(qkv, qkv, qkv, fq, fk)


def _attn_bwd(qkv, o, do, fq, fk, lse, *, heads, name, T=256, deps=()):
    S, D3 = qkv.shape
    D = D3 // 3
    dh = D // heads
    T = _tile(S, T, 16)
    nq = S // T
    scale = dh ** -0.5
    hp = fk.shape[1]

    def body(q_ref, k_ref, v_ref, o_ref, do_ref, fq_ref, fk_ref, lse_ref, dqkv_ref, dfq_ref, dfk_ref,
             dq_acc, fq_col, lse_col, delta_col, dfq_col):
        h = pl.program_id(0)

        @pl.when(h == 0)
        def _():
            dfq_ref[...] = jnp.zeros_like(dfq_ref)
            dfk_ref[...] = jnp.zeros_like(dfk_ref)
        lane = lax.broadcasted_iota(jnp.int32, (1, LANES), 1) == h
        row = lax.broadcasted_iota(jnp.int32, (T, T), 0)
        col = lax.broadcasted_iota(jnp.int32, (T, T), 1)
        dq_acc[...] = jnp.zeros_like(dq_acc)
        dfq_col[...] = jnp.zeros_like(dfq_col)

        def prep(qi, _):
            rows = pl.ds(pl.multiple_of(qi * T, T), T)
            fq_col[rows, :] = _head_col(fq_ref, rows, lane)
            lse_col[rows, :] = _head_col(lse_ref, rows, lane)
            delta_col[rows, :] = jnp.sum(do_ref[rows, :].astype(F32) * o_ref[rows, :].astype(F32), axis=1, keepdims=True)
            return 0

        lax.fori_loop(0, nq, prep, 0)

        def kv_block(kj, _):
            cols = pl.ds(pl.multiple_of(kj * T, T), T)
            k, v = k_ref[cols, :], v_ref[cols, :]
            fk_row = fk_ref[kj, pl.ds(h, 1), :]

            def q_block(qi, carry, diag):
                dk, dv, dfk = carry
                rows = pl.ds(pl.multiple_of(qi * T, T), T)
                q, dov = q_ref[rows, :], do_ref[rows, :]
                s = lax.dot_general(q, k, (((1,), (1,)), ((), ())), preferred_element_type=F32) * scale
                s = s + (fq_col[rows, :] - fk_row)
                p = jnp.exp(s - lse_col[rows, :])
                if diag:
                    p = jnp.where(col <= row, p, 0.0)
                dp = lax.dot_general(dov, v, (((1,), (1,)), ((), ())), preferred_element_type=F32)
                ds = p * (dp - delta_col[rows, :])
                dsb = ds.astype(BF16)
                dv = dv + lax.dot_general(p.astype(BF16), dov, (((0,), (0,)), ((), ())), preferred_element_type=F32)
                dk = dk + lax.dot_general(dsb, q, (((0,), (0,)), ((), ())), preferred_element_type=F32)
                dq_acc[rows, :] += jnp.dot(dsb, k, preferred_element_type=F32)
                dfq_col[rows, :] += jnp.sum(ds, axis=1, keepdims=True)
                dfk = dfk + jnp.sum(ds, axis=0, keepdims=True)
                return dk, dv, dfk

            init = (jnp.zeros((T, dh), F32), jnp.zeros((T, dh), F32), jnp.zeros((1, T), F32))
            carry = q_block(kj, init, True)
            dk, dv, dfk = lax.fori_loop(kj + 1, nq, lambda qi, cr: q_block(qi, cr, False), carry)
            dqkv_ref[1, cols, :] = (dk * scale).astype(BF16)
            dqkv_ref[2, cols, :] = dv.astype(BF16)
            dfk_ref[kj, pl.ds(h, 1), :] = dfk
            return 0

        lax.fori_loop(0, nq, kv_block, 0)

        def finish(qi, _):
            rows = pl.ds(pl.multiple_of(qi * T, T), T)
            dqkv_ref[0, rows, :] = (dq_acc[rows, :] * scale).astype(BF16)
            dfq_ref[rows, :] = jnp.where(lane, dfq_col[rows, :], dfq_ref[rows, :])
            return 0

        lax.fori_loop(0, nq, finish, 0)

    head = lambda part: pl.BlockSpec((S, dh), lambda h: (0, part * heads + h))
    own = pl.BlockSpec((S, dh), lambda h: (0, h))
    full = pl.BlockSpec((S, LANES), lambda h: (0, 0))
    krow = pl.BlockSpec((nq, hp, T), lambda h: (0, 0, 0))
    return _call(
        body, deps, name=name, grid=(heads,),
        in_specs=[head(0), head(1), head(2), own, own, full, krow, full],
        out_specs=[pl.BlockSpec((3, S, dh), lambda h: (0, 0, h)), full, krow],
        out_shape=[jax.ShapeDtypeStruct((3, S, D), BF16), jax.ShapeDtypeStruct((S, LANES), F32),
                   jax.ShapeDtypeStruct((nq, hp, T), F32)],
        scratch_shapes=[pltpu.VMEM((S, dh), F32)] + [pltpu.VMEM((S, 1), F32)] * 4,
        compiler_params=_params(("arbitrary",)),
    )(qkv, qkv, qkv, o, do, fq, fk, lse)


def _shift_down(v, n):
    rows = lax.broadcasted_iota(jnp.int32, v.shape, 0)
    return jnp.where(rows >= n, pltpu.roll(v, n, axis=0), 0.0)


def _shift_up(v, n):
    S = v.shape[0]
    rows = lax.broadcasted_iota(jnp.int32, v.shape, 0)
    return jnp.where(rows < S - n, pltpu.roll(v, S - n, axis=0), 0.0)


def _conv_fwd(proj, conv_w, *, name, cb=LANES, deps=()):
    S, D3 = proj.shape
    D = D3 // 3
    nb = D // cb

    def body(bg_ref, cg_ref, u_ref, w_ref, z_ref):
        uc = cg_ref[...].astype(F32) * u_ref[...].astype(F32)
        w = w_ref[...]
        y = w[2:3, :] * uc + w[1:2, :] * _shift_down(uc, 1) + w[0:1, :] * _shift_down(uc, 2)
        z_ref[...] = (bg_ref[...].astype(F32) * y).astype(BF16)

    part = lambda g: pl.BlockSpec((S, cb), lambda j: (0, g * nb + j))
    return _call(
        body, deps, name=name, grid=(nb,),
        in_specs=[part(0), part(1), part(2), pl.BlockSpec((3, cb), lambda j: (0, j))],
        out_specs=pl.BlockSpec((S, cb), lambda j: (0, j)),
        out_shape=jax.ShapeDtypeStruct((S, D), BF16), compiler_params=_params(("parallel",)),
    )(proj, proj, proj, conv_w)


def _conv_bwd(proj, conv_w, dz, *, name, cb=LANES, deps=()):
    S, D3 = proj.shape
    D = D3 // 3
    nb = D // cb

    def body(bg_ref, cg_ref, u_ref, w_ref, dz_ref, dp_ref, dw_ref):
        cg, u = cg_ref[...].astype(F32), u_ref[...].astype(F32)
        uc = cg * u
        w = w_ref[...]
        uc1, uc2 = _shift_down(uc, 1), _shift_down(uc, 2)
        y = w[2:3, :] * uc + w[1:2, :] * uc1 + w[0:1, :] * uc2
        dz = dz_ref[...].astype(F32)
        dp_ref[0] = (dz * y).astype(BF16)
        dy = dz * bg_ref[...].astype(F32)
        duc = w[2:3, :] * dy + w[1:2, :] * _shift_up(dy, 1) + w[0:1, :] * _shift_up(dy, 2)
        dp_ref[1] = (duc * u).astype(BF16)
        dp_ref[2] = (duc * cg).astype(BF16)
        dw_ref[...] = jnp.zeros_like(dw_ref)
        dw_ref[0:1, :] = jnp.sum(dy * uc2, axis=0, keepdims=True)
        dw_ref[1:2, :] = jnp.sum(dy * uc1, axis=0, keepdims=True)
        dw_ref[2:3, :] = jnp.sum(dy * uc, axis=0, keepdims=True)

    part = lambda g: pl.BlockSpec((S, cb), lambda j: (0, g * nb + j))
    return _call(
        body, deps, name=name, grid=(nb,),
        in_specs=[part(0), part(1), part(2), pl.BlockSpec((3, cb), lambda j: (0, j)), pl.BlockSpec((S, cb), lambda j: (0, j))],
        out_specs=[pl.BlockSpec((3, S, cb), lambda j: (0, 0, j)), pl.BlockSpec((8, cb), lambda j: (0, j))],
        out_shape=[jax.ShapeDtypeStruct((3, S, D), BF16), jax.ShapeDtypeStruct((8, D), F32)],
        compiler_params=_params(("parallel",)),
    )(proj, proj, proj, conv_w, dz)


def _ada_fwd(c_all, ada_w, *, name, deps=()):
    L, D, Ns = ada_w.shape
    tn = _tile(Ns, 512)

    def body(c_ref, w_ref, o_ref, act_ref):
        cv = c_ref[...]
        act = cv * (1.0 / (1.0 + jnp.exp(-cv)))
        act_ref[...] = act
        o_ref[...] = jnp.dot(act.astype(BF16), w_ref[...].astype(BF16), preferred_element_type=F32)

    return _call(
        body, deps, name=name, grid=(L, Ns // tn),
        in_specs=[pl.BlockSpec((N_DEV, D), lambda l, j: (0, 0)), pl.BlockSpec((None, D, tn), lambda l, j: (l, 0, j))],
        out_specs=[pl.BlockSpec((None, N_DEV, tn), lambda l, j: (l, 0, j)), pl.BlockSpec((N_DEV, D), lambda l, j: (0, 0))],
        out_shape=[jax.ShapeDtypeStruct((L, N_DEV, Ns), F32), jax.ShapeDtypeStruct((N_DEV, D), F32)],
        compiler_params=_params(("arbitrary", "arbitrary")),
    )(c_all, ada_w)


def _select_mod(gathered, *, name, deps=()):
    _, LB, Ns = gathered.shape
    L = LB // N_DEV

    def body(g_ref, o_ref):
        x, y, c = _me()
        b = 4 * x + 2 * y + c
        for j in range(N_CHIPS):
            for l in range(L):
                o_ref[j, pl.ds(l, 1), :] = g_ref[2 * j + c, pl.ds(l * N_DEV + b, 1), :]

    return _call(
        body, deps, name=name, out_shape=jax.ShapeDtypeStruct((N_CHIPS, L, Ns), F32),
        in_specs=[pl.BlockSpec(memory_space=pltpu.VMEM)], out_specs=pl.BlockSpec(memory_space=pltpu.VMEM),
        compiler_params=_params(),
    )(gathered)


def _adamw_math(w, g, m, v):
    m = ADAM_B1 * m + (1.0 - ADAM_B1) * g
    v = ADAM_B2 * v + (1.0 - ADAM_B2) * jnp.square(g)
    m_hat = m / (1.0 - ADAM_B1 ** ADAM_STEP)
    v_hat = v / (1.0 - ADAM_B2 ** ADAM_STEP)
    delta = -ADAM_LR * (m_hat / (jnp.sqrt(v_hat) + ADAM_EPS) + ADAM_WD * w)
    return delta, m, v


def _adamw_shards(w, m, v, groups, chip, *, name):
    L, R, C = w.shape
    if R % 16 == 0:
        tr, tc = _tile(R, 128, 16), C
    else:
        tr, tc = R, _tile(C, 256)
    nr, nc = R // tr, C // tc

    def body(chip_ref, w_ref, m_ref, v_ref, *rest):
        srcs, (g_ref, d_ref, mo_ref, vo_ref) = rest[:2 * N_CHIPS * L], rest[2 * N_CHIPS * L:]
        for l in range(L):
            @pl.when(pl.program_id(0) == l)
            def _():
                s = srcs[2 * N_CHIPS * l:2 * N_CHIPS * (l + 1)]
                mine, other = s[0][...].astype(F32), s[N_CHIPS][...].astype(F32)
                for k in range(1, N_CHIPS):
                    mine = mine + s[k][...].astype(F32)
                    other = other + s[N_CHIPS + k][...].astype(F32)
                g = mine + other
                delta, mn, vn = _adamw_math(w_ref[...], g, m_ref[...], v_ref[...])
                g_ref[...] = g
                d_ref[...] = delta
                mo_ref[...] = mn
                vo_ref[...] = vn

    tile = pl.BlockSpec((None, tr, tc), lambda l, i, j, chip_ref: (l, i, j))

    def block(layer, k):
        def index(l, i, j, chip_ref):
            idle_i, idle_j = jnp.where(l < layer, 0, nr - 1), jnp.where(l < layer, 0, nc - 1)
            return (jnp.bitwise_xor(chip_ref[0], k), jnp.where(l == layer, i, idle_i), jnp.where(l == layer, j, idle_j))
        return pl.BlockSpec((None, tr, tc), index)

    in_specs, args = [tile] * 3, [w, m, v]
    for layer, (parts, land, sib) in enumerate(groups):
        in_specs += [block(layer, k) for k in range(N_CHIPS)] * 2
        args += [parts, land, land, land, sib, sib, sib, sib]
    return pl.pallas_call(
        body, name=name,
        grid_spec=pltpu.PrefetchScalarGridSpec(num_scalar_prefetch=1, grid=(L, nr, nc), in_specs=in_specs, out_specs=[tile] * 4),
        out_shape=[jax.ShapeDtypeStruct((L, R, C), F32)] * 4, compiler_params=_params(("arbitrary", "arbitrary", "arbitrary")),
    )(chip, *args)


def _adamw_ada(w, m, v, act_t, dmod, *, name, tr=256, deps=()):
    L, D, Ns = w.shape
    tr = _tile(D, tr, 8)

    def body(w_ref, m_ref, v_ref, a_ref, d_ref, g_ref, dl_ref, mo_ref, vo_ref):
        x, y, _ = _me()
        g = jnp.dot(a_ref[...], d_ref[2 * x + y], preferred_element_type=F32, precision=lax.Precision.HIGHEST)
        delta, mn, vn = _adamw_math(w_ref[...], g, m_ref[...], v_ref[...])
        g_ref[...] = g
        dl_ref[...] = delta
        mo_ref[...] = mn
        vo_ref[...] = vn

    tile = pl.BlockSpec((None, tr, Ns), lambda l, i: (l, i, 0))
    return _call(
        body, deps, name=name, grid=(L, D // tr),
        in_specs=[tile] * 3 + [pl.BlockSpec((tr, N_DEV), lambda l, i: (i, 0)),
                               pl.BlockSpec((N_CHIPS, None, N_DEV, Ns), lambda l, i: (0, l, 0, 0))],
        out_specs=[tile] * 4, out_shape=[jax.ShapeDtypeStruct((L, D, Ns), F32)] * 4,
        compiler_params=_params(("parallel", "parallel")),
    )(w, m, v, act_t, dmod)


def _adamw_small(w, m, v, gathered, *, rows, name, deps=()):
    n, D = w.shape

    def body(w_ref, m_ref, v_ref, s_ref, g_ref, d_ref, mo_ref, vo_ref):
        for r, src in enumerate(rows):
            g = s_ref[0, src:src + 1, :]
            for d in range(1, N_DEV):
                g = g + s_ref[d, src:src + 1, :]
            g_ref[r:r + 1, :] = g
        g = g_ref[...]
        delta, mn, vn = _adamw_math(w_ref[...], g, m_ref[...], v_ref[...])
        d_ref[...] = delta
        mo_ref[...] = mn
        vo_ref[...] = vn

    vm = pl.BlockSpec(memory_space=pltpu.VMEM)
    return _call(
        body, deps, name=name, in_specs=[vm] * 4, out_specs=[vm] * 4,
        out_shape=[jax.ShapeDtypeStruct((n, D), F32)] * 4, compiler_params=_params(),
    )(w, m, v, gathered)


def _adamw_conv_w(w, m, v, gathered4, *, name, deps=()):
    Cs = w.shape[1]

    def body(w_ref, m_ref, v_ref, s_ref, g_ref, d_ref, mo_ref, vo_ref):
        x, y, _ = _me()
        j = 2 * x + y
        g = s_ref[j, 0]
        for d in range(1, N_DEV):
            g = g + s_ref[j, d]
        delta, mn, vn = _adamw_math(w_ref[...], g, m_ref[...], v_ref[...])
        g_ref[...] = g
        d_ref[...] = delta
        mo_ref[...] = mn
        vo_ref[...] = vn

    vm = pl.BlockSpec(memory_space=pltpu.VMEM)
    return _call(
        body, deps, name=name, in_specs=[vm] * 4, out_specs=[vm] * 4,
        out_shape=[jax.ShapeDtypeStruct((8, Cs), F32)] * 4, compiler_params=_params(),
    )(w, m, v, gathered4)


def _loss_sum(gathered, *, row, name, deps=()):
    _, _, D = gathered.shape

    def body(s_ref, o_ref):
        t = s_ref[0, row:row + 1, :]
        for d in range(1, N_DEV):
            t = t + s_ref[d, row:row + 1, :]
        o_ref[...] = jnp.broadcast_to(t, (8, D))

    vm = pl.BlockSpec(memory_space=pltpu.VMEM)
    return pl.pallas_call(body, name=name, in_specs=[vm], out_specs=vm, out_shape=jax.ShapeDtypeStruct((8, D), F32),
                          compiler_params=_params())(gathered)


def _pad_rows(a, n):
    return jnp.pad(a, ((0, n - a.shape[0]), (0, 0)))


def kernel(x, c, ada_w, ada_b, norm_mix, norm_mlp, fox_w_in, fox_b_f, fox_w_out, conv_w_in, conv_w, conv_w_out, mlp_w_up, mlp_w_down, final_norm, loss_target, m_ada_w, m_ada_b, m_norm_mix, m_norm_mlp, m_fox_w_in, m_fox_b_f, m_fox_w_out, m_conv_w_in, m_conv_w, m_conv_w_out, m_mlp_w_up, m_mlp_w_down, m_final_norm, v_ada_w, v_ada_b, v_norm_mix, v_norm_mlp, v_fox_w_in, v_fox_b_f, v_fox_w_out, v_conv_w_in, v_conv_w, v_conv_w_out, v_mlp_w_up, v_mlp_w_down, v_final_norm):
    S, D = x.shape[1], x.shape[2]
    H = fox_b_f.shape[-1]
    L = ada_w.shape[0]
    NM = ada_b.shape[1] // D
    Ns_ada = ada_w.shape[2]
    Cs_fox = fox_w_in.shape[2]
    Cs_conv = conv_w.shape[2]
    x0 = x[0]
    target = loss_target[0]

    chip = (2 * lax.axis_index("x") + lax.axis_index("y")).astype(jnp.int32).reshape(1)

    fin_t = jnp.transpose(fox_w_in[0])
    shards = dict(fin=fin_t, fout=fox_w_out[0], up0=mlp_w_up[0], dn0=mlp_w_down[0], cin=conv_w_in[0], cout=conv_w_out[0],
                  up1=mlp_w_up[1], dn1=mlp_w_down[1])
    gathers = {}

    def start_gather(key, dep=None):
        land = _place_cast(shards[key], chip, name="place_" + key)
        gathers[key] = _split_start("gather", [(land,)], name="gather_start_" + key, dep=dep)
        return gathers[key][3]

    def gathered(key, after):
        return _split_wait("gather", gathers[key], after, name="gather_wait_" + key)[0][0]

    tok = start_gather("fin")
    tok = start_gather("fout", tok)
    tok = start_gather("up0", tok)

    c_all = _allgather8(_pad_rows(c, 8), name="gather_c", deps=(tok,))[:, 0, :]
    mod_part, c_act = _ada_fwd(c_all, ada_w, name="ada_fwd")
    mod_all = _allgather8(mod_part.reshape(L * N_DEV, Ns_ada), name="gather_mod")
    mod = _select_mod(mod_all, name="select_mod")
    mod = jnp.transpose(mod, (1, 0, 2)).reshape(L, NM, 1, D) + ada_b.reshape(L, NM, 1, D)
    conv_w_all = _allgather8(_pad_rows(conv_w[0], 8), name="gather_conv_w")
    conv_w_full = jnp.transpose(conv_w_all[0::2, :3, :], (1, 0, 2)).reshape(3, D)

    def vec(a):
        return a.reshape(1, D)

    h0 = _norm_fwd(x0, vec(norm_mix[0]), mod[0, 1], mod[0, 0], name="norm_mix0")
    w_fin_t = gathered("fin", [h0]).reshape(N_CHIPS * Cs_fox, D)
    w_f_t = _pad_rows(w_fin_t[3 * D:], LANES)
    qkv = _mm_nt(h0, w_fin_t, n=3 * D, name="fox_in", out_dtype=BF16)
    tok = start_gather("dn0", qkv)
    b_f = jnp.pad(fox_b_f, ((0, 0), (0, LANES - H)))
    z_f, F_col = _fgate_fwd(h0, w_f_t, b_f, name="fgate_fwd", deps=(tok,))
    hp = max(8, H)
    at = _tile(S, 256, 16)
    F_row = jnp.transpose(_pad_rows(jnp.transpose(F_col[:, :H]), hp).reshape(hp, S // at, at), (1, 0, 2))
    o, lse = _attn_fwd(qkv, F_col, F_row, heads=H, name="attn_fwd", T=at)
    w_fout = gathered("fout", [o]).reshape(D, D)
    x1, mix0 = _mm_nn(o, w_fout, name="fox_out", epilogue="resid", res=x0, gate=mod[0, 2], tm=512)
    tok = start_gather("cin", x1)
    h1 = _norm_fwd(x1, vec(norm_mlp[0]), mod[0, 4], mod[0, 3], name="norm_mlp0", deps=(tok,))
    w_up0 = gathered("up0", [h1])
    u0, a0 = _mm_nn(h1, w_up0, name="mlp_up0", epilogue="relu2")
    tok = start_gather("cout", u0)
    w_dn0 = gathered("dn0", [a0, tok]).reshape(-1, D)
    x2, y0 = _mm_nn(a0, w_dn0, name="mlp_down0", epilogue="resid", res=x1, gate=mod[0, 5], tm=512)
    tok = start_gather("up1", x2)
    h2 = _norm_fwd(x2, vec(norm_mix[1]), mod[1, 1], mod[1, 0], name="norm_mix1", deps=(tok,))
    w_cin = jnp.transpose(gathered("cin", [h2]), (1, 0, 2)).reshape(D, 3 * D)
    proj = _mm_nn(h2, w_cin, name="conv_in")
    tok = start_gather("dn1", proj)
    zc = _conv_fwd(proj, conv_w_full, name="conv_fwd", deps=(tok,))
    w_cout = gathered("cout", [zc]).reshape(D, D)
    x3, mix1 = _mm_nn(zc, w_cout, name="conv_out", epilogue="resid", res=x2, gate=mod[1, 2], tm=512)
    h3 = _norm_fwd(x3, vec(norm_mlp[1]), mod[1, 4], mod[1, 3], name="norm_mlp1")
    w_up1 = gathered("up1", [h3])
    u1, a1 = _mm_nn(h3, w_up1, name="mlp_up1", epilogue="relu2")
    w_dn1 = gathered("dn1", [a1]).reshape(-1, D)
    x4, y1 = _mm_nn(a1, w_dn1, name="mlp_down1", epilogue="resid", res=x3, gate=mod[1, 5], tm=512)
    w_up, w_dn = [w_up0, w_up1], [w_dn0, w_dn1]

    dx4, dy1, sums_f = _loss_bwd(x4, target, vec(final_norm), mod[1, 5], name="loss_bwd")
    du1 = _mm_nt(dy1, w_dn[1], name="mlp_down1_dx", epilogue="drelu2", extra=u1, out_dtype=BF16)
    def start_scatter(tag, parts_list):
        groups = [(p, lax.empty(p.shape, p.dtype)) for p in parts_list]
        return _split_start("scatter", groups, name="scatter_start_" + tag)

    def start_sibling(tag, scatter, after):
        landed = _split_wait("scatter", scatter, after, name="scatter_wait_" + tag)
        groups = [(p, ld, lax.empty(p.shape, p.dtype)) for p, ld in landed]
        return _split_start("sibling", groups, name="sibling_start_" + tag)

    gw_dn1 = _mm_tn(a1, dy1, name="mlp_down1_dw")
    gw_up1 = _mm_tn(h3, du1, name="mlp_up1_dw", out_parts=N_CHIPS)
    sc1 = start_scatter("mlp1", [gw_dn1.reshape(N_CHIPS, -1, D), gw_up1])
    dh3 = _mm_nt(du1, w_up[1], name="mlp_up1_dx", deps=(sc1[3],))
    dx3, dmix1, sums_mlp1 = _norm_bwd(x3, dh3, dx4, y1, vec(norm_mlp[1]), mod[1, 4], mod[1, 2], name="norm_mlp1_bwd")
    dzc = _mm_nt(dmix1, w_cout, name="conv_out_dx", out_dtype=BF16)
    gw_cout = _mm_tn(zc, dmix1, name="conv_out_dw")
    dproj, dconv_w = _conv_bwd(proj, conv_w_full, dzc, name="conv_bwd")
    gw_cin = _mm_tn(h2, dproj, name="conv_in_dw", out_parts=N_CHIPS, tn=512)
    sc2 = start_scatter("conv", [gw_cout.reshape(N_CHIPS, -1, D), gw_cin])
    sb1 = start_sibling("mlp1", sc1, [sc2[3]])
    dh2 = _mm_nt(dproj, w_cin, name="conv_in_dx", deps=(sb1[3],))
    dx2, dy0, sums_mix1 = _norm_bwd(x2, dh2, dx3, mix1, vec(norm_mix[1]), mod[1, 1], mod[0, 5], name="norm_mix1_bwd")
    du0 = _mm_nt(dy0, w_dn[0], name="mlp_down0_dx", epilogue="drelu2", extra=u0, out_dtype=BF16)
    gw_dn0 = _mm_tn(a0, dy0, name="mlp_down0_dw")
    gw_up0 = _mm_tn(h1, du0, name="mlp_up0_dw", out_parts=N_CHIPS)
    sc3 = start_scatter("mlp0", [gw_dn0.reshape(N_CHIPS, -1, D), gw_up0])
    sb2 = start_sibling("conv", sc2, [sc3[3]])
    dh1 = _mm_nt(du0, w_up[0], name="mlp_up0_dx", deps=(sb2[3],))
    dx1, dmix0, sums_mlp0 = _norm_bwd(x1, dh1, dx2, y0, vec(norm_mlp[0]), mod[0, 4], mod[0, 2], name="norm_mlp0_bwd")
    do = _mm_nt(dmix0, w_fout, name="fox_out_dx", out_dtype=BF16)
    gw_fout = _mm_tn(o, dmix0, name="fox_out_dw")
    dqkv, dfq, dfk = _attn_bwd(qkv, o, do, F_col, F_row, lse, heads=H, name="attn_bwd", T=at)
    dfk_col = jnp.pad(jnp.transpose(jnp.transpose(dfk, (1, 0, 2)).reshape(hp, S)[:H]), ((0, 0), (0, LANES - H)))
    dz_f, sums_bf = _fgate_bwd(dfq, dfk_col, z_f, name="fgate_bwd")
    gw_qkv_t = _mm_tn(dqkv, h0, name="fox_in_dw")
    gw_f_t = _mm_tn(dz_f, h0, name="fox_gate_dw")
    gw_fin_t = jnp.concatenate([gw_qkv_t, gw_f_t[:H]], axis=0).reshape(N_CHIPS, Cs_fox, D)
    sc4 = start_scatter("fox", [gw_fout.reshape(N_CHIPS, -1, D), gw_fin_t])
    sb3 = start_sibling("mlp0", sc3, [sc4[3]])
    dh0_f = _mm_nn(dz_f, w_f_t, name="fox_gate_dx", out_dtype=F32, deps=(sb3[3],))
    dh0 = _mm_nn(dqkv, w_fin_t, name="fox_in_dx", epilogue="add", res=dh0_f, out_dtype=F32)
    grad_x, _, sums_mix0 = _norm_bwd(x0, dh0, dx1, mix0, vec(norm_mix[0]), mod[0, 1], None, name="norm_mix0_bwd")

    outs = {}

    def put(name_, res, shape):
        for kind, r in zip(("grad", "delta", "new_m", "new_v"), res):
            outs[kind + "_" + name_] = r.reshape(shape)

    dmod_rows = []
    for sm, sl in ((sums_mix0, sums_mlp0), (sums_mix1, sums_mlp1)):
        dmod_rows += [sm[0:1], sm[1:2], sm[3:4], sl[0:1], sl[1:2], sl[3:4]]
    bf_row = jnp.pad(sums_bf[0:1], ((0, 0), (0, D - LANES)))
    small = jnp.concatenate([sums_mix0[2:3], sums_mix1[2:3], sums_mlp0[2:3], sums_mlp1[2:3], sums_f[0:1], sums_f[1:2], bf_row,
                             jnp.zeros((1, D), F32)] + dmod_rows + [dconv_w[0:3]], axis=0)
    small_all = _allgather8(_pad_rows(small, -(-small.shape[0] // 8) * 8), name="gather_small")
    loss = _loss_sum(small_all, row=5, name="loss_sum")[0, 0]

    def rows_of(a_mix, a_mlp, a_fin, a_bf, a_ada):
        return jnp.concatenate([a_mix, a_mlp, a_fin.reshape(1, D), jnp.pad(a_bf, ((0, 0), (0, D - H))),
                                a_ada.reshape(L * NM, D)], axis=0)
    n_small = 2 * L + 2 + L * NM
    rw = -(-n_small // 8) * 8
    w_s = _pad_rows(rows_of(norm_mix, norm_mlp, final_norm, fox_b_f, ada_b), rw)
    m_s = _pad_rows(rows_of(m_norm_mix, m_norm_mlp, m_final_norm, m_fox_b_f, m_ada_b), rw)
    v_s = _pad_rows(rows_of(v_norm_mix, v_norm_mlp, v_final_norm, v_fox_b_f, v_ada_b), rw)
    src_rows = [0, 1, 2, 3, 4, 6] + [8 + r for r in range(L * NM)] + [7] * (rw - n_small)
    res_s = _adamw_small(w_s, m_s, v_s, small_all, rows=tuple(src_rows), name="adamw_small")
    for kind, r in zip(("grad", "delta", "new_m", "new_v"), res_s):
        outs[kind + "_norm_mix"] = r[0:L]
        outs[kind + "_norm_mlp"] = r[L:2 * L]
        outs[kind + "_final_norm"] = r[2 * L]
        outs[kind + "_fox_b_f"] = r[2 * L + 1:2 * L + 2, :H]
        outs[kind + "_ada_b"] = r[2 * L + 2:n_small].reshape(L, NM * D)

    dmod_all = small_all[:, 8:8 + L * NM, :].reshape(N_DEV, L, N_CHIPS, Ns_ada)
    dmod4 = jnp.transpose(dmod_all, (2, 1, 0, 3))
    act_t = jnp.transpose(c_act)
    put("ada_w", _adamw_ada(ada_w, m_ada_w, v_ada_w, act_t, dmod4, name="adamw_ada"), ada_w.shape)

    r0 = 8 + L * NM
    dconv_all = jnp.pad(small_all[:, r0:r0 + 3, :], ((0, 0), (0, 5), (0, 0)))
    dconv4 = jnp.transpose(dconv_all.reshape(N_DEV, 8, N_CHIPS, Cs_conv), (2, 0, 1, 3))
    res_c = _adamw_conv_w(_pad_rows(conv_w[0], 8), _pad_rows(m_conv_w[0], 8), _pad_rows(v_conv_w[0], 8), dconv4,
                          name="adamw_conv_w")
    for kind, r in zip(("grad", "delta", "new_m", "new_v"), res_c):
        outs[kind + "_conv_w"] = r[:3].reshape(conv_w.shape)

    def shards_update(tag, w_, m_, v_, groups):
        return _adamw_shards(w_, m_, v_, groups, chip, name="adamw_" + tag)

    g_conv = _split_wait("sibling", sb2, [res_c[0]], name="sibling_wait_conv")
    put("conv_w_out", shards_update("conv_out", conv_w_out, m_conv_w_out, v_conv_w_out, g_conv[0:1]), conv_w_out.shape)
    r_cin = shards_update("conv_in", conv_w_in, m_conv_w_in, v_conv_w_in, g_conv[1:2])
    put("conv_w_in", r_cin, conv_w_in.shape)
    g_mlp1 = _split_wait("sibling", sb1, [r_cin[0]], name="sibling_wait_mlp1")
    g_mlp0 = _split_wait("sibling", sb3, [r_cin[0]], name="sibling_wait_mlp0")
    put("mlp_w_down", shards_update("mlp_down", mlp_w_down, m_mlp_w_down, v_mlp_w_down, [g_mlp0[0], g_mlp1[0]]), mlp_w_down.shape)
    r_up = shards_update("mlp_up", mlp_w_up, m_mlp_w_up, v_mlp_w_up, [g_mlp0[1], g_mlp1[1]])
    put("mlp_w_up", r_up, mlp_w_up.shape)
    sb4 = start_sibling("fox", sc4, [r_up[0]])
    g_fox = _split_wait("sibling", sb4, [sb4[3]], name="sibling_wait_fox")
    put("fox_w_out", shards_update("fox_out", fox_w_out, m_fox_w_out, v_fox_w_out, g_fox[0:1]), fox_w_out.shape)
    t3 = lambda a: jnp.transpose(a, (0, 2, 1))
    for kind, r in zip(("grad", "delta", "new_m", "new_v"),
                       shards_update("fox_in", t3(fox_w_in), t3(m_fox_w_in), t3(v_fox_w_in), g_fox[1:2])):
        outs[kind + "_fox_w_in"] = t3(r)

    names = ["ada_w", "ada_b", "norm_mix", "norm_mlp", "fox_w_in", "fox_b_f", "fox_w_out", "conv_w_in", "conv_w", "conv_w_out",
             "mlp_w_up", "mlp_w_down", "final_norm"]
    return (loss, grad_x[None], *[outs["grad_" + n] for n in names], *[outs["delta_" + n] for n in names],
            *[outs["new_m_" + n] for n in names], *[outs["new_v_" + n] for n in names])
```

```python
import functools

import jax
import jax.numpy as jnp
from jax import lax
from jax.experimental import pallas as pl
from jax.experimental.pallas import tpu as pltpu

F32 = jnp.float32
BF16 = jnp.bfloat16
MESH = pl.DeviceIdType.MESH
ANY = pl.BlockSpec(memory_space=pl.ANY)
HBM = pl.BlockSpec(memory_space=pltpu.HBM)
SEM = pl.BlockSpec(memory_space=pltpu.SEMAPHORE)
EFFECT = pltpu.SideEffectType.DATAFLOW_SIDE_EFFECTING

RMS_EPS = 1e-6
ADAM_LR = 0.001
ADAM_B1 = 0.9
ADAM_B2 = 0.999
ADAM_EPS = 1e-08
ADAM_WD = 0.01
ADAM_STEP = 10
N_CHIPS = 4
N_DEV = 8
LANES = 128
VMEM_LIMIT = 56 * 1024 * 1024
NEG = -1e30


def _params(sems=None, vmem=VMEM_LIMIT):
    return pltpu.CompilerParams(dimension_semantics=sems, vmem_limit_bytes=vmem)


def _tile(n, pref, unit=LANES):
    if n <= pref:
        return n
    t = (pref // unit) * unit
    while n % t:
        t -= unit
    return t


def _me():
    return lax.axis_index("x"), lax.axis_index("y"), lax.axis_index("c")


def _call(body, deps, **kw):
    nd = len(deps)

    def wrapped(*refs):
        body(*refs[nd:])

    kw["in_specs"] = [ANY] * nd + list(kw["in_specs"])
    fn = pl.pallas_call(wrapped, **kw)
    return lambda *args: fn(*deps, *args)


def _allgather8(v, *, name, deps=()):
    R, C = v.shape

    def body(v_ref, out_ref, send_sems, recv_sems):
        x, y, c = _me()
        me = 4 * x + 2 * y + c
        out_ref[me] = v_ref[...]
        copies = []
        for k in range(1, N_DEV):
            px, py, pc = (x + (k >> 2)) % 2, (y + ((k >> 1) & 1)) % 2, (c + (k & 1)) % 2
            copies.append(pltpu.make_async_remote_copy(
                src_ref=v_ref, dst_ref=out_ref.at[me], send_sem=send_sems.at[k - 1], recv_sem=recv_sems.at[k - 1],
                device_id=(px, py, pc), device_id_type=MESH))
        for cp in copies:
            cp.start()
        for k in range(1, N_DEV):
            px, py, pc = (x + (k >> 2)) % 2, (y + ((k >> 1) & 1)) % 2, (c + (k & 1)) % 2
            peer = 4 * px + 2 * py + pc
            pltpu.make_async_remote_copy(
                src_ref=v_ref, dst_ref=out_ref.at[peer], send_sem=send_sems.at[k - 1], recv_sem=recv_sems.at[k - 1],
                device_id=(px, py, pc), device_id_type=MESH).wait_recv()
        for cp in copies:
            cp.wait_send()

    return _call(
        body, deps, name=name,
        out_shape=jax.ShapeDtypeStruct((N_DEV, R, C), v.dtype),
        in_specs=[pl.BlockSpec(memory_space=pltpu.VMEM)],
        out_specs=pl.BlockSpec(memory_space=pltpu.VMEM),
        scratch_shapes=[pltpu.SemaphoreType.DMA((N_DEV - 1,)), pltpu.SemaphoreType.DMA((N_DEV - 1,))],
        compiler_params=_params(),
    )(v)


def _chip_peers(x, y):
    return [((x + (k >> 1)) % 2, (y + (k & 1)) % 2) for k in range(1, N_CHIPS)]


def _slot(x, y, k):
    return 2 * ((x + (k >> 1)) % 2) + (y + (k & 1)) % 2


def _split_copies(kind, groups, send_sems, recv_sems):
    x, y, c = _me()
    j = 2 * x + y
    copies = []
    for a, g in enumerate(groups):
        if kind == "sibling":
            parts, land, sib = g
            for k in range(N_CHIPS):
                s = _slot(x, y, k)
                copies.append(pltpu.make_async_remote_copy(
                    src_ref=(parts if k == 0 else land).at[s], dst_ref=sib.at[s], send_sem=send_sems.at[N_CHIPS * a + k],
                    recv_sem=recv_sems.at[N_CHIPS * a + k], device_id=(x, y, 1 - c), device_id_type=MESH))
            continue
        land = g[-1]
        for k, (px, py) in enumerate(_chip_peers(x, y)):
            if kind == "gather1":
                src, dst, to = land.at[j, c], land.at[j, c], (px, py, c)
            elif kind == "gather2":
                src, dst, to = land.at[2 * px + py, c], land.at[2 * px + py, c], (x, y, 1 - c)
            else:
                src, dst, to = g[0].at[2 * px + py], land.at[j], (px, py, c)
            copies.append(pltpu.make_async_remote_copy(
                src_ref=src, dst_ref=dst, send_sem=send_sems.at[3 * a + k], recv_sem=recv_sems.at[3 * a + k],
                device_id=to, device_id_type=MESH))
    return copies


def _split_start(kind, groups, *, name, dep=None):
    flat = [a for g in groups for a in g]
    nf, per = len(flat), len(groups[0])
    ncp = len(groups) * (N_CHIPS if kind == "sibling" else 3)
    nd = 0 if dep is None else 1

    def body(*refs):
        ins = refs[nd:nd + nf]
        send_sems, recv_sems, token = refs[nd + nf], refs[nd + nf + 1], refs[-1]
        for cp in _split_copies(kind, [ins[i:i + per] for i in range(0, nf, per)], send_sems, recv_sems):
            cp.start()
        token[...] = jnp.zeros_like(token)

    outs = pl.pallas_call(
        body, name=name,
        out_shape=(pltpu.SemaphoreType.DMA((ncp,)), pltpu.SemaphoreType.DMA((ncp,)), *[pltpu.HBM(a.shape, a.dtype) for a in flat],
                   jax.ShapeDtypeStruct((8, LANES), F32)),
        in_specs=[ANY] * nd + [HBM] * nf,
        out_specs=(SEM, SEM, *[HBM] * nf, pl.BlockSpec(memory_space=pltpu.VMEM)),
        input_output_aliases={nd + i: 2 + i for i in range(nf)},
        compiler_params=pltpu.CompilerParams(has_side_effects=EFFECT),
    )(*([dep] if nd else []), *[pltpu.with_memory_space_constraint(a, pltpu.HBM) for a in flat])
    thru = list(outs[2:2 + nf])
    return outs[0], outs[1], [tuple(thru[i:i + per]) for i in range(0, nf, per)], outs[-1]


def _split_wait(kind, started, after, *, name):
    send_sems, recv_sems, groups, _ = started
    flat = [a for g in groups for a in g]
    nf, per = len(flat), len(groups[0])

    def body(*refs):
        ins = refs[:nf]
        for cp in _split_copies(kind, [ins[i:i + per] for i in range(0, nf, per)], refs[nf], refs[nf + 1]):
            cp.wait_send()
            cp.wait_recv()

    outs = pl.pallas_call(
        body, name=name,
        out_shape=tuple(pltpu.HBM(a.shape, a.dtype) for a in flat),
        in_specs=[HBM] * nf + [SEM, SEM] + [ANY] * len(after), out_specs=tuple([HBM] * nf),
        input_output_aliases={i: i for i in range(nf)},
        compiler_params=pltpu.CompilerParams(has_side_effects=EFFECT),
    )(*flat, send_sems, recv_sems, *after)
    outs = list(outs)
    return [tuple(outs[i:i + per]) for i in range(0, nf, per)]


def _place_cast(shard, chip, *, halves, name):
    R, C = shard.shape
    if halves == "rows":
        hr, hc = R // 2, C
    else:
        hr, hc = R, C // 2
    if hr % 16 == 0:
        tr, tc = _tile(hr, 512, 16), hc
    else:
        tr, tc = hr, _tile(hc, 256)
    nr, nc = hr // tr, hc // tc

    def body(chip_ref, x_ref, o_ref):
        o_ref[...] = x_ref[...].astype(BF16)

    if halves == "rows":
        o_map = lambda i, j, chip_ref: (chip_ref[0], i // nr, i % nr, j)
    else:
        o_map = lambda i, j, chip_ref: (chip_ref[0], j // nc, i, j % nc)
    return pl.pallas_call(
        body, name=name,
        grid_spec=pltpu.PrefetchScalarGridSpec(
            num_scalar_prefetch=1, grid=(R // tr, C // tc),
            in_specs=[pl.BlockSpec((tr, tc), lambda i, j, chip_ref: (i, j))],
            out_specs=pl.BlockSpec((None, None, tr, tc), o_map)),
        out_shape=jax.ShapeDtypeStruct((N_CHIPS, 2, hr, hc), BF16), compiler_params=_params(("parallel", "parallel")),
    )(chip, shard)


def _accumulate(part, acc_ref, nk, finalize):
    if nk == 1:
        finalize(part)
        return
    k = pl.program_id(2)

    @pl.when(k == 0)
    def _():
        acc_ref[...] = part

    @pl.when(k > 0)
    def _():
        acc_ref[...] += part

    @pl.when(k == nk - 1)
    def _():
        finalize(acc_ref[...])


def _mm_nn(a, b, *, name, epilogue="plain", res=None, gate=None, out_dtype=BF16, tm=1024, tn=1024, tk=2048, deps=()):
    if a.ndim == 3:
        Q, M, Kq = a.shape
        K = Q * Kq
    else:
        (M, K), Kq = a.shape, a.shape[1]
    tm, tk = _tile(M, tm, 16), _tile(Kq, tk)
    if a.ndim == 3:
        pa = Kq // tk
        a_spec = pl.BlockSpec((None, tm, tk), lambda i, j, k: (k // pa, i, k % pa))
    else:
        a_spec = pl.BlockSpec((tm, tk), lambda i, j, k: (i, k))
    if b.ndim == 3:
        P, _, Ns = b.shape
        N = P * Ns
        tn = _tile(Ns, tn)
        per = Ns // tn
        b_spec = pl.BlockSpec((None, tk, tn), lambda i, j, k: (j // per, k, j % per))
    else:
        N = b.shape[1]
        tn = _tile(N, tn)
        b_spec = pl.BlockSpec((tk, tn), lambda i, j, k: (k, j))
    nk = K // tk
    tile = pl.BlockSpec((tm, tn), lambda i, j, k: (i, j))

    def body(*refs):
        acc_ref = refs[-1] if nk > 1 else None
        a_ref, b_ref = refs[0], refs[1]
        part = jnp.dot(a_ref[...], b_ref[...], preferred_element_type=F32)
        if epilogue == "plain":
            def fin(acc):
                refs[2][...] = acc.astype(out_dtype)
        elif epilogue == "relu2":
            def fin(acc):
                refs[2][...] = acc.astype(BF16)
                refs[3][...] = jnp.square(jnp.maximum(acc, 0.0)).astype(BF16)
        elif epilogue == "add":
            def fin(acc):
                refs[3][...] = (acc + refs[2][...]).astype(out_dtype)
        else:
            def fin(acc):
                refs[4][...] = refs[2][...] + refs[3][...] * acc
                refs[5][...] = acc.astype(BF16)
        _accumulate(part, acc_ref, nk, fin)

    in_specs = [a_spec, b_spec]
    args = [a, b]
    if epilogue == "plain":
        out_shape, out_specs = jax.ShapeDtypeStruct((M, N), out_dtype), tile
    elif epilogue == "relu2":
        out_shape, out_specs = [jax.ShapeDtypeStruct((M, N), BF16)] * 2, [tile, tile]
    elif epilogue == "add":
        in_specs.append(tile)
        args.append(res)
        out_shape, out_specs = jax.ShapeDtypeStruct((M, N), out_dtype), tile
    else:
        in_specs += [tile, pl.BlockSpec((1, tn), lambda i, j, k: (0, j))]
        args += [res, gate]
        out_shape, out_specs = [jax.ShapeDtypeStruct((M, N), F32), jax.ShapeDtypeStruct((M, N), BF16)], [tile, tile]
    return _call(
        body, deps, name=name, grid=(M // tm, N // tn, nk), in_specs=in_specs, out_specs=out_specs, out_shape=out_shape,
        scratch_shapes=[pltpu.VMEM((tm, tn), F32)] if nk > 1 else [],
        compiler_params=_params(("parallel", "parallel", "arbitrary")),
    )(*args)


def _mm_nt(a, b, *, name, n=None, epilogue="plain", extra=None, out_dtype=F32, tm=1024, tn=1024, tk=2048, deps=()):
    if a.ndim == 3:
        Q, M, Kq = a.shape
        K = Q * Kq
    else:
        (M, K), Kq = a.shape, a.shape[1]
    if b.ndim == 3:
        P, N, Ks = b.shape
    else:
        N, Ks = b.shape
    N = n or N
    tm, tn, tk = _tile(M, tm, 16), _tile(N, tn), _tile(min(Kq, Ks), tk)
    nk = K // tk
    if a.ndim == 3:
        pa = Kq // tk
        a_spec = pl.BlockSpec((None, tm, tk), lambda i, j, k: (k // pa, i, k % pa))
    else:
        a_spec = pl.BlockSpec((tm, tk), lambda i, j, k: (i, k))
    if b.ndim == 3:
        pb = Ks // tk
        b_spec = pl.BlockSpec((None, tn, tk), lambda i, j, k: (k // pb, j, k % pb))
    else:
        b_spec = pl.BlockSpec((tn, tk), lambda i, j, k: (j, k))
    tile = pl.BlockSpec((tm, tn), lambda i, j, k: (i, j))

    def body(*refs):
        acc_ref = refs[-1] if nk > 1 else None
        part = lax.dot_general(refs[0][...], refs[1][...], (((1,), (1,)), ((), ())), preferred_element_type=F32)
        if epilogue == "plain":
            def fin(acc):
                refs[2][...] = acc.astype(out_dtype)
        elif epilogue == "add":
            def fin(acc):
                refs[3][...] = (acc + refs[2][...]).astype(out_dtype)
        else:
            def fin(acc):
                refs[3][...] = (acc * (2.0 * jnp.maximum(refs[2][...].astype(F32), 0.0))).astype(out_dtype)
        _accumulate(part, acc_ref, nk, fin)

    in_specs, args = [a_spec, b_spec], [a, b]
    if epilogue != "plain":
        in_specs.append(tile)
        args.append(extra)
    return _call(
        body, deps, name=name, grid=(M // tm, N // tn, nk), in_specs=in_specs, out_specs=tile,
        out_shape=jax.ShapeDtypeStruct((M, N), out_dtype),
        scratch_shapes=[pltpu.VMEM((tm, tn), F32)] if nk > 1 else [],
        compiler_params=_params(("parallel", "parallel", "arbitrary")),
    )(*args)


def _mm_tn(a, b, *, name, out_parts=1, tm=1024, tn=1024, tk=2048, deps=()):
    if a.ndim == 3:
        Qa, M, Kq = a.shape
        Kd = Qa * Kq
    else:
        (M, Kd), Kq = a.shape, a.shape[1]
    if b.ndim == 3:
        Q, _, Nq = b.shape
        N = Q * Nq
    else:
        N, Nq = b.shape[1], b.shape[1]
    Ns = N // out_parts
    tn = _tile(Ns, tn)
    while Nq % tn or Ns % tn:
        tn -= LANES
    tm, tk = _tile(Kq, tm), _tile(M, tk, 16)
    nk = M // tk
    if a.ndim == 3:
        pa = Kq // tm
        a_spec = pl.BlockSpec((None, tk, tm), lambda i, j, k: (i // pa, k, i % pa))
    else:
        a_spec = pl.BlockSpec((tk, tm), lambda i, j, k: (k, i))
    if b.ndim == 3:
        pb = Nq // tn
        b_spec = pl.BlockSpec((None, tk, tn), lambda i, j, k: (j // pb, k, j % pb))
    else:
        b_spec = pl.BlockSpec((tk, tn), lambda i, j, k: (k, j))
    if out_parts > 1:
        po = Ns // tn
        o_spec = pl.BlockSpec((None, tm, tn), lambda i, j, k: (j // po, i, j % po))
        out_shape = jax.ShapeDtypeStruct((out_parts, Kd, Ns), BF16)
    else:
        o_spec = pl.BlockSpec((tm, tn), lambda i, j, k: (i, j))
        out_shape = jax.ShapeDtypeStruct((Kd, N), BF16)

    def body(*refs):
        acc_ref = refs[-1] if nk > 1 else None
        part = lax.dot_general(refs[0][...], refs[1][...], (((0,), (0,)), ((), ())), preferred_element_type=F32)

        def fin(acc):
            refs[2][...] = acc.astype(BF16)
        _accumulate(part, acc_ref, nk, fin)

    return _call(
        body, deps, name=name, grid=(Kd // tm, N // tn, nk),
        in_specs=[a_spec, b_spec], out_specs=o_spec, out_shape=out_shape,
        scratch_shapes=[pltpu.VMEM((tm, tn), F32)] if nk > 1 else [],
        compiler_params=_params(("parallel", "parallel", "arbitrary")),
    )(a, b)


def _rows(S, D, i_map=lambda i: (i, 0), ts=512):
    return pl.BlockSpec((ts, D), i_map)


def _norm_fwd(x, gain, sc, sh, *, name, deps=()):
    S, D = x.shape
    ts = _tile(S, 512, 16)
    vec = pl.BlockSpec((1, D), lambda i: (0, 0))

    def body(x_ref, g_ref, sc_ref, sh_ref, h_ref):
        xv = x_ref[...]
        r = lax.rsqrt(jnp.mean(xv * xv, axis=-1, keepdims=True) + RMS_EPS)
        h = (xv * r) * g_ref[...]
        h_ref[...] = (h * (1.0 + sc_ref[...]) + sh_ref[...]).astype(BF16)

    return _call(
        body, deps, name=name, grid=(S // ts,), in_specs=[_rows(S, D, ts=ts), vec, vec, vec], out_specs=_rows(S, D, ts=ts),
        out_shape=jax.ShapeDtypeStruct((S, D), BF16), compiler_params=_params(("parallel",)),
    )(x, gain, sc, sh)


def _loss_bwd(x, target, gain, gate_prev, *, name, deps=()):
    S, D = x.shape
    ts = _tile(S, 256, 16)
    vec = pl.BlockSpec((1, D), lambda i: (0, 0))

    def body(x_ref, t_ref, g_ref, gp_ref, dx_ref, dp_ref, sums_ref):
        @pl.when(pl.program_id(0) == 0)
        def _():
            sums_ref[...] = jnp.zeros_like(sums_ref)
        xv = x_ref[...]
        r = lax.rsqrt(jnp.mean(xv * xv, axis=-1, keepdims=True) + RMS_EPS)
        xn = xv * r
        err = xn * g_ref[...] - t_ref[...]
        loss = 0.5 * jnp.sum(jnp.mean(err * err, axis=-1, keepdims=True), axis=0, keepdims=True)
        dy = err * (1.0 / D)
        dxn = dy * g_ref[...]
        dx = r * (dxn - xn * jnp.mean(dxn * xn, axis=-1, keepdims=True))
        dx_ref[...] = dx
        dp_ref[...] = (gp_ref[...] * dx).astype(BF16)
        sums_ref[0:1, :] += jnp.sum(dy * xn, axis=0, keepdims=True)
        sums_ref[1:2, :] += jnp.broadcast_to(loss, (1, D))

    return _call(
        body, deps, name=name, grid=(S // ts,),
        in_specs=[_rows(S, D, ts=ts), _rows(S, D, ts=ts), vec, vec],
        out_specs=[_rows(S, D, ts=ts), _rows(S, D, ts=ts), pl.BlockSpec((8, D), lambda i: (0, 0))],
        out_shape=[jax.ShapeDtypeStruct((S, D), F32), jax.ShapeDtypeStruct((S, D), BF16), jax.ShapeDtypeStruct((8, D), F32)],
        compiler_params=_params(("arbitrary",)),
    )(x, target, gain, gate_prev)


def _norm_bwd(x, dh, dxp, mix, gain, sc, gate_prev, *, name, deps=()):
    S, D = x.shape
    ts = _tile(S, 256, 16)
    vec = pl.BlockSpec((1, D), lambda i: (0, 0))
    with_prev = gate_prev is not None

    def body(*refs):
        x_ref, dh_ref, dxp_ref, mix_ref, g_ref, sc_ref = refs[:6]
        outs = refs[7:] if with_prev else refs[6:]
        sums_ref = outs[-1]

        @pl.when(pl.program_id(0) == 0)
        def _():
            sums_ref[...] = jnp.zeros_like(sums_ref)
        xv, dhv, dxpv = x_ref[...], dh_ref[...], dxp_ref[...]
        r = lax.rsqrt(jnp.mean(xv * xv, axis=-1, keepdims=True) + RMS_EPS)
        xn = xv * r
        hn = xn * g_ref[...]
        dhn = dhv * (1.0 + sc_ref[...])
        dxn = dhn * g_ref[...]
        dx = dxpv + r * (dxn - xn * jnp.mean(dxn * xn, axis=-1, keepdims=True))
        outs[0][...] = dx
        if with_prev:
            outs[1][...] = (refs[6][...] * dx).astype(BF16)
        sums_ref[0:1, :] += jnp.sum(dhv, axis=0, keepdims=True)
        sums_ref[1:2, :] += jnp.sum(dhv * hn, axis=0, keepdims=True)
        sums_ref[2:3, :] += jnp.sum(dhn * xn, axis=0, keepdims=True)
        sums_ref[3:4, :] += jnp.sum(dxpv * mix_ref[...].astype(F32), axis=0, keepdims=True)

    tile = _rows(S, D, ts=ts)
    in_specs = [tile, tile, tile, tile, vec, vec] + ([vec] if with_prev else [])
    args = [x, dh, dxp, mix, gain, sc] + ([gate_prev] if with_prev else [])
    out_specs = [tile] + ([tile] if with_prev else []) + [pl.BlockSpec((8, D), lambda i: (0, 0))]
    out_shape = ([jax.ShapeDtypeStruct((S, D), F32)] + ([jax.ShapeDtypeStruct((S, D), BF16)] if with_prev else [])
                 + [jax.ShapeDtypeStruct((8, D), F32)])
    outs = _call(
        body, deps, name=name, grid=(S // ts,), in_specs=in_specs, out_specs=out_specs, out_shape=out_shape,
        compiler_params=_params(("arbitrary",)),
    )(*args)
    return (outs[0], outs[1], outs[2]) if with_prev else (outs[0], None, outs[1])


def _fgate_fwd(h, wf, bf, *, name, deps=()):
    S, D = h.shape
    ts = _tile(S, 256, 16)

    def body(h_ref, w_ref, b_ref, z_ref, f_ref, carry):
        @pl.when(pl.program_id(0) == 0)
        def _():
            carry[...] = jnp.zeros_like(carry)
        z = lax.dot_general(h_ref[...], w_ref[...], (((1,), (1,)), ((), ())), preferred_element_type=F32) + b_ref[...]
        logf = jnp.minimum(z, 0.0) - jnp.log(1.0 + jnp.exp(-jnp.abs(z)))
        row = lax.broadcasted_iota(jnp.int32, (ts, ts), 0)
        col = lax.broadcasted_iota(jnp.int32, (ts, ts), 1)
        tril = (col <= row).astype(F32)
        run = jnp.dot(tril, logf, preferred_element_type=F32, precision=lax.Precision.HIGHEST) + carry[0:1, :]
        z_ref[...] = z
        f_ref[...] = run
        carry[0:1, :] = run[ts - 1:ts, :]

    return _call(
        body, deps, name=name, grid=(S // ts,),
        in_specs=[pl.BlockSpec((ts, D), lambda i: (i, 0)), pl.BlockSpec((LANES, D), lambda i: (0, 0)),
                  pl.BlockSpec((1, LANES), lambda i: (0, 0))],
        out_specs=[pl.BlockSpec((ts, LANES), lambda i: (i, 0))] * 2,
        out_shape=[jax.ShapeDtypeStruct((S, LANES), F32)] * 2,
        scratch_shapes=[pltpu.VMEM((8, LANES), F32)],
        compiler_params=_params(("arbitrary",)),
    )(h, wf, bf)


def _fgate_bwd(dfq, dfk, z, *, name, deps=()):
    S = z.shape[0]
    ts = _tile(S, 256, 16)
    n = S // ts

    def body(dq_ref, dk_ref, z_ref, dz_ref, sums_ref, carry):
        @pl.when(pl.program_id(0) == 0)
        def _():
            carry[...] = jnp.zeros_like(carry)
            sums_ref[...] = jnp.zeros_like(sums_ref)
        df = dq_ref[...] - dk_ref[...]
        row = lax.broadcasted_iota(jnp.int32, (ts, ts), 0)
        col = lax.broadcasted_iota(jnp.int32, (ts, ts), 1)
        triu = (col >= row).astype(F32)
        run = jnp.dot(triu, df, preferred_element_type=F32, precision=lax.Precision.HIGHEST) + carry[0:1, :]
        zv = z_ref[...]
        dz = run * (1.0 / (1.0 + jnp.exp(zv)))
        dz_ref[...] = dz.astype(BF16)
        sums_ref[0:1, :] += jnp.sum(dz, axis=0, keepdims=True)
        carry[0:1, :] = run[0:1, :]

    rev = pl.BlockSpec((ts, LANES), lambda i: (n - 1 - i, 0))
    return _call(
        body, deps, name=name, grid=(n,), in_specs=[rev, rev, rev],
        out_specs=[rev, pl.BlockSpec((8, LANES), lambda i: (0, 0))],
        out_shape=[jax.ShapeDtypeStruct((S, LANES), BF16), jax.ShapeDtypeStruct((8, LANES), F32)],
        scratch_shapes=[pltpu.VMEM((8, LANES), F32)],
        compiler_params=_params(("arbitrary",)),
    )(dfq, dfk, z)


def _head_col(ref, rows, lane_mask):
    return jnp.sum(jnp.where(lane_mask, ref[rows, :], 0.0), axis=1, keepdims=True)


def _attn_fwd(qkv, fq, fk, *, heads, name, T=256, deps=()):
    S, D3 = qkv.shape
    D = D3 // 3
    dh = D // heads
    T = _tile(S, T, 16)
    nq = S // T
    scale = dh ** -0.5
    hp = fk.shape[1]

    def body(q_ref, k_ref, v_ref, fq_ref, fk_ref, o_ref, lse_ref):
        h = pl.program_id(0)

        @pl.when(h == 0)
        def _():
            lse_ref[...] = jnp.zeros_like(lse_ref)
        lane = lax.broadcasted_iota(jnp.int32, (1, LANES), 1) == h
        row = lax.broadcasted_iota(jnp.int32, (T, T), 0)
        col = lax.broadcasted_iota(jnp.int32, (T, T), 1)

        def q_block(qi, _):
            rows = pl.ds(pl.multiple_of(qi * T, T), T)
            q = q_ref[rows, :]
            fq_col = _head_col(fq_ref, rows, lane)

            def kv_block(kj, carry, diag):
                m, l, acc = carry
                cols = pl.ds(pl.multiple_of(kj * T, T), T)
                s = lax.dot_general(q, k_ref[cols, :], (((1,), (1,)), ((), ())), preferred_element_type=F32) * scale
                s = s + (fq_col - fk_ref[kj, pl.ds(h, 1), :])
                if diag:
                    s = jnp.where(col <= row, s, NEG)
                m_new = jnp.maximum(m, jnp.max(s, axis=1, keepdims=True))
                p = jnp.exp(s - m_new)
                alpha = jnp.exp(m - m_new)
                l = alpha * l + jnp.sum(p, axis=1, keepdims=True)
                acc = alpha * acc + jnp.dot(p.astype(BF16), v_ref[cols, :], preferred_element_type=F32)
                return m_new, l, acc

            init = (jnp.full((T, 1), NEG, F32), jnp.zeros((T, 1), F32), jnp.zeros((T, dh), F32))
            carry = lax.fori_loop(0, qi, lambda kj, cr: kv_block(kj, cr, False), init)
            m, l, acc = kv_block(qi, carry, True)
            o_ref[rows, :] = (acc / l).astype(BF16)
            lse_ref[rows, :] = jnp.where(lane, m + jnp.log(l), lse_ref[rows, :])
            return 0

        lax.fori_loop(0, nq, q_block, 0)

    head = lambda part: pl.BlockSpec((S, dh), lambda h: (0, part * heads + h))
    return _call(
        body, deps, name=name, grid=(heads,),
        in_specs=[head(0), head(1), head(2), pl.BlockSpec((S, LANES), lambda h: (0, 0)),
                  pl.BlockSpec((nq, hp, T), lambda h: (0, 0, 0))],
        out_specs=[pl.BlockSpec((S, dh), lambda h: (0, h)), pl.BlockSpec((S, LANES), lambda h: (0, 0))],
        out_shape=[jax.ShapeDtypeStruct((S, D), BF16), jax.ShapeDtypeStruct((S, LANES), F32)],
        compiler_params=_params(("arbitrary",)),
    )(qkv, qkv, qkv, fq, fk)


def _attn_bwd(qkv, o, do, fq, fk, lse, *, heads, name, T=256, deps=()):
    S, D3 = qkv.shape
    D = D3 // 3
    dh = D // heads
    T = _tile(S, T, 16)
    nq = S // T
    scale = dh ** -0.5
    hp = fk.shape[1]

    def body(q_ref, k_ref, v_ref, o_ref, do_ref, fq_ref, fk_ref, lse_ref, dqkv_ref, dfq_ref, dfk_ref,
             dq_acc, fq_col, lse_col, delta_col, dfq_col):
        h = pl.program_id(0)

        @pl.when(h == 0)
        def _():
            dfq_ref[...] = jnp.zeros_like(dfq_ref)
            dfk_ref[...] = jnp.zeros_like(dfk_ref)
        lane = lax.broadcasted_iota(jnp.int32, (1, LANES), 1) == h
        row = lax.broadcasted_iota(jnp.int32, (T, T), 0)
        col = lax.broadcasted_iota(jnp.int32, (T, T), 1)
        dq_acc[...] = jnp.zeros_like(dq_acc)
        dfq_col[...] = jnp.zeros_like(dfq_col)

        def prep(qi, _):
            rows = pl.ds(pl.multiple_of(qi * T, T), T)
            fq_col[rows, :] = _head_col(fq_ref, rows, lane)
            lse_col[rows, :] = _head_col(lse_ref, rows, lane)
            delta_col[rows, :] = jnp.sum(do_ref[rows, :].astype(F32) * o_ref[rows, :].astype(F32), axis=1, keepdims=True)
            return 0

        lax.fori_loop(0, nq, prep, 0)

        def kv_block(kj, _):
            cols = pl.ds(pl.multiple_of(kj * T, T), T)
            k, v = k_ref[cols, :], v_ref[cols, :]
            fk_row = fk_ref[kj, pl.ds(h, 1), :]

            def q_block(qi, carry, diag):
                dk, dv, dfk = carry
                rows = pl.ds(pl.multiple_of(qi * T, T), T)
                q, dov = q_ref[rows, :], do_ref[rows, :]
                s = lax.dot_general(q, k, (((1,), (1,)), ((), ())), preferred_element_type=F32) * scale
                s = s + (fq_col[rows, :] - fk_row)
                p = jnp.exp(s - lse_col[rows, :])
                if diag:
                    p = jnp.where(col <= row, p, 0.0)
                dp = lax.dot_general(dov, v, (((1,), (1,)), ((), ())), preferred_element_type=F32)
                ds = p * (dp - delta_col[rows, :])
                dsb = ds.astype(BF16)
                dv = dv + lax.dot_general(p.astype(BF16), dov, (((0,), (0,)), ((), ())), preferred_element_type=F32)
                dk = dk + lax.dot_general(dsb, q, (((0,), (0,)), ((), ())), preferred_element_type=F32)
                dq_acc[rows, :] += jnp.dot(dsb, k, preferred_element_type=F32)
                dfq_col[rows, :] += jnp.sum(ds, axis=1, keepdims=True)
                dfk = dfk + jnp.sum(ds, axis=0, keepdims=True)
                return dk, dv, dfk

            init = (jnp.zeros((T, dh), F32), jnp.zeros((T, dh), F32), jnp.zeros((1, T), F32))
            carry = q_block(kj, init, True)
            dk, dv, dfk = lax.fori_loop(kj + 1, nq, lambda qi, cr: q_block(qi, cr, False), carry)
            dqkv_ref[1, cols, :] = (dk * scale).astype(BF16)
            dqkv_ref[2, cols, :] = dv.astype(BF16)
            dfk_ref[kj, pl.ds(h, 1), :] = dfk
            return 0

        lax.fori_loop(0, nq, kv_block, 0)

        def finish(qi, _):
            rows = pl.ds(pl.multiple_of(qi * T, T), T)
            dqkv_ref[0, rows, :] = (dq_acc[rows, :] * scale).astype(BF16)
            dfq_ref[rows, :] = jnp.where(lane, dfq_col[rows, :], dfq_ref[rows, :])
            return 0

        lax.fori_loop(0, nq, finish, 0)

    head = lambda part: pl.BlockSpec((S, dh), lambda h: (0, part * heads + h))
    own = pl.BlockSpec((S, dh), lambda h: (0, h))
    full = pl.BlockSpec((S, LANES), lambda h: (0, 0))
    krow = pl.BlockSpec((nq, hp, T), lambda h: (0, 0, 0))
    return _call(
        body, deps, name=name, grid=(heads,),
        in_specs=[head(0), head(1), head(2), own, own, full, krow, full],
        out_specs=[pl.BlockSpec((3, S, dh), lambda h: (0, 0, h)), full, krow],
        out_shape=[jax.ShapeDtypeStruct((3, S, D), BF16), jax.ShapeDtypeStruct((S, LANES), F32),
                   jax.ShapeDtypeStruct((nq, hp, T), F32)],
        scratch_shapes=[pltpu.VMEM((S, dh), F32)] + [pltpu.VMEM((S, 1), F32)] * 4,
        compiler_params=_params(("arbitrary",)),
    )(qkv, qkv, qkv, o, do, fq, fk, lse)


def _shift_down(v, n):
    rows = lax.broadcasted_iota(jnp.int32, v.shape, 0)
    return jnp.where(rows >= n, pltpu.roll(v, n, axis=0), 0.0)


def _shift_up(v, n):
    S = v.shape[0]
    rows = lax.broadcasted_iota(jnp.int32, v.shape, 0)
    return jnp.where(rows < S - n, pltpu.roll(v, S - n, axis=0), 0.0)


def _conv_fwd(proj, conv_w, *, name, cb=LANES, deps=()):
    S, D3 = proj.shape
    D = D3 // 3
    nb = D // cb

    def body(bg_ref, cg_ref, u_ref, w_ref, z_ref):
        uc = cg_ref[...].astype(F32) * u_ref[...].astype(F32)
        w = w_ref[...]
        y = w[2:3, :] * uc + w[1:2, :] * _shift_down(uc, 1) + w[0:1, :] * _shift_down(uc, 2)
        z_ref[...] = (bg_ref[...].astype(F32) * y).astype(BF16)

    part = lambda g: pl.BlockSpec((S, cb), lambda j: (0, g * nb + j))
    return _call(
        body, deps, name=name, grid=(nb,),
        in_specs=[part(0), part(1), part(2), pl.BlockSpec((3, cb), lambda j: (0, j))],
        out_specs=pl.BlockSpec((S, cb), lambda j: (0, j)),
        out_shape=jax.ShapeDtypeStruct((S, D), BF16), compiler_params=_params(("parallel",)),
    )(proj, proj, proj, conv_w)


def _conv_bwd(proj, conv_w, dz, *, name, cb=LANES, deps=()):
    S, D3 = proj.shape
    D = D3 // 3
    nb = D // cb

    def body(bg_ref, cg_ref, u_ref, w_ref, dz_ref, dp_ref, dw_ref):
        cg, u = cg_ref[...].astype(F32), u_ref[...].astype(F32)
        uc = cg * u
        w = w_ref[...]
        uc1, uc2 = _shift_down(uc, 1), _shift_down(uc, 2)
        y = w[2:3, :] * uc + w[1:2, :] * uc1 + w[0:1, :] * uc2
        dz = dz_ref[...].astype(F32)
        dp_ref[0] = (dz * y).astype(BF16)
        dy = dz * bg_ref[...].astype(F32)
        duc = w[2:3, :] * dy + w[1:2, :] * _shift_up(dy, 1) + w[0:1, :] * _shift_up(dy, 2)
        dp_ref[1] = (duc * u).astype(BF16)
        dp_ref[2] = (duc * cg).astype(BF16)
        dw_ref[...] = jnp.zeros_like(dw_ref)
        dw_ref[0:1, :] = jnp.sum(dy * uc2, axis=0, keepdims=True)
        dw_ref[1:2, :] = jnp.sum(dy * uc1, axis=0, keepdims=True)
        dw_ref[2:3, :] = jnp.sum(dy * uc, axis=0, keepdims=True)

    part = lambda g: pl.BlockSpec((S, cb), lambda j: (0, g * nb + j))
    return _call(
        body, deps, name=name, grid=(nb,),
        in_specs=[part(0), part(1), part(2), pl.BlockSpec((3, cb), lambda j: (0, j)), pl.BlockSpec((S, cb), lambda j: (0, j))],
        out_specs=[pl.BlockSpec((3, S, cb), lambda j: (0, 0, j)), pl.BlockSpec((8, cb), lambda j: (0, j))],
        out_shape=[jax.ShapeDtypeStruct((3, S, D), BF16), jax.ShapeDtypeStruct((8, D), F32)],
        compiler_params=_params(("parallel",)),
    )(proj, proj, proj, conv_w, dz)


def _ada_fwd(c_all, ada_w, *, name, deps=()):
    L, D, Ns = ada_w.shape
    tn = _tile(Ns, 512)

    def body(c_ref, w_ref, o_ref, act_ref):
        cv = c_ref[...]
        act = cv * (1.0 / (1.0 + jnp.exp(-cv)))
        act_ref[...] = act
        o_ref[...] = jnp.dot(act.astype(BF16), w_ref[...].astype(BF16), preferred_element_type=F32)

    return _call(
        body, deps, name=name, grid=(L, Ns // tn),
        in_specs=[pl.BlockSpec((N_DEV, D), lambda l, j: (0, 0)), pl.BlockSpec((None, D, tn), lambda l, j: (l, 0, j))],
        out_specs=[pl.BlockSpec((None, N_DEV, tn), lambda l, j: (l, 0, j)), pl.BlockSpec((N_DEV, D), lambda l, j: (0, 0))],
        out_shape=[jax.ShapeDtypeStruct((L, N_DEV, Ns), F32), jax.ShapeDtypeStruct((N_DEV, D), F32)],
        compiler_params=_params(("arbitrary", "arbitrary")),
    )(c_all, ada_w)


def _select_mod(gathered, *, name, deps=()):
    _, LB, Ns = gathered.shape
    L = LB // N_DEV

    def body(g_ref, o_ref):
        x, y, c = _me()
        b = 4 * x + 2 * y + c
        for j in range(N_CHIPS):
            for l in range(L):
                o_ref[j, pl.ds(l, 1), :] = g_ref[2 * j + c, pl.ds(l * N_DEV + b, 1), :]

    return _call(
        body, deps, name=name, out_shape=jax.ShapeDtypeStruct((N_CHIPS, L, Ns), F32),
        in_specs=[pl.BlockSpec(memory_space=pltpu.VMEM)], out_specs=pl.BlockSpec(memory_space=pltpu.VMEM),
        compiler_params=_params(),
    )(gathered)


def _adamw_math(w, g, m, v):
    m = ADAM_B1 * m + (1.0 - ADAM_B1) * g
    v = ADAM_B2 * v + (1.0 - ADAM_B2) * jnp.square(g)
    m_hat = m / (1.0 - ADAM_B1 ** ADAM_STEP)
    v_hat = v / (1.0 - ADAM_B2 ** ADAM_STEP)
    delta = -ADAM_LR * (m_hat / (jnp.sqrt(v_hat) + ADAM_EPS) + ADAM_WD * w)
    return delta, m, v


def _adamw_shards(w, m, v, groups, chip, *, name):
    L, R, C = w.shape
    if R % 16 == 0:
        tr, tc = _tile(R, 128, 16), C
    else:
        tr, tc = R, _tile(C, 256)
    nr, nc = R // tr, C // tc

    def body(chip_ref, w_ref, m_ref, v_ref, *rest):
        srcs, (g_ref, d_ref, mo_ref, vo_ref) = rest[:2 * N_CHIPS * L], rest[2 * N_CHIPS * L:]
        for l in range(L):
            @pl.when(pl.program_id(0) == l)
            def _():
                s = srcs[2 * N_CHIPS * l:2 * N_CHIPS * (l + 1)]
                mine, other = s[0][...].astype(F32), s[N_CHIPS][...].astype(F32)
                for k in range(1, N_CHIPS):
                    mine = mine + s[k][...].astype(F32)
                    other = other + s[N_CHIPS + k][...].astype(F32)
                g = mine + other
                delta, mn, vn = _adamw_math(w_ref[...], g, m_ref[...], v_ref[...])
                g_ref[...] = g
                d_ref[...] = delta
                mo_ref[...] = mn
                vo_ref[...] = vn

    tile = pl.BlockSpec((None, tr, tc), lambda l, i, j, chip_ref: (l, i, j))

    def block(layer, k):
        def index(l, i, j, chip_ref):
            idle_i, idle_j = jnp.where(l < layer, 0, nr - 1), jnp.where(l < layer, 0, nc - 1)
            return (jnp.bitwise_xor(chip_ref[0], k), jnp.where(l == layer, i, idle_i), jnp.where(l == layer, j, idle_j))
        return pl.BlockSpec((None, tr, tc), index)

    in_specs, args = [tile] * 3, [w, m, v]
    for layer, (parts, land, sib) in enumerate(groups):
        in_specs += [block(layer, k) for k in range(N_CHIPS)] * 2
        args += [parts, land, land, land, sib, sib, sib, sib]
    return pl.pallas_call(
        body, name=name,
        grid_spec=pltpu.PrefetchScalarGridSpec(num_scalar_prefetch=1, grid=(L, nr, nc), in_specs=in_specs, out_specs=[tile] * 4),
        out_shape=[jax.ShapeDtypeStruct((L, R, C), F32)] * 4, compiler_params=_params(("arbitrary", "arbitrary", "arbitrary")),
    )(chip, *args)


def _adamw_ada(w, m, v, act_t, dmod, *, name, tr=256, deps=()):
    L, D, Ns = w.shape
    tr = _tile(D, tr, 8)

    def body(w_ref, m_ref, v_ref, a_ref, d_ref, g_ref, dl_ref, mo_ref, vo_ref):
        x, y, _ = _me()
        g = jnp.dot(a_ref[...], d_ref[2 * x + y], preferred_element_type=F32, precision=lax.Precision.HIGHEST)
        delta, mn, vn = _adamw_math(w_ref[...], g, m_ref[...], v_ref[...])
        g_ref[...] = g
        dl_ref[...] = delta
        mo_ref[...] = mn
        vo_ref[...] = vn

    tile = pl.BlockSpec((None, tr, Ns), lambda l, i: (l, i, 0))
    return _call(
        body, deps, name=name, grid=(L, D // tr),
        in_specs=[tile] * 3 + [pl.BlockSpec((tr, N_DEV), lambda l, i: (i, 0)),
                               pl.BlockSpec((N_CHIPS, None, N_DEV, Ns), lambda l, i: (0, l, 0, 0))],
        out_specs=[tile] * 4, out_shape=[jax.ShapeDtypeStruct((L, D, Ns), F32)] * 4,
        compiler_params=_params(("parallel", "parallel")),
    )(w, m, v, act_t, dmod)


def _adamw_small(w, m, v, gathered, *, rows, name, deps=()):
    n, D = w.shape

    def body(w_ref, m_ref, v_ref, s_ref, g_ref, d_ref, mo_ref, vo_ref):
        for r, src in enumerate(rows):
            g = s_ref[0, src:src + 1, :]
            for d in range(1, N_DEV):
                g = g + s_ref[d, src:src + 1, :]
            g_ref[r:r + 1, :] = g
        g = g_ref[...]
        delta, mn, vn = _adamw_math(w_ref[...], g, m_ref[...], v_ref[...])
        d_ref[...] = delta
        mo_ref[...] = mn
        vo_ref[...] = vn

    vm = pl.BlockSpec(memory_space=pltpu.VMEM)
    return _call(
        body, deps, name=name, in_specs=[vm] * 4, out_specs=[vm] * 4,
        out_shape=[jax.ShapeDtypeStruct((n, D), F32)] * 4, compiler_params=_params(),
    )(w, m, v, gathered)


def _adamw_conv_w(w, m, v, gathered4, *, name, deps=()):
    Cs = w.shape[1]

    def body(w_ref, m_ref, v_ref, s_ref, g_ref, d_ref, mo_ref, vo_ref):
        x, y, _ = _me()
        j = 2 * x + y
        g = s_ref[j, 0]
        for d in range(1, N_DEV):
            g = g + s_ref[j, d]
        delta, mn, vn = _adamw_math(w_ref[...], g, m_ref[...], v_ref[...])
        g_ref[...] = g
        d_ref[...] = delta
        mo_ref[...] = mn
        vo_ref[...] = vn

    vm = pl.BlockSpec(memory_space=pltpu.VMEM)
    return _call(
        body, deps, name=name, in_specs=[vm] * 4, out_specs=[vm] * 4,
        out_shape=[jax.ShapeDtypeStruct((8, Cs), F32)] * 4, compiler_params=_params(),
    )(w, m, v, gathered4)


def _loss_sum(gathered, *, row, name, deps=()):
    _, _, D = gathered.shape

    def body(s_ref, o_ref):
        t = s_ref[0, row:row + 1, :]
        for d in range(1, N_DEV):
            t = t + s_ref[d, row:row + 1, :]
        o_ref[...] = jnp.broadcast_to(t, (8, D))

    vm = pl.BlockSpec(memory_space=pltpu.VMEM)
    return pl.pallas_call(body, name=name, in_specs=[vm], out_specs=vm, out_shape=jax.ShapeDtypeStruct((8, D), F32),
                          compiler_params=_params())(gathered)


def _pad_rows(a, n):
    return jnp.pad(a, ((0, n - a.shape[0]), (0, 0)))


def kernel(x, c, ada_w, ada_b, norm_mix, norm_mlp, fox_w_in, fox_b_f, fox_w_out, conv_w_in, conv_w, conv_w_out, mlp_w_up, mlp_w_down, final_norm, loss_target, m_ada_w, m_ada_b, m_norm_mix, m_norm_mlp, m_fox_w_in, m_fox_b_f, m_fox_w_out, m_conv_w_in, m_conv_w, m_conv_w_out, m_mlp_w_up, m_mlp_w_down, m_final_norm, v_ada_w, v_ada_b, v_norm_mix, v_norm_mlp, v_fox_w_in, v_fox_b_f, v_fox_w_out, v_conv_w_in, v_conv_w, v_conv_w_out, v_mlp_w_up, v_mlp_w_down, v_final_norm):
    S, D = x.shape[1], x.shape[2]
    H = fox_b_f.shape[-1]
    L = ada_w.shape[0]
    NM = ada_b.shape[1] // D
    Ns_ada = ada_w.shape[2]
    Cs_fox = fox_w_in.shape[2]
    Cs_conv = conv_w.shape[2]
    x0 = x[0]
    target = loss_target[0]

    chip = (2 * lax.axis_index("x") + lax.axis_index("y")).astype(jnp.int32).reshape(1)

    fin_t = jnp.transpose(fox_w_in[0])
    shards = dict(fin=fin_t, fout=fox_w_out[0], up0=mlp_w_up[0], dn0=mlp_w_down[0], cin=conv_w_in[0], cout=conv_w_out[0],
                  up1=mlp_w_up[1], dn1=mlp_w_down[1])
    halves = dict(fin="cols", fout="rows", up0="cols", dn0="rows", cin="cols", cout="rows", up1="cols", dn1="rows")
    gathers = {}

    def start_gather(key, dep=None):
        land = _place_cast(shards[key], chip, halves=halves[key], name="place_" + key)
        gathers[key] = _split_start("gather1", [(land,)], name="gather_start_" + key, dep=dep)
        return gathers[key][3]

    def pass_gather(key, after):
        landed = _split_wait("gather1", gathers[key], after, name="gather_landed_" + key)
        gathers[key] = _split_start("gather2", landed, name="gather_pass_" + key)
        return gathers[key][3]

    def gathered(key, after):
        return _split_wait("gather2", gathers[key], after, name="gather_wait_" + key)[0][0]

    tok = start_gather("fin")

    c_all = _allgather8(_pad_rows(c, 8), name="gather_c", deps=(tok,))[:, 0, :]
    mod_part, c_act = _ada_fwd(c_all, ada_w, name="ada_fwd")
    mod_all = _allgather8(mod_part.reshape(L * N_DEV, Ns_ada), name="gather_mod")
    mod = _select_mod(mod_all, name="select_mod")
    mod = jnp.transpose(mod, (1, 0, 2)).reshape(L, NM, 1, D) + ada_b.reshape(L, NM, 1, D)
    conv_w_all = _allgather8(_pad_rows(conv_w[0], 8), name="gather_conv_w")
    conv_w_full = jnp.transpose(conv_w_all[0::2, :3, :], (1, 0, 2)).reshape(3, D)

    def vec(a):
        return a.reshape(1, D)

    h0 = _norm_fwd(x0, vec(norm_mix[0]), mod[0, 1], mod[0, 0], name="norm_mix0", deps=(conv_w_all,))
    tok = start_gather("fout", h0)
    tok = start_gather("up0", tok)
    tok = pass_gather("fin", [h0, tok])
    w_fin_t = jnp.transpose(gathered("fin", [tok]), (0, 2, 1, 3)).reshape(N_CHIPS * Cs_fox, D)
    w_f_t = _pad_rows(w_fin_t[3 * D:], LANES)
    tok = pass_gather("fout", [w_fin_t])
    qkv = _mm_nt(h0, w_fin_t, n=3 * D, name="fox_in", out_dtype=BF16, deps=(tok,))
    tok = start_gather("dn0", qkv)
    tok = pass_gather("up0", [tok])
    b_f = jnp.pad(fox_b_f, ((0, 0), (0, LANES - H)))
    z_f, F_col = _fgate_fwd(h0, w_f_t, b_f, name="fgate_fwd", deps=(tok,))
    hp = max(8, H)
    at_f, at = _tile(S, 1024, 16), _tile(S, 512, 16)
    F_rows = _pad_rows(jnp.transpose(F_col[:, :H]), hp)
    F_row = jnp.transpose(F_rows.reshape(hp, S // at, at), (1, 0, 2))
    o, lse = _attn_fwd(qkv, F_col, jnp.transpose(F_rows.reshape(hp, S // at_f, at_f), (1, 0, 2)), heads=H, name="attn_fwd", T=at_f)
    w_fout = gathered("fout", [o]).reshape(D, D)
    tok = pass_gather("dn0", [o])
    x1, mix0 = _mm_nn(o, w_fout, name="fox_out", epilogue="resid", res=x0, gate=mod[0, 2], tm=512, deps=(tok,))
    tok = start_gather("cin", x1)
    h1 = _norm_fwd(x1, vec(norm_mlp[0]), mod[0, 4], mod[0, 3], name="norm_mlp0", deps=(tok,))
    w_up0 = gathered("up0", [h1]).reshape(2 * N_CHIPS, D, -1)
    tok = pass_gather("cin", [h1])
    u0, a0 = _mm_nn(h1, w_up0, name="mlp_up0", epilogue="relu2", deps=(tok,))
    tok = start_gather("cout", u0)
    w_dn0 = gathered("dn0", [a0, tok]).reshape(-1, D)
    tok = pass_gather("cout", [a0, tok])
    x2, y0 = _mm_nn(a0, w_dn0, name="mlp_down0", epilogue="resid", res=x1, gate=mod[0, 5], tm=512, deps=(tok,))
    tok = start_gather("up1", x2)
    h2 = _norm_fwd(x2, vec(norm_mix[1]), mod[1, 1], mod[1, 0], name="norm_mix1", deps=(tok,))
    g_cin = gathered("cin", [h2]).reshape(2 * N_CHIPS, D, -1)
    tok = pass_gather("up1", [h2])
    proj = _mm_nn(h2, g_cin, name="conv_in", deps=(tok,))
    w_cin = jnp.transpose(g_cin, (1, 0, 2)).reshape(D, 3 * D)
    tok = start_gather("dn1", proj)
    zc = _conv_fwd(proj, conv_w_full, name="conv_fwd", deps=(tok,))
    w_cout = gathered("cout", [zc]).reshape(D, D)
    tok = pass_gather("dn1", [zc])
    x3, mix1 = _mm_nn(zc, w_cout, name="conv_out", epilogue="resid", res=x2, gate=mod[1, 2], tm=512, deps=(tok,))
    h3 = _norm_fwd(x3, vec(norm_mlp[1]), mod[1, 4], mod[1, 3], name="norm_mlp1")
    w_up1 = gathered("up1", [h3]).reshape(2 * N_CHIPS, D, -1)
    u1, a1 = _mm_nn(h3, w_up1, name="mlp_up1", epilogue="relu2")
    w_dn1 = gathered("dn1", [a1]).reshape(-1, D)
    x4, y1 = _mm_nn(a1, w_dn1, name="mlp_down1", epilogue="resid", res=x3, gate=mod[1, 5], tm=512)
    w_up, w_dn = [w_up0, w_up1], [w_dn0, w_dn1]

    dx4, dy1, sums_f = _loss_bwd(x4, target, vec(final_norm), mod[1, 5], name="loss_bwd")
    du1 = _mm_nt(dy1, w_dn[1], name="mlp_down1_dx", epilogue="drelu2", extra=u1, out_dtype=BF16)
    def start_scatter(tag, parts_list):
        groups = [(p, lax.empty(p.shape, p.dtype)) for p in parts_list]
        return _split_start("scatter", groups, name="scatter_start_" + tag)

    def start_sibling(tag, scatter, after):
        landed = _split_wait("scatter", scatter, after, name="scatter_wait_" + tag)
        groups = [(p, ld, lax.empty(p.shape, p.dtype)) for p, ld in landed]
        return _split_start("sibling", groups, name="sibling_start_" + tag)

    gw_dn1 = _mm_tn(a1, dy1, name="mlp_down1_dw")
    gw_up1 = _mm_tn(h3, du1, name="mlp_up1_dw", out_parts=N_CHIPS)
    sc1 = start_scatter("mlp1", [gw_dn1.reshape(N_CHIPS, -1, D), gw_up1])
    dh3 = _mm_nt(du1, w_up[1], name="mlp_up1_dx", deps=(sc1[3],))
    dx3, dmix1, sums_mlp1 = _norm_bwd(x3, dh3, dx4, y1, vec(norm_mlp[1]), mod[1, 4], mod[1, 2], name="norm_mlp1_bwd")
    dzc = _mm_nt(dmix1, w_cout, name="conv_out_dx", out_dtype=BF16)
    gw_cout = _mm_tn(zc, dmix1, name="conv_out_dw")
    dproj, dconv_w = _conv_bwd(proj, conv_w_full, dzc, name="conv_bwd")
    gw_cin = _mm_tn(h2, dproj, name="conv_in_dw", out_parts=N_CHIPS, tn=512)
    sc2 = start_scatter("conv", [gw_cout.reshape(N_CHIPS, -1, D), gw_cin])
    dh2 = _mm_nt(dproj, w_cin, name="conv_in_dx", deps=(sc2[3],))
    dx2, dy0, sums_mix1 = _norm_bwd(x2, dh2, dx3, mix1, vec(norm_mix[1]), mod[1, 1], mod[0, 5], name="norm_mix1_bwd")
    du0 = _mm_nt(dy0, w_dn[0], name="mlp_down0_dx", epilogue="drelu2", extra=u0, out_dtype=BF16)
    gw_dn0 = _mm_tn(a0, dy0, name="mlp_down0_dw")
    gw_up0 = _mm_tn(h1, du0, name="mlp_up0_dw", out_parts=N_CHIPS)
    sc3 = start_scatter("mlp0", [gw_dn0.reshape(N_CHIPS, -1, D), gw_up0])
    sb1 = start_sibling("mlp1", sc1, [sc3[3]])
    dh1 = _mm_nt(du0, w_up[0], name="mlp_up0_dx", deps=(sb1[3],))
    dx1, dmix0, sums_mlp0 = _norm_bwd(x1, dh1, dx2, y0, vec(norm_mlp[0]), mod[0, 4], mod[0, 2], name="norm_mlp0_bwd")
    do = _mm_nt(dmix0, w_fout, name="fox_out_dx", out_dtype=BF16)
    gw_fout = _mm_tn(o, dmix0, name="fox_out_dw")
    dqkv, dfq, dfk = _attn_bwd(qkv, o, do, F_col, F_row, lse, heads=H, name="attn_bwd", T=at)
    dfk_col = jnp.pad(jnp.transpose(jnp.transpose(dfk, (1, 0, 2)).reshape(hp, S)[:H]), ((0, 0), (0, LANES - H)))
    sb3 = start_sibling("mlp0", sc3, [dqkv])
    dz_f, sums_bf = _fgate_bwd(dfq, dfk_col, z_f, name="fgate_bwd", deps=(sb3[3],))
    gw_qkv_t = _mm_tn(dqkv, h0, name="fox_in_dw")
    gw_f_t = _mm_tn(dz_f, h0, name="fox_gate_dw")
    gw_fin_t = jnp.concatenate([gw_qkv_t, gw_f_t[:H]], axis=0).reshape(N_CHIPS, Cs_fox, D)
    sc4 = start_scatter("fox", [gw_fout.reshape(N_CHIPS, -1, D), gw_fin_t])
    sb2 = start_sibling("conv", sc2, [sc4[3]])
    dh0_f = _mm_nn(dz_f, w_f_t, name="fox_gate_dx", out_dtype=F32, deps=(sb2[3],))
    dh0 = _mm_nn(dqkv, w_fin_t, name="fox_in_dx", epilogue="add", res=dh0_f, out_dtype=F32)
    grad_x, _, sums_mix0 = _norm_bwd(x0, dh0, dx1, mix0, vec(norm_mix[0]), mod[0, 1], None, name="norm_mix0_bwd")

    outs = {}

    def put(name_, res, shape):
        for kind, r in zip(("grad", "delta", "new_m", "new_v"), res):
            outs[kind + "_" + name_] = r.reshape(shape)

    def shards_update(tag, w_, m_, v_, groups):
        return _adamw_shards(w_, m_, v_, groups, chip, name="adamw_" + tag)

    g_conv = _split_wait("sibling", sb2, [grad_x], name="sibling_wait_conv")
    put("conv_w_out", shards_update("conv_out", conv_w_out, m_conv_w_out, v_conv_w_out, g_conv[0:1]), conv_w_out.shape)
    r_cin = shards_update("conv_in", conv_w_in, m_conv_w_in, v_conv_w_in, g_conv[1:2])
    put("conv_w_in", r_cin, conv_w_in.shape)
    g_mlp1 = _split_wait("sibling", sb1, [r_cin[0]], name="sibling_wait_mlp1")
    g_mlp0 = _split_wait("sibling", sb3, [r_cin[0]], name="sibling_wait_mlp0")
    put("mlp_w_down", shards_update("mlp_down", mlp_w_down, m_mlp_w_down, v_mlp_w_down, [g_mlp0[0], g_mlp1[0]]), mlp_w_down.shape)
    r_up = shards_update("mlp_up", mlp_w_up, m_mlp_w_up, v_mlp_w_up, [g_mlp0[1], g_mlp1[1]])
    put("mlp_w_up", r_up, mlp_w_up.shape)
    sb4 = start_sibling("fox", sc4, [r_up[0]])

    dmod_rows = []
    for sm, sl in ((sums_mix0, sums_mlp0), (sums_mix1, sums_mlp1)):
        dmod_rows += [sm[0:1], sm[1:2], sm[3:4], sl[0:1], sl[1:2], sl[3:4]]
    bf_row = jnp.pad(sums_bf[0:1], ((0, 0), (0, D - LANES)))
    small = jnp.concatenate([sums_mix0[2:3], sums_mix1[2:3], sums_mlp0[2:3], sums_mlp1[2:3], sums_f[0:1], sums_f[1:2], bf_row,
                             jnp.zeros((1, D), F32)] + dmod_rows + [dconv_w[0:3]], axis=0)
    small_all = _allgather8(_pad_rows(small, -(-small.shape[0] // 8) * 8), name="gather_small", deps=(sb4[3],))
    loss = _loss_sum(small_all, row=5, name="loss_sum")[0, 0]

    def rows_of(a_mix, a_mlp, a_fin, a_bf, a_ada):
        return jnp.concatenate([a_mix, a_mlp, a_fin.reshape(1, D), jnp.pad(a_bf, ((0, 0), (0, D - H))),
                                a_ada.reshape(L * NM, D)], axis=0)
    n_small = 2 * L + 2 + L * NM
    rw = -(-n_small // 8) * 8
    w_s = _pad_rows(rows_of(norm_mix, norm_mlp, final_norm, fox_b_f, ada_b), rw)
    m_s = _pad_rows(rows_of(m_norm_mix, m_norm_mlp, m_final_norm, m_fox_b_f, m_ada_b), rw)
    v_s = _pad_rows(rows_of(v_norm_mix, v_norm_mlp, v_final_norm, v_fox_b_f, v_ada_b), rw)
    src_rows = [0, 1, 2, 3, 4, 6] + [8 + r for r in range(L * NM)] + [7] * (rw - n_small)
    res_s = _adamw_small(w_s, m_s, v_s, small_all, rows=tuple(src_rows), name="adamw_small")
    for kind, r in zip(("grad", "delta", "new_m", "new_v"), res_s):
        outs[kind + "_norm_mix"] = r[0:L]
        outs[kind + "_norm_mlp"] = r[L:2 * L]
        outs[kind + "_final_norm"] = r[2 * L]
        outs[kind + "_fox_b_f"] = r[2 * L + 1:2 * L + 2, :H]
        outs[kind + "_ada_b"] = r[2 * L + 2:n_small].reshape(L, NM * D)

    dmod_all = small_all[:, 8:8 + L * NM, :].reshape(N_DEV, L, N_CHIPS, Ns_ada)
    dmod4 = jnp.transpose(dmod_all, (2, 1, 0, 3))
    act_t = jnp.transpose(c_act)
    res_a = _adamw_ada(ada_w, m_ada_w, v_ada_w, act_t, dmod4, name="adamw_ada")
    put("ada_w", res_a, ada_w.shape)

    r0 = 8 + L * NM
    dconv_all = jnp.pad(small_all[:, r0:r0 + 3, :], ((0, 0), (0, 5), (0, 0)))
    dconv4 = jnp.transpose(dconv_all.reshape(N_DEV, 8, N_CHIPS, Cs_conv), (2, 0, 1, 3))
    res_c = _adamw_conv_w(_pad_rows(conv_w[0], 8), _pad_rows(m_conv_w[0], 8), _pad_rows(v_conv_w[0], 8), dconv4,
                          name="adamw_conv_w")
    for kind, r in zip(("grad", "delta", "new_m", "new_v"), res_c):
        outs[kind + "_conv_w"] = r[:3].reshape(conv_w.shape)

    g_fox = _split_wait("sibling", sb4, [res_a[0], res_c[0], res_s[0]], name="sibling_wait_fox")
    put("fox_w_out", shards_update("fox_out", fox_w_out, m_fox_w_out, v_fox_w_out, g_fox[0:1]), fox_w_out.shape)
    t3 = lambda a: jnp.transpose(a, (0, 2, 1))
    for kind, r in zip(("grad", "delta", "new_m", "new_v"),
                       shards_update("fox_in", t3(fox_w_in), t3(m_fox_w_in), t3(v_fox_w_in), g_fox[1:2])):
        outs[kind + "_fox_w_in"] = t3(r)

    names = ["ada_w", "ada_b", "norm_mix", "norm_mlp", "fox_w_in", "fox_b_f", "fox_w_out", "conv_w_in", "conv_w", "conv_w_out",
             "mlp_w_up", "mlp_w_down", "final_norm"]
    return (loss, grad_x[None], *[outs["grad_" + n] for n in names], *[outs["delta_" + n] for n in names],
            *[outs["new_m_" + n] for n in names], *[outs["new_v_" + n] for n in names])
```

```python
import functools

import jax
import jax.numpy as jnp
from jax import lax
from jax.experimental import pallas as pl
from jax.experimental.pallas import tpu as pltpu

F32 = jnp.float32
BF16 = jnp.bfloat16
MESH = pl.DeviceIdType.MESH
ANY = pl.BlockSpec(memory_space=pl.ANY)
HBM = pl.BlockSpec(memory_space=pltpu.HBM)
SEM = pl.BlockSpec(memory_space=pltpu.SEMAPHORE)
EFFECT = pltpu.SideEffectType.DATAFLOW_SIDE_EFFECTING

RMS_EPS = 1e-6
ADAM_LR = 0.001
ADAM_B1 = 0.9
ADAM_B2 = 0.999
ADAM_EPS = 1e-08
ADAM_WD = 0.01
ADAM_STEP = 10
N_CHIPS = 4
N_DEV = 8
LANES = 128
VMEM_LIMIT = 56 * 1024 * 1024
NEG = -1e30


def _params(sems=None, vmem=VMEM_LIMIT):
    return pltpu.CompilerParams(dimension_semantics=sems, vmem_limit_bytes=vmem)


def _tile(n, pref, unit=LANES):
    if n <= pref:
        return n
    t = (pref // unit) * unit
    while n % t:
        t -= unit
    return t


def _me():
    return lax.axis_index("x"), lax.axis_index("y"), lax.axis_index("c")


def _call(body, deps, **kw):
    nd = len(deps)

    def wrapped(*refs):
        body(*refs[nd:])

    kw["in_specs"] = [ANY] * nd + list(kw["in_specs"])
    fn = pl.pallas_call(wrapped, **kw)
    return lambda *args: fn(*deps, *args)


def _allgather8(v, *, name, deps=()):
    R, C = v.shape

    def body(v_ref, out_ref, send_sems, recv_sems):
        x, y, c = _me()
        me = 4 * x + 2 * y + c
        out_ref[me] = v_ref[...]
        copies = []
        for k in range(1, N_DEV):
            px, py, pc = (x + (k >> 2)) % 2, (y + ((k >> 1) & 1)) % 2, (c + (k & 1)) % 2
            copies.append(pltpu.make_async_remote_copy(
                src_ref=v_ref, dst_ref=out_ref.at[me], send_sem=send_sems.at[k - 1], recv_sem=recv_sems.at[k - 1],
                device_id=(px, py, pc), device_id_type=MESH))
        for cp in copies:
            cp.start()
        for k in range(1, N_DEV):
            px, py, pc = (x + (k >> 2)) % 2, (y + ((k >> 1) & 1)) % 2, (c + (k & 1)) % 2
            peer = 4 * px + 2 * py + pc
            pltpu.make_async_remote_copy(
                src_ref=v_ref, dst_ref=out_ref.at[peer], send_sem=send_sems.at[k - 1], recv_sem=recv_sems.at[k - 1],
                device_id=(px, py, pc), device_id_type=MESH).wait_recv()
        for cp in copies:
            cp.wait_send()

    return _call(
        body, deps, name=name,
        out_shape=jax.ShapeDtypeStruct((N_DEV, R, C), v.dtype),
        in_specs=[pl.BlockSpec(memory_space=pltpu.VMEM)],
        out_specs=pl.BlockSpec(memory_space=pltpu.VMEM),
        scratch_shapes=[pltpu.SemaphoreType.DMA((N_DEV - 1,)), pltpu.SemaphoreType.DMA((N_DEV - 1,))],
        compiler_params=_params(),
    )(v)


def _chip_peers(x, y):
    return [((x + (k >> 1)) % 2, (y + (k & 1)) % 2) for k in range(1, N_CHIPS)]


def _slot(x, y, k):
    return 2 * ((x + (k >> 1)) % 2) + (y + (k & 1)) % 2


def _split_copies(kind, groups, send_sems, recv_sems):
    x, y, c = _me()
    j = 2 * x + y
    copies = []
    for a, g in enumerate(groups):
        if kind == "sibling":
            parts, land, sib = g
            for k in range(N_CHIPS):
                s = _slot(x, y, k)
                copies.append(pltpu.make_async_remote_copy(
                    src_ref=(parts if k == 0 else land).at[s], dst_ref=sib.at[s], send_sem=send_sems.at[N_CHIPS * a + k],
                    recv_sem=recv_sems.at[N_CHIPS * a + k], device_id=(x, y, 1 - c), device_id_type=MESH))
            continue
        land = g[-1]
        for k, (px, py) in enumerate(_chip_peers(x, y)):
            if kind == "gather1":
                src, dst, to = land.at[j, c], land.at[j, c], (px, py, c)
            elif kind == "gather2":
                src, dst, to = land.at[2 * px + py, c], land.at[2 * px + py, c], (x, y, 1 - c)
            else:
                src, dst, to = g[0].at[2 * px + py], land.at[j], (px, py, c)
            copies.append(pltpu.make_async_remote_copy(
                src_ref=src, dst_ref=dst, send_sem=send_sems.at[3 * a + k], recv_sem=recv_sems.at[3 * a + k],
                device_id=to, device_id_type=MESH))
    return copies


def _split_start(kind, groups, *, name, dep=None):
    flat = [a for g in groups for a in g]
    nf, per = len(flat), len(groups[0])
    ncp = len(groups) * (N_CHIPS if kind == "sibling" else 3)
    nd = 0 if dep is None else 1

    def body(*refs):
        ins = refs[nd:nd + nf]
        send_sems, recv_sems, token = refs[nd + nf], refs[nd + nf + 1], refs[-1]
        for cp in _split_copies(kind, [ins[i:i + per] for i in range(0, nf, per)], send_sems, recv_sems):
            cp.start()
        token[...] = jnp.zeros_like(token)

    outs = pl.pallas_call(
        body, name=name,
        out_shape=(pltpu.SemaphoreType.DMA((ncp,)), pltpu.SemaphoreType.DMA((ncp,)), *[pltpu.HBM(a.shape, a.dtype) for a in flat],
                   jax.ShapeDtypeStruct((8, LANES), F32)),
        in_specs=[ANY] * nd + [HBM] * nf,
        out_specs=(SEM, SEM, *[HBM] * nf, pl.BlockSpec(memory_space=pltpu.VMEM)),
        input_output_aliases={nd + i: 2 + i for i in range(nf)},
        compiler_params=pltpu.CompilerParams(has_side_effects=EFFECT),
    )(*([dep] if nd else []), *[pltpu.with_memory_space_constraint(a, pltpu.HBM) for a in flat])
    thru = list(outs[2:2 + nf])
    return outs[0], outs[1], [tuple(thru[i:i + per]) for i in range(0, nf, per)], outs[-1]


def _split_wait(kind, started, after, *, name):
    send_sems, recv_sems, groups, _ = started
    flat = [a for g in groups for a in g]
    nf, per = len(flat), len(groups[0])

    def body(*refs):
        ins = refs[:nf]
        for cp in _split_copies(kind, [ins[i:i + per] for i in range(0, nf, per)], refs[nf], refs[nf + 1]):
            cp.wait_send()
            cp.wait_recv()

    outs = pl.pallas_call(
        body, name=name,
        out_shape=tuple(pltpu.HBM(a.shape, a.dtype) for a in flat),
        in_specs=[HBM] * nf + [SEM, SEM] + [ANY] * len(after), out_specs=tuple([HBM] * nf),
        input_output_aliases={i: i for i in range(nf)},
        compiler_params=pltpu.CompilerParams(has_side_effects=EFFECT),
    )(*flat, send_sems, recv_sems, *after)
    outs = list(outs)
    return [tuple(outs[i:i + per]) for i in range(0, nf, per)]


def _place_cast(shards, layer, chip, *, halves, name):
    _, R, C = shards.shape
    if halves == "rows":
        hr, hc = R // 2, C
    else:
        hr, hc = R, C // 2
    if hr % 16 == 0:
        tr, tc = _tile(hr, 512, 16), hc
    else:
        tr, tc = hr, _tile(hc, 256)
    nr, nc = hr // tr, hc // tc

    def body(chip_ref, x_ref, o_ref):
        o_ref[...] = x_ref[...].astype(BF16)

    if halves == "rows":
        o_map = lambda i, j, chip_ref: (chip_ref[0], i // nr, i % nr, j)
    else:
        o_map = lambda i, j, chip_ref: (chip_ref[0], j // nc, i, j % nc)
    return pl.pallas_call(
        body, name=name,
        grid_spec=pltpu.PrefetchScalarGridSpec(
            num_scalar_prefetch=1, grid=(R // tr, C // tc),
            in_specs=[pl.BlockSpec((None, tr, tc), lambda i, j, chip_ref: (layer, i, j))],
            out_specs=pl.BlockSpec((None, None, tr, tc), o_map)),
        out_shape=jax.ShapeDtypeStruct((N_CHIPS, 2, hr, hc), BF16), compiler_params=_params(("parallel", "parallel")),
    )(chip, shards)


def _accumulate(part, acc_ref, nk, finalize):
    if nk == 1:
        finalize(part)
        return
    k = pl.program_id(2)

    @pl.when(k == 0)
    def _():
        acc_ref[...] = part

    @pl.when(k > 0)
    def _():
        acc_ref[...] += part

    @pl.when(k == nk - 1)
    def _():
        finalize(acc_ref[...])


def _mm_nn(a, b, *, name, epilogue="plain", res=None, gate=None, out_dtype=BF16, tm=1024, tn=1024, tk=2048, deps=()):
    if a.ndim == 3:
        Q, M, Kq = a.shape
        K = Q * Kq
    else:
        (M, K), Kq = a.shape, a.shape[1]
    tm, tk = _tile(M, tm, 16), _tile(Kq, tk)
    if a.ndim == 3:
        pa = Kq // tk
        a_spec = pl.BlockSpec((None, tm, tk), lambda i, j, k: (k // pa, i, k % pa))
    else:
        a_spec = pl.BlockSpec((tm, tk), lambda i, j, k: (i, k))
    if b.ndim == 3:
        P, _, Ns = b.shape
        N = P * Ns
        tn = _tile(Ns, tn)
        per = Ns // tn
        b_spec = pl.BlockSpec((None, tk, tn), lambda i, j, k: (j // per, k, j % per))
    else:
        N = b.shape[1]
        tn = _tile(N, tn)
        b_spec = pl.BlockSpec((tk, tn), lambda i, j, k: (k, j))
    nk = K // tk
    tile = pl.BlockSpec((tm, tn), lambda i, j, k: (i, j))

    def body(*refs):
        acc_ref = refs[-1] if nk > 1 else None
        a_ref, b_ref = refs[0], refs[1]
        part = jnp.dot(a_ref[...], b_ref[...], preferred_element_type=F32)
        if epilogue == "plain":
            def fin(acc):
                refs[2][...] = acc.astype(out_dtype)
        elif epilogue == "relu2":
            def fin(acc):
                refs[2][...] = acc.astype(BF16)
                refs[3][...] = jnp.square(jnp.maximum(acc, 0.0)).astype(BF16)
        elif epilogue == "add":
            def fin(acc):
                refs[3][...] = (acc + refs[2][...]).astype(out_dtype)
        else:
            def fin(acc):
                refs[4][...] = refs[2][...] + refs[3][...] * acc
                refs[5][...] = acc.astype(BF16)
        _accumulate(part, acc_ref, nk, fin)

    in_specs = [a_spec, b_spec]
    args = [a, b]
    if epilogue == "plain":
        out_shape, out_specs = jax.ShapeDtypeStruct((M, N), out_dtype), tile
    elif epilogue == "relu2":
        out_shape, out_specs = [jax.ShapeDtypeStruct((M, N), BF16)] * 2, [tile, tile]
    elif epilogue == "add":
        in_specs.append(tile)
        args.append(res)
        out_shape, out_specs = jax.ShapeDtypeStruct((M, N), out_dtype), tile
    else:
        in_specs += [tile, pl.BlockSpec((1, tn), lambda i, j, k: (0, j))]
        args += [res, gate]
        out_shape, out_specs = [jax.ShapeDtypeStruct((M, N), F32), jax.ShapeDtypeStruct((M, N), BF16)], [tile, tile]
    return _call(
        body, deps, name=name, grid=(M // tm, N // tn, nk), in_specs=in_specs, out_specs=out_specs, out_shape=out_shape,
        scratch_shapes=[pltpu.VMEM((tm, tn), F32)] if nk > 1 else [],
        compiler_params=_params(("parallel", "parallel", "arbitrary")),
    )(*args)


def _mm_nt(a, b, *, name, n=None, epilogue="plain", extra=None, out_dtype=F32, tm=1024, tn=1024, tk=2048, deps=()):
    if a.ndim == 3:
        Q, M, Kq = a.shape
        K = Q * Kq
    else:
        (M, K), Kq = a.shape, a.shape[1]
    if b.ndim == 3:
        P, N, Ks = b.shape
    else:
        N, Ks = b.shape
    N = n or N
    tm, tn, tk = _tile(M, tm, 16), _tile(N, tn), _tile(min(Kq, Ks), tk)
    nk = K // tk
    if a.ndim == 3:
        pa = Kq // tk
        a_spec = pl.BlockSpec((None, tm, tk), lambda i, j, k: (k // pa, i, k % pa))
    else:
        a_spec = pl.BlockSpec((tm, tk), lambda i, j, k: (i, k))
    if b.ndim == 3:
        pb = Ks // tk
        b_spec = pl.BlockSpec((None, tn, tk), lambda i, j, k: (k // pb, j, k % pb))
    else:
        b_spec = pl.BlockSpec((tn, tk), lambda i, j, k: (j, k))
    tile = pl.BlockSpec((tm, tn), lambda i, j, k: (i, j))

    def body(*refs):
        acc_ref = refs[-1] if nk > 1 else None
        part = lax.dot_general(refs[0][...], refs[1][...], (((1,), (1,)), ((), ())), preferred_element_type=F32)
        if epilogue == "plain":
            def fin(acc):
                refs[2][...] = acc.astype(out_dtype)
        elif epilogue == "add":
            def fin(acc):
                refs[3][...] = (acc + refs[2][...]).astype(out_dtype)
        else:
            def fin(acc):
                refs[3][...] = (acc * (2.0 * jnp.maximum(refs[2][...].astype(F32), 0.0))).astype(out_dtype)
        _accumulate(part, acc_ref, nk, fin)

    in_specs, args = [a_spec, b_spec], [a, b]
    if epilogue != "plain":
        in_specs.append(tile)
        args.append(extra)
    return _call(
        body, deps, name=name, grid=(M // tm, N // tn, nk), in_specs=in_specs, out_specs=tile,
        out_shape=jax.ShapeDtypeStruct((M, N), out_dtype),
        scratch_shapes=[pltpu.VMEM((tm, tn), F32)] if nk > 1 else [],
        compiler_params=_params(("parallel", "parallel", "arbitrary")),
    )(*args)


def _mm_tn(a, b, *, name, out_parts=1, tm=1024, tn=1024, tk=4096, deps=()):
    if a.ndim == 3:
        Qa, M, Kq = a.shape
        Kd = Qa * Kq
    else:
        (M, Kd), Kq = a.shape, a.shape[1]
    if b.ndim == 3:
        Q, _, Nq = b.shape
        N = Q * Nq
    else:
        N, Nq = b.shape[1], b.shape[1]
    Ns = N // out_parts
    tn = _tile(Ns, tn)
    while Nq % tn or Ns % tn:
        tn -= LANES
    tm, tk = _tile(Kq, tm), _tile(M, tk, 16)
    nk = M // tk
    if a.ndim == 3:
        pa = Kq // tm
        a_spec = pl.BlockSpec((None, tk, tm), lambda i, j, k: (i // pa, k, i % pa))
    else:
        a_spec = pl.BlockSpec((tk, tm), lambda i, j, k: (k, i))
    if b.ndim == 3:
        pb = Nq // tn
        b_spec = pl.BlockSpec((None, tk, tn), lambda i, j, k: (j // pb, k, j % pb))
    else:
        b_spec = pl.BlockSpec((tk, tn), lambda i, j, k: (k, j))
    if out_parts > 1:
        po = Ns // tn
        o_spec = pl.BlockSpec((None, tm, tn), lambda i, j, k: (j // po, i, j % po))
        out_shape = jax.ShapeDtypeStruct((out_parts, Kd, Ns), BF16)
    else:
        o_spec = pl.BlockSpec((tm, tn), lambda i, j, k: (i, j))
        out_shape = jax.ShapeDtypeStruct((Kd, N), BF16)

    def body(*refs):
        acc_ref = refs[-1] if nk > 1 else None
        part = lax.dot_general(refs[0][...], refs[1][...], (((0,), (0,)), ((), ())), preferred_element_type=F32)

        def fin(acc):
            refs[2][...] = acc.astype(BF16)
        _accumulate(part, acc_ref, nk, fin)

    return _call(
        body, deps, name=name, grid=(Kd // tm, N // tn, nk),
        in_specs=[a_spec, b_spec], out_specs=o_spec, out_shape=out_shape,
        scratch_shapes=[pltpu.VMEM((tm, tn), F32)] if nk > 1 else [],
        compiler_params=_params(("parallel", "parallel", "arbitrary")),
    )(a, b)


def _rows(S, D, i_map=lambda i: (i, 0), ts=512):
    return pl.BlockSpec((ts, D), i_map)


def _norm_fwd(x, gain, sc, sh, *, name, deps=()):
    S, D = x.shape
    ts = _tile(S, 512, 16)
    vec = pl.BlockSpec((1, D), lambda i: (0, 0))

    def body(x_ref, g_ref, sc_ref, sh_ref, h_ref):
        xv = x_ref[...]
        r = lax.rsqrt(jnp.mean(xv * xv, axis=-1, keepdims=True) + RMS_EPS)
        h = (xv * r) * g_ref[...]
        h_ref[...] = (h * (1.0 + sc_ref[...]) + sh_ref[...]).astype(BF16)

    return _call(
        body, deps, name=name, grid=(S // ts,), in_specs=[_rows(S, D, ts=ts), vec, vec, vec], out_specs=_rows(S, D, ts=ts),
        out_shape=jax.ShapeDtypeStruct((S, D), BF16), compiler_params=_params(("parallel",)),
    )(x, gain, sc, sh)


def _loss_bwd(x, target, gain, gate_prev, *, name, deps=()):
    S, D = x.shape
    ts = _tile(S, 256, 16)
    vec = pl.BlockSpec((1, D), lambda i: (0, 0))

    def body(x_ref, t_ref, g_ref, gp_ref, dx_ref, dp_ref, sums_ref):
        @pl.when(pl.program_id(0) == 0)
        def _():
            sums_ref[...] = jnp.zeros_like(sums_ref)
        xv = x_ref[...]
        r = lax.rsqrt(jnp.mean(xv * xv, axis=-1, keepdims=True) + RMS_EPS)
        xn = xv * r
        err = xn * g_ref[...] - t_ref[...]
        loss = 0.5 * jnp.sum(jnp.mean(err * err, axis=-1, keepdims=True), axis=0, keepdims=True)
        dy = err * (1.0 / D)
        dxn = dy * g_ref[...]
        dx = r * (dxn - xn * jnp.mean(dxn * xn, axis=-1, keepdims=True))
        dx_ref[...] = dx
        dp_ref[...] = (gp_ref[...] * dx).astype(BF16)
        sums_ref[0:1, :] += jnp.sum(dy * xn, axis=0, keepdims=True)
        sums_ref[1:2, :] += jnp.broadcast_to(loss, (1, D))

    return _call(
        body, deps, name=name, grid=(S // ts,),
        in_specs=[_rows(S, D, ts=ts), _rows(S, D, ts=ts), vec, vec],
        out_specs=[_rows(S, D, ts=ts), _rows(S, D, ts=ts), pl.BlockSpec((8, D), lambda i: (0, 0))],
        out_shape=[jax.ShapeDtypeStruct((S, D), F32), jax.ShapeDtypeStruct((S, D), BF16), jax.ShapeDtypeStruct((8, D), F32)],
        compiler_params=_params(("arbitrary",)),
    )(x, target, gain, gate_prev)


def _norm_bwd(x, dh, dxp, mix, gain, sc, gate_prev, *, name, deps=()):
    S, D = x.shape
    ts = _tile(S, 256, 16)
    vec = pl.BlockSpec((1, D), lambda i: (0, 0))
    with_prev = gate_prev is not None

    def body(*refs):
        x_ref, dh_ref, dxp_ref, mix_ref, g_ref, sc_ref = refs[:6]
        outs = refs[7:] if with_prev else refs[6:]
        sums_ref = outs[-1]

        @pl.when(pl.program_id(0) == 0)
        def _():
            sums_ref[...] = jnp.zeros_like(sums_ref)
        xv, dhv, dxpv = x_ref[...], dh_ref[...], dxp_ref[...]
        r = lax.rsqrt(jnp.mean(xv * xv, axis=-1, keepdims=True) + RMS_EPS)
        xn = xv * r
        hn = xn * g_ref[...]
        dhn = dhv * (1.0 + sc_ref[...])
        dxn = dhn * g_ref[...]
        dx = dxpv + r * (dxn - xn * jnp.mean(dxn * xn, axis=-1, keepdims=True))
        outs[0][...] = dx
        if with_prev:
            outs[1][...] = (refs[6][...] * dx).astype(BF16)
        sums_ref[0:1, :] += jnp.sum(dhv, axis=0, keepdims=True)
        sums_ref[1:2, :] += jnp.sum(dhv * hn, axis=0, keepdims=True)
        sums_ref[2:3, :] += jnp.sum(dhn * xn, axis=0, keepdims=True)
        sums_ref[3:4, :] += jnp.sum(dxpv * mix_ref[...].astype(F32), axis=0, keepdims=True)

    tile = _rows(S, D, ts=ts)
    in_specs = [tile, tile, tile, tile, vec, vec] + ([vec] if with_prev else [])
    args = [x, dh, dxp, mix, gain, sc] + ([gate_prev] if with_prev else [])
    out_specs = [tile] + ([tile] if with_prev else []) + [pl.BlockSpec((8, D), lambda i: (0, 0))]
    out_shape = ([jax.ShapeDtypeStruct((S, D), F32)] + ([jax.ShapeDtypeStruct((S, D), BF16)] if with_prev else [])
                 + [jax.ShapeDtypeStruct((8, D), F32)])
    outs = _call(
        body, deps, name=name, grid=(S // ts,), in_specs=in_specs, out_specs=out_specs, out_shape=out_shape,
        compiler_params=_params(("arbitrary",)),
    )(*args)
    return (outs[0], outs[1], outs[2]) if with_prev else (outs[0], None, outs[1])


def _fgate_fwd(h, wf, bf, *, name, deps=()):
    S, D = h.shape
    ts = _tile(S, 256, 16)

    def body(h_ref, w_ref, b_ref, z_ref, f_ref, carry):
        @pl.when(pl.program_id(0) == 0)
        def _():
            carry[...] = jnp.zeros_like(carry)
        z = lax.dot_general(h_ref[...], w_ref[...], (((1,), (1,)), ((), ())), preferred_element_type=F32) + b_ref[...]
        logf = jnp.minimum(z, 0.0) - jnp.log(1.0 + jnp.exp(-jnp.abs(z)))
        row = lax.broadcasted_iota(jnp.int32, (ts, ts), 0)
        col = lax.broadcasted_iota(jnp.int32, (ts, ts), 1)
        tril = (col <= row).astype(F32)
        run = jnp.dot(tril, logf, preferred_element_type=F32, precision=lax.Precision.HIGHEST) + carry[0:1, :]
        z_ref[...] = z
        f_ref[...] = run
        carry[0:1, :] = run[ts - 1:ts, :]

    return _call(
        body, deps, name=name, grid=(S // ts,),
        in_specs=[pl.BlockSpec((ts, D), lambda i: (i, 0)), pl.BlockSpec((LANES, D), lambda i: (0, 0)),
                  pl.BlockSpec((1, LANES), lambda i: (0, 0))],
        out_specs=[pl.BlockSpec((ts, LANES), lambda i: (i, 0))] * 2,
        out_shape=[jax.ShapeDtypeStruct((S, LANES), F32)] * 2,
        scratch_shapes=[pltpu.VMEM((8, LANES), F32)],
        compiler_params=_params(("arbitrary",)),
    )(h, wf, bf)


def _fgate_bwd(dfq, dfk, z, *, name, deps=()):
    S = z.shape[0]
    ts = _tile(S, 256, 16)
    n = S // ts

    def body(dq_ref, dk_ref, z_ref, dz_ref, sums_ref, carry):
        @pl.when(pl.program_id(0) == 0)
        def _():
            carry[...] = jnp.zeros_like(carry)
            sums_ref[...] = jnp.zeros_like(sums_ref)
        df = dq_ref[...] - dk_ref[...]
        row = lax.broadcasted_iota(jnp.int32, (ts, ts), 0)
        col = lax.broadcasted_iota(jnp.int32, (ts, ts), 1)
        triu = (col >= row).astype(F32)
        run = jnp.dot(triu, df, preferred_element_type=F32, precision=lax.Precision.HIGHEST) + carry[0:1, :]
        zv = z_ref[...]
        dz = run * (1.0 / (1.0 + jnp.exp(zv)))
        dz_ref[...] = dz.astype(BF16)
        sums_ref[0:1, :] += jnp.sum(dz, axis=0, keepdims=True)
        carry[0:1, :] = run[0:1, :]

    rev = pl.BlockSpec((ts, LANES), lambda i: (n - 1 - i, 0))
    return _call(
        body, deps, name=name, grid=(n,), in_specs=[rev, rev, rev],
        out_specs=[rev, pl.BlockSpec((8, LANES), lambda i: (0, 0))],
        out_shape=[jax.ShapeDtypeStruct((S, LANES), BF16), jax.ShapeDtypeStruct((8, LANES), F32)],
        scratch_shapes=[pltpu.VMEM((8, LANES), F32)],
        compiler_params=_params(("arbitrary",)),
    )(dfq, dfk, z)


def _head_col(ref, rows, lane_mask):
    return jnp.sum(jnp.where(lane_mask, ref[rows, :], 0.0), axis=1, keepdims=True)


def _attn_fwd(qkv, fq, fk, *, heads, name, T=256, deps=()):
    S, D3 = qkv.shape
    D = D3 // 3
    dh = D // heads
    T = _tile(S, T, 16)
    nq = S // T
    scale = dh ** -0.5
    hp = fk.shape[1]

    def body(q_ref, k_ref, v_ref, fq_ref, fk_ref, o_ref, lse_ref):
        h = pl.program_id(0)

        @pl.when(h == 0)
        def _():
            lse_ref[...] = jnp.zeros_like(lse_ref)
        lane = lax.broadcasted_iota(jnp.int32, (1, LANES), 1) == h
        row = lax.broadcasted_iota(jnp.int32, (T, T), 0)
        col = lax.broadcasted_iota(jnp.int32, (T, T), 1)

        def q_block(qi, _):
            rows = pl.ds(pl.multiple_of(qi * T, T), T)
            q = q_ref[rows, :]
            fq_col = _head_col(fq_ref, rows, lane)

            def kv_block(kj, carry, diag):
                m, l, acc = carry
                cols = pl.ds(pl.multiple_of(kj * T, T), T)
                s = lax.dot_general(q, k_ref[cols, :], (((1,), (1,)), ((), ())), preferred_element_type=F32) * scale
                s = s + (fq_col - fk_ref[kj, pl.ds(h, 1), :])
                if diag:
                    s = jnp.where(col <= row, s, NEG)
                m_new = jnp.maximum(m, jnp.max(s, axis=1, keepdims=True))
                p = jnp.exp(s - m_new)
                alpha = jnp.exp(m - m_new)
                l = alpha * l + jnp.sum(p, axis=1, keepdims=True)
                acc = alpha * acc + jnp.dot(p.astype(BF16), v_ref[cols, :], preferred_element_type=F32)
                return m_new, l, acc

            init = (jnp.full((T, 1), NEG, F32), jnp.zeros((T, 1), F32), jnp.zeros((T, dh), F32))
            carry = lax.fori_loop(0, qi, lambda kj, cr: kv_block(kj, cr, False), init)
            m, l, acc = kv_block(qi, carry, True)
            o_ref[rows, :] = (acc / l).astype(BF16)
            lse_ref[rows, :] = jnp.where(lane, m + jnp.log(l), lse_ref[rows, :])
            return 0

        lax.fori_loop(0, nq, q_block, 0)

    head = lambda part: pl.BlockSpec((S, dh), lambda h: (0, part * heads + h))
    return _call(
        body, deps, name=name, grid=(heads,),
        in_specs=[head(0), head(1), head(2), pl.BlockSpec((S, LANES), lambda h: (0, 0)),
                  pl.BlockSpec((nq, hp, T), lambda h: (0, 0, 0))],
        out_specs=[pl.BlockSpec((S, dh), lambda h: (0, h)), pl.BlockSpec((S, LANES), lambda h: (0, 0))],
        out_shape=[jax.ShapeDtypeStruct((S, D), BF16), jax.ShapeDtypeStruct((S, LANES), F32)],
        compiler_params=_params(("arbitrary",)),
    )(qkv, qkv, qkv, fq, fk)


def _attn_bwd(qkv, o, do, fq, fk, lse, *, heads, name, T=256, deps=()):
    S, D3 = qkv.shape
    D = D3 // 3
    dh = D // heads
    T = _tile(S, T, 16)
    nq = S // T
    scale = dh ** -0.5
    hp = fk.shape[1]

    def body(q_ref, k_ref, v_ref, o_ref, do_ref, fq_ref, fk_ref, lse_ref, dqkv_ref, dfq_ref, dfk_ref,
             dq_acc, fq_col, lse_col, delta_col, dfq_col):
        h = pl.program_id(0)

        @pl.when(h == 0)
        def _():
            dfq_ref[...] = jnp.zeros_like(dfq_ref)
            dfk_ref[...] = jnp.zeros_like(dfk_ref)
        lane = lax.broadcasted_iota(jnp.int32, (1, LANES), 1) == h
        row = lax.broadcasted_iota(jnp.int32, (T, T), 0)
        col = lax.broadcasted_iota(jnp.int32, (T, T), 1)
        dq_acc[...] = jnp.zeros_like(dq_acc)
        dfq_col[...] = jnp.zeros_like(dfq_col)

        def prep(qi, _):
            rows = pl.ds(pl.multiple_of(qi * T, T), T)
            fq_col[rows, :] = _head_col(fq_ref, rows, lane)
            lse_col[rows, :] = _head_col(lse_ref, rows, lane)
            delta_col[rows, :] = jnp.sum(do_ref[rows, :].astype(F32) * o_ref[rows, :].astype(F32), axis=1, keepdims=True)
            return 0

        lax.fori_loop(0, nq, prep, 0)

        def kv_block(kj, _):
            cols = pl.ds(pl.multiple_of(kj * T, T), T)
            k, v = k_ref[cols, :], v_ref[cols, :]
            fk_row = fk_ref[kj, pl.ds(h, 1), :]

            def q_block(qi, carry, diag):
                dk, dv, dfk = carry
                rows = pl.ds(pl.multiple_of(qi * T, T), T)
                q, dov = q_ref[rows, :], do_ref[rows, :]
                s = lax.dot_general(q, k, (((1,), (1,)), ((), ())), preferred_element_type=F32) * scale
                s = s + (fq_col[rows, :] - fk_row)
                p = jnp.exp(s - lse_col[rows, :])
                if diag:
                    p = jnp.where(col <= row, p, 0.0)
                dp = lax.dot_general(dov, v, (((1,), (1,)), ((), ())), preferred_element_type=F32)
                ds = p * (dp - delta_col[rows, :])
                dsb = ds.astype(BF16)
                dv = dv + lax.dot_general(p.astype(BF16), dov, (((0,), (0,)), ((), ())), preferred_element_type=F32)
                dk = dk + lax.dot_general(dsb, q, (((0,), (0,)), ((), ())), preferred_element_type=F32)
                dq_acc[rows, :] += jnp.dot(dsb, k, preferred_element_type=F32)
                dfq_col[rows, :] += jnp.sum(ds, axis=1, keepdims=True)
                dfk = dfk + jnp.sum(ds, axis=0, keepdims=True)
                return dk, dv, dfk

            init = (jnp.zeros((T, dh), F32), jnp.zeros((T, dh), F32), jnp.zeros((1, T), F32))
            carry = q_block(kj, init, True)
            dk, dv, dfk = lax.fori_loop(kj + 1, nq, lambda qi, cr: q_block(qi, cr, False), carry)
            dqkv_ref[1, cols, :] = (dk * scale).astype(BF16)
            dqkv_ref[2, cols, :] = dv.astype(BF16)
            dfk_ref[kj, pl.ds(h, 1), :] = dfk
            return 0

        lax.fori_loop(0, nq, kv_block, 0)

        def finish(qi, _):
            rows = pl.ds(pl.multiple_of(qi * T, T), T)
            dqkv_ref[0, rows, :] = (dq_acc[rows, :] * scale).astype(BF16)
            dfq_ref[rows, :] = jnp.where(lane, dfq_col[rows, :], dfq_ref[rows, :])
            return 0

        lax.fori_loop(0, nq, finish, 0)

    head = lambda part: pl.BlockSpec((S, dh), lambda h: (0, part * heads + h))
    own = pl.BlockSpec((S, dh), lambda h: (0, h))
    full = pl.BlockSpec((S, LANES), lambda h: (0, 0))
    krow = pl.BlockSpec((nq, hp, T), lambda h: (0, 0, 0))
    return _call(
        body, deps, name=name, grid=(heads,),
        in_specs=[head(0), head(1), head(2), own, own, full, krow, full],
        out_specs=[pl.BlockSpec((3, S, dh), lambda h: (0, 0, h)), full, krow],
        out_shape=[jax.ShapeDtypeStruct((3, S, D), BF16), jax.ShapeDtypeStruct((S, LANES), F32),
                   jax.ShapeDtypeStruct((nq, hp, T), F32)],
        scratch_shapes=[pltpu.VMEM((S, dh), F32)] + [pltpu.VMEM((S, 1), F32)] * 4,
        compiler_params=_params(("arbitrary",)),
    )(qkv, qkv, qkv, o, do, fq, fk, lse)


def _shift_down(v, n):
    rows = lax.broadcasted_iota(jnp.int32, v.shape, 0)
    return jnp.where(rows >= n, pltpu.roll(v, n, axis=0), 0.0)


def _shift_up(v, n):
    S = v.shape[0]
    rows = lax.broadcasted_iota(jnp.int32, v.shape, 0)
    return jnp.where(rows < S - n, pltpu.roll(v, S - n, axis=0), 0.0)


def _conv_fwd(proj, conv_w, *, name, cb=LANES, deps=()):
    S, D3 = proj.shape
    D = D3 // 3
    nb = D // cb

    def body(bg_ref, cg_ref, u_ref, w_ref, z_ref):
        uc = cg_ref[...].astype(F32) * u_ref[...].astype(F32)
        w = w_ref[...]
        y = w[2:3, :] * uc + w[1:2, :] * _shift_down(uc, 1) + w[0:1, :] * _shift_down(uc, 2)
        z_ref[...] = (bg_ref[...].astype(F32) * y).astype(BF16)

    part = lambda g: pl.BlockSpec((S, cb), lambda j: (0, g * nb + j))
    return _call(
        body, deps, name=name, grid=(nb,),
        in_specs=[part(0), part(1), part(2), pl.BlockSpec((3, cb), lambda j: (0, j))],
        out_specs=pl.BlockSpec((S, cb), lambda j: (0, j)),
        out_shape=jax.ShapeDtypeStruct((S, D), BF16), compiler_params=_params(("parallel",)),
    )(proj, proj, proj, conv_w)


def _conv_bwd(proj, conv_w, dz, *, name, cb=LANES, deps=()):
    S, D3 = proj.shape
    D = D3 // 3
    nb = D // cb

    def body(bg_ref, cg_ref, u_ref, w_ref, dz_ref, dp_ref, dw_ref):
        cg, u = cg_ref[...].astype(F32), u_ref[...].astype(F32)
        uc = cg * u
        w = w_ref[...]
        uc1, uc2 = _shift_down(uc, 1), _shift_down(uc, 2)
        y = w[2:3, :] * uc + w[1:2, :] * uc1 + w[0:1, :] * uc2
        dz = dz_ref[...].astype(F32)
        dp_ref[0] = (dz * y).astype(BF16)
        dy = dz * bg_ref[...].astype(F32)
        duc = w[2:3, :] * dy + w[1:2, :] * _shift_up(dy, 1) + w[0:1, :] * _shift_up(dy, 2)
        dp_ref[1] = (duc * u).astype(BF16)
        dp_ref[2] = (duc * cg).astype(BF16)
        dw_ref[...] = jnp.zeros_like(dw_ref)
        dw_ref[0:1, :] = jnp.sum(dy * uc2, axis=0, keepdims=True)
        dw_ref[1:2, :] = jnp.sum(dy * uc1, axis=0, keepdims=True)
        dw_ref[2:3, :] = jnp.sum(dy * uc, axis=0, keepdims=True)

    part = lambda g: pl.BlockSpec((S, cb), lambda j: (0, g * nb + j))
    return _call(
        body, deps, name=name, grid=(nb,),
        in_specs=[part(0), part(1), part(2), pl.BlockSpec((3, cb), lambda j: (0, j)), pl.BlockSpec((S, cb), lambda j: (0, j))],
        out_specs=[pl.BlockSpec((3, S, cb), lambda j: (0, 0, j)), pl.BlockSpec((8, cb), lambda j: (0, j))],
        out_shape=[jax.ShapeDtypeStruct((3, S, D), BF16), jax.ShapeDtypeStruct((8, D), F32)],
        compiler_params=_params(("parallel",)),
    )(proj, proj, proj, conv_w, dz)


def _ada_fwd(c_all, ada_w, *, name, deps=()):
    L, D, Ns = ada_w.shape
    tn = _tile(Ns, 512)

    def body(c_ref, w_ref, o_ref, act_ref):
        cv = c_ref[...]
        act = cv * (1.0 / (1.0 + jnp.exp(-cv)))
        act_ref[...] = act
        o_ref[...] = jnp.dot(act.astype(BF16), w_ref[...].astype(BF16), preferred_element_type=F32)

    return _call(
        body, deps, name=name, grid=(L, Ns // tn),
        in_specs=[pl.BlockSpec((N_DEV, D), lambda l, j: (0, 0)), pl.BlockSpec((None, D, tn), lambda l, j: (l, 0, j))],
        out_specs=[pl.BlockSpec((None, N_DEV, tn), lambda l, j: (l, 0, j)), pl.BlockSpec((N_DEV, D), lambda l, j: (0, 0))],
        out_shape=[jax.ShapeDtypeStruct((L, N_DEV, Ns), F32), jax.ShapeDtypeStruct((N_DEV, D), F32)],
        compiler_params=_params(("arbitrary", "arbitrary")),
    )(c_all, ada_w)


def _select_mod(gathered, *, name, deps=()):
    _, LB, Ns = gathered.shape
    L = LB // N_DEV

    def body(g_ref, o_ref):
        x, y, c = _me()
        b = 4 * x + 2 * y + c
        for j in range(N_CHIPS):
            for l in range(L):
                o_ref[j, pl.ds(l, 1), :] = g_ref[2 * j + c, pl.ds(l * N_DEV + b, 1), :]

    return _call(
        body, deps, name=name, out_shape=jax.ShapeDtypeStruct((N_CHIPS, L, Ns), F32),
        in_specs=[pl.BlockSpec(memory_space=pltpu.VMEM)], out_specs=pl.BlockSpec(memory_space=pltpu.VMEM),
        compiler_params=_params(),
    )(gathered)


def _adamw_math(w, g, m, v):
    m = ADAM_B1 * m + (1.0 - ADAM_B1) * g
    v = ADAM_B2 * v + (1.0 - ADAM_B2) * jnp.square(g)
    m_hat = m / (1.0 - ADAM_B1 ** ADAM_STEP)
    v_hat = v / (1.0 - ADAM_B2 ** ADAM_STEP)
    delta = -ADAM_LR * (m_hat / (jnp.sqrt(v_hat) + ADAM_EPS) + ADAM_WD * w)
    return delta, m, v


def _adamw_shards(w, m, v, groups, chip, *, name):
    L, R, C = w.shape
    if R % 16 == 0:
        tr, tc = _tile(R, 128, 16), C
    else:
        tr, tc = R, _tile(C, 256)
    nr, nc = R // tr, C // tc

    def body(chip_ref, w_ref, m_ref, v_ref, *rest):
        srcs, (g_ref, d_ref, mo_ref, vo_ref) = rest[:2 * N_CHIPS * L], rest[2 * N_CHIPS * L:]
        for l in range(L):
            @pl.when(pl.program_id(0) == l)
            def _():
                s = srcs[2 * N_CHIPS * l:2 * N_CHIPS * (l + 1)]
                mine, other = s[0][...].astype(F32), s[N_CHIPS][...].astype(F32)
                for k in range(1, N_CHIPS):
                    mine = mine + s[k][...].astype(F32)
                    other = other + s[N_CHIPS + k][...].astype(F32)
                g = mine + other
                delta, mn, vn = _adamw_math(w_ref[...], g, m_ref[...], v_ref[...])
                g_ref[...] = g
                d_ref[...] = delta
                mo_ref[...] = mn
                vo_ref[...] = vn

    tile = pl.BlockSpec((None, tr, tc), lambda l, i, j, chip_ref: (l, i, j))

    def block(layer, k):
        def index(l, i, j, chip_ref):
            idle_i, idle_j = jnp.where(l < layer, 0, nr - 1), jnp.where(l < layer, 0, nc - 1)
            return (jnp.bitwise_xor(chip_ref[0], k), jnp.where(l == layer, i, idle_i), jnp.where(l == layer, j, idle_j))
        return pl.BlockSpec((None, tr, tc), index)

    in_specs, args = [tile] * 3, [w, m, v]
    for layer, (parts, land, sib) in enumerate(groups):
        in_specs += [block(layer, k) for k in range(N_CHIPS)] * 2
        args += [parts, land, land, land, sib, sib, sib, sib]
    return pl.pallas_call(
        body, name=name,
        grid_spec=pltpu.PrefetchScalarGridSpec(num_scalar_prefetch=1, grid=(L, nr, nc), in_specs=in_specs, out_specs=[tile] * 4),
        out_shape=[jax.ShapeDtypeStruct((L, R, C), F32)] * 4, compiler_params=_params(("arbitrary", "arbitrary", "arbitrary")),
    )(chip, *args)


def _adamw_ada(w, m, v, act_t, dmod, *, name, tr=256, deps=()):
    L, D, Ns = w.shape
    tr = _tile(D, tr, 8)

    def body(w_ref, m_ref, v_ref, a_ref, d_ref, g_ref, dl_ref, mo_ref, vo_ref):
        x, y, _ = _me()
        g = jnp.dot(a_ref[...], d_ref[2 * x + y], preferred_element_type=F32, precision=lax.Precision.HIGHEST)
        delta, mn, vn = _adamw_math(w_ref[...], g, m_ref[...], v_ref[...])
        g_ref[...] = g
        dl_ref[...] = delta
        mo_ref[...] = mn
        vo_ref[...] = vn

    tile = pl.BlockSpec((None, tr, Ns), lambda l, i: (l, i, 0))
    return _call(
        body, deps, name=name, grid=(L, D // tr),
        in_specs=[tile] * 3 + [pl.BlockSpec((tr, N_DEV), lambda l, i: (i, 0)),
                               pl.BlockSpec((N_CHIPS, None, N_DEV, Ns), lambda l, i: (0, l, 0, 0))],
        out_specs=[tile] * 4, out_shape=[jax.ShapeDtypeStruct((L, D, Ns), F32)] * 4,
        compiler_params=_params(("parallel", "parallel")),
    )(w, m, v, act_t, dmod)


def _adamw_small(w, m, v, gathered, *, rows, name, deps=()):
    n, D = w.shape

    def body(w_ref, m_ref, v_ref, s_ref, g_ref, d_ref, mo_ref, vo_ref):
        for r, src in enumerate(rows):
            g = s_ref[0, src:src + 1, :]
            for d in range(1, N_DEV):
                g = g + s_ref[d, src:src + 1, :]
            g_ref[r:r + 1, :] = g
        g = g_ref[...]
        delta, mn, vn = _adamw_math(w_ref[...], g, m_ref[...], v_ref[...])
        d_ref[...] = delta
        mo_ref[...] = mn
        vo_ref[...] = vn

    vm = pl.BlockSpec(memory_space=pltpu.VMEM)
    return _call(
        body, deps, name=name, in_specs=[vm] * 4, out_specs=[vm] * 4,
        out_shape=[jax.ShapeDtypeStruct((n, D), F32)] * 4, compiler_params=_params(),
    )(w, m, v, gathered)


def _adamw_conv_w(w, m, v, gathered4, *, name, deps=()):
    Cs = w.shape[1]

    def body(w_ref, m_ref, v_ref, s_ref, g_ref, d_ref, mo_ref, vo_ref):
        x, y, _ = _me()
        j = 2 * x + y
        g = s_ref[j, 0]
        for d in range(1, N_DEV):
            g = g + s_ref[j, d]
        delta, mn, vn = _adamw_math(w_ref[...], g, m_ref[...], v_ref[...])
        g_ref[...] = g
        d_ref[...] = delta
        mo_ref[...] = mn
        vo_ref[...] = vn

    vm = pl.BlockSpec(memory_space=pltpu.VMEM)
    return _call(
        body, deps, name=name, in_specs=[vm] * 4, out_specs=[vm] * 4,
        out_shape=[jax.ShapeDtypeStruct((8, Cs), F32)] * 4, compiler_params=_params(),
    )(w, m, v, gathered4)


def _loss_sum(gathered, *, row, name, deps=()):
    _, _, D = gathered.shape

    def body(s_ref, o_ref):
        t = s_ref[0, row:row + 1, :]
        for d in range(1, N_DEV):
            t = t + s_ref[d, row:row + 1, :]
        o_ref[...] = jnp.broadcast_to(t, (8, D))

    vm = pl.BlockSpec(memory_space=pltpu.VMEM)
    return pl.pallas_call(body, name=name, in_specs=[vm], out_specs=vm, out_shape=jax.ShapeDtypeStruct((8, D), F32),
                          compiler_params=_params())(gathered)


def _pad_rows(a, n):
    return jnp.pad(a, ((0, n - a.shape[0]), (0, 0)))


def kernel(x, c, ada_w, ada_b, norm_mix, norm_mlp, fox_w_in, fox_b_f, fox_w_out, conv_w_in, conv_w, conv_w_out, mlp_w_up, mlp_w_down, final_norm, loss_target, m_ada_w, m_ada_b, m_norm_mix, m_norm_mlp, m_fox_w_in, m_fox_b_f, m_fox_w_out, m_conv_w_in, m_conv_w, m_conv_w_out, m_mlp_w_up, m_mlp_w_down, m_final_norm, v_ada_w, v_ada_b, v_norm_mix, v_norm_mlp, v_fox_w_in, v_fox_b_f, v_fox_w_out, v_conv_w_in, v_conv_w, v_conv_w_out, v_mlp_w_up, v_mlp_w_down, v_final_norm):
    S, D = x.shape[1], x.shape[2]
    H = fox_b_f.shape[-1]
    L = ada_w.shape[0]
    NM = ada_b.shape[1] // D
    Ns_ada = ada_w.shape[2]
    Cs_fox = fox_w_in.shape[2]
    Cs_conv = conv_w.shape[2]
    x0 = x[0]
    target = loss_target[0]

    chip = (2 * lax.axis_index("x") + lax.axis_index("y")).astype(jnp.int32).reshape(1)

    fin_t = jnp.transpose(fox_w_in, (0, 2, 1))
    shards = dict(fin=(fin_t, 0), fout=(fox_w_out, 0), up0=(mlp_w_up, 0), dn0=(mlp_w_down, 0), cin=(conv_w_in, 0),
                  cout=(conv_w_out, 0), up1=(mlp_w_up, 1), dn1=(mlp_w_down, 1))
    halves = dict(fin="cols", fout="rows", up0="cols", dn0="rows", cin="cols", cout="rows", up1="cols", dn1="rows")
    gathers = {}

    def start_gather(key, dep=None):
        land = _place_cast(*shards[key], chip, halves=halves[key], name="place_" + key)
        gathers[key] = _split_start("gather1", [(land,)], name="gather_start_" + key, dep=dep)
        return gathers[key][3]

    def pass_gather(key, after):
        landed = _split_wait("gather1", gathers[key], after, name="gather_landed_" + key)
        gathers[key] = _split_start("gather2", landed, name="gather_pass_" + key)
        return gathers[key][3]

    def gathered(key, after):
        return _split_wait("gather2", gathers[key], after, name="gather_wait_" + key)[0][0]

    tok = start_gather("fin")

    c_all = _allgather8(_pad_rows(c, 8), name="gather_c", deps=(tok,))[:, 0, :]
    mod_part, c_act = _ada_fwd(c_all, ada_w, name="ada_fwd")
    mod_all = _allgather8(mod_part.reshape(L * N_DEV, Ns_ada), name="gather_mod")
    mod = _select_mod(mod_all, name="select_mod")
    mod = jnp.transpose(mod, (1, 0, 2)).reshape(L, NM, 1, D) + ada_b.reshape(L, NM, 1, D)
    conv_w_all = _allgather8(_pad_rows(conv_w[0], 8), name="gather_conv_w")
    conv_w_full = jnp.transpose(conv_w_all[0::2, :3, :], (1, 0, 2)).reshape(3, D)

    def vec(a):
        return a.reshape(1, D)

    h0 = _norm_fwd(x0, vec(norm_mix[0]), mod[0, 1], mod[0, 0], name="norm_mix0", deps=(conv_w_all,))
    tok = h0
    for key in ("fout", "up0", "dn0", "cin", "cout", "up1", "dn1"):
        tok = start_gather(key, tok)
    tok = pass_gather("fin", [h0, tok])
    w_fin_t = jnp.transpose(gathered("fin", [tok]), (0, 2, 1, 3)).reshape(N_CHIPS * Cs_fox, D)
    w_f_t = _pad_rows(w_fin_t[3 * D:], LANES)
    tok = pass_gather("fout", [w_fin_t])
    qkv = _mm_nt(h0, w_fin_t, n=3 * D, name="fox_in", out_dtype=BF16, deps=(tok,))
    tok = pass_gather("up0", [qkv])
    b_f = jnp.pad(fox_b_f, ((0, 0), (0, LANES - H)))
    z_f, F_col = _fgate_fwd(h0, w_f_t, b_f, name="fgate_fwd", deps=(tok,))
    hp = max(8, H)
    at_f, at = _tile(S, 1024, 16), _tile(S, 512, 16)
    F_rows = _pad_rows(jnp.transpose(F_col[:, :H]), hp)
    F_row = jnp.transpose(F_rows.reshape(hp, S // at, at), (1, 0, 2))
    o, lse = _attn_fwd(qkv, F_col, jnp.transpose(F_rows.reshape(hp, S // at_f, at_f), (1, 0, 2)), heads=H, name="attn_fwd", T=at_f)
    w_fout = gathered("fout", [o]).reshape(D, D)
    tok = pass_gather("dn0", [o])
    x1, mix0 = _mm_nn(o, w_fout, name="fox_out", epilogue="resid", res=x0, gate=mod[0, 2], tm=512, deps=(tok,))
    h1 = _norm_fwd(x1, vec(norm_mlp[0]), mod[0, 4], mod[0, 3], name="norm_mlp0")
    w_up0 = gathered("up0", [h1]).reshape(2 * N_CHIPS, D, -1)
    tok = pass_gather("cin", [h1])
    u0, a0 = _mm_nn(h1, w_up0, name="mlp_up0", epilogue="relu2", deps=(tok,))
    w_dn0 = gathered("dn0", [a0]).reshape(-1, D)
    tok = pass_gather("cout", [a0])
    x2, y0 = _mm_nn(a0, w_dn0, name="mlp_down0", epilogue="resid", res=x1, gate=mod[0, 5], deps=(tok,))
    h2 = _norm_fwd(x2, vec(norm_mix[1]), mod[1, 1], mod[1, 0], name="norm_mix1")
    g_cin = gathered("cin", [h2]).reshape(2 * N_CHIPS, D, -1)
    tok = pass_gather("up1", [h2])
    proj = _mm_nn(h2, g_cin, name="conv_in", deps=(tok,))
    w_cin = jnp.transpose(g_cin, (1, 0, 2)).reshape(D, 3 * D)
    zc = _conv_fwd(proj, conv_w_full, name="conv_fwd")
    w_cout = gathered("cout", [zc]).reshape(D, D)
    tok = pass_gather("dn1", [zc])
    x3, mix1 = _mm_nn(zc, w_cout, name="conv_out", epilogue="resid", res=x2, gate=mod[1, 2], tm=512, deps=(tok,))
    h3 = _norm_fwd(x3, vec(norm_mlp[1]), mod[1, 4], mod[1, 3], name="norm_mlp1")
    w_up1 = gathered("up1", [h3]).reshape(2 * N_CHIPS, D, -1)
    u1, a1 = _mm_nn(h3, w_up1, name="mlp_up1", epilogue="relu2")
    w_dn1 = gathered("dn1", [a1]).reshape(-1, D)
    x4, y1 = _mm_nn(a1, w_dn1, name="mlp_down1", epilogue="resid", res=x3, gate=mod[1, 5])
    w_up = [jnp.transpose(w_up0, (1, 0, 2)).reshape(D, -1), jnp.transpose(w_up1, (1, 0, 2)).reshape(D, -1)]
    w_dn = [w_dn0, w_dn1]

    dx4, dy1, sums_f = _loss_bwd(x4, target, vec(final_norm), mod[1, 5], name="loss_bwd")
    du1 = _mm_nt(dy1, w_dn[1], name="mlp_down1_dx", epilogue="drelu2", extra=u1, out_dtype=BF16)
    def start_scatter(tag, parts_list):
        groups = [(p, lax.empty(p.shape, p.dtype)) for p in parts_list]
        return _split_start("scatter", groups, name="scatter_start_" + tag)

    def start_sibling(tag, scatter, after):
        landed = _split_wait("scatter", scatter, after, name="scatter_wait_" + tag)
        groups = [(p, ld, lax.empty(p.shape, p.dtype)) for p, ld in landed]
        return _split_start("sibling", groups, name="sibling_start_" + tag)

    gw_dn1 = _mm_tn(a1, dy1, name="mlp_down1_dw")
    gw_up1 = _mm_tn(h3, du1, name="mlp_up1_dw", out_parts=N_CHIPS)
    sc1 = start_scatter("mlp1", [gw_dn1.reshape(N_CHIPS, -1, D), gw_up1])
    dh3 = _mm_nt(du1, w_up[1], name="mlp_up1_dx", deps=(sc1[3],))
    dx3, dmix1, sums_mlp1 = _norm_bwd(x3, dh3, dx4, y1, vec(norm_mlp[1]), mod[1, 4], mod[1, 2], name="norm_mlp1_bwd")
    dzc = _mm_nt(dmix1, w_cout, name="conv_out_dx", out_dtype=BF16)
    gw_cout = _mm_tn(zc, dmix1, name="conv_out_dw")
    dproj, dconv_w = _conv_bwd(proj, conv_w_full, dzc, name="conv_bwd")
    gw_cin = _mm_tn(h2, dproj, name="conv_in_dw", out_parts=N_CHIPS, tn=512)
    sc2 = start_scatter("conv", [gw_cout.reshape(N_CHIPS, -1, D), gw_cin])
    dh2 = _mm_nt(dproj, w_cin, name="conv_in_dx", deps=(sc2[3],))
    dx2, dy0, sums_mix1 = _norm_bwd(x2, dh2, dx3, mix1, vec(norm_mix[1]), mod[1, 1], mod[0, 5], name="norm_mix1_bwd")
    du0 = _mm_nt(dy0, w_dn[0], name="mlp_down0_dx", epilogue="drelu2", extra=u0, out_dtype=BF16)
    gw_dn0 = _mm_tn(a0, dy0, name="mlp_down0_dw")
    gw_up0 = _mm_tn(h1, du0, name="mlp_up0_dw", out_parts=N_CHIPS)
    sc3 = start_scatter("mlp0", [gw_dn0.reshape(N_CHIPS, -1, D), gw_up0])
    sb1 = start_sibling("mlp1", sc1, [sc3[3]])
    dh1 = _mm_nt(du0, w_up[0], name="mlp_up0_dx", deps=(sb1[3],))
    dx1, dmix0, sums_mlp0 = _norm_bwd(x1, dh1, dx2, y0, vec(norm_mlp[0]), mod[0, 4], mod[0, 2], name="norm_mlp0_bwd")
    do = _mm_nt(dmix0, w_fout, name="fox_out_dx", out_dtype=BF16)
    gw_fout = _mm_tn(o, dmix0, name="fox_out_dw")
    dqkv, dfq, dfk = _attn_bwd(qkv, o, do, F_col, F_row, lse, heads=H, name="attn_bwd", T=at)
    dfk_col = jnp.pad(jnp.transpose(jnp.transpose(dfk, (1, 0, 2)).reshape(hp, S)[:H]), ((0, 0), (0, LANES - H)))
    sb3 = start_sibling("mlp0", sc3, [dqkv])
    dz_f, sums_bf = _fgate_bwd(dfq, dfk_col, z_f, name="fgate_bwd", deps=(sb3[3],))
    gw_qkv_t = _mm_tn(dqkv, h0, name="fox_in_dw")
    gw_f_t = _mm_tn(dz_f, h0, name="fox_gate_dw")
    gw_fin_t = jnp.concatenate([gw_qkv_t, gw_f_t[:H]], axis=0).reshape(N_CHIPS, Cs_fox, D)
    sc4 = start_scatter("fox", [gw_fout.reshape(N_CHIPS, -1, D), gw_fin_t])
    sb2 = start_sibling("conv", sc2, [sc4[3]])
    dh0_f = _mm_nn(dz_f, w_f_t, name="fox_gate_dx", out_dtype=F32, deps=(sb2[3],))
    dh0 = _mm_nn(dqkv, w_fin_t, name="fox_in_dx", epilogue="add", res=dh0_f, out_dtype=F32)
    grad_x, _, sums_mix0 = _norm_bwd(x0, dh0, dx1, mix0, vec(norm_mix[0]), mod[0, 1], None, name="norm_mix0_bwd")

    outs = {}

    def put(name_, res, shape):
        for kind, r in zip(("grad", "delta", "new_m", "new_v"), res):
            outs[kind + "_" + name_] = r.reshape(shape)

    def shards_update(tag, w_, m_, v_, groups):
        return _adamw_shards(w_, m_, v_, groups, chip, name="adamw_" + tag)

    g_conv = _split_wait("sibling", sb2, [grad_x], name="sibling_wait_conv")
    put("conv_w_out", shards_update("conv_out", conv_w_out, m_conv_w_out, v_conv_w_out, g_conv[0:1]), conv_w_out.shape)
    r_cin = shards_update("conv_in", conv_w_in, m_conv_w_in, v_conv_w_in, g_conv[1:2])
    put("conv_w_in", r_cin, conv_w_in.shape)
    g_mlp1 = _split_wait("sibling", sb1, [r_cin[0]], name="sibling_wait_mlp1")
    g_mlp0 = _split_wait("sibling", sb3, [r_cin[0]], name="sibling_wait_mlp0")
    put("mlp_w_down", shards_update("mlp_down", mlp_w_down, m_mlp_w_down, v_mlp_w_down, [g_mlp0[0], g_mlp1[0]]), mlp_w_down.shape)
    r_up = shards_update("mlp_up", mlp_w_up, m_mlp_w_up, v_mlp_w_up, [g_mlp0[1], g_mlp1[1]])
    put("mlp_w_up", r_up, mlp_w_up.shape)
    sb4 = start_sibling("fox", sc4, [r_up[0]])

    dmod_rows = []
    for sm, sl in ((sums_mix0, sums_mlp0), (sums_mix1, sums_mlp1)):
        dmod_rows += [sm[0:1], sm[1:2], sm[3:4], sl[0:1], sl[1:2], sl[3:4]]
    bf_row = jnp.pad(sums_bf[0:1], ((0, 0), (0, D - LANES)))
    small = jnp.concatenate([sums_mix0[2:3], sums_mix1[2:3], sums_mlp0[2:3], sums_mlp1[2:3], sums_f[0:1], sums_f[1:2], bf_row,
                             jnp.zeros((1, D), F32)] + dmod_rows + [dconv_w[0:3]], axis=0)
    small_all = _allgather8(_pad_rows(small, -(-small.shape[0] // 8) * 8), name="gather_small", deps=(sb4[3],))
    loss = _loss_sum(small_all, row=5, name="loss_sum")[0, 0]

    def rows_of(a_mix, a_mlp, a_fin, a_bf, a_ada):
        return jnp.concatenate([a_mix, a_mlp, a_fin.reshape(1, D), jnp.pad(a_bf, ((0, 0), (0, D - H))),
                                a_ada.reshape(L * NM, D)], axis=0)
    n_small = 2 * L + 2 + L * NM
    rw = -(-n_small // 8) * 8
    w_s = _pad_rows(rows_of(norm_mix, norm_mlp, final_norm, fox_b_f, ada_b), rw)
    m_s = _pad_rows(rows_of(m_norm_mix, m_norm_mlp, m_final_norm, m_fox_b_f, m_ada_b), rw)
    v_s = _pad_rows(rows_of(v_norm_mix, v_norm_mlp, v_final_norm, v_fox_b_f, v_ada_b), rw)
    src_rows = [0, 1, 2, 3, 4, 6] + [8 + r for r in range(L * NM)] + [7] * (rw - n_small)
    res_s = _adamw_small(w_s, m_s, v_s, small_all, rows=tuple(src_rows), name="adamw_small")
    for kind, r in zip(("grad", "delta", "new_m", "new_v"), res_s):
        outs[kind + "_norm_mix"] = r[0:L]
        outs[kind + "_norm_mlp"] = r[L:2 * L]
        outs[kind + "_final_norm"] = r[2 * L]
        outs[kind + "_fox_b_f"] = r[2 * L + 1:2 * L + 2, :H]
        outs[kind + "_ada_b"] = r[2 * L + 2:n_small].reshape(L, NM * D)

    dmod_all = small_all[:, 8:8 + L * NM, :].reshape(N_DEV, L, N_CHIPS, Ns_ada)
    dmod4 = jnp.transpose(dmod_all, (2, 1, 0, 3))
    act_t = jnp.transpose(c_act)
    res_a = _adamw_ada(ada_w, m_ada_w, v_ada_w, act_t, dmod4, name="adamw_ada")
    put("ada_w", res_a, ada_w.shape)

    r0 = 8 + L * NM
    dconv_all = jnp.pad(small_all[:, r0:r0 + 3, :], ((0, 0), (0, 5), (0, 0)))
    dconv4 = jnp.transpose(dconv_all.reshape(N_DEV, 8, N_CHIPS, Cs_conv), (2, 0, 1, 3))
    res_c = _adamw_conv_w(_pad_rows(conv_w[0], 8), _pad_rows(m_conv_w[0], 8), _pad_rows(v_conv_w[0], 8), dconv4,
                          name="adamw_conv_w")
    for kind, r in zip(("grad", "delta", "new_m", "new_v"), res_c):
        outs[kind + "_conv_w"] = r[:3].reshape(conv_w.shape)

    g_fox = _split_wait("sibling", sb4, [res_a[0], res_c[0], res_s[0]], name="sibling_wait_fox")
    put("fox_w_out", shards_update("fox_out", fox_w_out, m_fox_w_out, v_fox_w_out, g_fox[0:1]), fox_w_out.shape)
    t3 = lambda a: jnp.transpose(a, (0, 2, 1))
    for kind, r in zip(("grad", "delta", "new_m", "new_v"),
                       shards_update("fox_in", t3(fox_w_in), t3(m_fox_w_in), t3(v_fox_w_in), g_fox[1:2])):
        outs[kind + "_fox_w_in"] = t3(r)

    names = ["ada_w", "ada_b", "norm_mix", "norm_mlp", "fox_w_in", "fox_b_f", "fox_w_out", "conv_w_in", "conv_w", "conv_w_out",
             "mlp_w_up", "mlp_w_down", "final_norm"]
    return (loss, grad_x[None], *[outs["grad_" + n] for n in names], *[outs["delta_" + n] for n in names],
            *[outs["new_m_" + n] for n in names], *[outs["new_v_" + n] for n in names])
```

```python
import functools

import jax
import jax.numpy as jnp
from jax import lax
from jax.experimental import pallas as pl
from jax.experimental.pallas import tpu as pltpu

F32 = jnp.float32
BF16 = jnp.bfloat16
MESH = pl.DeviceIdType.MESH
ANY = pl.BlockSpec(memory_space=pl.ANY)
HBM = pl.BlockSpec(memory_space=pltpu.HBM)
SEM = pl.BlockSpec(memory_space=pltpu.SEMAPHORE)
EFFECT = pltpu.SideEffectType.DATAFLOW_SIDE_EFFECTING

RMS_EPS = 1e-6
ADAM_LR = 0.001
ADAM_B1 = 0.9
ADAM_B2 = 0.999
ADAM_EPS = 1e-08
ADAM_WD = 0.01
ADAM_STEP = 10
N_CHIPS = 4
N_DEV = 8
LANES = 128
VMEM_LIMIT = 56 * 1024 * 1024
NEG = -1e30


def _params(sems=None, vmem=VMEM_LIMIT):
    return pltpu.CompilerParams(dimension_semantics=sems, vmem_limit_bytes=vmem)


def _tile(n, pref, unit=LANES):
    if n <= pref:
        return n
    t = (pref // unit) * unit
    while n % t:
        t -= unit
    return t


def _me():
    return lax.axis_index("x"), lax.axis_index("y"), lax.axis_index("c")


def _call(body, deps, **kw):
    nd = len(deps)

    def wrapped(*refs):
        body(*refs[nd:])

    kw["in_specs"] = [ANY] * nd + list(kw["in_specs"])
    fn = pl.pallas_call(wrapped, **kw)
    return lambda *args: fn(*deps, *args)


def _allgather8(v, *, name, deps=()):
    R, C = v.shape

    def body(v_ref, out_ref, send_sems, recv_sems):
        x, y, c = _me()
        me = 4 * x + 2 * y + c
        out_ref[me] = v_ref[...]
        copies = []
        for k in range(1, N_DEV):
            px, py, pc = (x + (k >> 2)) % 2, (y + ((k >> 1) & 1)) % 2, (c + (k & 1)) % 2
            copies.append(pltpu.make_async_remote_copy(
                src_ref=v_ref, dst_ref=out_ref.at[me], send_sem=send_sems.at[k - 1], recv_sem=recv_sems.at[k - 1],
                device_id=(px, py, pc), device_id_type=MESH))
        for cp in copies:
            cp.start()
        for k in range(1, N_DEV):
            px, py, pc = (x + (k >> 2)) % 2, (y + ((k >> 1) & 1)) % 2, (c + (k & 1)) % 2
            peer = 4 * px + 2 * py + pc
            pltpu.make_async_remote_copy(
                src_ref=v_ref, dst_ref=out_ref.at[peer], send_sem=send_sems.at[k - 1], recv_sem=recv_sems.at[k - 1],
                device_id=(px, py, pc), device_id_type=MESH).wait_recv()
        for cp in copies:
            cp.wait_send()

    return _call(
        body, deps, name=name,
        out_shape=jax.ShapeDtypeStruct((N_DEV, R, C), v.dtype),
        in_specs=[pl.BlockSpec(memory_space=pltpu.VMEM)],
        out_specs=pl.BlockSpec(memory_space=pltpu.VMEM),
        scratch_shapes=[pltpu.SemaphoreType.DMA((N_DEV - 1,)), pltpu.SemaphoreType.DMA((N_DEV - 1,))],
        compiler_params=_params(),
    )(v)


def _chip_peers(x, y):
    return [((x + (k >> 1)) % 2, (y + (k & 1)) % 2) for k in range(1, N_CHIPS)]


def _slot(x, y, k):
    return 2 * ((x + (k >> 1)) % 2) + (y + (k & 1)) % 2


def _split_copies(kind, groups, send_sems, recv_sems):
    x, y, c = _me()
    j = 2 * x + y
    copies = []
    for a, g in enumerate(groups):
        if kind == "sibling":
            parts, land, sib = g
            for k in range(N_CHIPS):
                s = _slot(x, y, k)
                copies.append(pltpu.make_async_remote_copy(
                    src_ref=(parts if k == 0 else land).at[s], dst_ref=sib.at[s], send_sem=send_sems.at[N_CHIPS * a + k],
                    recv_sem=recv_sems.at[N_CHIPS * a + k], device_id=(x, y, 1 - c), device_id_type=MESH))
            continue
        land = g[-1]
        for k, (px, py) in enumerate(_chip_peers(x, y)):
            if kind == "gather1":
                src, dst, to = land.at[j, c], land.at[j, c], (px, py, c)
            elif kind == "gather2":
                src, dst, to = land.at[2 * px + py, c], land.at[2 * px + py, c], (x, y, 1 - c)
            else:
                src, dst, to = g[0].at[2 * px + py], land.at[j], (px, py, c)
            copies.append(pltpu.make_async_remote_copy(
                src_ref=src, dst_ref=dst, send_sem=send_sems.at[3 * a + k], recv_sem=recv_sems.at[3 * a + k],
                device_id=to, device_id_type=MESH))
    return copies


def _split_start(kind, groups, *, name, dep=None):
    flat = [a for g in groups for a in g]
    nf, per = len(flat), len(groups[0])
    ncp = len(groups) * (N_CHIPS if kind == "sibling" else 3)
    nd = 0 if dep is None else 1

    def body(*refs):
        ins = refs[nd:nd + nf]
        send_sems, recv_sems, token = refs[nd + nf], refs[nd + nf + 1], refs[-1]
        for cp in _split_copies(kind, [ins[i:i + per] for i in range(0, nf, per)], send_sems, recv_sems):
            cp.start()
        token[...] = jnp.zeros_like(token)

    outs = pl.pallas_call(
        body, name=name,
        out_shape=(pltpu.SemaphoreType.DMA((ncp,)), pltpu.SemaphoreType.DMA((ncp,)), *[pltpu.HBM(a.shape, a.dtype) for a in flat],
                   jax.ShapeDtypeStruct((8, LANES), F32)),
        in_specs=[ANY] * nd + [HBM] * nf,
        out_specs=(SEM, SEM, *[HBM] * nf, pl.BlockSpec(memory_space=pltpu.VMEM)),
        input_output_aliases={nd + i: 2 + i for i in range(nf)},
        compiler_params=pltpu.CompilerParams(has_side_effects=EFFECT),
    )(*([dep] if nd else []), *[pltpu.with_memory_space_constraint(a, pltpu.HBM) for a in flat])
    thru = list(outs[2:2 + nf])
    return outs[0], outs[1], [tuple(thru[i:i + per]) for i in range(0, nf, per)], outs[-1]


def _split_wait(kind, started, after, *, name):
    send_sems, recv_sems, groups, _ = started
    flat = [a for g in groups for a in g]
    nf, per = len(flat), len(groups[0])

    def body(*refs):
        ins = refs[:nf]
        for cp in _split_copies(kind, [ins[i:i + per] for i in range(0, nf, per)], refs[nf], refs[nf + 1]):
            cp.wait_send()
            cp.wait_recv()

    outs = pl.pallas_call(
        body, name=name,
        out_shape=tuple(pltpu.HBM(a.shape, a.dtype) for a in flat),
        in_specs=[HBM] * nf + [SEM, SEM] + [ANY] * len(after), out_specs=tuple([HBM] * nf),
        input_output_aliases={i: i for i in range(nf)},
        compiler_params=pltpu.CompilerParams(has_side_effects=EFFECT),
    )(*flat, send_sems, recv_sems, *after)
    outs = list(outs)
    return [tuple(outs[i:i + per]) for i in range(0, nf, per)]


def _place_cast(shards, layer, chip, *, halves, name):
    _, R, C = shards.shape
    if halves == "rows":
        hr, hc = R // 2, C
    else:
        hr, hc = R, C // 2
    if hr % 16 == 0:
        tr, tc = _tile(hr, 512, 16), hc
    else:
        tr, tc = hr, _tile(hc, 256)
    nr, nc = hr // tr, hc // tc

    def body(chip_ref, x_ref, o_ref):
        o_ref[...] = x_ref[...].astype(BF16)

    if halves == "rows":
        o_map = lambda i, j, chip_ref: (chip_ref[0], i // nr, i % nr, j)
    else:
        o_map = lambda i, j, chip_ref: (chip_ref[0], j // nc, i, j % nc)
    return pl.pallas_call(
        body, name=name,
        grid_spec=pltpu.PrefetchScalarGridSpec(
            num_scalar_prefetch=1, grid=(R // tr, C // tc),
            in_specs=[pl.BlockSpec((None, tr, tc), lambda i, j, chip_ref: (layer, i, j))],
            out_specs=pl.BlockSpec((None, None, tr, tc), o_map)),
        out_shape=jax.ShapeDtypeStruct((N_CHIPS, 2, hr, hc), BF16), compiler_params=_params(("parallel", "parallel")),
    )(chip, shards)


def _accumulate(part, acc_ref, nk, finalize):
    if nk == 1:
        finalize(part)
        return
    k = pl.program_id(2)

    @pl.when(k == 0)
    def _():
        acc_ref[...] = part

    @pl.when(k > 0)
    def _():
        acc_ref[...] += part

    @pl.when(k == nk - 1)
    def _():
        finalize(acc_ref[...])


def _mm_nn(a, b, *, name, epilogue="plain", res=None, gate=None, norm=None, out_dtype=BF16, tm=1024, tn=1024, tk=2048, deps=()):
    if a.ndim == 3:
        Q, M, Kq = a.shape
        K = Q * Kq
    else:
        (M, K), Kq = a.shape, a.shape[1]
    tm, tk = _tile(M, tm, 16), _tile(Kq, tk)
    if a.ndim == 3:
        pa = Kq // tk
        a_spec = pl.BlockSpec((None, tm, tk), lambda i, j, k: (k // pa, i, k % pa))
    else:
        a_spec = pl.BlockSpec((tm, tk), lambda i, j, k: (i, k))
    if b.ndim == 3:
        P, _, Ns = b.shape
        N = P * Ns
        tn = _tile(Ns, tn)
        per = Ns // tn
        b_spec = pl.BlockSpec((None, tk, tn), lambda i, j, k: (j // per, k, j % per))
    else:
        N = b.shape[1]
        tn = _tile(N, tn)
        b_spec = pl.BlockSpec((tk, tn), lambda i, j, k: (k, j))
    nk = K // tk
    tile = pl.BlockSpec((tm, tn), lambda i, j, k: (i, j))

    def body(*refs):
        acc_ref = refs[-1] if nk > 1 else None
        a_ref, b_ref = refs[0], refs[1]
        part = jnp.dot(a_ref[...], b_ref[...], preferred_element_type=F32)
        if epilogue == "plain":
            def fin(acc):
                refs[2][...] = acc.astype(out_dtype)
        elif epilogue == "relu2":
            def fin(acc):
                refs[2][...] = acc.astype(BF16)
                refs[3][...] = jnp.square(jnp.maximum(acc, 0.0)).astype(BF16)
        elif epilogue == "add":
            def fin(acc):
                refs[3][...] = (acc + refs[2][...]).astype(out_dtype)
        elif epilogue == "resid":
            def fin(acc):
                refs[4][...] = refs[2][...] + refs[3][...] * acc
                refs[5][...] = acc.astype(BF16)
        else:
            def fin(acc):
                xv = refs[2][...] + refs[3][...] * acc
                refs[7][...] = xv
                refs[8][...] = acc.astype(BF16)
                r = lax.rsqrt(jnp.mean(xv * xv, axis=-1, keepdims=True) + RMS_EPS)
                refs[9][...] = (((xv * r) * refs[4][...]) * (1.0 + refs[5][...]) + refs[6][...]).astype(BF16)
        _accumulate(part, acc_ref, nk, fin)

    in_specs = [a_spec, b_spec]
    args = [a, b]
    if epilogue == "plain":
        out_shape, out_specs = jax.ShapeDtypeStruct((M, N), out_dtype), tile
    elif epilogue == "relu2":
        out_shape, out_specs = [jax.ShapeDtypeStruct((M, N), BF16)] * 2, [tile, tile]
    elif epilogue == "add":
        in_specs.append(tile)
        args.append(res)
        out_shape, out_specs = jax.ShapeDtypeStruct((M, N), out_dtype), tile
    else:
        row = pl.BlockSpec((1, tn), lambda i, j, k: (0, j))
        in_specs += [tile, row]
        args += [res, gate]
        out_shape, out_specs = [jax.ShapeDtypeStruct((M, N), F32), jax.ShapeDtypeStruct((M, N), BF16)], [tile, tile]
        if epilogue == "resid_norm":
            assert tn == N, "the next norm needs whole rows"
            in_specs += [row, row, row]
            args += list(norm)
            out_shape, out_specs = out_shape + [jax.ShapeDtypeStruct((M, N), BF16)], out_specs + [tile]
    return _call(
        body, deps, name=name, grid=(M // tm, N // tn, nk), in_specs=in_specs, out_specs=out_specs, out_shape=out_shape,
        scratch_shapes=[pltpu.VMEM((tm, tn), F32)] if nk > 1 else [],
        compiler_params=_params(("parallel", "parallel", "arbitrary")),
    )(*args)


def _mm_nt(a, b, *, name, n=None, epilogue="plain", extra=None, out_dtype=F32, tm=1024, tn=1024, tk=2048, deps=()):
    if a.ndim == 3:
        Q, M, Kq = a.shape
        K = Q * Kq
    else:
        (M, K), Kq = a.shape, a.shape[1]
    if b.ndim == 3:
        P, N, Ks = b.shape
    else:
        N, Ks = b.shape
    N = n or N
    tm, tn, tk = _tile(M, tm, 16), _tile(N, tn), _tile(min(Kq, Ks), tk)
    nk = K // tk
    if a.ndim == 3:
        pa = Kq // tk
        a_spec = pl.BlockSpec((None, tm, tk), lambda i, j, k: (k // pa, i, k % pa))
    else:
        a_spec = pl.BlockSpec((tm, tk), lambda i, j, k: (i, k))
    if b.ndim == 3:
        pb = Ks // tk
        b_spec = pl.BlockSpec((None, tn, tk), lambda i, j, k: (k // pb, j, k % pb))
    else:
        b_spec = pl.BlockSpec((tn, tk), lambda i, j, k: (j, k))
    tile = pl.BlockSpec((tm, tn), lambda i, j, k: (i, j))

    def body(*refs):
        acc_ref = refs[-1] if nk > 1 else None
        part = lax.dot_general(refs[0][...], refs[1][...], (((1,), (1,)), ((), ())), preferred_element_type=F32)
        if epilogue == "plain":
            def fin(acc):
                refs[2][...] = acc.astype(out_dtype)
        elif epilogue == "add":
            def fin(acc):
                refs[3][...] = (acc + refs[2][...]).astype(out_dtype)
        else:
            def fin(acc):
                refs[3][...] = (acc * (2.0 * jnp.maximum(refs[2][...].astype(F32), 0.0))).astype(out_dtype)
        _accumulate(part, acc_ref, nk, fin)

    in_specs, args = [a_spec, b_spec], [a, b]
    if epilogue != "plain":
        in_specs.append(tile)
        args.append(extra)
    return _call(
        body, deps, name=name, grid=(M // tm, N // tn, nk), in_specs=in_specs, out_specs=tile,
        out_shape=jax.ShapeDtypeStruct((M, N), out_dtype),
        scratch_shapes=[pltpu.VMEM((tm, tn), F32)] if nk > 1 else [],
        compiler_params=_params(("parallel", "parallel", "arbitrary")),
    )(*args)


def _mm_tn(a, b, *, name, out_parts=1, tm=1024, tn=1024, tk=4096, deps=()):
    if a.ndim == 3:
        Qa, M, Kq = a.shape
        Kd = Qa * Kq
    else:
        (M, Kd), Kq = a.shape, a.shape[1]
    if b.ndim == 3:
        Q, _, Nq = b.shape
        N = Q * Nq
    else:
        N, Nq = b.shape[1], b.shape[1]
    Ns = N // out_parts
    tn = _tile(Ns, tn)
    while Nq % tn or Ns % tn:
        tn -= LANES
    tm, tk = _tile(Kq, tm), _tile(M, tk, 16)
    nk = M // tk
    if a.ndim == 3:
        pa = Kq // tm
        a_spec = pl.BlockSpec((None, tk, tm), lambda i, j, k: (i // pa, k, i % pa))
    else:
        a_spec = pl.BlockSpec((tk, tm), lambda i, j, k: (k, i))
    if b.ndim == 3:
        pb = Nq // tn
        b_spec = pl.BlockSpec((None, tk, tn), lambda i, j, k: (j // pb, k, j % pb))
    else:
        b_spec = pl.BlockSpec((tk, tn), lambda i, j, k: (k, j))
    if out_parts > 1:
        po = Ns // tn
        o_spec = pl.BlockSpec((None, tm, tn), lambda i, j, k: (j // po, i, j % po))
        out_shape = jax.ShapeDtypeStruct((out_parts, Kd, Ns), BF16)
    else:
        o_spec = pl.BlockSpec((tm, tn), lambda i, j, k: (i, j))
        out_shape = jax.ShapeDtypeStruct((Kd, N), BF16)

    def body(*refs):
        acc_ref = refs[-1] if nk > 1 else None
        part = lax.dot_general(refs[0][...], refs[1][...], (((0,), (0,)), ((), ())), preferred_element_type=F32)

        def fin(acc):
            refs[2][...] = acc.astype(BF16)
        _accumulate(part, acc_ref, nk, fin)

    return _call(
        body, deps, name=name, grid=(Kd // tm, N // tn, nk),
        in_specs=[a_spec, b_spec], out_specs=o_spec, out_shape=out_shape,
        scratch_shapes=[pltpu.VMEM((tm, tn), F32)] if nk > 1 else [],
        compiler_params=_params(("parallel", "parallel", "arbitrary")),
    )(a, b)


def _rows(S, D, i_map=lambda i: (i, 0), ts=512):
    return pl.BlockSpec((ts, D), i_map)


def _norm_fwd(x, gain, sc, sh, *, name, deps=()):
    S, D = x.shape
    ts = _tile(S, 512, 16)
    vec = pl.BlockSpec((1, D), lambda i: (0, 0))

    def body(x_ref, g_ref, sc_ref, sh_ref, h_ref):
        xv = x_ref[...]
        r = lax.rsqrt(jnp.mean(xv * xv, axis=-1, keepdims=True) + RMS_EPS)
        h = (xv * r) * g_ref[...]
        h_ref[...] = (h * (1.0 + sc_ref[...]) + sh_ref[...]).astype(BF16)

    return _call(
        body, deps, name=name, grid=(S // ts,), in_specs=[_rows(S, D, ts=ts), vec, vec, vec], out_specs=_rows(S, D, ts=ts),
        out_shape=jax.ShapeDtypeStruct((S, D), BF16), compiler_params=_params(("parallel",)),
    )(x, gain, sc, sh)


def _loss_bwd(x, target, gain, gate_prev, *, name, deps=()):
    S, D = x.shape
    ts = _tile(S, 256, 16)
    vec = pl.BlockSpec((1, D), lambda i: (0, 0))

    def body(x_ref, t_ref, g_ref, gp_ref, dx_ref, dp_ref, sums_ref):
        @pl.when(pl.program_id(0) == 0)
        def _():
            sums_ref[...] = jnp.zeros_like(sums_ref)
        xv = x_ref[...]
        r = lax.rsqrt(jnp.mean(xv * xv, axis=-1, keepdims=True) + RMS_EPS)
        xn = xv * r
        err = xn * g_ref[...] - t_ref[...]
        loss = 0.5 * jnp.sum(jnp.mean(err * err, axis=-1, keepdims=True), axis=0, keepdims=True)
        dy = err * (1.0 / D)
        dxn = dy * g_ref[...]
        dx = r * (dxn - xn * jnp.mean(dxn * xn, axis=-1, keepdims=True))
        dx_ref[...] = dx
        dp_ref[...] = (gp_ref[...] * dx).astype(BF16)
        sums_ref[0:1, :] += jnp.sum(dy * xn, axis=0, keepdims=True)
        sums_ref[1:2, :] += jnp.broadcast_to(loss, (1, D))

    return _call(
        body, deps, name=name, grid=(S // ts,),
        in_specs=[_rows(S, D, ts=ts), _rows(S, D, ts=ts), vec, vec],
        out_specs=[_rows(S, D, ts=ts), _rows(S, D, ts=ts), pl.BlockSpec((8, D), lambda i: (0, 0))],
        out_shape=[jax.ShapeDtypeStruct((S, D), F32), jax.ShapeDtypeStruct((S, D), BF16), jax.ShapeDtypeStruct((8, D), F32)],
        compiler_params=_params(("arbitrary",)),
    )(x, target, gain, gate_prev)


def _norm_bwd(x, dh, dxp, mix, gain, sc, gate_prev, *, name, deps=()):
    S, D = x.shape
    ts = _tile(S, 256, 16)
    vec = pl.BlockSpec((1, D), lambda i: (0, 0))
    with_prev = gate_prev is not None

    def body(*refs):
        x_ref, dh_ref, dxp_ref, mix_ref, g_ref, sc_ref = refs[:6]
        outs = refs[7:] if with_prev else refs[6:]
        sums_ref = outs[-1]

        @pl.when(pl.program_id(0) == 0)
        def _():
            sums_ref[...] = jnp.zeros_like(sums_ref)
        xv, dhv, dxpv = x_ref[...], dh_ref[...].astype(F32), dxp_ref[...]
        r = lax.rsqrt(jnp.mean(xv * xv, axis=-1, keepdims=True) + RMS_EPS)
        xn = xv * r
        hn = xn * g_ref[...]
        dhn = dhv * (1.0 + sc_ref[...])
        dxn = dhn * g_ref[...]
        dx = dxpv + r * (dxn - xn * jnp.mean(dxn * xn, axis=-1, keepdims=True))
        outs[0][...] = dx
        if with_prev:
            outs[1][...] = (refs[6][...] * dx).astype(BF16)
        sums_ref[0:1, :] += jnp.sum(dhv, axis=0, keepdims=True)
        sums_ref[1:2, :] += jnp.sum(dhv * hn, axis=0, keepdims=True)
        sums_ref[2:3, :] += jnp.sum(dhn * xn, axis=0, keepdims=True)
        sums_ref[3:4, :] += jnp.sum(dxpv * mix_ref[...].astype(F32), axis=0, keepdims=True)

    tile = _rows(S, D, ts=ts)
    in_specs = [tile, tile, tile, tile, vec, vec] + ([vec] if with_prev else [])
    args = [x, dh, dxp, mix, gain, sc] + ([gate_prev] if with_prev else [])
    out_specs = [tile] + ([tile] if with_prev else []) + [pl.BlockSpec((8, D), lambda i: (0, 0))]
    out_shape = ([jax.ShapeDtypeStruct((S, D), F32)] + ([jax.ShapeDtypeStruct((S, D), BF16)] if with_prev else [])
                 + [jax.ShapeDtypeStruct((8, D), F32)])
    outs = _call(
        body, deps, name=name, grid=(S // ts,), in_specs=in_specs, out_specs=out_specs, out_shape=out_shape,
        compiler_params=_params(("arbitrary",)),
    )(*args)
    return (outs[0], outs[1], outs[2]) if with_prev else (outs[0], None, outs[1])


def _fgate_fwd(h, wf, bf, *, name, deps=()):
    S, D = h.shape
    ts = _tile(S, 256, 16)

    def body(h_ref, w_ref, b_ref, z_ref, f_ref, carry):
        @pl.when(pl.program_id(0) == 0)
        def _():
            carry[...] = jnp.zeros_like(carry)
        z = lax.dot_general(h_ref[...], w_ref[...], (((1,), (1,)), ((), ())), preferred_element_type=F32) + b_ref[...]
        logf = jnp.minimum(z, 0.0) - jnp.log(1.0 + jnp.exp(-jnp.abs(z)))
        row = lax.broadcasted_iota(jnp.int32, (ts, ts), 0)
        col = lax.broadcasted_iota(jnp.int32, (ts, ts), 1)
        tril = (col <= row).astype(F32)
        run = jnp.dot(tril, logf, preferred_element_type=F32, precision=lax.Precision.HIGHEST) + carry[0:1, :]
        z_ref[...] = z
        f_ref[...] = run
        carry[0:1, :] = run[ts - 1:ts, :]

    return _call(
        body, deps, name=name, grid=(S // ts,),
        in_specs=[pl.BlockSpec((ts, D), lambda i: (i, 0)), pl.BlockSpec((LANES, D), lambda i: (0, 0)),
                  pl.BlockSpec((1, LANES), lambda i: (0, 0))],
        out_specs=[pl.BlockSpec((ts, LANES), lambda i: (i, 0))] * 2,
        out_shape=[jax.ShapeDtypeStruct((S, LANES), F32)] * 2,
        scratch_shapes=[pltpu.VMEM((8, LANES), F32)],
        compiler_params=_params(("arbitrary",)),
    )(h, wf, bf)


def _fgate_bwd(dfq, dfk, z, *, name, deps=()):
    S = z.shape[0]
    ts = _tile(S, 256, 16)
    n = S // ts

    def body(dq_ref, dk_ref, z_ref, dz_ref, sums_ref, carry):
        @pl.when(pl.program_id(0) == 0)
        def _():
            carry[...] = jnp.zeros_like(carry)
            sums_ref[...] = jnp.zeros_like(sums_ref)
        df = dq_ref[...] - dk_ref[...]
        row = lax.broadcasted_iota(jnp.int32, (ts, ts), 0)
        col = lax.broadcasted_iota(jnp.int32, (ts, ts), 1)
        triu = (col >= row).astype(F32)
        run = jnp.dot(triu, df, preferred_element_type=F32, precision=lax.Precision.HIGHEST) + carry[0:1, :]
        zv = z_ref[...]
        dz = run * (1.0 / (1.0 + jnp.exp(zv)))
        dz_ref[...] = dz.astype(BF16)
        sums_ref[0:1, :] += jnp.sum(dz, axis=0, keepdims=True)
        carry[0:1, :] = run[0:1, :]

    rev = pl.BlockSpec((ts, LANES), lambda i: (n - 1 - i, 0))
    return _call(
        body, deps, name=name, grid=(n,), in_specs=[rev, rev, rev],
        out_specs=[rev, pl.BlockSpec((8, LANES), lambda i: (0, 0))],
        out_shape=[jax.ShapeDtypeStruct((S, LANES), BF16), jax.ShapeDtypeStruct((8, LANES), F32)],
        scratch_shapes=[pltpu.VMEM((8, LANES), F32)],
        compiler_params=_params(("arbitrary",)),
    )(dfq, dfk, z)


def _head_col(ref, rows, lane_mask):
    return jnp.sum(jnp.where(lane_mask, ref[rows, :], 0.0), axis=1, keepdims=True)


def _attn_fwd(qkv, fq, *, heads, name, T=256, deps=()):
    S, D3 = qkv.shape
    D = D3 // 3
    dh = D // heads
    T = _tile(S, T, 16)
    nq = S // T
    scale = dh ** -0.5
    log2e = 1.4426950408889634
    c = scale * log2e

    def body(q_ref, k_ref, v_ref, fq_ref, o_ref, lse_ref, qa, ka):
        h = pl.program_id(0)

        @pl.when(h == 0)
        def _():
            lse_ref[...] = jnp.zeros_like(lse_ref)
        lane = lax.broadcasted_iota(jnp.int32, (1, LANES), 1) == h
        row = lax.broadcasted_iota(jnp.int32, (T, T), 0)
        col = lax.broadcasted_iota(jnp.int32, (T, T), 1)
        pos = lax.broadcasted_iota(jnp.int32, (1, LANES), 1)

        def augment(qi, _):
            rows = pl.ds(pl.multiple_of(qi * T, T), T)
            g = _head_col(fq_ref, rows, lane) * (1.0 / scale)
            g1 = g.astype(BF16).astype(F32)
            g2 = (g - g1).astype(BF16).astype(F32)
            g3 = (g - g1 - g2).astype(BF16).astype(F32)
            pieces = jnp.where(pos == 0, g1, jnp.where(pos == 1, g2, jnp.where(pos == 2, g3, 0.0)))
            ones_q = jnp.where((pos >= 3) & (pos < 6), 1.0, 0.0)
            ones_k = jnp.where(pos < 3, 1.0, 0.0)
            shifted = jnp.where(pos == 3, g1, jnp.where(pos == 4, g2, jnp.where(pos == 5, g3, 0.0)))
            qa[rows, 0:dh] = q_ref[rows, :]
            qa[rows, dh:dh + LANES] = (pieces + ones_q).astype(BF16)
            ka[rows, 0:dh] = k_ref[rows, :]
            ka[rows, dh:dh + LANES] = (ones_k - shifted).astype(BF16)
            return 0

        lax.fori_loop(0, nq, augment, 0)

        def q_block(qi, _):
            rows = pl.ds(pl.multiple_of(qi * T, T), T)
            q = qa[rows, :]

            def kv_block(kj, carry, diag):
                m, l, acc = carry
                cols = pl.ds(pl.multiple_of(kj * T, T), T)
                raw = lax.dot_general(q, ka[cols, :], (((1,), (1,)), ((), ())), preferred_element_type=F32)
                if diag:
                    raw = jnp.where(col <= row, raw, NEG)
                m_new = jnp.maximum(m, jnp.max(raw, axis=1, keepdims=True) * c)
                p = jnp.exp2(raw * c - m_new)
                alpha = jnp.exp2(m - m_new)
                l = alpha * l + jnp.sum(p, axis=1, keepdims=True)
                acc = alpha * acc + jnp.dot(p.astype(BF16), v_ref[cols, :], preferred_element_type=F32)
                return m_new, l, acc

            init = (jnp.full((T, 1), NEG, F32), jnp.zeros((T, 1), F32), jnp.zeros((T, dh), F32))
            carry = lax.fori_loop(0, qi, lambda kj, cr: kv_block(kj, cr, False), init)
            m, l, acc = kv_block(qi, carry, True)
            o_ref[rows, :] = (acc / l).astype(BF16)
            lse_ref[rows, :] = jnp.where(lane, m * (1.0 / log2e) + jnp.log(l), lse_ref[rows, :])
            return 0

        lax.fori_loop(0, nq, q_block, 0)

    head = lambda part: pl.BlockSpec((S, dh), lambda h: (0, part * heads + h))
    return _call(
        body, deps, name=name, grid=(heads,),
        in_specs=[head(0), head(1), head(2), pl.BlockSpec((S, LANES), lambda h: (0, 0))],
        out_specs=[pl.BlockSpec((S, dh), lambda h: (0, h)), pl.BlockSpec((S, LANES), lambda h: (0, 0))],
        out_shape=[jax.ShapeDtypeStruct((S, D), BF16), jax.ShapeDtypeStruct((S, LANES), F32)],
        scratch_shapes=[pltpu.VMEM((S, dh + LANES), BF16), pltpu.VMEM((S, dh + LANES), BF16)],
        compiler_params=_params(("arbitrary",)),
    )(qkv, qkv, qkv, fq)


def _attn_bwd(qkv, o, do, fq, fk, lse, *, heads, name, T=256, deps=()):
    S, D3 = qkv.shape
    D = D3 // 3
    dh = D // heads
    T = _tile(S, T, 16)
    nq = S // T
    scale = dh ** -0.5
    hp = fk.shape[1]

    def body(q_ref, k_ref, v_ref, o_ref, do_ref, fq_ref, fk_ref, lse_ref, dqkv_ref, dfq_ref, dfk_ref,
             dq_acc, fq_col, lse_col, delta_col, dfq_col):
        h = pl.program_id(0)

        @pl.when(h == 0)
        def _():
            dfq_ref[...] = jnp.zeros_like(dfq_ref)
            dfk_ref[...] = jnp.zeros_like(dfk_ref)
        lane = lax.broadcasted_iota(jnp.int32, (1, LANES), 1) == h
        row = lax.broadcasted_iota(jnp.int32, (T, T), 0)
        col = lax.broadcasted_iota(jnp.int32, (T, T), 1)
        dq_acc[...] = jnp.zeros_like(dq_acc)
        dfq_col[...] = jnp.zeros_like(dfq_col)

        def prep(qi, _):
            rows = pl.ds(pl.multiple_of(qi * T, T), T)
            fq_col[rows, :] = _head_col(fq_ref, rows, lane)
            lse_col[rows, :] = _head_col(lse_ref, rows, lane)
            delta_col[rows, :] = jnp.sum(do_ref[rows, :].astype(F32) * o_ref[rows, :].astype(F32), axis=1, keepdims=True)
            return 0

        lax.fori_loop(0, nq, prep, 0)

        def kv_block(kj, _):
            cols = pl.ds(pl.multiple_of(kj * T, T), T)
            k, v = k_ref[cols, :], v_ref[cols, :]
            fk_row = fk_ref[kj, pl.ds(h, 1), :]

            def q_block(qi, carry, diag):
                dk, dv, dfk = carry
                rows = pl.ds(pl.multiple_of(qi * T, T), T)
                q, dov = q_ref[rows, :], do_ref[rows, :]
                s = lax.dot_general(q, k, (((1,), (1,)), ((), ())), preferred_element_type=F32) * scale
                s = s + (fq_col[rows, :] - fk_row)
                p = jnp.exp(s - lse_col[rows, :])
                if diag:
                    p = jnp.where(col <= row, p, 0.0)
                dp = lax.dot_general(dov, v, (((1,), (1,)), ((), ())), preferred_element_type=F32)
                ds = p * (dp - delta_col[rows, :])
                dsb = ds.astype(BF16)
                dv = dv + lax.dot_general(p.astype(BF16), dov, (((0,), (0,)), ((), ())), preferred_element_type=F32)
                dk = dk + lax.dot_general(dsb, q, (((0,), (0,)), ((), ())), preferred_element_type=F32)
                dq_acc[rows, :] += jnp.dot(dsb, k, preferred_element_type=F32)
                dfq_col[rows, :] += jnp.sum(ds, axis=1, keepdims=True)
                dfk = dfk + jnp.sum(ds, axis=0, keepdims=True)
                return dk, dv, dfk

            init = (jnp.zeros((T, dh), F32), jnp.zeros((T, dh), F32), jnp.zeros((1, T), F32))
            carry = q_block(kj, init, True)
            dk, dv, dfk = lax.fori_loop(kj + 1, nq, lambda qi, cr: q_block(qi, cr, False), carry)
            dqkv_ref[1, cols, :] = (dk * scale).astype(BF16)
            dqkv_ref[2, cols, :] = dv.astype(BF16)
            dfk_ref[kj, pl.ds(h, 1), :] = dfk
            return 0

        lax.fori_loop(0, nq, kv_block, 0)

        def finish(qi, _):
            rows = pl.ds(pl.multiple_of(qi * T, T), T)
            dqkv_ref[0, rows, :] = (dq_acc[rows, :] * scale).astype(BF16)
            dfq_ref[rows, :] = jnp.where(lane, dfq_col[rows, :], dfq_ref[rows, :])
            return 0

        lax.fori_loop(0, nq, finish, 0)

    head = lambda part: pl.BlockSpec((S, dh), lambda h: (0, part * heads + h))
    own = pl.BlockSpec((S, dh), lambda h: (0, h))
    full = pl.BlockSpec((S, LANES), lambda h: (0, 0))
    krow = pl.BlockSpec((nq, hp, T), lambda h: (0, 0, 0))
    return _call(
        body, deps, name=name, grid=(heads,),
        in_specs=[head(0), head(1), head(2), own, own, full, krow, full],
        out_specs=[pl.BlockSpec((3, S, dh), lambda h: (0, 0, h)), full, krow],
        out_shape=[jax.ShapeDtypeStruct((3, S, D), BF16), jax.ShapeDtypeStruct((S, LANES), F32),
                   jax.ShapeDtypeStruct((nq, hp, T), F32)],
        scratch_shapes=[pltpu.VMEM((S, dh), F32)] + [pltpu.VMEM((S, 1), F32)] * 4,
        compiler_params=_params(("arbitrary",)),
    )(qkv, qkv, qkv, o, do, fq, fk, lse)


def _shift_down(v, n):
    rows = lax.broadcasted_iota(jnp.int32, v.shape, 0)
    return jnp.where(rows >= n, pltpu.roll(v, n, axis=0), 0.0)


def _shift_up(v, n):
    S = v.shape[0]
    rows = lax.broadcasted_iota(jnp.int32, v.shape, 0)
    return jnp.where(rows < S - n, pltpu.roll(v, S - n, axis=0), 0.0)


def _conv_fwd(proj, conv_w, *, name, cb=LANES, deps=()):
    S, D3 = proj.shape
    D = D3 // 3
    nb = D // cb

    def body(bg_ref, cg_ref, u_ref, w_ref, z_ref):
        uc = cg_ref[...].astype(F32) * u_ref[...].astype(F32)
        w = w_ref[...]
        y = w[2:3, :] * uc + w[1:2, :] * _shift_down(uc, 1) + w[0:1, :] * _shift_down(uc, 2)
        z_ref[...] = (bg_ref[...].astype(F32) * y).astype(BF16)

    part = lambda g: pl.BlockSpec((S, cb), lambda j: (0, g * nb + j))
    return _call(
        body, deps, name=name, grid=(nb,),
        in_specs=[part(0), part(1), part(2), pl.BlockSpec((3, cb), lambda j: (0, j))],
        out_specs=pl.BlockSpec((S, cb), lambda j: (0, j)),
        out_shape=jax.ShapeDtypeStruct((S, D), BF16), compiler_params=_params(("parallel",)),
    )(proj, proj, proj, conv_w)


def _conv_bwd(proj, conv_w, dz, *, name, cb=LANES, deps=()):
    S, D3 = proj.shape
    D = D3 // 3
    nb = D // cb

    def body(bg_ref, cg_ref, u_ref, w_ref, dz_ref, dp_ref, dw_ref):
        cg, u = cg_ref[...].astype(F32), u_ref[...].astype(F32)
        uc = cg * u
        w = w_ref[...]
        uc1, uc2 = _shift_down(uc, 1), _shift_down(uc, 2)
        y = w[2:3, :] * uc + w[1:2, :] * uc1 + w[0:1, :] * uc2
        dz = dz_ref[...].astype(F32)
        dp_ref[0] = (dz * y).astype(BF16)
        dy = dz * bg_ref[...].astype(F32)
        duc = w[2:3, :] * dy + w[1:2, :] * _shift_up(dy, 1) + w[0:1, :] * _shift_up(dy, 2)
        dp_ref[1] = (duc * u).astype(BF16)
        dp_ref[2] = (duc * cg).astype(BF16)
        dw_ref[...] = jnp.zeros_like(dw_ref)
        dw_ref[0:1, :] = jnp.sum(dy * uc2, axis=0, keepdims=True)
        dw_ref[1:2, :] = jnp.sum(dy * uc1, axis=0, keepdims=True)
        dw_ref[2:3, :] = jnp.sum(dy * uc, axis=0, keepdims=True)

    part = lambda g: pl.BlockSpec((S, cb), lambda j: (0, g * nb + j))
    return _call(
        body, deps, name=name, grid=(nb,),
        in_specs=[part(0), part(1), part(2), pl.BlockSpec((3, cb), lambda j: (0, j)), pl.BlockSpec((S, cb), lambda j: (0, j))],
        out_specs=[pl.BlockSpec((3, S, cb), lambda j: (0, 0, j)), pl.BlockSpec((8, cb), lambda j: (0, j))],
        out_shape=[jax.ShapeDtypeStruct((3, S, D), BF16), jax.ShapeDtypeStruct((8, D), F32)],
        compiler_params=_params(("parallel",)),
    )(proj, proj, proj, conv_w, dz)


def _ada_fwd(c_all, ada_w, *, name, deps=()):
    L, D, Ns = ada_w.shape
    tn = _tile(Ns, 512)

    def body(c_ref, w_ref, o_ref, act_ref):
        cv = c_ref[...]
        act = cv * (1.0 / (1.0 + jnp.exp(-cv)))
        act_ref[...] = act
        o_ref[...] = jnp.dot(act.astype(BF16), w_ref[...].astype(BF16), preferred_element_type=F32)

    return _call(
        body, deps, name=name, grid=(L, Ns // tn),
        in_specs=[pl.BlockSpec((N_DEV, D), lambda l, j: (0, 0)), pl.BlockSpec((None, D, tn), lambda l, j: (l, 0, j))],
        out_specs=[pl.BlockSpec((None, N_DEV, tn), lambda l, j: (l, 0, j)), pl.BlockSpec((N_DEV, D), lambda l, j: (0, 0))],
        out_shape=[jax.ShapeDtypeStruct((L, N_DEV, Ns), F32), jax.ShapeDtypeStruct((N_DEV, D), F32)],
        compiler_params=_params(("arbitrary", "arbitrary")),
    )(c_all, ada_w)


def _select_mod(gathered, *, name, deps=()):
    _, LB, Ns = gathered.shape
    L = LB // N_DEV

    def body(g_ref, o_ref):
        x, y, c = _me()
        b = 4 * x + 2 * y + c
        for j in range(N_CHIPS):
            for l in range(L):
                o_ref[j, pl.ds(l, 1), :] = g_ref[2 * j + c, pl.ds(l * N_DEV + b, 1), :]

    return _call(
        body, deps, name=name, out_shape=jax.ShapeDtypeStruct((N_CHIPS, L, Ns), F32),
        in_specs=[pl.BlockSpec(memory_space=pltpu.VMEM)], out_specs=pl.BlockSpec(memory_space=pltpu.VMEM),
        compiler_params=_params(),
    )(gathered)


def _adamw_math(w, g, m, v):
    m = ADAM_B1 * m + (1.0 - ADAM_B1) * g
    v = ADAM_B2 * v + (1.0 - ADAM_B2) * jnp.square(g)
    m_hat = m / (1.0 - ADAM_B1 ** ADAM_STEP)
    v_hat = v / (1.0 - ADAM_B2 ** ADAM_STEP)
    delta = -ADAM_LR * (m_hat / (jnp.sqrt(v_hat) + ADAM_EPS) + ADAM_WD * w)
    return delta, m, v


def _adamw_shards(w, m, v, groups, chip, *, name):
    L, R, C = w.shape
    if R % 16 == 0:
        tr, tc = _tile(R, 128, 16), C
    else:
        tr, tc = R, _tile(C, 256)
    nr, nc = R // tr, C // tc

    def body(chip_ref, w_ref, m_ref, v_ref, *rest):
        srcs, (g_ref, d_ref, mo_ref, vo_ref) = rest[:2 * N_CHIPS * L], rest[2 * N_CHIPS * L:]
        for l in range(L):
            @pl.when(pl.program_id(0) == l)
            def _():
                s = srcs[2 * N_CHIPS * l:2 * N_CHIPS * (l + 1)]
                mine, other = s[0][...].astype(F32), s[N_CHIPS][...].astype(F32)
                for k in range(1, N_CHIPS):
                    mine = mine + s[k][...].astype(F32)
                    other = other + s[N_CHIPS + k][...].astype(F32)
                g = mine + other
                delta, mn, vn = _adamw_math(w_ref[...], g, m_ref[...], v_ref[...])
                g_ref[...] = g
                d_ref[...] = delta
                mo_ref[...] = mn
                vo_ref[...] = vn

    tile = pl.BlockSpec((None, tr, tc), lambda l, i, j, chip_ref: (l, i, j))

    def block(layer, k):
        def index(l, i, j, chip_ref):
            idle_i, idle_j = jnp.where(l < layer, 0, nr - 1), jnp.where(l < layer, 0, nc - 1)
            return (jnp.bitwise_xor(chip_ref[0], k), jnp.where(l == layer, i, idle_i), jnp.where(l == layer, j, idle_j))
        return pl.BlockSpec((None, tr, tc), index)

    in_specs, args = [tile] * 3, [w, m, v]
    for layer, (parts, land, sib) in enumerate(groups):
        in_specs += [block(layer, k) for k in range(N_CHIPS)] * 2
        args += [parts, land, land, land, sib, sib, sib, sib]
    return pl.pallas_call(
        body, name=name,
        grid_spec=pltpu.PrefetchScalarGridSpec(num_scalar_prefetch=1, grid=(L, nr, nc), in_specs=in_specs, out_specs=[tile] * 4),
        out_shape=[jax.ShapeDtypeStruct((L, R, C), F32)] * 4, compiler_params=_params(("arbitrary", "arbitrary", "arbitrary")),
    )(chip, *args)


def _adamw_ada(w, m, v, act_t, dmod, *, name, tr=256, deps=()):
    L, D, Ns = w.shape
    tr = _tile(D, tr, 8)

    def body(w_ref, m_ref, v_ref, a_ref, d_ref, g_ref, dl_ref, mo_ref, vo_ref):
        x, y, _ = _me()
        g = jnp.dot(a_ref[...], d_ref[2 * x + y], preferred_element_type=F32, precision=lax.Precision.HIGHEST)
        delta, mn, vn = _adamw_math(w_ref[...], g, m_ref[...], v_ref[...])
        g_ref[...] = g
        dl_ref[...] = delta
        mo_ref[...] = mn
        vo_ref[...] = vn

    tile = pl.BlockSpec((None, tr, Ns), lambda l, i: (l, i, 0))
    return _call(
        body, deps, name=name, grid=(L, D // tr),
        in_specs=[tile] * 3 + [pl.BlockSpec((tr, N_DEV), lambda l, i: (i, 0)),
                               pl.BlockSpec((N_CHIPS, None, N_DEV, Ns), lambda l, i: (0, l, 0, 0))],
        out_specs=[tile] * 4, out_shape=[jax.ShapeDtypeStruct((L, D, Ns), F32)] * 4,
        compiler_params=_params(("parallel", "parallel")),
    )(w, m, v, act_t, dmod)


def _adamw_small(w, m, v, gathered, *, rows, name, deps=()):
    n, D = w.shape

    def body(w_ref, m_ref, v_ref, s_ref, g_ref, d_ref, mo_ref, vo_ref):
        for r, src in enumerate(rows):
            g = s_ref[0, src:src + 1, :]
            for d in range(1, N_DEV):
                g = g + s_ref[d, src:src + 1, :]
            g_ref[r:r + 1, :] = g
        g = g_ref[...]
        delta, mn, vn = _adamw_math(w_ref[...], g, m_ref[...], v_ref[...])
        d_ref[...] = delta
        mo_ref[...] = mn
        vo_ref[...] = vn

    vm = pl.BlockSpec(memory_space=pltpu.VMEM)
    return _call(
        body, deps, name=name, in_specs=[vm] * 4, out_specs=[vm] * 4,
        out_shape=[jax.ShapeDtypeStruct((n, D), F32)] * 4, compiler_params=_params(),
    )(w, m, v, gathered)


def _adamw_conv_w(w, m, v, gathered4, *, name, deps=()):
    Cs = w.shape[1]

    def body(w_ref, m_ref, v_ref, s_ref, g_ref, d_ref, mo_ref, vo_ref):
        x, y, _ = _me()
        j = 2 * x + y
        g = s_ref[j, 0]
        for d in range(1, N_DEV):
            g = g + s_ref[j, d]
        delta, mn, vn = _adamw_math(w_ref[...], g, m_ref[...], v_ref[...])
        g_ref[...] = g
        d_ref[...] = delta
        mo_ref[...] = mn
        vo_ref[...] = vn

    vm = pl.BlockSpec(memory_space=pltpu.VMEM)
    return _call(
        body, deps, name=name, in_specs=[vm] * 4, out_specs=[vm] * 4,
        out_shape=[jax.ShapeDtypeStruct((8, Cs), F32)] * 4, compiler_params=_params(),
    )(w, m, v, gathered4)


def _loss_sum(gathered, *, row, name, deps=()):
    _, _, D = gathered.shape

    def body(s_ref, o_ref):
        t = s_ref[0, row:row + 1, :]
        for d in range(1, N_DEV):
            t = t + s_ref[d, row:row + 1, :]
        o_ref[...] = jnp.broadcast_to(t, (8, D))

    vm = pl.BlockSpec(memory_space=pltpu.VMEM)
    return pl.pallas_call(body, name=name, in_specs=[vm], out_specs=vm, out_shape=jax.ShapeDtypeStruct((8, D), F32),
                          compiler_params=_params())(gathered)


def _pad_rows(a, n):
    return jnp.pad(a, ((0, n - a.shape[0]), (0, 0)))


def kernel(x, c, ada_w, ada_b, norm_mix, norm_mlp, fox_w_in, fox_b_f, fox_w_out, conv_w_in, conv_w, conv_w_out, mlp_w_up, mlp_w_down, final_norm, loss_target, m_ada_w, m_ada_b, m_norm_mix, m_norm_mlp, m_fox_w_in, m_fox_b_f, m_fox_w_out, m_conv_w_in, m_conv_w, m_conv_w_out, m_mlp_w_up, m_mlp_w_down, m_final_norm, v_ada_w, v_ada_b, v_norm_mix, v_norm_mlp, v_fox_w_in, v_fox_b_f, v_fox_w_out, v_conv_w_in, v_conv_w, v_conv_w_out, v_mlp_w_up, v_mlp_w_down, v_final_norm):
    S, D = x.shape[1], x.shape[2]
    H = fox_b_f.shape[-1]
    L = ada_w.shape[0]
    NM = ada_b.shape[1] // D
    Ns_ada = ada_w.shape[2]
    Cs_fox = fox_w_in.shape[2]
    Cs_conv = conv_w.shape[2]
    x0 = x[0]
    target = loss_target[0]

    chip = (2 * lax.axis_index("x") + lax.axis_index("y")).astype(jnp.int32).reshape(1)

    fin_t = jnp.transpose(fox_w_in, (0, 2, 1))
    shards = dict(fin=(fin_t, 0), fout=(fox_w_out, 0), up0=(mlp_w_up, 0), dn0=(mlp_w_down, 0), cin=(conv_w_in, 0),
                  cout=(conv_w_out, 0), up1=(mlp_w_up, 1), dn1=(mlp_w_down, 1))
    halves = dict(fin="cols", fout="rows", up0="cols", dn0="rows", cin="cols", cout="rows", up1="cols", dn1="rows")
    gathers = {}

    def start_gather(key, dep=None):
        land = _place_cast(*shards[key], chip, halves=halves[key], name="place_" + key)
        gathers[key] = _split_start("gather1", [(land,)], name="gather_start_" + key, dep=dep)
        return gathers[key][3]

    def pass_gather(key, after):
        landed = _split_wait("gather1", gathers[key], after, name="gather_landed_" + key)
        gathers[key] = _split_start("gather2", landed, name="gather_pass_" + key)
        return gathers[key][3]

    def gathered(key, after):
        return _split_wait("gather2", gathers[key], after, name="gather_wait_" + key)[0][0]

    tok = start_gather("fin")

    c_all = _allgather8(_pad_rows(c, 8), name="gather_c", deps=(tok,))[:, 0, :]
    mod_part, c_act = _ada_fwd(c_all, ada_w, name="ada_fwd")
    mod_all = _allgather8(mod_part.reshape(L * N_DEV, Ns_ada), name="gather_mod")
    mod = _select_mod(mod_all, name="select_mod")
    mod = jnp.transpose(mod, (1, 0, 2)).reshape(L, NM, 1, D) + ada_b.reshape(L, NM, 1, D)
    conv_w_all = _allgather8(_pad_rows(conv_w[0], 8), name="gather_conv_w")
    conv_w_full = jnp.transpose(conv_w_all[0::2, :3, :], (1, 0, 2)).reshape(3, D)

    def vec(a):
        return a.reshape(1, D)

    h0 = _norm_fwd(x0, vec(norm_mix[0]), mod[0, 1], mod[0, 0], name="norm_mix0", deps=(conv_w_all,))
    tok = h0
    for key in ("fout", "up0", "dn0", "cin", "cout", "up1", "dn1"):
        tok = start_gather(key, tok)
    tok = pass_gather("fin", [h0, tok])
    w_fin_t = jnp.transpose(gathered("fin", [tok]), (0, 2, 1, 3)).reshape(N_CHIPS * Cs_fox, D)
    w_f_t = _pad_rows(w_fin_t[3 * D:], LANES)
    tok = pass_gather("fout", [w_fin_t])
    qkv = _mm_nt(h0, w_fin_t, n=3 * D, name="fox_in", out_dtype=BF16, deps=(tok,))
    tok = pass_gather("up0", [qkv])
    b_f = jnp.pad(fox_b_f, ((0, 0), (0, LANES - H)))
    z_f, F_col = _fgate_fwd(h0, w_f_t, b_f, name="fgate_fwd", deps=(tok,))
    hp = max(8, H)
    at_f, at = _tile(S, 1024, 16), _tile(S, 512, 16)
    F_rows = _pad_rows(jnp.transpose(F_col[:, :H]), hp)
    F_row = jnp.transpose(F_rows.reshape(hp, S // at, at), (1, 0, 2))
    o, lse = _attn_fwd(qkv, F_col, heads=H, name="attn_fwd", T=at_f)
    w_fout = gathered("fout", [o]).reshape(D, D)
    tok = pass_gather("dn0", [o])
    x1, mix0, h1 = _mm_nn(o, w_fout, name="fox_out", epilogue="resid_norm", res=x0, gate=mod[0, 2],
                          norm=(vec(norm_mlp[0]), mod[0, 4], mod[0, 3]), tm=512, tn=D, deps=(tok,))
    w_up0 = gathered("up0", [h1]).reshape(2 * N_CHIPS, D, -1)
    tok = pass_gather("cin", [h1])
    u0, a0 = _mm_nn(h1, w_up0, name="mlp_up0", epilogue="relu2", deps=(tok,))
    w_dn0 = gathered("dn0", [a0]).reshape(-1, D)
    tok = pass_gather("cout", [a0])
    x2, y0 = _mm_nn(a0, w_dn0, name="mlp_down0", epilogue="resid", res=x1, gate=mod[0, 5], deps=(tok,))
    h2 = _norm_fwd(x2, vec(norm_mix[1]), mod[1, 1], mod[1, 0], name="norm_mix1")
    g_cin = gathered("cin", [h2]).reshape(2 * N_CHIPS, D, -1)
    tok = pass_gather("up1", [h2])
    proj = _mm_nn(h2, g_cin, name="conv_in", deps=(tok,))
    w_cin = jnp.transpose(g_cin, (1, 0, 2)).reshape(D, 3 * D)
    zc = _conv_fwd(proj, conv_w_full, name="conv_fwd")
    w_cout = gathered("cout", [zc]).reshape(D, D)
    tok = pass_gather("dn1", [zc])
    x3, mix1, h3 = _mm_nn(zc, w_cout, name="conv_out", epilogue="resid_norm", res=x2, gate=mod[1, 2],
                          norm=(vec(norm_mlp[1]), mod[1, 4], mod[1, 3]), tm=512, tn=D, deps=(tok,))
    w_up1 = gathered("up1", [h3]).reshape(2 * N_CHIPS, D, -1)
    u1, a1 = _mm_nn(h3, w_up1, name="mlp_up1", epilogue="relu2")
    w_dn1 = gathered("dn1", [a1]).reshape(-1, D)
    x4, y1 = _mm_nn(a1, w_dn1, name="mlp_down1", epilogue="resid", res=x3, gate=mod[1, 5])
    w_up = [jnp.transpose(w_up0, (1, 0, 2)).reshape(D, -1), jnp.transpose(w_up1, (1, 0, 2)).reshape(D, -1)]
    w_dn = [w_dn0, w_dn1]

    dx4, dy1, sums_f = _loss_bwd(x4, target, vec(final_norm), mod[1, 5], name="loss_bwd")
    du1 = _mm_nt(dy1, w_dn[1], name="mlp_down1_dx", epilogue="drelu2", extra=u1, out_dtype=BF16)
    def start_scatter(tag, parts_list):
        groups = [(p, lax.empty(p.shape, p.dtype)) for p in parts_list]
        return _split_start("scatter", groups, name="scatter_start_" + tag)

    def start_sibling(tag, scatter, after):
        landed = _split_wait("scatter", scatter, after, name="scatter_wait_" + tag)
        groups = [(p, ld, lax.empty(p.shape, p.dtype)) for p, ld in landed]
        return _split_start("sibling", groups, name="sibling_start_" + tag)

    gw_dn1 = _mm_tn(a1, dy1, name="mlp_down1_dw")
    gw_up1 = _mm_tn(h3, du1, name="mlp_up1_dw", out_parts=N_CHIPS)
    sc1 = start_scatter("mlp1", [gw_dn1.reshape(N_CHIPS, -1, D), gw_up1])
    dh3 = _mm_nt(du1, w_up[1], name="mlp_up1_dx", out_dtype=BF16, deps=(sc1[3],))
    dx3, dmix1, sums_mlp1 = _norm_bwd(x3, dh3, dx4, y1, vec(norm_mlp[1]), mod[1, 4], mod[1, 2], name="norm_mlp1_bwd")
    dzc = _mm_nt(dmix1, w_cout, name="conv_out_dx", out_dtype=BF16)
    gw_cout = _mm_tn(zc, dmix1, name="conv_out_dw")
    dproj, dconv_w = _conv_bwd(proj, conv_w_full, dzc, name="conv_bwd")
    gw_cin = _mm_tn(h2, dproj, name="conv_in_dw", out_parts=N_CHIPS, tn=512)
    sc2 = start_scatter("conv", [gw_cout.reshape(N_CHIPS, -1, D), gw_cin])
    dh2 = _mm_nt(dproj, w_cin, name="conv_in_dx", out_dtype=BF16, deps=(sc2[3],))
    dx2, dy0, sums_mix1 = _norm_bwd(x2, dh2, dx3, mix1, vec(norm_mix[1]), mod[1, 1], mod[0, 5], name="norm_mix1_bwd")
    du0 = _mm_nt(dy0, w_dn[0], name="mlp_down0_dx", epilogue="drelu2", extra=u0, out_dtype=BF16)
    gw_dn0 = _mm_tn(a0, dy0, name="mlp_down0_dw")
    gw_up0 = _mm_tn(h1, du0, name="mlp_up0_dw", out_parts=N_CHIPS)
    sc3 = start_scatter("mlp0", [gw_dn0.reshape(N_CHIPS, -1, D), gw_up0])
    sb1 = start_sibling("mlp1", sc1, [sc3[3]])
    dh1 = _mm_nt(du0, w_up[0], name="mlp_up0_dx", out_dtype=BF16, deps=(sb1[3],))
    dx1, dmix0, sums_mlp0 = _norm_bwd(x1, dh1, dx2, y0, vec(norm_mlp[0]), mod[0, 4], mod[0, 2], name="norm_mlp0_bwd")
    do = _mm_nt(dmix0, w_fout, name="fox_out_dx", out_dtype=BF16)
    gw_fout = _mm_tn(o, dmix0, name="fox_out_dw")
    dqkv, dfq, dfk = _attn_bwd(qkv, o, do, F_col, F_row, lse, heads=H, name="attn_bwd", T=at)
    dfk_col = jnp.pad(jnp.transpose(jnp.transpose(dfk, (1, 0, 2)).reshape(hp, S)[:H]), ((0, 0), (0, LANES - H)))
    sb3 = start_sibling("mlp0", sc3, [dqkv])
    dz_f, sums_bf = _fgate_bwd(dfq, dfk_col, z_f, name="fgate_bwd", deps=(sb3[3],))
    gw_qkv_t = _mm_tn(dqkv, h0, name="fox_in_dw")
    gw_f_t = _mm_tn(dz_f, h0, name="fox_gate_dw")
    gw_fin_t = jnp.concatenate([gw_qkv_t, gw_f_t[:H]], axis=0).reshape(N_CHIPS, Cs_fox, D)
    sc4 = start_scatter("fox", [gw_fout.reshape(N_CHIPS, -1, D), gw_fin_t])
    sb2 = start_sibling("conv", sc2, [sc4[3]])
    dh0_f = _mm_nn(dz_f, w_f_t, name="fox_gate_dx", out_dtype=F32, deps=(sb2[3],))
    dh0 = _mm_nn(dqkv, w_fin_t, name="fox_in_dx", epilogue="add", res=dh0_f, out_dtype=BF16)
    grad_x, _, sums_mix0 = _norm_bwd(x0, dh0, dx1, mix0, vec(norm_mix[0]), mod[0, 1], None, name="norm_mix0_bwd")

    outs = {}

    def put(name_, res, shape):
        for kind, r in zip(("grad", "delta", "new_m", "new_v"), res):
            outs[kind + "_" + name_] = r.reshape(shape)

    def shards_update(tag, w_, m_, v_, groups):
        return _adamw_shards(w_, m_, v_, groups, chip, name="adamw_" + tag)

    g_conv = _split_wait("sibling", sb2, [grad_x], name="sibling_wait_conv")
    put("conv_w_out", shards_update("conv_out", conv_w_out, m_conv_w_out, v_conv_w_out, g_conv[0:1]), conv_w_out.shape)
    r_cin = shards_update("conv_in", conv_w_in, m_conv_w_in, v_conv_w_in, g_conv[1:2])
    put("conv_w_in", r_cin, conv_w_in.shape)
    g_mlp1 = _split_wait("sibling", sb1, [r_cin[0]], name="sibling_wait_mlp1")
    g_mlp0 = _split_wait("sibling", sb3, [r_cin[0]], name="sibling_wait_mlp0")
    put("mlp_w_down", shards_update("mlp_down", mlp_w_down, m_mlp_w_down, v_mlp_w_down, [g_mlp0[0], g_mlp1[0]]), mlp_w_down.shape)
    r_up = shards_update("mlp_up", mlp_w_up, m_mlp_w_up, v_mlp_w_up, [g_mlp0[1], g_mlp1[1]])
    put("mlp_w_up", r_up, mlp_w_up.shape)
    sb4 = start_sibling("fox", sc4, [r_up[0]])

    dmod_rows = []
    for sm, sl in ((sums_mix0, sums_mlp0), (sums_mix1, sums_mlp1)):
        dmod_rows += [sm[0:1], sm[1:2], sm[3:4], sl[0:1], sl[1:2], sl[3:4]]
    bf_row = jnp.pad(sums_bf[0:1], ((0, 0), (0, D - LANES)))
    small = jnp.concatenate([sums_mix0[2:3], sums_mix1[2:3], sums_mlp0[2:3], sums_mlp1[2:3], sums_f[0:1], sums_f[1:2], bf_row,
                             jnp.zeros((1, D), F32)] + dmod_rows + [dconv_w[0:3]], axis=0)
    small_all = _allgather8(_pad_rows(small, -(-small.shape[0] // 8) * 8), name="gather_small", deps=(sb4[3],))
    loss = _loss_sum(small_all, row=5, name="loss_sum")[0, 0]

    def rows_of(a_mix, a_mlp, a_fin, a_bf, a_ada):
        return jnp.concatenate([a_mix, a_mlp, a_fin.reshape(1, D), jnp.pad(a_bf, ((0, 0), (0, D - H))),
                                a_ada.reshape(L * NM, D)], axis=0)
    n_small = 2 * L + 2 + L * NM
    rw = -(-n_small // 8) * 8
    w_s = _pad_rows(rows_of(norm_mix, norm_mlp, final_norm, fox_b_f, ada_b), rw)
    m_s = _pad_rows(rows_of(m_norm_mix, m_norm_mlp, m_final_norm, m_fox_b_f, m_ada_b), rw)
    v_s = _pad_rows(rows_of(v_norm_mix, v_norm_mlp, v_final_norm, v_fox_b_f, v_ada_b), rw)
    src_rows = [0, 1, 2, 3, 4, 6] + [8 + r for r in range(L * NM)] + [7] * (rw - n_small)
    res_s = _adamw_small(w_s, m_s, v_s, small_all, rows=tuple(src_rows), name="adamw_small")
    for kind, r in zip(("grad", "delta", "new_m", "new_v"), res_s):
        outs[kind + "_norm_mix"] = r[0:L]
        outs[kind + "_norm_mlp"] = r[L:2 * L]
        outs[kind + "_final_norm"] = r[2 * L]
        outs[kind + "_fox_b_f"] = r[2 * L + 1:2 * L + 2, :H]
        outs[kind + "_ada_b"] = r[2 * L + 2:n_small].reshape(L, NM * D)

    dmod_all = small_all[:, 8:8 + L * NM, :].reshape(N_DEV, L, N_CHIPS, Ns_ada)
    dmod4 = jnp.transpose(dmod_all, (2, 1, 0, 3))
    act_t = jnp.transpose(c_act)
    res_a = _adamw_ada(ada_w, m_ada_w, v_ada_w, act_t, dmod4, name="adamw_ada")
    put("ada_w", res_a, ada_w.shape)

    r0 = 8 + L * NM
    dconv_all = jnp.pad(small_all[:, r0:r0 + 3, :], ((0, 0), (0, 5), (0, 0)))
    dconv4 = jnp.transpose(dconv_all.reshape(N_DEV, 8, N_CHIPS, Cs_conv), (2, 0, 1, 3))
    res_c = _adamw_conv_w(_pad_rows(conv_w[0], 8), _pad_rows(m_conv_w[0], 8), _pad_rows(v_conv_w[0], 8), dconv4,
                          name="adamw_conv_w")
    for kind, r in zip(("grad", "delta", "new_m", "new_v"), res_c):
        outs[kind + "_conv_w"] = r[:3].reshape(conv_w.shape)

    g_fox = _split_wait("sibling", sb4, [res_a[0], res_c[0], res_s[0]], name="sibling_wait_fox")
    put("fox_w_out", shards_update("fox_out", fox_w_out, m_fox_w_out, v_fox_w_out, g_fox[0:1]), fox_w_out.shape)
    t3 = lambda a: jnp.transpose(a, (0, 2, 1))
    for kind, r in zip(("grad", "delta", "new_m", "new_v"),
                       shards_update("fox_in", t3(fox_w_in), t3(m_fox_w_in), t3(v_fox_w_in), g_fox[1:2])):
        outs[kind + "_fox_w_in"] = t3(r)

    names = ["ada_w", "ada_b", "norm_mix", "norm_mlp", "fox_w_in", "fox_b_f", "fox_w_out", "conv_w_in", "conv_w", "conv_w_out",
             "mlp_w_up", "mlp_w_down", "final_norm"]
    return (loss, grad_x[None], *[outs["grad_" + n] for n in names], *[outs["delta_" + n] for n in names],
            *[outs["new_m_" + n] for n in names], *[outs["new_v_" + n] for n in names])
```

```python
import functools

import jax
import jax.numpy as jnp
from jax import lax
from jax.experimental import pallas as pl
from jax.experimental.pallas import tpu as pltpu

F32 = jnp.float32
BF16 = jnp.bfloat16
MESH = pl.DeviceIdType.MESH
ANY = pl.BlockSpec(memory_space=pl.ANY)
HBM = pl.BlockSpec(memory_space=pltpu.HBM)
SEM = pl.BlockSpec(memory_space=pltpu.SEMAPHORE)
EFFECT = pltpu.SideEffectType.DATAFLOW_SIDE_EFFECTING

RMS_EPS = 1e-6
ADAM_LR = 0.001
ADAM_B1 = 0.9
ADAM_B2 = 0.999
ADAM_EPS = 1e-08
ADAM_WD = 0.01
ADAM_STEP = 10
N_CHIPS = 4
N_DEV = 8
LANES = 128
VMEM_LIMIT = 56 * 1024 * 1024
NEG = -1e30


def _params(sems=None, vmem=VMEM_LIMIT):
    return pltpu.CompilerParams(dimension_semantics=sems, vmem_limit_bytes=vmem)


def _tile(n, pref, unit=LANES):
    if n <= pref:
        return n
    t = (pref // unit) * unit
    while n % t:
        t -= unit
    return t


def _me():
    return lax.axis_index("x"), lax.axis_index("y"), lax.axis_index("c")


def _call(body, deps, **kw):
    nd = len(deps)

    def wrapped(*refs):
        body(*refs[nd:])

    kw["in_specs"] = [ANY] * nd + list(kw["in_specs"])
    fn = pl.pallas_call(wrapped, **kw)
    return lambda *args: fn(*deps, *args)


def _allgather8(v, *, name, deps=()):
    R, C = v.shape

    def body(v_ref, out_ref, send_sems, recv_sems):
        x, y, c = _me()
        me = 4 * x + 2 * y + c
        out_ref[me] = v_ref[...]
        copies = []
        for k in range(1, N_DEV):
            px, py, pc = (x + (k >> 2)) % 2, (y + ((k >> 1) & 1)) % 2, (c + (k & 1)) % 2
            copies.append(pltpu.make_async_remote_copy(
                src_ref=v_ref, dst_ref=out_ref.at[me], send_sem=send_sems.at[k - 1], recv_sem=recv_sems.at[k - 1],
                device_id=(px, py, pc), device_id_type=MESH))
        for cp in copies:
            cp.start()
        for k in range(1, N_DEV):
            px, py, pc = (x + (k >> 2)) % 2, (y + ((k >> 1) & 1)) % 2, (c + (k & 1)) % 2
            peer = 4 * px + 2 * py + pc
            pltpu.make_async_remote_copy(
                src_ref=v_ref, dst_ref=out_ref.at[peer], send_sem=send_sems.at[k - 1], recv_sem=recv_sems.at[k - 1],
                device_id=(px, py, pc), device_id_type=MESH).wait_recv()
        for cp in copies:
            cp.wait_send()

    return _call(
        body, deps, name=name,
        out_shape=jax.ShapeDtypeStruct((N_DEV, R, C), v.dtype),
        in_specs=[pl.BlockSpec(memory_space=pltpu.VMEM)],
        out_specs=pl.BlockSpec(memory_space=pltpu.VMEM),
        scratch_shapes=[pltpu.SemaphoreType.DMA((N_DEV - 1,)), pltpu.SemaphoreType.DMA((N_DEV - 1,))],
        compiler_params=_params(),
    )(v)


def _chip_peers(x, y):
    return [((x + (k >> 1)) % 2, (y + (k & 1)) % 2) for k in range(1, N_CHIPS)]


def _slot(x, y, k):
    return 2 * ((x + (k >> 1)) % 2) + (y + (k & 1)) % 2


def _split_copies(kind, groups, send_sems, recv_sems):
    x, y, c = _me()
    j = 2 * x + y
    copies = []
    for a, g in enumerate(groups):
        if kind == "sibling":
            parts, land, sib = g
            for k in range(N_CHIPS):
                s = _slot(x, y, k)
                copies.append(pltpu.make_async_remote_copy(
                    src_ref=(parts if k == 0 else land).at[s], dst_ref=sib.at[s], send_sem=send_sems.at[N_CHIPS * a + k],
                    recv_sem=recv_sems.at[N_CHIPS * a + k], device_id=(x, y, 1 - c), device_id_type=MESH))
            continue
        land = g[-1]
        for k, (px, py) in enumerate(_chip_peers(x, y)):
            if kind == "gather1":
                src, dst, to = land.at[j, c], land.at[j, c], (px, py, c)
            elif kind == "gather2":
                src, dst, to = land.at[2 * px + py, c], land.at[2 * px + py, c], (x, y, 1 - c)
            else:
                src, dst, to = g[0].at[2 * px + py], land.at[j], (px, py, c)
            copies.append(pltpu.make_async_remote_copy(
                src_ref=src, dst_ref=dst, send_sem=send_sems.at[3 * a + k], recv_sem=recv_sems.at[3 * a + k],
                device_id=to, device_id_type=MESH))
    return copies


def _split_start(kind, groups, *, name, dep=None):
    flat = [a for g in groups for a in g]
    nf, per = len(flat), len(groups[0])
    ncp = len(groups) * (N_CHIPS if kind == "sibling" else 3)
    nd = 0 if dep is None else 1

    def body(*refs):
        ins = refs[nd:nd + nf]
        send_sems, recv_sems, token = refs[nd + nf], refs[nd + nf + 1], refs[-1]
        for cp in _split_copies(kind, [ins[i:i + per] for i in range(0, nf, per)], send_sems, recv_sems):
            cp.start()
        token[...] = jnp.zeros_like(token)

    outs = pl.pallas_call(
        body, name=name,
        out_shape=(pltpu.SemaphoreType.DMA((ncp,)), pltpu.SemaphoreType.DMA((ncp,)), *[pltpu.HBM(a.shape, a.dtype) for a in flat],
                   jax.ShapeDtypeStruct((8, LANES), F32)),
        in_specs=[ANY] * nd + [HBM] * nf,
        out_specs=(SEM, SEM, *[HBM] * nf, pl.BlockSpec(memory_space=pltpu.VMEM)),
        input_output_aliases={nd + i: 2 + i for i in range(nf)},
        compiler_params=pltpu.CompilerParams(has_side_effects=EFFECT),
    )(*([dep] if nd else []), *[pltpu.with_memory_space_constraint(a, pltpu.HBM) for a in flat])
    thru = list(outs[2:2 + nf])
    return outs[0], outs[1], [tuple(thru[i:i + per]) for i in range(0, nf, per)], outs[-1]


def _split_wait(kind, started, after, *, name):
    send_sems, recv_sems, groups, _ = started
    flat = [a for g in groups for a in g]
    nf, per = len(flat), len(groups[0])

    def body(*refs):
        ins = refs[:nf]
        for cp in _split_copies(kind, [ins[i:i + per] for i in range(0, nf, per)], refs[nf], refs[nf + 1]):
            cp.wait_send()
            cp.wait_recv()

    outs = pl.pallas_call(
        body, name=name,
        out_shape=tuple(pltpu.HBM(a.shape, a.dtype) for a in flat),
        in_specs=[HBM] * nf + [SEM, SEM] + [ANY] * len(after), out_specs=tuple([HBM] * nf),
        input_output_aliases={i: i for i in range(nf)},
        compiler_params=pltpu.CompilerParams(has_side_effects=EFFECT),
    )(*flat, send_sems, recv_sems, *after)
    outs = list(outs)
    return [tuple(outs[i:i + per]) for i in range(0, nf, per)]


def _place_cast(shards, layer, chip, *, halves, name, dep=None):
    _, R, C = shards.shape
    if halves == "rows":
        hr, hc = R // 2, C
    else:
        hr, hc = R, C // 2
    if hr % 16 == 0:
        tr, tc = _tile(hr, 512, 16), hc
    else:
        tr, tc = hr, _tile(hc, 256)
    nr, nc = hr // tr, hc // tc

    def body(chip_ref, x_ref, *rest):
        rest[-1][...] = x_ref[...].astype(BF16)

    if halves == "rows":
        o_map = lambda i, j, chip_ref: (chip_ref[0], i // nr, i % nr, j)
    else:
        o_map = lambda i, j, chip_ref: (chip_ref[0], j // nc, i, j % nc)
    return pl.pallas_call(
        body, name=name,
        grid_spec=pltpu.PrefetchScalarGridSpec(
            num_scalar_prefetch=1, grid=(R // tr, C // tc),
            in_specs=[pl.BlockSpec((None, tr, tc), lambda i, j, chip_ref: (layer, i, j))] + ([] if dep is None else [ANY]),
            out_specs=pl.BlockSpec((None, None, tr, tc), o_map)),
        out_shape=jax.ShapeDtypeStruct((N_CHIPS, 2, hr, hc), BF16), compiler_params=_params(("parallel", "parallel")),
    )(chip, shards, *([] if dep is None else [dep]))


def _accumulate(part, acc_ref, nk, finalize):
    if nk == 1:
        finalize(part)
        return
    k = pl.program_id(2)

    @pl.when(k == 0)
    def _():
        acc_ref[...] = part

    @pl.when(k > 0)
    def _():
        acc_ref[...] += part

    @pl.when(k == nk - 1)
    def _():
        finalize(acc_ref[...])


def _mm_nn(a, b, *, name, epilogue="plain", res=None, gate=None, norm=None, out_dtype=BF16, tm=1024, tn=1024, tk=2048, deps=()):
    if a.ndim == 3:
        Q, M, Kq = a.shape
        K = Q * Kq
    else:
        (M, K), Kq = a.shape, a.shape[1]
    tm, tk = _tile(M, tm, 16), _tile(Kq, tk)
    if a.ndim == 3:
        pa = Kq // tk
        a_spec = pl.BlockSpec((None, tm, tk), lambda i, j, k: (k // pa, i, k % pa))
    else:
        a_spec = pl.BlockSpec((tm, tk), lambda i, j, k: (i, k))
    if b.ndim == 3:
        P, _, Ns = b.shape
        N = P * Ns
        tn = _tile(Ns, tn)
        per = Ns // tn
        b_spec = pl.BlockSpec((None, tk, tn), lambda i, j, k: (j // per, k, j % per))
    else:
        N = b.shape[1]
        tn = _tile(N, tn)
        b_spec = pl.BlockSpec((tk, tn), lambda i, j, k: (k, j))
    nk = K // tk
    tile = pl.BlockSpec((tm, tn), lambda i, j, k: (i, j))

    def body(*refs):
        acc_ref = refs[-1] if nk > 1 else None
        a_ref, b_ref = refs[0], refs[1]
        part = jnp.dot(a_ref[...], b_ref[...], preferred_element_type=F32)
        if epilogue == "plain":
            def fin(acc):
                refs[2][...] = acc.astype(out_dtype)
        elif epilogue == "relu2":
            def fin(acc):
                refs[2][...] = acc.astype(BF16)
                refs[3][...] = jnp.square(jnp.maximum(acc, 0.0)).astype(BF16)
        elif epilogue == "add":
            def fin(acc):
                refs[3][...] = (acc + refs[2][...]).astype(out_dtype)
        elif epilogue == "resid":
            def fin(acc):
                refs[4][...] = refs[2][...] + refs[3][...] * acc
                refs[5][...] = acc.astype(BF16)
        else:
            def fin(acc):
                xv = refs[2][...] + refs[3][...] * acc
                refs[7][...] = xv
                refs[8][...] = acc.astype(BF16)
                r = lax.rsqrt(jnp.mean(xv * xv, axis=-1, keepdims=True) + RMS_EPS)
                refs[9][...] = (((xv * r) * refs[4][...]) * (1.0 + refs[5][...]) + refs[6][...]).astype(BF16)
        _accumulate(part, acc_ref, nk, fin)

    in_specs = [a_spec, b_spec]
    args = [a, b]
    if epilogue == "plain":
        out_shape, out_specs = jax.ShapeDtypeStruct((M, N), out_dtype), tile
    elif epilogue == "relu2":
        out_shape, out_specs = [jax.ShapeDtypeStruct((M, N), BF16)] * 2, [tile, tile]
    elif epilogue == "add":
        in_specs.append(tile)
        args.append(res)
        out_shape, out_specs = jax.ShapeDtypeStruct((M, N), out_dtype), tile
    else:
        row = pl.BlockSpec((1, tn), lambda i, j, k: (0, j))
        in_specs += [tile, row]
        args += [res, gate]
        out_shape, out_specs = [jax.ShapeDtypeStruct((M, N), F32), jax.ShapeDtypeStruct((M, N), BF16)], [tile, tile]
        if epilogue == "resid_norm":
            assert tn == N, "the next norm needs whole rows"
            in_specs += [row, row, row]
            args += list(norm)
            out_shape, out_specs = out_shape + [jax.ShapeDtypeStruct((M, N), BF16)], out_specs + [tile]
    return _call(
        body, deps, name=name, grid=(M // tm, N // tn, nk), in_specs=in_specs, out_specs=out_specs, out_shape=out_shape,
        scratch_shapes=[pltpu.VMEM((tm, tn), F32)] if nk > 1 else [],
        compiler_params=_params(("parallel", "parallel", "arbitrary")),
    )(*args)


def _mm_nt(a, b, *, name, n=None, epilogue="plain", extra=None, out_dtype=F32, tm=1024, tn=1024, tk=2048, deps=()):
    if a.ndim == 3:
        Q, M, Kq = a.shape
        K = Q * Kq
    else:
        (M, K), Kq = a.shape, a.shape[1]
    if b.ndim == 3:
        P, N, Ks = b.shape
    else:
        N, Ks = b.shape
    N = n or N
    tm, tn, tk = _tile(M, tm, 16), _tile(N, tn), _tile(min(Kq, Ks), tk)
    nk = K // tk
    if a.ndim == 3:
        pa = Kq // tk
        a_spec = pl.BlockSpec((None, tm, tk), lambda i, j, k: (k // pa, i, k % pa))
    else:
        a_spec = pl.BlockSpec((tm, tk), lambda i, j, k: (i, k))
    if b.ndim == 3:
        pb = Ks // tk
        b_spec = pl.BlockSpec((None, tn, tk), lambda i, j, k: (k // pb, j, k % pb))
    else:
        b_spec = pl.BlockSpec((tn, tk), lambda i, j, k: (j, k))
    tile = pl.BlockSpec((tm, tn), lambda i, j, k: (i, j))

    def body(*refs):
        acc_ref = refs[-1] if nk > 1 else None
        part = lax.dot_general(refs[0][...], refs[1][...], (((1,), (1,)), ((), ())), preferred_element_type=F32)
        if epilogue == "plain":
            def fin(acc):
                refs[2][...] = acc.astype(out_dtype)
        elif epilogue == "add":
            def fin(acc):
                refs[3][...] = (acc + refs[2][...]).astype(out_dtype)
        else:
            def fin(acc):
                refs[3][...] = (acc * (2.0 * jnp.maximum(refs[2][...].astype(F32), 0.0))).astype(out_dtype)
        _accumulate(part, acc_ref, nk, fin)

    in_specs, args = [a_spec, b_spec], [a, b]
    if epilogue != "plain":
        in_specs.append(tile)
        args.append(extra)
    return _call(
        body, deps, name=name, grid=(M // tm, N // tn, nk), in_specs=in_specs, out_specs=tile,
        out_shape=jax.ShapeDtypeStruct((M, N), out_dtype),
        scratch_shapes=[pltpu.VMEM((tm, tn), F32)] if nk > 1 else [],
        compiler_params=_params(("parallel", "parallel", "arbitrary")),
    )(*args)


def _mm_tn(a, b, *, name, out_parts=1, tm=1024, tn=1024, tk=4096, deps=()):
    if a.ndim == 3:
        Qa, M, Kq = a.shape
        Kd = Qa * Kq
    else:
        (M, Kd), Kq = a.shape, a.shape[1]
    if b.ndim == 3:
        Q, _, Nq = b.shape
        N = Q * Nq
    else:
        N, Nq = b.shape[1], b.shape[1]
    Ns = N // out_parts
    tn = _tile(Ns, tn)
    while Nq % tn or Ns % tn:
        tn -= LANES
    tm, tk = _tile(Kq, tm), _tile(M, tk, 16)
    nk = M // tk
    if a.ndim == 3:
        pa = Kq // tm
        a_spec = pl.BlockSpec((None, tk, tm), lambda i, j, k: (i // pa, k, i % pa))
    else:
        a_spec = pl.BlockSpec((tk, tm), lambda i, j, k: (k, i))
    if b.ndim == 3:
        pb = Nq // tn
        b_spec = pl.BlockSpec((None, tk, tn), lambda i, j, k: (j // pb, k, j % pb))
    else:
        b_spec = pl.BlockSpec((tk, tn), lambda i, j, k: (k, j))
    if out_parts > 1:
        po = Ns // tn
        o_spec = pl.BlockSpec((None, tm, tn), lambda i, j, k: (j // po, i, j % po))
        out_shape = jax.ShapeDtypeStruct((out_parts, Kd, Ns), BF16)
    else:
        o_spec = pl.BlockSpec((tm, tn), lambda i, j, k: (i, j))
        out_shape = jax.ShapeDtypeStruct((Kd, N), BF16)

    def body(*refs):
        acc_ref = refs[-1] if nk > 1 else None
        part = lax.dot_general(refs[0][...], refs[1][...], (((0,), (0,)), ((), ())), preferred_element_type=F32)

        def fin(acc):
            refs[2][...] = acc.astype(BF16)
        _accumulate(part, acc_ref, nk, fin)

    return _call(
        body, deps, name=name, grid=(Kd // tm, N // tn, nk),
        in_specs=[a_spec, b_spec], out_specs=o_spec, out_shape=out_shape,
        scratch_shapes=[pltpu.VMEM((tm, tn), F32)] if nk > 1 else [],
        compiler_params=_params(("parallel", "parallel", "arbitrary")),
    )(a, b)


def _rows(S, D, i_map=lambda i: (i, 0), ts=512):
    return pl.BlockSpec((ts, D), i_map)


def _norm_fwd(x, gain, sc, sh, *, name, deps=()):
    S, D = x.shape
    ts = _tile(S, 512, 16)
    vec = pl.BlockSpec((1, D), lambda i: (0, 0))

    def body(x_ref, g_ref, sc_ref, sh_ref, h_ref):
        xv = x_ref[...]
        r = lax.rsqrt(jnp.mean(xv * xv, axis=-1, keepdims=True) + RMS_EPS)
        h = (xv * r) * g_ref[...]
        h_ref[...] = (h * (1.0 + sc_ref[...]) + sh_ref[...]).astype(BF16)

    return _call(
        body, deps, name=name, grid=(S // ts,), in_specs=[_rows(S, D, ts=ts), vec, vec, vec], out_specs=_rows(S, D, ts=ts),
        out_shape=jax.ShapeDtypeStruct((S, D), BF16), compiler_params=_params(("parallel",)),
    )(x, gain, sc, sh)


def _loss_bwd(x, target, gain, gate_prev, *, name, deps=()):
    S, D = x.shape
    ts = _tile(S, 256, 16)
    vec = pl.BlockSpec((1, D), lambda i: (0, 0))

    def body(x_ref, t_ref, g_ref, gp_ref, dx_ref, dp_ref, sums_ref):
        @pl.when(pl.program_id(0) == 0)
        def _():
            sums_ref[...] = jnp.zeros_like(sums_ref)
        xv = x_ref[...]
        r = lax.rsqrt(jnp.mean(xv * xv, axis=-1, keepdims=True) + RMS_EPS)
        xn = xv * r
        err = xn * g_ref[...] - t_ref[...]
        loss = 0.5 * jnp.sum(jnp.mean(err * err, axis=-1, keepdims=True), axis=0, keepdims=True)
        dy = err * (1.0 / D)
        dxn = dy * g_ref[...]
        dx = r * (dxn - xn * jnp.mean(dxn * xn, axis=-1, keepdims=True))
        dx_ref[...] = dx
        dp_ref[...] = (gp_ref[...] * dx).astype(BF16)
        sums_ref[0:1, :] += jnp.sum(dy * xn, axis=0, keepdims=True)
        sums_ref[1:2, :] += jnp.broadcast_to(loss, (1, D))

    return _call(
        body, deps, name=name, grid=(S // ts,),
        in_specs=[_rows(S, D, ts=ts), _rows(S, D, ts=ts), vec, vec],
        out_specs=[_rows(S, D, ts=ts), _rows(S, D, ts=ts), pl.BlockSpec((8, D), lambda i: (0, 0))],
        out_shape=[jax.ShapeDtypeStruct((S, D), F32), jax.ShapeDtypeStruct((S, D), BF16), jax.ShapeDtypeStruct((8, D), F32)],
        compiler_params=_params(("arbitrary",)),
    )(x, target, gain, gate_prev)


def _norm_bwd(x, dh, dxp, mix, gain, sc, gate_prev, *, name, deps=()):
    S, D = x.shape
    ts = _tile(S, 256, 16)
    vec = pl.BlockSpec((1, D), lambda i: (0, 0))
    with_prev = gate_prev is not None

    def body(*refs):
        x_ref, dh_ref, dxp_ref, mix_ref, g_ref, sc_ref = refs[:6]
        outs = refs[7:] if with_prev else refs[6:]
        sums_ref = outs[-1]

        @pl.when(pl.program_id(0) == 0)
        def _():
            sums_ref[...] = jnp.zeros_like(sums_ref)
        xv, dhv, dxpv = x_ref[...], dh_ref[...].astype(F32), dxp_ref[...]
        r = lax.rsqrt(jnp.mean(xv * xv, axis=-1, keepdims=True) + RMS_EPS)
        xn = xv * r
        hn = xn * g_ref[...]
        dhn = dhv * (1.0 + sc_ref[...])
        dxn = dhn * g_ref[...]
        dx = dxpv + r * (dxn - xn * jnp.mean(dxn * xn, axis=-1, keepdims=True))
        outs[0][...] = dx
        if with_prev:
            outs[1][...] = (refs[6][...] * dx).astype(BF16)
        sums_ref[0:1, :] += jnp.sum(dhv, axis=0, keepdims=True)
        sums_ref[1:2, :] += jnp.sum(dhv * hn, axis=0, keepdims=True)
        sums_ref[2:3, :] += jnp.sum(dhn * xn, axis=0, keepdims=True)
        sums_ref[3:4, :] += jnp.sum(dxpv * mix_ref[...].astype(F32), axis=0, keepdims=True)

    tile = _rows(S, D, ts=ts)
    in_specs = [tile, tile, tile, tile, vec, vec] + ([vec] if with_prev else [])
    args = [x, dh, dxp, mix, gain, sc] + ([gate_prev] if with_prev else [])
    out_specs = [tile] + ([tile] if with_prev else []) + [pl.BlockSpec((8, D), lambda i: (0, 0))]
    out_shape = ([jax.ShapeDtypeStruct((S, D), F32)] + ([jax.ShapeDtypeStruct((S, D), BF16)] if with_prev else [])
                 + [jax.ShapeDtypeStruct((8, D), F32)])
    outs = _call(
        body, deps, name=name, grid=(S // ts,), in_specs=in_specs, out_specs=out_specs, out_shape=out_shape,
        compiler_params=_params(("arbitrary",)),
    )(*args)
    return (outs[0], outs[1], outs[2]) if with_prev else (outs[0], None, outs[1])


def _fgate_fwd(h, wf, bf, *, name, deps=()):
    S, D = h.shape
    ts = _tile(S, 256, 16)

    def body(h_ref, w_ref, b_ref, z_ref, f_ref, carry):
        @pl.when(pl.program_id(0) == 0)
        def _():
            carry[...] = jnp.zeros_like(carry)
        z = lax.dot_general(h_ref[...], w_ref[...], (((1,), (1,)), ((), ())), preferred_element_type=F32) + b_ref[...]
        logf = jnp.minimum(z, 0.0) - jnp.log(1.0 + jnp.exp(-jnp.abs(z)))
        row = lax.broadcasted_iota(jnp.int32, (ts, ts), 0)
        col = lax.broadcasted_iota(jnp.int32, (ts, ts), 1)
        tril = (col <= row).astype(F32)
        run = jnp.dot(tril, logf, preferred_element_type=F32, precision=lax.Precision.HIGHEST) + carry[0:1, :]
        z_ref[...] = z
        f_ref[...] = run
        carry[0:1, :] = run[ts - 1:ts, :]

    return _call(
        body, deps, name=name, grid=(S // ts,),
        in_specs=[pl.BlockSpec((ts, D), lambda i: (i, 0)), pl.BlockSpec((LANES, D), lambda i: (0, 0)),
                  pl.BlockSpec((1, LANES), lambda i: (0, 0))],
        out_specs=[pl.BlockSpec((ts, LANES), lambda i: (i, 0))] * 2,
        out_shape=[jax.ShapeDtypeStruct((S, LANES), F32)] * 2,
        scratch_shapes=[pltpu.VMEM((8, LANES), F32)],
        compiler_params=_params(("arbitrary",)),
    )(h, wf, bf)


def _fgate_bwd(dfq, dfk, z, *, name, deps=()):
    S = z.shape[0]
    ts = _tile(S, 256, 16)
    n = S // ts

    def body(dq_ref, dk_ref, z_ref, dz_ref, sums_ref, carry):
        @pl.when(pl.program_id(0) == 0)
        def _():
            carry[...] = jnp.zeros_like(carry)
            sums_ref[...] = jnp.zeros_like(sums_ref)
        df = dq_ref[...] - dk_ref[...]
        row = lax.broadcasted_iota(jnp.int32, (ts, ts), 0)
        col = lax.broadcasted_iota(jnp.int32, (ts, ts), 1)
        triu = (col >= row).astype(F32)
        run = jnp.dot(triu, df, preferred_element_type=F32, precision=lax.Precision.HIGHEST) + carry[0:1, :]
        zv = z_ref[...]
        dz = run * (1.0 / (1.0 + jnp.exp(zv)))
        dz_ref[...] = dz.astype(BF16)
        sums_ref[0:1, :] += jnp.sum(dz, axis=0, keepdims=True)
        carry[0:1, :] = run[0:1, :]

    rev = pl.BlockSpec((ts, LANES), lambda i: (n - 1 - i, 0))
    return _call(
        body, deps, name=name, grid=(n,), in_specs=[rev, rev, rev],
        out_specs=[rev, pl.BlockSpec((8, LANES), lambda i: (0, 0))],
        out_shape=[jax.ShapeDtypeStruct((S, LANES), BF16), jax.ShapeDtypeStruct((8, LANES), F32)],
        scratch_shapes=[pltpu.VMEM((8, LANES), F32)],
        compiler_params=_params(("arbitrary",)),
    )(dfq, dfk, z)


def _head_col(ref, rows, lane_mask):
    return jnp.sum(jnp.where(lane_mask, ref[rows, :], 0.0), axis=1, keepdims=True)


def _attn_fwd(qkv, fq, fk, *, heads, name, T=256, deps=()):
    S, D3 = qkv.shape
    D = D3 // 3
    dh = D // heads
    T = _tile(S, T, 16)
    nq = S // T
    scale = dh ** -0.5
    hp = fk.shape[1]

    def body(q_ref, k_ref, v_ref, fq_ref, fk_ref, o_ref, lse_ref):
        h = pl.program_id(0)

        @pl.when(h == 0)
        def _():
            lse_ref[...] = jnp.zeros_like(lse_ref)
        lane = lax.broadcasted_iota(jnp.int32, (1, LANES), 1) == h
        row = lax.broadcasted_iota(jnp.int32, (T, T), 0)
        col = lax.broadcasted_iota(jnp.int32, (T, T), 1)

        def q_block(qi, _):
            rows = pl.ds(pl.multiple_of(qi * T, T), T)
            q = q_ref[rows, :]
            fq_col = _head_col(fq_ref, rows, lane)

            def kv_block(kj, carry, diag):
                m, l, acc = carry
                cols = pl.ds(pl.multiple_of(kj * T, T), T)
                s = lax.dot_general(q, k_ref[cols, :], (((1,), (1,)), ((), ())), preferred_element_type=F32) * scale
                s = s + (fq_col - fk_ref[kj, pl.ds(h, 1), :])
                if diag:
                    s = jnp.where(col <= row, s, NEG)
                m_new = jnp.maximum(m, jnp.max(s, axis=1, keepdims=True))
                p = jnp.exp(s - m_new)
                alpha = jnp.exp(m - m_new)
                l = alpha * l + jnp.sum(p, axis=1, keepdims=True)
                acc = alpha * acc + jnp.dot(p.astype(BF16), v_ref[cols, :], preferred_element_type=F32)
                return m_new, l, acc

            init = (jnp.full((T, 1), NEG, F32), jnp.zeros((T, 1), F32), jnp.zeros((T, dh), F32))
            carry = lax.fori_loop(0, qi, lambda kj, cr: kv_block(kj, cr, False), init)
            m, l, acc = kv_block(qi, carry, True)
            o_ref[rows, :] = (acc / l).astype(BF16)
            lse_ref[rows, :] = jnp.where(lane, m + jnp.log(l), lse_ref[rows, :])
            return 0

        lax.fori_loop(0, nq, q_block, 0)

    head = lambda part: pl.BlockSpec((S, dh), lambda h: (0, part * heads + h))
    return _call(
        body, deps, name=name, grid=(heads,),
        in_specs=[head(0), head(1), head(2), pl.BlockSpec((S, LANES), lambda h: (0, 0)),
                  pl.BlockSpec((nq, hp, T), lambda h: (0, 0, 0))],
        out_specs=[pl.BlockSpec((S, dh), lambda h: (0, h)), pl.BlockSpec((S, LANES), lambda h: (0, 0))],
        out_shape=[jax.ShapeDtypeStruct((S, D), BF16), jax.ShapeDtypeStruct((S, LANES), F32)],
        compiler_params=_params(("arbitrary",)),
    )(qkv, qkv, qkv, fq, fk)


def _attn_bwd(qkv, o, do, fq, fk, lse, *, heads, name, T=256, deps=()):
    S, D3 = qkv.shape
    D = D3 // 3
    dh = D // heads
    T = _tile(S, T, 16)
    nq = S // T
    scale = dh ** -0.5
    hp = fk.shape[1]

    def body(q_ref, k_ref, v_ref, o_ref, do_ref, fq_ref, fk_ref, lse_ref, dqkv_ref, dfq_ref, dfk_ref,
             dq_acc, fq_col, lse_col, delta_col, dfq_col):
        h = pl.program_id(0)

        @pl.when(h == 0)
        def _():
            dfq_ref[...] = jnp.zeros_like(dfq_ref)
            dfk_ref[...] = jnp.zeros_like(dfk_ref)
        lane = lax.broadcasted_iota(jnp.int32, (1, LANES), 1) == h
        row = lax.broadcasted_iota(jnp.int32, (T, T), 0)
        col = lax.broadcasted_iota(jnp.int32, (T, T), 1)
        dq_acc[...] = jnp.zeros_like(dq_acc)
        dfq_col[...] = jnp.zeros_like(dfq_col)

        def prep(qi, _):
            rows = pl.ds(pl.multiple_of(qi * T, T), T)
            fq_col[rows, :] = _head_col(fq_ref, rows, lane)
            lse_col[rows, :] = _head_col(lse_ref, rows, lane)
            delta_col[rows, :] = jnp.sum(do_ref[rows, :].astype(F32) * o_ref[rows, :].astype(F32), axis=1, keepdims=True)
            return 0

        lax.fori_loop(0, nq, prep, 0)

        def kv_block(kj, _):
            cols = pl.ds(pl.multiple_of(kj * T, T), T)
            k, v = k_ref[cols, :], v_ref[cols, :]
            fk_row = fk_ref[kj, pl.ds(h, 1), :]

            def q_block(qi, carry, diag):
                dk, dv, dfk = carry
                rows = pl.ds(pl.multiple_of(qi * T, T), T)
                q, dov = q_ref[rows, :], do_ref[rows, :]
                s = lax.dot_general(q, k, (((1,), (1,)), ((), ())), preferred_element_type=F32) * scale
                s = s + (fq_col[rows, :] - fk_row)
                p = jnp.exp(s - lse_col[rows, :])
                if diag:
                    p = jnp.where(col <= row, p, 0.0)
                dp = lax.dot_general(dov, v, (((1,), (1,)), ((), ())), preferred_element_type=F32)
                ds = p * (dp - delta_col[rows, :])
                dsb = ds.astype(BF16)
                dv = dv + lax.dot_general(p.astype(BF16), dov, (((0,), (0,)), ((), ())), preferred_element_type=F32)
                dk = dk + lax.dot_general(dsb, q, (((0,), (0,)), ((), ())), preferred_element_type=F32)
                dq_acc[rows, :] += jnp.dot(dsb, k, preferred_element_type=F32)
                dfq_col[rows, :] += jnp.sum(ds, axis=1, keepdims=True)
                dfk = dfk + jnp.sum(ds, axis=0, keepdims=True)
                return dk, dv, dfk

            init = (jnp.zeros((T, dh), F32), jnp.zeros((T, dh), F32), jnp.zeros((1, T), F32))
            carry = q_block(kj, init, True)
            dk, dv, dfk = lax.fori_loop(kj + 1, nq, lambda qi, cr: q_block(qi, cr, False), carry)
            dqkv_ref[1, cols, :] = (dk * scale).astype(BF16)
            dqkv_ref[2, cols, :] = dv.astype(BF16)
            dfk_ref[kj, pl.ds(h, 1), :] = dfk
            return 0

        lax.fori_loop(0, nq, kv_block, 0)

        def finish(qi, _):
            rows = pl.ds(pl.multiple_of(qi * T, T), T)
            dqkv_ref[0, rows, :] = (dq_acc[rows, :] * scale).astype(BF16)
            dfq_ref[rows, :] = jnp.where(lane, dfq_col[rows, :], dfq_ref[rows, :])
            return 0

        lax.fori_loop(0, nq, finish, 0)

    head = lambda part: pl.BlockSpec((S, dh), lambda h: (0, part * heads + h))
    own = pl.BlockSpec((S, dh), lambda h: (0, h))
    full = pl.BlockSpec((S, LANES), lambda h: (0, 0))
    krow = pl.BlockSpec((nq, hp, T), lambda h: (0, 0, 0))
    return _call(
        body, deps, name=name, grid=(heads,),
        in_specs=[head(0), head(1), head(2), own, own, full, krow, full],
        out_specs=[pl.BlockSpec((3, S, dh), lambda h: (0, 0, h)), full, krow],
        out_shape=[jax.ShapeDtypeStruct((3, S, D), BF16), jax.ShapeDtypeStruct((S, LANES), F32),
                   jax.ShapeDtypeStruct((nq, hp, T), F32)],
        scratch_shapes=[pltpu.VMEM((S, dh), F32)] + [pltpu.VMEM((S, 1), F32)] * 4,
        compiler_params=_params(("arbitrary",)),
    )(qkv, qkv, qkv, o, do, fq, fk, lse)


def _shift_down(v, n):
    rows = lax.broadcasted_iota(jnp.int32, v.shape, 0)
    return jnp.where(rows >= n, pltpu.roll(v, n, axis=0), 0.0)


def _shift_up(v, n):
    S = v.shape[0]
    rows = lax.broadcasted_iota(jnp.int32, v.shape, 0)
    return jnp.where(rows < S - n, pltpu.roll(v, S - n, axis=0), 0.0)


def _conv_fwd(proj, conv_w, *, name, cb=LANES, deps=()):
    S, D3 = proj.shape
    D = D3 // 3
    nb = D // cb

    def body(bg_ref, cg_ref, u_ref, w_ref, z_ref):
        uc = cg_ref[...].astype(F32) * u_ref[...].astype(F32)
        w = w_ref[...]
        y = w[2:3, :] * uc + w[1:2, :] * _shift_down(uc, 1) + w[0:1, :] * _shift_down(uc, 2)
        z_ref[...] = (bg_ref[...].astype(F32) * y).astype(BF16)

    part = lambda g: pl.BlockSpec((S, cb), lambda j: (0, g * nb + j))
    return _call(
        body, deps, name=name, grid=(nb,),
        in_specs=[part(0), part(1), part(2), pl.BlockSpec((3, cb), lambda j: (0, j))],
        out_specs=pl.BlockSpec((S, cb), lambda j: (0, j)),
        out_shape=jax.ShapeDtypeStruct((S, D), BF16), compiler_params=_params(("parallel",)),
    )(proj, proj, proj, conv_w)


def _conv_bwd(proj, conv_w, dz, *, name, cb=LANES, deps=()):
    S, D3 = proj.shape
    D = D3 // 3
    nb = D // cb

    def body(bg_ref, cg_ref, u_ref, w_ref, dz_ref, dp_ref, dw_ref):
        cg, u = cg_ref[...].astype(F32), u_ref[...].astype(F32)
        uc = cg * u
        w = w_ref[...]
        uc1, uc2 = _shift_down(uc, 1), _shift_down(uc, 2)
        y = w[2:3, :] * uc + w[1:2, :] * uc1 + w[0:1, :] * uc2
        dz = dz_ref[...].astype(F32)
        dp_ref[0] = (dz * y).astype(BF16)
        dy = dz * bg_ref[...].astype(F32)
        duc = w[2:3, :] * dy + w[1:2, :] * _shift_up(dy, 1) + w[0:1, :] * _shift_up(dy, 2)
        dp_ref[1] = (duc * u).astype(BF16)
        dp_ref[2] = (duc * cg).astype(BF16)
        dw_ref[...] = jnp.zeros_like(dw_ref)
        dw_ref[0:1, :] = jnp.sum(dy * uc2, axis=0, keepdims=True)
        dw_ref[1:2, :] = jnp.sum(dy * uc1, axis=0, keepdims=True)
        dw_ref[2:3, :] = jnp.sum(dy * uc, axis=0, keepdims=True)

    part = lambda g: pl.BlockSpec((S, cb), lambda j: (0, g * nb + j))
    return _call(
        body, deps, name=name, grid=(nb,),
        in_specs=[part(0), part(1), part(2), pl.BlockSpec((3, cb), lambda j: (0, j)), pl.BlockSpec((S, cb), lambda j: (0, j))],
        out_specs=[pl.BlockSpec((3, S, cb), lambda j: (0, 0, j)), pl.BlockSpec((8, cb), lambda j: (0, j))],
        out_shape=[jax.ShapeDtypeStruct((3, S, D), BF16), jax.ShapeDtypeStruct((8, D), F32)],
        compiler_params=_params(("parallel",)),
    )(proj, proj, proj, conv_w, dz)


def _ada_fwd(c_all, ada_w, *, name, deps=()):
    L, D, Ns = ada_w.shape
    tn = _tile(Ns, 512)

    def body(c_ref, w_ref, o_ref, act_ref):
        cv = c_ref[...]
        act = cv * (1.0 / (1.0 + jnp.exp(-cv)))
        act_ref[...] = act
        o_ref[...] = jnp.dot(act.astype(BF16), w_ref[...].astype(BF16), preferred_element_type=F32)

    return _call(
        body, deps, name=name, grid=(L, Ns // tn),
        in_specs=[pl.BlockSpec((N_DEV, D), lambda l, j: (0, 0)), pl.BlockSpec((None, D, tn), lambda l, j: (l, 0, j))],
        out_specs=[pl.BlockSpec((None, N_DEV, tn), lambda l, j: (l, 0, j)), pl.BlockSpec((N_DEV, D), lambda l, j: (0, 0))],
        out_shape=[jax.ShapeDtypeStruct((L, N_DEV, Ns), F32), jax.ShapeDtypeStruct((N_DEV, D), F32)],
        compiler_params=_params(("arbitrary", "arbitrary")),
    )(c_all, ada_w)


def _select_mod(gathered, *, name, deps=()):
    _, LB, Ns = gathered.shape
    L = LB // N_DEV

    def body(g_ref, o_ref):
        x, y, c = _me()
        b = 4 * x + 2 * y + c
        for j in range(N_CHIPS):
            for l in range(L):
                o_ref[j, pl.ds(l, 1), :] = g_ref[2 * j + c, pl.ds(l * N_DEV + b, 1), :]

    return _call(
        body, deps, name=name, out_shape=jax.ShapeDtypeStruct((N_CHIPS, L, Ns), F32),
        in_specs=[pl.BlockSpec(memory_space=pltpu.VMEM)], out_specs=pl.BlockSpec(memory_space=pltpu.VMEM),
        compiler_params=_params(),
    )(gathered)


def _adamw_math(w, g, m, v):
    m = ADAM_B1 * m + (1.0 - ADAM_B1) * g
    v = ADAM_B2 * v + (1.0 - ADAM_B2) * jnp.square(g)
    m_hat = m / (1.0 - ADAM_B1 ** ADAM_STEP)
    v_hat = v / (1.0 - ADAM_B2 ** ADAM_STEP)
    delta = -ADAM_LR * (m_hat / (jnp.sqrt(v_hat) + ADAM_EPS) + ADAM_WD * w)
    return delta, m, v


def _adamw_shards(w, m, v, groups, chip, *, name):
    L, R, C = w.shape
    if R % 16 == 0:
        tr, tc = _tile(R, 128, 16), C
    else:
        tr, tc = R, _tile(C, 256)
    nr, nc = R // tr, C // tc

    def body(chip_ref, w_ref, m_ref, v_ref, *rest):
        srcs, (g_ref, d_ref, mo_ref, vo_ref) = rest[:2 * N_CHIPS * L], rest[2 * N_CHIPS * L:]
        for l in range(L):
            @pl.when(pl.program_id(0) == l)
            def _():
                s = srcs[2 * N_CHIPS * l:2 * N_CHIPS * (l + 1)]
                mine, other = s[0][...].astype(F32), s[N_CHIPS][...].astype(F32)
                for k in range(1, N_CHIPS):
                    mine = mine + s[k][...].astype(F32)
                    other = other + s[N_CHIPS + k][...].astype(F32)
                g = mine + other
                delta, mn, vn = _adamw_math(w_ref[...], g, m_ref[...], v_ref[...])
                g_ref[...] = g
                d_ref[...] = delta
                mo_ref[...] = mn
                vo_ref[...] = vn

    tile = pl.BlockSpec((None, tr, tc), lambda l, i, j, chip_ref: (l, i, j))

    def block(layer, k):
        def index(l, i, j, chip_ref):
            idle_i, idle_j = jnp.where(l < layer, 0, nr - 1), jnp.where(l < layer, 0, nc - 1)
            return (jnp.bitwise_xor(chip_ref[0], k), jnp.where(l == layer, i, idle_i), jnp.where(l == layer, j, idle_j))
        return pl.BlockSpec((None, tr, tc), index)

    in_specs, args = [tile] * 3, [w, m, v]
    for layer, (parts, land, sib) in enumerate(groups):
        in_specs += [block(layer, k) for k in range(N_CHIPS)] * 2
        args += [parts, land, land, land, sib, sib, sib, sib]
    return pl.pallas_call(
        body, name=name,
        grid_spec=pltpu.PrefetchScalarGridSpec(num_scalar_prefetch=1, grid=(L, nr, nc), in_specs=in_specs, out_specs=[tile] * 4),
        out_shape=[jax.ShapeDtypeStruct((L, R, C), F32)] * 4, compiler_params=_params(("arbitrary", "arbitrary", "arbitrary")),
    )(chip, *args)


def _adamw_ada(w, m, v, act_t, dmod, *, name, tr=256, deps=()):
    L, D, Ns = w.shape
    tr = _tile(D, tr, 8)

    def body(w_ref, m_ref, v_ref, a_ref, d_ref, g_ref, dl_ref, mo_ref, vo_ref):
        x, y, _ = _me()
        g = jnp.dot(a_ref[...], d_ref[2 * x + y], preferred_element_type=F32, precision=lax.Precision.HIGHEST)
        delta, mn, vn = _adamw_math(w_ref[...], g, m_ref[...], v_ref[...])
        g_ref[...] = g
        dl_ref[...] = delta
        mo_ref[...] = mn
        vo_ref[...] = vn

    tile = pl.BlockSpec((None, tr, Ns), lambda l, i: (l, i, 0))
    return _call(
        body, deps, name=name, grid=(L, D // tr),
        in_specs=[tile] * 3 + [pl.BlockSpec((tr, N_DEV), lambda l, i: (i, 0)),
                               pl.BlockSpec((N_CHIPS, None, N_DEV, Ns), lambda l, i: (0, l, 0, 0))],
        out_specs=[tile] * 4, out_shape=[jax.ShapeDtypeStruct((L, D, Ns), F32)] * 4,
        compiler_params=_params(("parallel", "parallel")),
    )(w, m, v, act_t, dmod)


def _adamw_small(w, m, v, gathered, *, rows, name, deps=()):
    n, D = w.shape

    def body(w_ref, m_ref, v_ref, s_ref, g_ref, d_ref, mo_ref, vo_ref):
        for r, src in enumerate(rows):
            g = s_ref[0, src:src + 1, :]
            for d in range(1, N_DEV):
                g = g + s_ref[d, src:src + 1, :]
            g_ref[r:r + 1, :] = g
        g = g_ref[...]
        delta, mn, vn = _adamw_math(w_ref[...], g, m_ref[...], v_ref[...])
        d_ref[...] = delta
        mo_ref[...] = mn
        vo_ref[...] = vn

    vm = pl.BlockSpec(memory_space=pltpu.VMEM)
    return _call(
        body, deps, name=name, in_specs=[vm] * 4, out_specs=[vm] * 4,
        out_shape=[jax.ShapeDtypeStruct((n, D), F32)] * 4, compiler_params=_params(),
    )(w, m, v, gathered)


def _adamw_conv_w(w, m, v, gathered4, *, name, deps=()):
    Cs = w.shape[1]

    def body(w_ref, m_ref, v_ref, s_ref, g_ref, d_ref, mo_ref, vo_ref):
        x, y, _ = _me()
        j = 2 * x + y
        g = s_ref[j, 0]
        for d in range(1, N_DEV):
            g = g + s_ref[j, d]
        delta, mn, vn = _adamw_math(w_ref[...], g, m_ref[...], v_ref[...])
        g_ref[...] = g
        d_ref[...] = delta
        mo_ref[...] = mn
        vo_ref[...] = vn

    vm = pl.BlockSpec(memory_space=pltpu.VMEM)
    return _call(
        body, deps, name=name, in_specs=[vm] * 4, out_specs=[vm] * 4,
        out_shape=[jax.ShapeDtypeStruct((8, Cs), F32)] * 4, compiler_params=_params(),
    )(w, m, v, gathered4)


def _loss_sum(gathered, *, row, name, deps=()):
    _, _, D = gathered.shape

    def body(s_ref, o_ref):
        t = s_ref[0, row:row + 1, :]
        for d in range(1, N_DEV):
            t = t + s_ref[d, row:row + 1, :]
        o_ref[...] = jnp.broadcast_to(t, (8, D))

    vm = pl.BlockSpec(memory_space=pltpu.VMEM)
    return pl.pallas_call(body, name=name, in_specs=[vm], out_specs=vm, out_shape=jax.ShapeDtypeStruct((8, D), F32),
                          compiler_params=_params())(gathered)


def _pad_rows(a, n):
    return jnp.pad(a, ((0, n - a.shape[0]), (0, 0)))


def kernel(x, c, ada_w, ada_b, norm_mix, norm_mlp, fox_w_in, fox_b_f, fox_w_out, conv_w_in, conv_w, conv_w_out, mlp_w_up, mlp_w_down, final_norm, loss_target, m_ada_w, m_ada_b, m_norm_mix, m_norm_mlp, m_fox_w_in, m_fox_b_f, m_fox_w_out, m_conv_w_in, m_conv_w, m_conv_w_out, m_mlp_w_up, m_mlp_w_down, m_final_norm, v_ada_w, v_ada_b, v_norm_mix, v_norm_mlp, v_fox_w_in, v_fox_b_f, v_fox_w_out, v_conv_w_in, v_conv_w, v_conv_w_out, v_mlp_w_up, v_mlp_w_down, v_final_norm):
    S, D = x.shape[1], x.shape[2]
    H = fox_b_f.shape[-1]
    L = ada_w.shape[0]
    NM = ada_b.shape[1] // D
    Ns_ada = ada_w.shape[2]
    Cs_fox = fox_w_in.shape[2]
    Cs_conv = conv_w.shape[2]
    x0 = x[0]
    target = loss_target[0]

    chip = (2 * lax.axis_index("x") + lax.axis_index("y")).astype(jnp.int32).reshape(1)

    fin_t = jnp.transpose(fox_w_in, (0, 2, 1))
    shards = dict(fin=(fin_t, 0), fout=(fox_w_out, 0), up0=(mlp_w_up, 0), dn0=(mlp_w_down, 0), cin=(conv_w_in, 0),
                  cout=(conv_w_out, 0), up1=(mlp_w_up, 1), dn1=(mlp_w_down, 1))
    halves = dict(fin="cols", fout="rows", up0="cols", dn0="rows", cin="cols", cout="rows", up1="cols", dn1="rows")
    gathers, placed = {}, {}

    def place(key, dep=None):
        placed[key] = _place_cast(*shards[key], chip, halves=halves[key], name="place_" + key, dep=dep)
        return placed[key]

    def start_gather(key, dep=None):
        gathers[key] = _split_start("gather1", [(placed[key],)], name="gather_start_" + key, dep=dep)
        return gathers[key][3]

    def pass_gather(key, after):
        landed = _split_wait("gather1", gathers[key], after, name="gather_landed_" + key)
        gathers[key] = _split_start("gather2", landed, name="gather_pass_" + key)
        return gathers[key][3]

    def gathered(key, after):
        return _split_wait("gather2", gathers[key], after, name="gather_wait_" + key)[0][0]

    place("fin")
    tok = start_gather("fin")
    for key in ("fout", "up0", "dn0", "cin", "cout", "up1", "dn1"):
        tok = place(key, tok)

    c_all = _allgather8(_pad_rows(c, 8), name="gather_c", deps=(tok,))[:, 0, :]
    mod_part, c_act = _ada_fwd(c_all, ada_w, name="ada_fwd")
    mod_all = _allgather8(mod_part.reshape(L * N_DEV, Ns_ada), name="gather_mod")
    mod = _select_mod(mod_all, name="select_mod")
    mod = jnp.transpose(mod, (1, 0, 2)).reshape(L, NM, 1, D) + ada_b.reshape(L, NM, 1, D)
    conv_w_all = _allgather8(_pad_rows(conv_w[0], 8), name="gather_conv_w")
    conv_w_full = jnp.transpose(conv_w_all[0::2, :3, :], (1, 0, 2)).reshape(3, D)

    def vec(a):
        return a.reshape(1, D)

    h0 = _norm_fwd(x0, vec(norm_mix[0]), mod[0, 1], mod[0, 0], name="norm_mix0", deps=(conv_w_all,))
    tok = h0
    for key in ("fout", "up0", "dn0", "cin", "cout", "up1", "dn1"):
        tok = start_gather(key, tok)
    tok = pass_gather("fin", [h0, tok])
    w_fin_t = jnp.transpose(gathered("fin", [tok]), (0, 2, 1, 3)).reshape(N_CHIPS * Cs_fox, D)
    w_f_t = _pad_rows(w_fin_t[3 * D:], LANES)
    tok = pass_gather("fout", [w_fin_t])
    qkv = _mm_nt(h0, w_fin_t, n=3 * D, name="fox_in", out_dtype=BF16, deps=(tok,))
    tok = pass_gather("up0", [qkv])
    b_f = jnp.pad(fox_b_f, ((0, 0), (0, LANES - H)))
    z_f, F_col = _fgate_fwd(h0, w_f_t, b_f, name="fgate_fwd", deps=(tok,))
    hp = max(8, H)
    at_f, at = _tile(S, 1024, 16), _tile(S, 512, 16)
    F_rows = _pad_rows(jnp.transpose(F_col[:, :H]), hp)
    F_row = jnp.transpose(F_rows.reshape(hp, S // at, at), (1, 0, 2))
    o, lse = _attn_fwd(qkv, F_col, jnp.transpose(F_rows.reshape(hp, S // at_f, at_f), (1, 0, 2)), heads=H, name="attn_fwd", T=at_f)
    w_fout = gathered("fout", [o]).reshape(D, D)
    tok = pass_gather("dn0", [o])
    x1, mix0, h1 = _mm_nn(o, w_fout, name="fox_out", epilogue="resid_norm", res=x0, gate=mod[0, 2],
                          norm=(vec(norm_mlp[0]), mod[0, 4], mod[0, 3]), tm=512, tn=D, deps=(tok,))
    w_up0 = gathered("up0", [h1]).reshape(2 * N_CHIPS, D, -1)
    tok = pass_gather("cin", [h1])
    u0, a0 = _mm_nn(h1, w_up0, name="mlp_up0", epilogue="relu2", deps=(tok,))
    w_dn0 = gathered("dn0", [a0]).reshape(-1, D)
    tok = pass_gather("cout", [a0])
    x2, y0 = _mm_nn(a0, w_dn0, name="mlp_down0", epilogue="resid", res=x1, gate=mod[0, 5], tm=512, tk=4096, deps=(tok,))
    h2 = _norm_fwd(x2, vec(norm_mix[1]), mod[1, 1], mod[1, 0], name="norm_mix1")
    g_cin = gathered("cin", [h2]).reshape(2 * N_CHIPS, D, -1)
    tok = pass_gather("up1", [h2])
    proj = _mm_nn(h2, g_cin, name="conv_in", deps=(tok,))
    w_cin = jnp.transpose(g_cin, (1, 0, 2)).reshape(D, 3 * D)
    zc = _conv_fwd(proj, conv_w_full, name="conv_fwd")
    w_cout = gathered("cout", [zc]).reshape(D, D)
    tok = pass_gather("dn1", [zc])
    x3, mix1, h3 = _mm_nn(zc, w_cout, name="conv_out", epilogue="resid_norm", res=x2, gate=mod[1, 2],
                          norm=(vec(norm_mlp[1]), mod[1, 4], mod[1, 3]), tm=512, tn=D, deps=(tok,))
    w_up1 = gathered("up1", [h3]).reshape(2 * N_CHIPS, D, -1)
    u1, a1 = _mm_nn(h3, w_up1, name="mlp_up1", epilogue="relu2")
    w_dn1 = gathered("dn1", [a1]).reshape(-1, D)
    x4, y1 = _mm_nn(a1, w_dn1, name="mlp_down1", epilogue="resid", res=x3, gate=mod[1, 5], tm=512, tk=4096)
    w_up = [jnp.transpose(w_up0, (1, 0, 2)).reshape(D, -1), jnp.transpose(w_up1, (1, 0, 2)).reshape(D, -1)]
    w_dn = [w_dn0, w_dn1]

    dx4, dy1, sums_f = _loss_bwd(x4, target, vec(final_norm), mod[1, 5], name="loss_bwd")
    du1 = _mm_nt(dy1, w_dn[1], name="mlp_down1_dx", epilogue="drelu2", extra=u1, out_dtype=BF16)
    def start_scatter(tag, parts_list):
        groups = [(p, lax.empty(p.shape, p.dtype)) for p in parts_list]
        return _split_start("scatter", groups, name="scatter_start_" + tag)

    def start_sibling(tag, scatter, after):
        landed = _split_wait("scatter", scatter, after, name="scatter_wait_" + tag)
        groups = [(p, ld, lax.empty(p.shape, p.dtype)) for p, ld in landed]
        return _split_start("sibling", groups, name="sibling_start_" + tag)

    gw_dn1 = _mm_tn(a1, dy1, name="mlp_down1_dw")
    gw_up1 = _mm_tn(h3, du1, name="mlp_up1_dw", out_parts=N_CHIPS)
    sc1 = start_scatter("mlp1", [gw_dn1.reshape(N_CHIPS, -1, D), gw_up1])
    dh3 = _mm_nt(du1, w_up[1], name="mlp_up1_dx", out_dtype=BF16, tk=4096, deps=(sc1[3],))
    dx3, dmix1, sums_mlp1 = _norm_bwd(x3, dh3, dx4, y1, vec(norm_mlp[1]), mod[1, 4], mod[1, 2], name="norm_mlp1_bwd")
    dzc = _mm_nt(dmix1, w_cout, name="conv_out_dx", out_dtype=BF16)
    gw_cout = _mm_tn(zc, dmix1, name="conv_out_dw")
    dproj, dconv_w = _conv_bwd(proj, conv_w_full, dzc, name="conv_bwd")
    gw_cin = _mm_tn(h2, dproj, name="conv_in_dw", out_parts=N_CHIPS, tn=512)
    sc2 = start_scatter("conv", [gw_cout.reshape(N_CHIPS, -1, D), gw_cin])
    dh2 = _mm_nt(dproj, w_cin, name="conv_in_dx", out_dtype=BF16, deps=(sc2[3],))
    dx2, dy0, sums_mix1 = _norm_bwd(x2, dh2, dx3, mix1, vec(norm_mix[1]), mod[1, 1], mod[0, 5], name="norm_mix1_bwd")
    du0 = _mm_nt(dy0, w_dn[0], name="mlp_down0_dx", epilogue="drelu2", extra=u0, out_dtype=BF16)
    gw_dn0 = _mm_tn(a0, dy0, name="mlp_down0_dw")
    gw_up0 = _mm_tn(h1, du0, name="mlp_up0_dw", out_parts=N_CHIPS)
    sc3 = start_scatter("mlp0", [gw_dn0.reshape(N_CHIPS, -1, D), gw_up0])
    sb1 = start_sibling("mlp1", sc1, [sc3[3]])
    dh1 = _mm_nt(du0, w_up[0], name="mlp_up0_dx", out_dtype=BF16, tk=4096, deps=(sb1[3],))
    dx1, dmix0, sums_mlp0 = _norm_bwd(x1, dh1, dx2, y0, vec(norm_mlp[0]), mod[0, 4], mod[0, 2], name="norm_mlp0_bwd")
    do = _mm_nt(dmix0, w_fout, name="fox_out_dx", out_dtype=BF16)
    gw_fout = _mm_tn(o, dmix0, name="fox_out_dw")
    dqkv, dfq, dfk = _attn_bwd(qkv, o, do, F_col, F_row, lse, heads=H, name="attn_bwd", T=at)
    dfk_col = jnp.pad(jnp.transpose(jnp.transpose(dfk, (1, 0, 2)).reshape(hp, S)[:H]), ((0, 0), (0, LANES - H)))
    sb3 = start_sibling("mlp0", sc3, [dqkv])
    dz_f, sums_bf = _fgate_bwd(dfq, dfk_col, z_f, name="fgate_bwd", deps=(sb3[3],))
    gw_qkv_t = _mm_tn(dqkv, h0, name="fox_in_dw")
    gw_f_t = _mm_tn(dz_f, h0, name="fox_gate_dw")
    gw_fin_t = jnp.concatenate([gw_qkv_t, gw_f_t[:H]], axis=0).reshape(N_CHIPS, Cs_fox, D)
    sc4 = start_scatter("fox", [gw_fout.reshape(N_CHIPS, -1, D), gw_fin_t])
    sb2 = start_sibling("conv", sc2, [sc4[3]])
    dh0_f = _mm_nn(dz_f, w_f_t, name="fox_gate_dx", out_dtype=F32, deps=(sb2[3],))
    dh0 = _mm_nn(dqkv, w_fin_t, name="fox_in_dx", epilogue="add", res=dh0_f, out_dtype=BF16)
    grad_x, _, sums_mix0 = _norm_bwd(x0, dh0, dx1, mix0, vec(norm_mix[0]), mod[0, 1], None, name="norm_mix0_bwd")

    outs = {}

    def put(name_, res, shape):
        for kind, r in zip(("grad", "delta", "new_m", "new_v"), res):
            outs[kind + "_" + name_] = r.reshape(shape)

    def shards_update(tag, w_, m_, v_, groups):
        return _adamw_shards(w_, m_, v_, groups, chip, name="adamw_" + tag)

    g_conv = _split_wait("sibling", sb2, [grad_x], name="sibling_wait_conv")
    put("conv_w_out", shards_update("conv_out", conv_w_out, m_conv_w_out, v_conv_w_out, g_conv[0:1]), conv_w_out.shape)
    r_cin = shards_update("conv_in", conv_w_in, m_conv_w_in, v_conv_w_in, g_conv[1:2])
    put("conv_w_in", r_cin, conv_w_in.shape)
    g_mlp1 = _split_wait("sibling", sb1, [r_cin[0]], name="sibling_wait_mlp1")
    g_mlp0 = _split_wait("sibling", sb3, [r_cin[0]], name="sibling_wait_mlp0")
    put("mlp_w_down", shards_update("mlp_down", mlp_w_down, m_mlp_w_down, v_mlp_w_down, [g_mlp0[0], g_mlp1[0]]), mlp_w_down.shape)
    r_up = shards_update("mlp_up", mlp_w_up, m_mlp_w_up, v_mlp_w_up, [g_mlp0[1], g_mlp1[1]])
    put("mlp_w_up", r_up, mlp_w_up.shape)
    sb4 = start_sibling("fox", sc4, [r_up[0]])

    dmod_rows = []
    for sm, sl in ((sums_mix0, sums_mlp0), (sums_mix1, sums_mlp1)):
        dmod_rows += [sm[0:1], sm[1:2], sm[3:4], sl[0:1], sl[1:2], sl[3:4]]
    bf_row = jnp.pad(sums_bf[0:1], ((0, 0), (0, D - LANES)))
    small = jnp.concatenate([sums_mix0[2:3], sums_mix1[2:3], sums_mlp0[2:3], sums_mlp1[2:3], sums_f[0:1], sums_f[1:2], bf_row,
                             jnp.zeros((1, D), F32)] + dmod_rows + [dconv_w[0:3]], axis=0)
    small_all = _allgather8(_pad_rows(small, -(-small.shape[0] // 8) * 8), name="gather_small", deps=(sb4[3],))
    loss = _loss_sum(small_all, row=5, name="loss_sum")[0, 0]

    def rows_of(a_mix, a_mlp, a_fin, a_bf, a_ada):
        return jnp.concatenate([a_mix, a_mlp, a_fin.reshape(1, D), jnp.pad(a_bf, ((0, 0), (0, D - H))),
                                a_ada.reshape(L * NM, D)], axis=0)
    n_small = 2 * L + 2 + L * NM
    rw = -(-n_small // 8) * 8
    w_s = _pad_rows(rows_of(norm_mix, norm_mlp, final_norm, fox_b_f, ada_b), rw)
    m_s = _pad_rows(rows_of(m_norm_mix, m_norm_mlp, m_final_norm, m_fox_b_f, m_ada_b), rw)
    v_s = _pad_rows(rows_of(v_norm_mix, v_norm_mlp, v_final_norm, v_fox_b_f, v_ada_b), rw)
    src_rows = [0, 1, 2, 3, 4, 6] + [8 + r for r in range(L * NM)] + [7] * (rw - n_small)
    res_s = _adamw_small(w_s, m_s, v_s, small_all, rows=tuple(src_rows), name="adamw_small")
    for kind, r in zip(("grad", "delta", "new_m", "new_v"), res_s):
        outs[kind + "_norm_mix"] = r[0:L]
        outs[kind + "_norm_mlp"] = r[L:2 * L]
        outs[kind + "_final_norm"] = r[2 * L]
        outs[kind + "_fox_b_f"] = r[2 * L + 1:2 * L + 2, :H]
        outs[kind + "_ada_b"] = r[2 * L + 2:n_small].reshape(L, NM * D)

    dmod_all = small_all[:, 8:8 + L * NM, :].reshape(N_DEV, L, N_CHIPS, Ns_ada)
    dmod4 = jnp.transpose(dmod_all, (2, 1, 0, 3))
    act_t = jnp.transpose(c_act)
    res_a = _adamw_ada(ada_w, m_ada_w, v_ada_w, act_t, dmod4, name="adamw_ada")
    put("ada_w", res_a, ada_w.shape)

    r0 = 8 + L * NM
    dconv_all = jnp.pad(small_all[:, r0:r0 + 3, :], ((0, 0), (0, 5), (0, 0)))
    dconv4 = jnp.transpose(dconv_all.reshape(N_DEV, 8, N_CHIPS, Cs_conv), (2, 0, 1, 3))
    res_c = _adamw_conv_w(_pad_rows(conv_w[0], 8), _pad_rows(m_conv_w[0], 8), _pad_rows(v_conv_w[0], 8), dconv4,
                          name="adamw_conv_w")
    for kind, r in zip(("grad", "delta", "new_m", "new_v"), res_c):
        outs[kind + "_conv_w"] = r[:3].reshape(conv_w.shape)

    g_fox = _split_wait("sibling", sb4, [res_a[0], res_c[0], res_s[0]], name="sibling_wait_fox")
    put("fox_w_out", shards_update("fox_out", fox_w_out, m_fox_w_out, v_fox_w_out, g_fox[0:1]), fox_w_out.shape)
    t3 = lambda a: jnp.transpose(a, (0, 2, 1))
    for kind, r in zip(("grad", "delta", "new_m", "new_v"),
                       shards_update("fox_in", t3(fox_w_in), t3(m_fox_w_in), t3(v_fox_w_in), g_fox[1:2])):
        outs[kind + "_fox_w_in"] = t3(r)

    names = ["ada_w", "ada_b", "norm_mix", "norm_mlp", "fox_w_in", "fox_b_f", "fox_w_out", "conv_w_in", "conv_w", "conv_w_out",
             "mlp_w_up", "mlp_w_down", "final_norm"]
    return (loss, grad_x[None], *[outs["grad_" + n] for n in names], *[outs["delta_" + n] for n in names],
            *[outs["new_m_" + n] for n in names], *[outs["new_v_" + n] for n in names])
```

```python
import functools

import jax
import jax.numpy as jnp
from jax import lax
from jax.experimental import pallas as pl
from jax.experimental.pallas import tpu as pltpu

F32 = jnp.float32
BF16 = jnp.bfloat16
MESH = pl.DeviceIdType.MESH
ANY = pl.BlockSpec(memory_space=pl.ANY)
HBM = pl.BlockSpec(memory_space=pltpu.HBM)
SEM = pl.BlockSpec(memory_space=pltpu.SEMAPHORE)
EFFECT = pltpu.SideEffectType.DATAFLOW_SIDE_EFFECTING

RMS_EPS = 1e-6
ADAM_LR = 0.001
ADAM_B1 = 0.9
ADAM_B2 = 0.999
ADAM_EPS = 1e-08
ADAM_WD = 0.01
ADAM_STEP = 10
N_CHIPS = 4
N_DEV = 8
LANES = 128
VMEM_LIMIT = 56 * 1024 * 1024
NEG = -1e30


def _params(sems=None, vmem=VMEM_LIMIT):
    return pltpu.CompilerParams(dimension_semantics=sems, vmem_limit_bytes=vmem)


def _tile(n, pref, unit=LANES):
    if n <= pref:
        return n
    t = (pref // unit) * unit
    while n % t:
        t -= unit
    return t


def _me():
    return lax.axis_index("x"), lax.axis_index("y"), lax.axis_index("c")


def _call(body, deps, **kw):
    nd = len(deps)

    def wrapped(*refs):
        body(*refs[nd:])

    kw["in_specs"] = [ANY] * nd + list(kw["in_specs"])
    fn = pl.pallas_call(wrapped, **kw)
    return lambda *args: fn(*deps, *args)


def _allgather8(v, *, name, deps=()):
    R, C = v.shape

    def body(v_ref, out_ref, send_sems, recv_sems):
        x, y, c = _me()
        me = 4 * x + 2 * y + c
        out_ref[me] = v_ref[...]
        copies = []
        for k in range(1, N_DEV):
            px, py, pc = (x + (k >> 2)) % 2, (y + ((k >> 1) & 1)) % 2, (c + (k & 1)) % 2
            copies.append(pltpu.make_async_remote_copy(
                src_ref=v_ref, dst_ref=out_ref.at[me], send_sem=send_sems.at[k - 1], recv_sem=recv_sems.at[k - 1],
                device_id=(px, py, pc), device_id_type=MESH))
        for cp in copies:
            cp.start()
        for k in range(1, N_DEV):
            px, py, pc = (x + (k >> 2)) % 2, (y + ((k >> 1) & 1)) % 2, (c + (k & 1)) % 2
            peer = 4 * px + 2 * py + pc
            pltpu.make_async_remote_copy(
                src_ref=v_ref, dst_ref=out_ref.at[peer], send_sem=send_sems.at[k - 1], recv_sem=recv_sems.at[k - 1],
                device_id=(px, py, pc), device_id_type=MESH).wait_recv()
        for cp in copies:
            cp.wait_send()

    return _call(
        body, deps, name=name,
        out_shape=jax.ShapeDtypeStruct((N_DEV, R, C), v.dtype),
        in_specs=[pl.BlockSpec(memory_space=pltpu.VMEM)],
        out_specs=pl.BlockSpec(memory_space=pltpu.VMEM),
        scratch_shapes=[pltpu.SemaphoreType.DMA((N_DEV - 1,)), pltpu.SemaphoreType.DMA((N_DEV - 1,))],
        compiler_params=_params(),
    )(v)


def _chip_peers(x, y):
    return [((x + (k >> 1)) % 2, (y + (k & 1)) % 2) for k in range(1, N_CHIPS)]


def _slot(x, y, k):
    return 2 * ((x + (k >> 1)) % 2) + (y + (k & 1)) % 2


def _split_copies(kind, groups, send_sems, recv_sems):
    x, y, c = _me()
    j = 2 * x + y
    copies = []
    for a, g in enumerate(groups):
        if kind == "sibling":
            parts, land, sib = g
            for k in range(N_CHIPS):
                s = _slot(x, y, k)
                copies.append(pltpu.make_async_remote_copy(
                    src_ref=(parts if k == 0 else land).at[s], dst_ref=sib.at[s], send_sem=send_sems.at[N_CHIPS * a + k],
                    recv_sem=recv_sems.at[N_CHIPS * a + k], device_id=(x, y, 1 - c), device_id_type=MESH))
            continue
        land = g[-1]
        for k, (px, py) in enumerate(_chip_peers(x, y)):
            if kind == "gather1":
                src, dst, to = land.at[j, c], land.at[j, c], (px, py, c)
            elif kind == "gather2":
                src, dst, to = land.at[2 * px + py, c], land.at[2 * px + py, c], (x, y, 1 - c)
            else:
                src, dst, to = g[0].at[2 * px + py], land.at[j], (px, py, c)
            copies.append(pltpu.make_async_remote_copy(
                src_ref=src, dst_ref=dst, send_sem=send_sems.at[3 * a + k], recv_sem=recv_sems.at[3 * a + k],
                device_id=to, device_id_type=MESH))
    return copies


def _split_start(kind, groups, *, name, dep=None):
    flat = [a for g in groups for a in g]
    nf, per = len(flat), len(groups[0])
    ncp = len(groups) * (N_CHIPS if kind == "sibling" else 3)
    nd = 0 if dep is None else 1

    def body(*refs):
        ins = refs[nd:nd + nf]
        send_sems, recv_sems, token = refs[nd + nf], refs[nd + nf + 1], refs[-1]
        for cp in _split_copies(kind, [ins[i:i + per] for i in range(0, nf, per)], send_sems, recv_sems):
            cp.start()
        token[...] = jnp.zeros_like(token)

    outs = pl.pallas_call(
        body, name=name,
        out_shape=(pltpu.SemaphoreType.DMA((ncp,)), pltpu.SemaphoreType.DMA((ncp,)), *[pltpu.HBM(a.shape, a.dtype) for a in flat],
                   jax.ShapeDtypeStruct((8, LANES), F32)),
        in_specs=[ANY] * nd + [HBM] * nf,
        out_specs=(SEM, SEM, *[HBM] * nf, pl.BlockSpec(memory_space=pltpu.VMEM)),
        input_output_aliases={nd + i: 2 + i for i in range(nf)},
        compiler_params=pltpu.CompilerParams(has_side_effects=EFFECT),
    )(*([dep] if nd else []), *[pltpu.with_memory_space_constraint(a, pltpu.HBM) for a in flat])
    thru = list(outs[2:2 + nf])
    return outs[0], outs[1], [tuple(thru[i:i + per]) for i in range(0, nf, per)], outs[-1]


def _split_wait(kind, started, after, *, name):
    send_sems, recv_sems, groups, _ = started
    flat = [a for g in groups for a in g]
    nf, per = len(flat), len(groups[0])

    def body(*refs):
        ins = refs[:nf]
        for cp in _split_copies(kind, [ins[i:i + per] for i in range(0, nf, per)], refs[nf], refs[nf + 1]):
            cp.wait_send()
            cp.wait_recv()

    outs = pl.pallas_call(
        body, name=name,
        out_shape=tuple(pltpu.HBM(a.shape, a.dtype) for a in flat),
        in_specs=[HBM] * nf + [SEM, SEM] + [ANY] * len(after), out_specs=tuple([HBM] * nf),
        input_output_aliases={i: i for i in range(nf)},
        compiler_params=pltpu.CompilerParams(has_side_effects=EFFECT),
    )(*flat, send_sems, recv_sems, *after)
    outs = list(outs)
    return [tuple(outs[i:i + per]) for i in range(0, nf, per)]


def _place_cast(shards, layer, chip, *, halves, name, dep=None):
    _, R, C = shards.shape
    if halves == "rows":
        hr, hc = R // 2, C
    else:
        hr, hc = R, C // 2
    if hr % 16 == 0:
        tr, tc = _tile(hr, 512, 16), hc
    else:
        tr, tc = hr, _tile(hc, 256)
    nr, nc = hr // tr, hc // tc

    def body(chip_ref, x_ref, *rest):
        rest[-1][...] = x_ref[...].astype(BF16)

    if halves == "rows":
        o_map = lambda i, j, chip_ref: (chip_ref[0], i // nr, i % nr, j)
    else:
        o_map = lambda i, j, chip_ref: (chip_ref[0], j // nc, i, j % nc)
    return pl.pallas_call(
        body, name=name,
        grid_spec=pltpu.PrefetchScalarGridSpec(
            num_scalar_prefetch=1, grid=(R // tr, C // tc),
            in_specs=[pl.BlockSpec((None, tr, tc), lambda i, j, chip_ref: (layer, i, j))] + ([] if dep is None else [ANY]),
            out_specs=pl.BlockSpec((None, None, tr, tc), o_map)),
        out_shape=jax.ShapeDtypeStruct((N_CHIPS, 2, hr, hc), BF16), compiler_params=_params(("parallel", "parallel")),
    )(chip, shards, *([] if dep is None else [dep]))


def _accumulate(part, acc_ref, nk, finalize):
    if nk == 1:
        finalize(part)
        return
    k = pl.program_id(2)

    @pl.when(k == 0)
    def _():
        acc_ref[...] = part

    @pl.when(k > 0)
    def _():
        acc_ref[...] += part

    @pl.when(k == nk - 1)
    def _():
        finalize(acc_ref[...])


def _mm_nn(a, b, *, name, epilogue="plain", res=None, gate=None, norm=None, out_dtype=BF16, tm=1024, tn=1024, tk=2048, deps=()):
    if a.ndim == 3:
        Q, M, Kq = a.shape
        K = Q * Kq
    else:
        (M, K), Kq = a.shape, a.shape[1]
    tm, tk = _tile(M, tm, 16), _tile(Kq, tk)
    if a.ndim == 3:
        pa = Kq // tk
        a_spec = pl.BlockSpec((None, tm, tk), lambda i, j, k: (k // pa, i, k % pa))
    else:
        a_spec = pl.BlockSpec((tm, tk), lambda i, j, k: (i, k))
    if b.ndim == 3:
        P, _, Ns = b.shape
        N = P * Ns
        tn = _tile(Ns, tn)
        per = Ns // tn
        b_spec = pl.BlockSpec((None, tk, tn), lambda i, j, k: (j // per, k, j % per))
    else:
        N = b.shape[1]
        tn = _tile(N, tn)
        b_spec = pl.BlockSpec((tk, tn), lambda i, j, k: (k, j))
    nk = K // tk
    tile = pl.BlockSpec((tm, tn), lambda i, j, k: (i, j))

    def body(*refs):
        acc_ref = refs[-1] if nk > 1 else None
        a_ref, b_ref = refs[0], refs[1]
        part = jnp.dot(a_ref[...], b_ref[...], preferred_element_type=F32)
        if epilogue == "plain":
            def fin(acc):
                refs[2][...] = acc.astype(out_dtype)
        elif epilogue == "relu2":
            def fin(acc):
                refs[2][...] = acc.astype(BF16)
                refs[3][...] = jnp.square(jnp.maximum(acc, 0.0)).astype(BF16)
        elif epilogue == "add":
            def fin(acc):
                refs[3][...] = (acc + refs[2][...]).astype(out_dtype)
        elif epilogue == "resid":
            def fin(acc):
                refs[4][...] = refs[2][...] + refs[3][...] * acc
                refs[5][...] = acc.astype(BF16)
        else:
            def fin(acc):
                xv = refs[2][...] + refs[3][...] * acc
                refs[7][...] = xv
                refs[8][...] = acc.astype(BF16)
                r = lax.rsqrt(jnp.mean(xv * xv, axis=-1, keepdims=True) + RMS_EPS)
                refs[9][...] = (((xv * r) * refs[4][...]) * (1.0 + refs[5][...]) + refs[6][...]).astype(BF16)
        _accumulate(part, acc_ref, nk, fin)

    in_specs = [a_spec, b_spec]
    args = [a, b]
    if epilogue == "plain":
        out_shape, out_specs = jax.ShapeDtypeStruct((M, N), out_dtype), tile
    elif epilogue == "relu2":
        out_shape, out_specs = [jax.ShapeDtypeStruct((M, N), BF16)] * 2, [tile, tile]
    elif epilogue == "add":
        in_specs.append(tile)
        args.append(res)
        out_shape, out_specs = jax.ShapeDtypeStruct((M, N), out_dtype), tile
    else:
        row = pl.BlockSpec((1, tn), lambda i, j, k: (0, j))
        in_specs += [tile, row]
        args += [res, gate]
        out_shape, out_specs = [jax.ShapeDtypeStruct((M, N), F32), jax.ShapeDtypeStruct((M, N), BF16)], [tile, tile]
        if epilogue == "resid_norm":
            assert tn == N, "the next norm needs whole rows"
            in_specs += [row, row, row]
            args += list(norm)
            out_shape, out_specs = out_shape + [jax.ShapeDtypeStruct((M, N), BF16)], out_specs + [tile]
    return _call(
        body, deps, name=name, grid=(M // tm, N // tn, nk), in_specs=in_specs, out_specs=out_specs, out_shape=out_shape,
        scratch_shapes=[pltpu.VMEM((tm, tn), F32)] if nk > 1 else [],
        compiler_params=_params(("parallel", "parallel", "arbitrary")),
    )(*args)


def _mm_nt(a, b, *, name, n=None, epilogue="plain", extra=None, out_dtype=F32, tm=1024, tn=1024, tk=2048, deps=()):
    if a.ndim == 3:
        Q, M, Kq = a.shape
        K = Q * Kq
    else:
        (M, K), Kq = a.shape, a.shape[1]
    if b.ndim == 3:
        P, N, Ks = b.shape
    else:
        N, Ks = b.shape
    N = n or N
    tm, tn, tk = _tile(M, tm, 16), _tile(N, tn), _tile(min(Kq, Ks), tk)
    nk = K // tk
    if a.ndim == 3:
        pa = Kq // tk
        a_spec = pl.BlockSpec((None, tm, tk), lambda i, j, k: (k // pa, i, k % pa))
    else:
        a_spec = pl.BlockSpec((tm, tk), lambda i, j, k: (i, k))
    if b.ndim == 3:
        pb = Ks // tk
        b_spec = pl.BlockSpec((None, tn, tk), lambda i, j, k: (k // pb, j, k % pb))
    else:
        b_spec = pl.BlockSpec((tn, tk), lambda i, j, k: (j, k))
    tile = pl.BlockSpec((tm, tn), lambda i, j, k: (i, j))

    def body(*refs):
        acc_ref = refs[-1] if nk > 1 else None
        part = lax.dot_general(refs[0][...], refs[1][...], (((1,), (1,)), ((), ())), preferred_element_type=F32)
        if epilogue == "plain":
            def fin(acc):
                refs[2][...] = acc.astype(out_dtype)
        elif epilogue == "add":
            def fin(acc):
                refs[3][...] = (acc + refs[2][...]).astype(out_dtype)
        else:
            def fin(acc):
                refs[3][...] = (acc * (2.0 * jnp.maximum(refs[2][...].astype(F32), 0.0))).astype(out_dtype)
        _accumulate(part, acc_ref, nk, fin)

    in_specs, args = [a_spec, b_spec], [a, b]
    if epilogue != "plain":
        in_specs.append(tile)
        args.append(extra)
    return _call(
        body, deps, name=name, grid=(M // tm, N // tn, nk), in_specs=in_specs, out_specs=tile,
        out_shape=jax.ShapeDtypeStruct((M, N), out_dtype),
        scratch_shapes=[pltpu.VMEM((tm, tn), F32)] if nk > 1 else [],
        compiler_params=_params(("parallel", "parallel", "arbitrary")),
    )(*args)


def _mm_tn(a, b, *, name, out_parts=1, tm=1024, tn=1024, tk=4096, deps=()):
    if a.ndim == 3:
        Qa, M, Kq = a.shape
        Kd = Qa * Kq
    else:
        (M, Kd), Kq = a.shape, a.shape[1]
    if b.ndim == 3:
        Q, _, Nq = b.shape
        N = Q * Nq
    else:
        N, Nq = b.shape[1], b.shape[1]
    Ns = N // out_parts
    tn = _tile(Ns, tn)
    while Nq % tn or Ns % tn:
        tn -= LANES
    tm, tk = _tile(Kq, tm), _tile(M, tk, 16)
    nk = M // tk
    if a.ndim == 3:
        pa = Kq // tm
        a_spec = pl.BlockSpec((None, tk, tm), lambda i, j, k: (i // pa, k, i % pa))
    else:
        a_spec = pl.BlockSpec((tk, tm), lambda i, j, k: (k, i))
    if b.ndim == 3:
        pb = Nq // tn
        b_spec = pl.BlockSpec((None, tk, tn), lambda i, j, k: (j // pb, k, j % pb))
    else:
        b_spec = pl.BlockSpec((tk, tn), lambda i, j, k: (k, j))
    if out_parts > 1:
        po = Ns // tn
        o_spec = pl.BlockSpec((None, tm, tn), lambda i, j, k: (j // po, i, j % po))
        out_shape = jax.ShapeDtypeStruct((out_parts, Kd, Ns), BF16)
    else:
        o_spec = pl.BlockSpec((tm, tn), lambda i, j, k: (i, j))
        out_shape = jax.ShapeDtypeStruct((Kd, N), BF16)

    def body(*refs):
        acc_ref = refs[-1] if nk > 1 else None
        part = lax.dot_general(refs[0][...], refs[1][...], (((0,), (0,)), ((), ())), preferred_element_type=F32)

        def fin(acc):
            refs[2][...] = acc.astype(BF16)
        _accumulate(part, acc_ref, nk, fin)

    return _call(
        body, deps, name=name, grid=(Kd // tm, N // tn, nk),
        in_specs=[a_spec, b_spec], out_specs=o_spec, out_shape=out_shape,
        scratch_shapes=[pltpu.VMEM((tm, tn), F32)] if nk > 1 else [],
        compiler_params=_params(("parallel", "parallel", "arbitrary")),
    )(a, b)


def _rows(S, D, i_map=lambda i: (i, 0), ts=512):
    return pl.BlockSpec((ts, D), i_map)


def _norm_fwd(x, gain, sc, sh, *, name, deps=()):
    S, D = x.shape
    ts = _tile(S, 512, 16)
    vec = pl.BlockSpec((1, D), lambda i: (0, 0))

    def body(x_ref, g_ref, sc_ref, sh_ref, h_ref):
        xv = x_ref[...]
        r = lax.rsqrt(jnp.mean(xv * xv, axis=-1, keepdims=True) + RMS_EPS)
        h = (xv * r) * g_ref[...]
        h_ref[...] = (h * (1.0 + sc_ref[...]) + sh_ref[...]).astype(BF16)

    return _call(
        body, deps, name=name, grid=(S // ts,), in_specs=[_rows(S, D, ts=ts), vec, vec, vec], out_specs=_rows(S, D, ts=ts),
        out_shape=jax.ShapeDtypeStruct((S, D), BF16), compiler_params=_params(("parallel",)),
    )(x, gain, sc, sh)


def _loss_bwd(x, target, gain, gate_prev, *, name, deps=()):
    S, D = x.shape
    ts = _tile(S, 256, 16)
    vec = pl.BlockSpec((1, D), lambda i: (0, 0))

    def body(x_ref, t_ref, g_ref, gp_ref, dx_ref, dp_ref, sums_ref):
        @pl.when(pl.program_id(0) == 0)
        def _():
            sums_ref[...] = jnp.zeros_like(sums_ref)
        xv = x_ref[...]
        r = lax.rsqrt(jnp.mean(xv * xv, axis=-1, keepdims=True) + RMS_EPS)
        xn = xv * r
        err = xn * g_ref[...] - t_ref[...]
        loss = 0.5 * jnp.sum(jnp.mean(err * err, axis=-1, keepdims=True), axis=0, keepdims=True)
        dy = err * (1.0 / D)
        dxn = dy * g_ref[...]
        dx = r * (dxn - xn * jnp.mean(dxn * xn, axis=-1, keepdims=True))
        dx_ref[...] = dx
        dp_ref[...] = (gp_ref[...] * dx).astype(BF16)
        sums_ref[0:1, :] += jnp.sum(dy * xn, axis=0, keepdims=True)
        sums_ref[1:2, :] += jnp.broadcast_to(loss, (1, D))

    return _call(
        body, deps, name=name, grid=(S // ts,),
        in_specs=[_rows(S, D, ts=ts), _rows(S, D, ts=ts), vec, vec],
        out_specs=[_rows(S, D, ts=ts), _rows(S, D, ts=ts), pl.BlockSpec((8, D), lambda i: (0, 0))],
        out_shape=[jax.ShapeDtypeStruct((S, D), F32), jax.ShapeDtypeStruct((S, D), BF16), jax.ShapeDtypeStruct((8, D), F32)],
        compiler_params=_params(("arbitrary",)),
    )(x, target, gain, gate_prev)


def _norm_bwd(x, dh, dxp, mix, gain, sc, gate_prev, *, name, deps=()):
    S, D = x.shape
    ts = _tile(S, 256, 16)
    vec = pl.BlockSpec((1, D), lambda i: (0, 0))
    with_prev = gate_prev is not None

    def body(*refs):
        x_ref, dh_ref, dxp_ref, mix_ref, g_ref, sc_ref = refs[:6]
        outs = refs[7:] if with_prev else refs[6:]
        sums_ref = outs[-1]

        @pl.when(pl.program_id(0) == 0)
        def _():
            sums_ref[...] = jnp.zeros_like(sums_ref)
        xv, dhv, dxpv = x_ref[...], dh_ref[...].astype(F32), dxp_ref[...]
        r = lax.rsqrt(jnp.mean(xv * xv, axis=-1, keepdims=True) + RMS_EPS)
        xn = xv * r
        hn = xn * g_ref[...]
        dhn = dhv * (1.0 + sc_ref[...])
        dxn = dhn * g_ref[...]
        dx = dxpv + r * (dxn - xn * jnp.mean(dxn * xn, axis=-1, keepdims=True))
        outs[0][...] = dx
        if with_prev:
            outs[1][...] = (refs[6][...] * dx).astype(BF16)
        sums_ref[0:1, :] += jnp.sum(dhv, axis=0, keepdims=True)
        sums_ref[1:2, :] += jnp.sum(dhv * hn, axis=0, keepdims=True)
        sums_ref[2:3, :] += jnp.sum(dhn * xn, axis=0, keepdims=True)
        sums_ref[3:4, :] += jnp.sum(dxpv * mix_ref[...].astype(F32), axis=0, keepdims=True)

    tile = _rows(S, D, ts=ts)
    in_specs = [tile, tile, tile, tile, vec, vec] + ([vec] if with_prev else [])
    args = [x, dh, dxp, mix, gain, sc] + ([gate_prev] if with_prev else [])
    out_specs = [tile] + ([tile] if with_prev else []) + [pl.BlockSpec((8, D), lambda i: (0, 0))]
    out_shape = ([jax.ShapeDtypeStruct((S, D), F32)] + ([jax.ShapeDtypeStruct((S, D), BF16)] if with_prev else [])
                 + [jax.ShapeDtypeStruct((8, D), F32)])
    outs = _call(
        body, deps, name=name, grid=(S // ts,), in_specs=in_specs, out_specs=out_specs, out_shape=out_shape,
        compiler_params=_params(("arbitrary",)),
    )(*args)
    return (outs[0], outs[1], outs[2]) if with_prev else (outs[0], None, outs[1])


def _fgate_fwd(h, wf, bf, *, name, deps=()):
    S, D = h.shape
    ts = _tile(S, 256, 16)

    def body(h_ref, w_ref, b_ref, z_ref, f_ref, carry):
        @pl.when(pl.program_id(0) == 0)
        def _():
            carry[...] = jnp.zeros_like(carry)
        z = lax.dot_general(h_ref[...], w_ref[...], (((1,), (1,)), ((), ())), preferred_element_type=F32) + b_ref[...]
        logf = jnp.minimum(z, 0.0) - jnp.log(1.0 + jnp.exp(-jnp.abs(z)))
        row = lax.broadcasted_iota(jnp.int32, (ts, ts), 0)
        col = lax.broadcasted_iota(jnp.int32, (ts, ts), 1)
        tril = (col <= row).astype(F32)
        run = jnp.dot(tril, logf, preferred_element_type=F32, precision=lax.Precision.HIGHEST) + carry[0:1, :]
        z_ref[...] = z
        f_ref[...] = run
        carry[0:1, :] = run[ts - 1:ts, :]

    return _call(
        body, deps, name=name, grid=(S // ts,),
        in_specs=[pl.BlockSpec((ts, D), lambda i: (i, 0)), pl.BlockSpec((LANES, D), lambda i: (0, 0)),
                  pl.BlockSpec((1, LANES), lambda i: (0, 0))],
        out_specs=[pl.BlockSpec((ts, LANES), lambda i: (i, 0))] * 2,
        out_shape=[jax.ShapeDtypeStruct((S, LANES), F32)] * 2,
        scratch_shapes=[pltpu.VMEM((8, LANES), F32)],
        compiler_params=_params(("arbitrary",)),
    )(h, wf, bf)


def _fgate_bwd(dfq, dfk, z, *, name, deps=()):
    S = z.shape[0]
    ts = _tile(S, 256, 16)
    n = S // ts

    def body(dq_ref, dk_ref, z_ref, dz_ref, sums_ref, carry):
        @pl.when(pl.program_id(0) == 0)
        def _():
            carry[...] = jnp.zeros_like(carry)
            sums_ref[...] = jnp.zeros_like(sums_ref)
        df = dq_ref[...] - dk_ref[...]
        row = lax.broadcasted_iota(jnp.int32, (ts, ts), 0)
        col = lax.broadcasted_iota(jnp.int32, (ts, ts), 1)
        triu = (col >= row).astype(F32)
        run = jnp.dot(triu, df, preferred_element_type=F32, precision=lax.Precision.HIGHEST) + carry[0:1, :]
        zv = z_ref[...]
        dz = run * (1.0 / (1.0 + jnp.exp(zv)))
        dz_ref[...] = dz.astype(BF16)
        sums_ref[0:1, :] += jnp.sum(dz, axis=0, keepdims=True)
        carry[0:1, :] = run[0:1, :]

    rev = pl.BlockSpec((ts, LANES), lambda i: (n - 1 - i, 0))
    return _call(
        body, deps, name=name, grid=(n,), in_specs=[rev, rev, rev],
        out_specs=[rev, pl.BlockSpec((8, LANES), lambda i: (0, 0))],
        out_shape=[jax.ShapeDtypeStruct((S, LANES), BF16), jax.ShapeDtypeStruct((8, LANES), F32)],
        scratch_shapes=[pltpu.VMEM((8, LANES), F32)],
        compiler_params=_params(("arbitrary",)),
    )(dfq, dfk, z)


def _head_col(ref, rows, lane_mask):
    return jnp.sum(jnp.where(lane_mask, ref[rows, :], 0.0), axis=1, keepdims=True)


def _attn_fwd(qkv, fq, fk, *, heads, name, T=256, deps=()):
    S, D3 = qkv.shape
    D = D3 // 3
    dh = D // heads
    T = _tile(S, T, 16)
    nq = S // T
    scale = dh ** -0.5
    hp = fk.shape[1]

    def body(q_ref, k_ref, v_ref, fq_ref, fk_ref, o_ref, lse_ref):
        h = pl.program_id(0)

        @pl.when(h == 0)
        def _():
            lse_ref[...] = jnp.zeros_like(lse_ref)
        lane = lax.broadcasted_iota(jnp.int32, (1, LANES), 1) == h
        row = lax.broadcasted_iota(jnp.int32, (T, T), 0)
        col = lax.broadcasted_iota(jnp.int32, (T, T), 1)

        def q_block(qi, _):
            rows = pl.ds(pl.multiple_of(qi * T, T), T)
            q = q_ref[rows, :]
            fq_col = _head_col(fq_ref, rows, lane)

            def kv_block(kj, carry, diag):
                m, l, acc = carry
                cols = pl.ds(pl.multiple_of(kj * T, T), T)
                s = lax.dot_general(q, k_ref[cols, :], (((1,), (1,)), ((), ())), preferred_element_type=F32) * scale
                s = s + (fq_col - fk_ref[kj, pl.ds(h, 1), :])
                if diag:
                    s = jnp.where(col <= row, s, NEG)
                m_new = jnp.maximum(m, jnp.max(s, axis=1, keepdims=True))
                p = jnp.exp(s - m_new)
                alpha = jnp.exp(m - m_new)
                l = alpha * l + jnp.sum(p, axis=1, keepdims=True)
                acc = alpha * acc + jnp.dot(p.astype(BF16), v_ref[cols, :], preferred_element_type=F32)
                return m_new, l, acc

            init = (jnp.full((T, 1), NEG, F32), jnp.zeros((T, 1), F32), jnp.zeros((T, dh), F32))
            carry = lax.fori_loop(0, qi, lambda kj, cr: kv_block(kj, cr, False), init)
            m, l, acc = kv_block(qi, carry, True)
            o_ref[rows, :] = (acc / l).astype(BF16)
            lse_ref[rows, :] = jnp.where(lane, m + jnp.log(l), lse_ref[rows, :])
            return 0

        lax.fori_loop(0, nq, q_block, 0)

    head = lambda part: pl.BlockSpec((S, dh), lambda h: (0, part * heads + h))
    return _call(
        body, deps, name=name, grid=(heads,),
        in_specs=[head(0), head(1), head(2), pl.BlockSpec((S, LANES), lambda h: (0, 0)),
                  pl.BlockSpec((nq, hp, T), lambda h: (0, 0, 0))],
        out_specs=[pl.BlockSpec((S, dh), lambda h: (0, h)), pl.BlockSpec((S, LANES), lambda h: (0, 0))],
        out_shape=[jax.ShapeDtypeStruct((S, D), BF16), jax.ShapeDtypeStruct((S, LANES), F32)],
        compiler_params=_params(("arbitrary",)),
    )(qkv, qkv, qkv, fq, fk)


def _attn_bwd(qkv, o, do, fq, fk, lse, *, heads, name, T=256, deps=()):
    S, D3 = qkv.shape
    D = D3 // 3
    dh = D // heads
    T = _tile(S, T, 16)
    nq = S // T
    scale = dh ** -0.5
    hp = fk.shape[1]

    def body(q_ref, k_ref, v_ref, o_ref, do_ref, fq_ref, fk_ref, lse_ref, dqkv_ref, dfq_ref, dfk_ref,
             dq_acc, fq_col, lse_col, delta_col, dfq_col):
        h = pl.program_id(0)

        @pl.when(h == 0)
        def _():
            dfq_ref[...] = jnp.zeros_like(dfq_ref)
            dfk_ref[...] = jnp.zeros_like(dfk_ref)
        lane = lax.broadcasted_iota(jnp.int32, (1, LANES), 1) == h
        row = lax.broadcasted_iota(jnp.int32, (T, T), 0)
        col = lax.broadcasted_iota(jnp.int32, (T, T), 1)
        dq_acc[...] = jnp.zeros_like(dq_acc)
        dfq_col[...] = jnp.zeros_like(dfq_col)

        def prep(qi, _):
            rows = pl.ds(pl.multiple_of(qi * T, T), T)
            fq_col[rows, :] = _head_col(fq_ref, rows, lane)
            lse_col[rows, :] = _head_col(lse_ref, rows, lane)
            delta_col[rows, :] = jnp.sum(do_ref[rows, :].astype(F32) * o_ref[rows, :].astype(F32), axis=1, keepdims=True)
            return 0

        lax.fori_loop(0, nq, prep, 0)

        def kv_block(kj, _):
            cols = pl.ds(pl.multiple_of(kj * T, T), T)
            k, v = k_ref[cols, :], v_ref[cols, :]
            fk_row = fk_ref[kj, pl.ds(h, 1), :]

            def q_block(qi, carry, diag):
                dk, dv, dfk = carry
                rows = pl.ds(pl.multiple_of(qi * T, T), T)
                q, dov = q_ref[rows, :], do_ref[rows, :]
                s = lax.dot_general(q, k, (((1,), (1,)), ((), ())), preferred_element_type=F32) * scale
                s = s + (fq_col[rows, :] - fk_row)
                p = jnp.exp(s - lse_col[rows, :])
                if diag:
                    p = jnp.where(col <= row, p, 0.0)
                dp = lax.dot_general(dov, v, (((1,), (1,)), ((), ())), preferred_element_type=F32)
                ds = p * (dp - delta_col[rows, :])
                dsb = ds.astype(BF16)
                dv = dv + lax.dot_general(p.astype(BF16), dov, (((0,), (0,)), ((), ())), preferred_element_type=F32)
                dk = dk + lax.dot_general(dsb, q, (((0,), (0,)), ((), ())), preferred_element_type=F32)
                dq_acc[rows, :] += jnp.dot(dsb, k, preferred_element_type=F32)
                dfq_col[rows, :] += jnp.sum(ds, axis=1, keepdims=True)
                dfk = dfk + jnp.sum(ds, axis=0, keepdims=True)
                return dk, dv, dfk

            init = (jnp.zeros((T, dh), F32), jnp.zeros((T, dh), F32), jnp.zeros((1, T), F32))
            carry = q_block(kj, init, True)
            dk, dv, dfk = lax.fori_loop(kj + 1, nq, lambda qi, cr: q_block(qi, cr, False), carry)
            dqkv_ref[1, cols, :] = (dk * scale).astype(BF16)
            dqkv_ref[2, cols, :] = dv.astype(BF16)
            dfk_ref[kj, pl.ds(h, 1), :] = dfk
            return 0

        lax.fori_loop(0, nq, kv_block, 0)

        def finish(qi, _):
            rows = pl.ds(pl.multiple_of(qi * T, T), T)
            dqkv_ref[0, rows, :] = (dq_acc[rows, :] * scale).astype(BF16)
            dfq_ref[rows, :] = jnp.where(lane, dfq_col[rows, :], dfq_ref[rows, :])
            return 0

        lax.fori_loop(0, nq, finish, 0)

    head = lambda part: pl.BlockSpec((S, dh), lambda h: (0, part * heads + h))
    own = pl.BlockSpec((S, dh), lambda h: (0, h))
    full = pl.BlockSpec((S, LANES), lambda h: (0, 0))
    krow = pl.BlockSpec((nq, hp, T), lambda h: (0, 0, 0))
    return _call(
        body, deps, name=name, grid=(heads,),
        in_specs=[head(0), head(1), head(2), own, own, full, krow, full],
        out_specs=[pl.BlockSpec((3, S, dh), lambda h: (0, 0, h)), full, krow],
        out_shape=[jax.ShapeDtypeStruct((3, S, D), BF16), jax.ShapeDtypeStruct((S, LANES), F32),
                   jax.ShapeDtypeStruct((nq, hp, T), F32)],
        scratch_shapes=[pltpu.VMEM((S, dh), F32)] + [pltpu.VMEM((S, 1), F32)] * 4,
        compiler_params=_params(("arbitrary",)),
    )(qkv, qkv, qkv, o, do, fq, fk, lse)


def _shift_down(v, n):
    rows = lax.broadcasted_iota(jnp.int32, v.shape, 0)
    return jnp.where(rows >= n, pltpu.roll(v, n, axis=0), 0.0)


def _shift_up(v, n):
    S = v.shape[0]
    rows = lax.broadcasted_iota(jnp.int32, v.shape, 0)
    return jnp.where(rows < S - n, pltpu.roll(v, S - n, axis=0), 0.0)


def _conv_fwd(proj, conv_w, *, name, cb=LANES, deps=()):
    S, D3 = proj.shape
    D = D3 // 3
    nb = D // cb

    def body(bg_ref, cg_ref, u_ref, w_ref, z_ref):
        uc = cg_ref[...].astype(F32) * u_ref[...].astype(F32)
        w = w_ref[...]
        y = w[2:3, :] * uc + w[1:2, :] * _shift_down(uc, 1) + w[0:1, :] * _shift_down(uc, 2)
        z_ref[...] = (bg_ref[...].astype(F32) * y).astype(BF16)

    part = lambda g: pl.BlockSpec((S, cb), lambda j: (0, g * nb + j))
    return _call(
        body, deps, name=name, grid=(nb,),
        in_specs=[part(0), part(1), part(2), pl.BlockSpec((3, cb), lambda j: (0, j))],
        out_specs=pl.BlockSpec((S, cb), lambda j: (0, j)),
        out_shape=jax.ShapeDtypeStruct((S, D), BF16), compiler_params=_params(("parallel",)),
    )(proj, proj, proj, conv_w)


def _conv_bwd(proj, conv_w, dz, *, name, cb=LANES, deps=()):
    S, D3 = proj.shape
    D = D3 // 3
    nb = D // cb

    def body(bg_ref, cg_ref, u_ref, w_ref, dz_ref, dp_ref, dw_ref):
        cg, u = cg_ref[...].astype(F32), u_ref[...].astype(F32)
        uc = cg * u
        w = w_ref[...]
        uc1, uc2 = _shift_down(uc, 1), _shift_down(uc, 2)
        y = w[2:3, :] * uc + w[1:2, :] * uc1 + w[0:1, :] * uc2
        dz = dz_ref[...].astype(F32)
        dp_ref[0] = (dz * y).astype(BF16)
        dy = dz * bg_ref[...].astype(F32)
        duc = w[2:3, :] * dy + w[1:2, :] * _shift_up(dy, 1) + w[0:1, :] * _shift_up(dy, 2)
        dp_ref[1] = (duc * u).astype(BF16)
        dp_ref[2] = (duc * cg).astype(BF16)
        dw_ref[...] = jnp.zeros_like(dw_ref)
        dw_ref[0:1, :] = jnp.sum(dy * uc2, axis=0, keepdims=True)
        dw_ref[1:2, :] = jnp.sum(dy * uc1, axis=0, keepdims=True)
        dw_ref[2:3, :] = jnp.sum(dy * uc, axis=0, keepdims=True)

    part = lambda g: pl.BlockSpec((S, cb), lambda j: (0, g * nb + j))
    return _call(
        body, deps, name=name, grid=(nb,),
        in_specs=[part(0), part(1), part(2), pl.BlockSpec((3, cb), lambda j: (0, j)), pl.BlockSpec((S, cb), lambda j: (0, j))],
        out_specs=[pl.BlockSpec((3, S, cb), lambda j: (0, 0, j)), pl.BlockSpec((8, cb), lambda j: (0, j))],
        out_shape=[jax.ShapeDtypeStruct((3, S, D), BF16), jax.ShapeDtypeStruct((8, D), F32)],
        compiler_params=_params(("parallel",)),
    )(proj, proj, proj, conv_w, dz)


def _ada_fwd(c_all, ada_w, *, name, deps=()):
    L, D, Ns = ada_w.shape
    tn = _tile(Ns, 512)

    def body(c_ref, w_ref, o_ref, act_ref):
        cv = c_ref[...]
        act = cv * (1.0 / (1.0 + jnp.exp(-cv)))
        act_ref[...] = act
        o_ref[...] = jnp.dot(act.astype(BF16), w_ref[...].astype(BF16), preferred_element_type=F32)

    return _call(
        body, deps, name=name, grid=(L, Ns // tn),
        in_specs=[pl.BlockSpec((N_DEV, D), lambda l, j: (0, 0)), pl.BlockSpec((None, D, tn), lambda l, j: (l, 0, j))],
        out_specs=[pl.BlockSpec((None, N_DEV, tn), lambda l, j: (l, 0, j)), pl.BlockSpec((N_DEV, D), lambda l, j: (0, 0))],
        out_shape=[jax.ShapeDtypeStruct((L, N_DEV, Ns), F32), jax.ShapeDtypeStruct((N_DEV, D), F32)],
        compiler_params=_params(("arbitrary", "arbitrary")),
    )(c_all, ada_w)


def _select_mod(gathered, *, name, deps=()):
    _, LB, Ns = gathered.shape
    L = LB // N_DEV

    def body(g_ref, o_ref):
        x, y, c = _me()
        b = 4 * x + 2 * y + c
        for j in range(N_CHIPS):
            for l in range(L):
                o_ref[j, pl.ds(l, 1), :] = g_ref[2 * j + c, pl.ds(l * N_DEV + b, 1), :]

    return _call(
        body, deps, name=name, out_shape=jax.ShapeDtypeStruct((N_CHIPS, L, Ns), F32),
        in_specs=[pl.BlockSpec(memory_space=pltpu.VMEM)], out_specs=pl.BlockSpec(memory_space=pltpu.VMEM),
        compiler_params=_params(),
    )(gathered)


def _adamw_math(w, g, m, v):
    m = ADAM_B1 * m + (1.0 - ADAM_B1) * g
    v = ADAM_B2 * v + (1.0 - ADAM_B2) * jnp.square(g)
    m_hat = m / (1.0 - ADAM_B1 ** ADAM_STEP)
    v_hat = v / (1.0 - ADAM_B2 ** ADAM_STEP)
    delta = -ADAM_LR * (m_hat / (jnp.sqrt(v_hat) + ADAM_EPS) + ADAM_WD * w)
    return delta, m, v


def _adamw_shards(w, m, v, groups, chip, *, name):
    L, R, C = w.shape
    if R % 16 == 0:
        tr, tc = _tile(R, 128, 16), C
    else:
        tr, tc = R, _tile(C, 256)
    nr, nc = R // tr, C // tc

    def body(chip_ref, w_ref, m_ref, v_ref, *rest):
        srcs, (g_ref, d_ref, mo_ref, vo_ref) = rest[:2 * N_CHIPS * L], rest[2 * N_CHIPS * L:]
        for l in range(L):
            @pl.when(pl.program_id(0) == l)
            def _():
                s = srcs[2 * N_CHIPS * l:2 * N_CHIPS * (l + 1)]
                mine, other = s[0][...].astype(F32), s[N_CHIPS][...].astype(F32)
                for k in range(1, N_CHIPS):
                    mine = mine + s[k][...].astype(F32)
                    other = other + s[N_CHIPS + k][...].astype(F32)
                g = mine + other
                delta, mn, vn = _adamw_math(w_ref[...], g, m_ref[...], v_ref[...])
                g_ref[...] = g
                d_ref[...] = delta
                mo_ref[...] = mn
                vo_ref[...] = vn

    tile = pl.BlockSpec((None, tr, tc), lambda l, i, j, chip_ref: (l, i, j))

    def block(layer, k):
        def index(l, i, j, chip_ref):
            idle_i, idle_j = jnp.where(l < layer, 0, nr - 1), jnp.where(l < layer, 0, nc - 1)
            return (jnp.bitwise_xor(chip_ref[0], k), jnp.where(l == layer, i, idle_i), jnp.where(l == layer, j, idle_j))
        return pl.BlockSpec((None, tr, tc), index)

    in_specs, args = [tile] * 3, [w, m, v]
    for layer, (parts, land, sib) in enumerate(groups):
        in_specs += [block(layer, k) for k in range(N_CHIPS)] * 2
        args += [parts, land, land, land, sib, sib, sib, sib]
    return pl.pallas_call(
        body, name=name,
        grid_spec=pltpu.PrefetchScalarGridSpec(num_scalar_prefetch=1, grid=(L, nr, nc), in_specs=in_specs, out_specs=[tile] * 4),
        out_shape=[jax.ShapeDtypeStruct((L, R, C), F32)] * 4, compiler_params=_params(("arbitrary", "arbitrary", "arbitrary")),
    )(chip, *args)


def _adamw_ada(w, m, v, act_t, dmod, *, name, tr=256, deps=()):
    L, D, Ns = w.shape
    tr = _tile(D, tr, 8)

    def body(w_ref, m_ref, v_ref, a_ref, d_ref, g_ref, dl_ref, mo_ref, vo_ref):
        x, y, _ = _me()
        g = jnp.dot(a_ref[...], d_ref[2 * x + y], preferred_element_type=F32, precision=lax.Precision.HIGHEST)
        delta, mn, vn = _adamw_math(w_ref[...], g, m_ref[...], v_ref[...])
        g_ref[...] = g
        dl_ref[...] = delta
        mo_ref[...] = mn
        vo_ref[...] = vn

    tile = pl.BlockSpec((None, tr, Ns), lambda l, i: (l, i, 0))
    return _call(
        body, deps, name=name, grid=(L, D // tr),
        in_specs=[tile] * 3 + [pl.BlockSpec((tr, N_DEV), lambda l, i: (i, 0)),
                               pl.BlockSpec((N_CHIPS, None, N_DEV, Ns), lambda l, i: (0, l, 0, 0))],
        out_specs=[tile] * 4, out_shape=[jax.ShapeDtypeStruct((L, D, Ns), F32)] * 4,
        compiler_params=_params(("parallel", "parallel")),
    )(w, m, v, act_t, dmod)


def _adamw_small(w, m, v, gathered, *, rows, name, deps=()):
    n, D = w.shape

    def body(w_ref, m_ref, v_ref, s_ref, g_ref, d_ref, mo_ref, vo_ref):
        for r, src in enumerate(rows):
            g = s_ref[0, src:src + 1, :]
            for d in range(1, N_DEV):
                g = g + s_ref[d, src:src + 1, :]
            g_ref[r:r + 1, :] = g
        g = g_ref[...]
        delta, mn, vn = _adamw_math(w_ref[...], g, m_ref[...], v_ref[...])
        d_ref[...] = delta
        mo_ref[...] = mn
        vo_ref[...] = vn

    vm = pl.BlockSpec(memory_space=pltpu.VMEM)
    return _call(
        body, deps, name=name, in_specs=[vm] * 4, out_specs=[vm] * 4,
        out_shape=[jax.ShapeDtypeStruct((n, D), F32)] * 4, compiler_params=_params(),
    )(w, m, v, gathered)


def _adamw_conv_w(w, m, v, gathered4, *, name, deps=()):
    Cs = w.shape[1]

    def body(w_ref, m_ref, v_ref, s_ref, g_ref, d_ref, mo_ref, vo_ref):
        x, y, _ = _me()
        j = 2 * x + y
        g = s_ref[j, 0]
        for d in range(1, N_DEV):
            g = g + s_ref[j, d]
        delta, mn, vn = _adamw_math(w_ref[...], g, m_ref[...], v_ref[...])
        g_ref[...] = g
        d_ref[...] = delta
        mo_ref[...] = mn
        vo_ref[...] = vn

    vm = pl.BlockSpec(memory_space=pltpu.VMEM)
    return _call(
        body, deps, name=name, in_specs=[vm] * 4, out_specs=[vm] * 4,
        out_shape=[jax.ShapeDtypeStruct((8, Cs), F32)] * 4, compiler_params=_params(),
    )(w, m, v, gathered4)


def _loss_sum(gathered, *, row, name, deps=()):
    _, _, D = gathered.shape

    def body(s_ref, o_ref):
        t = s_ref[0, row:row + 1, :]
        for d in range(1, N_DEV):
            t = t + s_ref[d, row:row + 1, :]
        o_ref[...] = jnp.broadcast_to(t, (8, D))

    vm = pl.BlockSpec(memory_space=pltpu.VMEM)
    return pl.pallas_call(body, name=name, in_specs=[vm], out_specs=vm, out_shape=jax.ShapeDtypeStruct((8, D), F32),
                          compiler_params=_params())(gathered)


def _pad_rows(a, n):
    return jnp.pad(a, ((0, n - a.shape[0]), (0, 0)))


def kernel(x, c, ada_w, ada_b, norm_mix, norm_mlp, fox_w_in, fox_b_f, fox_w_out, conv_w_in, conv_w, conv_w_out, mlp_w_up, mlp_w_down, final_norm, loss_target, m_ada_w, m_ada_b, m_norm_mix, m_norm_mlp, m_fox_w_in, m_fox_b_f, m_fox_w_out, m_conv_w_in, m_conv_w, m_conv_w_out, m_mlp_w_up, m_mlp_w_down, m_final_norm, v_ada_w, v_ada_b, v_norm_mix, v_norm_mlp, v_fox_w_in, v_fox_b_f, v_fox_w_out, v_conv_w_in, v_conv_w, v_conv_w_out, v_mlp_w_up, v_mlp_w_down, v_final_norm):
    S, D = x.shape[1], x.shape[2]
    H = fox_b_f.shape[-1]
    L = ada_w.shape[0]
    NM = ada_b.shape[1] // D
    Ns_ada = ada_w.shape[2]
    Cs_fox = fox_w_in.shape[2]
    Cs_conv = conv_w.shape[2]
    x0 = x[0]
    target = loss_target[0]

    chip = (2 * lax.axis_index("x") + lax.axis_index("y")).astype(jnp.int32).reshape(1)

    fin_t = jnp.transpose(fox_w_in, (0, 2, 1))
    shards = dict(fin=(fin_t, 0), fout=(fox_w_out, 0), up0=(mlp_w_up, 0), dn0=(mlp_w_down, 0), cin=(conv_w_in, 0),
                  cout=(conv_w_out, 0), up1=(mlp_w_up, 1), dn1=(mlp_w_down, 1))
    halves = dict(fin="cols", fout="rows", up0="cols", dn0="rows", cin="cols", cout="rows", up1="cols", dn1="rows")
    gathers, placed = {}, {}

    def place(key, dep=None):
        placed[key] = _place_cast(*shards[key], chip, halves=halves[key], name="place_" + key, dep=dep)
        return placed[key]

    def start_gather(key, dep=None):
        gathers[key] = _split_start("gather1", [(placed[key],)], name="gather_start_" + key, dep=dep)
        return gathers[key][3]

    def pass_gather(key, after):
        landed = _split_wait("gather1", gathers[key], after, name="gather_landed_" + key)
        gathers[key] = _split_start("gather2", landed, name="gather_pass_" + key)
        return gathers[key][3]

    def gathered(key, after):
        return _split_wait("gather2", gathers[key], after, name="gather_wait_" + key)[0][0]

    place("fin")
    tok = start_gather("fin")
    for key in ("fout", "up0", "dn0", "cin", "cout", "up1", "dn1"):
        tok = place(key, tok)

    c_all = _allgather8(_pad_rows(c, 8), name="gather_c", deps=(tok,))[:, 0, :]
    mod_part, c_act = _ada_fwd(c_all, ada_w, name="ada_fwd")
    mod_all = _allgather8(mod_part.reshape(L * N_DEV, Ns_ada), name="gather_mod")
    mod = _select_mod(mod_all, name="select_mod")
    mod = jnp.transpose(mod, (1, 0, 2)).reshape(L, NM, 1, D) + ada_b.reshape(L, NM, 1, D)
    conv_w_all = _allgather8(_pad_rows(conv_w[0], 8), name="gather_conv_w")
    conv_w_full = jnp.transpose(conv_w_all[0::2, :3, :], (1, 0, 2)).reshape(3, D)

    def vec(a):
        return a.reshape(1, D)

    h0 = _norm_fwd(x0, vec(norm_mix[0]), mod[0, 1], mod[0, 0], name="norm_mix0", deps=(conv_w_all,))
    tok = h0
    for key in ("fout", "up0", "dn0", "cin", "cout", "up1", "dn1"):
        tok = start_gather(key, tok)
    tok = pass_gather("fin", [h0, tok])
    w_fin_t = jnp.transpose(gathered("fin", [tok]), (0, 2, 1, 3)).reshape(N_CHIPS * Cs_fox, D)
    w_f_t = _pad_rows(w_fin_t[3 * D:], LANES)
    tok = pass_gather("fout", [w_fin_t])
    qkv = _mm_nt(h0, w_fin_t, n=3 * D, name="fox_in", out_dtype=BF16, deps=(tok,))
    tok = pass_gather("up0", [qkv])
    b_f = jnp.pad(fox_b_f, ((0, 0), (0, LANES - H)))
    z_f, F_col = _fgate_fwd(h0, w_f_t, b_f, name="fgate_fwd", deps=(tok,))
    hp = max(8, H)
    at_f, at = _tile(S, 1024, 16), _tile(S, 1024, 16)
    F_rows = _pad_rows(jnp.transpose(F_col[:, :H]), hp)
    F_row = jnp.transpose(F_rows.reshape(hp, S // at, at), (1, 0, 2))
    o, lse = _attn_fwd(qkv, F_col, jnp.transpose(F_rows.reshape(hp, S // at_f, at_f), (1, 0, 2)), heads=H, name="attn_fwd", T=at_f)
    w_fout = gathered("fout", [o]).reshape(D, D)
    tok = pass_gather("dn0", [o])
    x1, mix0, h1 = _mm_nn(o, w_fout, name="fox_out", epilogue="resid_norm", res=x0, gate=mod[0, 2],
                          norm=(vec(norm_mlp[0]), mod[0, 4], mod[0, 3]), tm=512, tn=D, deps=(tok,))
    w_up0 = gathered("up0", [h1]).reshape(2 * N_CHIPS, D, -1)
    tok = pass_gather("cin", [h1])
    u0, a0 = _mm_nn(h1, w_up0, name="mlp_up0", epilogue="relu2", deps=(tok,))
    w_dn0 = gathered("dn0", [a0]).reshape(-1, D)
    tok = pass_gather("cout", [a0])
    x2, y0 = _mm_nn(a0, w_dn0, name="mlp_down0", epilogue="resid", res=x1, gate=mod[0, 5], deps=(tok,))
    h2 = _norm_fwd(x2, vec(norm_mix[1]), mod[1, 1], mod[1, 0], name="norm_mix1")
    g_cin = gathered("cin", [h2]).reshape(2 * N_CHIPS, D, -1)
    tok = pass_gather("up1", [h2])
    proj = _mm_nn(h2, g_cin, name="conv_in", deps=(tok,))
    w_cin = jnp.transpose(g_cin, (1, 0, 2)).reshape(D, 3 * D)
    zc = _conv_fwd(proj, conv_w_full, name="conv_fwd")
    w_cout = gathered("cout", [zc]).reshape(D, D)
    tok = pass_gather("dn1", [zc])
    x3, mix1, h3 = _mm_nn(zc, w_cout, name="conv_out", epilogue="resid_norm", res=x2, gate=mod[1, 2],
                          norm=(vec(norm_mlp[1]), mod[1, 4], mod[1, 3]), tm=512, tn=D, deps=(tok,))
    w_up1 = gathered("up1", [h3]).reshape(2 * N_CHIPS, D, -1)
    u1, a1 = _mm_nn(h3, w_up1, name="mlp_up1", epilogue="relu2")
    w_dn1 = gathered("dn1", [a1]).reshape(-1, D)
    x4, y1 = _mm_nn(a1, w_dn1, name="mlp_down1", epilogue="resid", res=x3, gate=mod[1, 5])
    w_up = [jnp.transpose(w_up0, (1, 0, 2)).reshape(D, -1), jnp.transpose(w_up1, (1, 0, 2)).reshape(D, -1)]
    w_dn = [w_dn0, w_dn1]

    dx4, dy1, sums_f = _loss_bwd(x4, target, vec(final_norm), mod[1, 5], name="loss_bwd")
    du1 = _mm_nt(dy1, w_dn[1], name="mlp_down1_dx", epilogue="drelu2", extra=u1, out_dtype=BF16)
    def start_scatter(tag, parts_list):
        groups = [(p, lax.empty(p.shape, p.dtype)) for p in parts_list]
        return _split_start("scatter", groups, name="scatter_start_" + tag)

    def start_sibling(tag, scatter, after):
        landed = _split_wait("scatter", scatter, after, name="scatter_wait_" + tag)
        groups = [(p, ld, lax.empty(p.shape, p.dtype)) for p, ld in landed]
        return _split_start("sibling", groups, name="sibling_start_" + tag)

    gw_dn1 = _mm_tn(a1, dy1, name="mlp_down1_dw")
    gw_up1 = _mm_tn(h3, du1, name="mlp_up1_dw", out_parts=N_CHIPS)
    sc1 = start_scatter("mlp1", [gw_dn1.reshape(N_CHIPS, -1, D), gw_up1])
    dh3 = _mm_nt(du1, w_up[1], name="mlp_up1_dx", out_dtype=BF16, tk=4096, deps=(sc1[3],))
    dx3, dmix1, sums_mlp1 = _norm_bwd(x3, dh3, dx4, y1, vec(norm_mlp[1]), mod[1, 4], mod[1, 2], name="norm_mlp1_bwd")
    dzc = _mm_nt(dmix1, w_cout, name="conv_out_dx", out_dtype=BF16)
    gw_cout = _mm_tn(zc, dmix1, name="conv_out_dw")
    dproj, dconv_w = _conv_bwd(proj, conv_w_full, dzc, name="conv_bwd")
    gw_cin = _mm_tn(h2, dproj, name="conv_in_dw", out_parts=N_CHIPS, tn=512)
    sc2 = start_scatter("conv", [gw_cout.reshape(N_CHIPS, -1, D), gw_cin])
    dh2 = _mm_nt(dproj, w_cin, name="conv_in_dx", out_dtype=BF16, deps=(sc2[3],))
    dx2, dy0, sums_mix1 = _norm_bwd(x2, dh2, dx3, mix1, vec(norm_mix[1]), mod[1, 1], mod[0, 5], name="norm_mix1_bwd")
    du0 = _mm_nt(dy0, w_dn[0], name="mlp_down0_dx", epilogue="drelu2", extra=u0, out_dtype=BF16)
    gw_dn0 = _mm_tn(a0, dy0, name="mlp_down0_dw")
    gw_up0 = _mm_tn(h1, du0, name="mlp_up0_dw", out_parts=N_CHIPS)
    sc3 = start_scatter("mlp0", [gw_dn0.reshape(N_CHIPS, -1, D), gw_up0])
    sb1 = start_sibling("mlp1", sc1, [sc3[3]])
    dh1 = _mm_nt(du0, w_up[0], name="mlp_up0_dx", out_dtype=BF16, tk=4096, deps=(sb1[3],))
    dx1, dmix0, sums_mlp0 = _norm_bwd(x1, dh1, dx2, y0, vec(norm_mlp[0]), mod[0, 4], mod[0, 2], name="norm_mlp0_bwd")
    do = _mm_nt(dmix0, w_fout, name="fox_out_dx", out_dtype=BF16)
    gw_fout = _mm_tn(o, dmix0, name="fox_out_dw")
    dqkv, dfq, dfk = _attn_bwd(qkv, o, do, F_col, F_row, lse, heads=H, name="attn_bwd", T=at)
    dfk_col = jnp.pad(jnp.transpose(jnp.transpose(dfk, (1, 0, 2)).reshape(hp, S)[:H]), ((0, 0), (0, LANES - H)))
    sb3 = start_sibling("mlp0", sc3, [dqkv])
    dz_f, sums_bf = _fgate_bwd(dfq, dfk_col, z_f, name="fgate_bwd", deps=(sb3[3],))
    gw_qkv_t = _mm_tn(dqkv, h0, name="fox_in_dw")
    gw_f_t = _mm_tn(dz_f, h0, name="fox_gate_dw")
    gw_fin_t = jnp.concatenate([gw_qkv_t, gw_f_t[:H]], axis=0).reshape(N_CHIPS, Cs_fox, D)
    sc4 = start_scatter("fox", [gw_fout.reshape(N_CHIPS, -1, D), gw_fin_t])
    sb2 = start_sibling("conv", sc2, [sc4[3]])
    dh0_f = _mm_nn(dz_f, w_f_t, name="fox_gate_dx", out_dtype=F32, deps=(sb2[3],))
    dh0 = _mm_nn(dqkv, w_fin_t, name="fox_in_dx", epilogue="add", res=dh0_f, out_dtype=BF16)
    grad_x, _, sums_mix0 = _norm_bwd(x0, dh0, dx1, mix0, vec(norm_mix[0]), mod[0, 1], None, name="norm_mix0_bwd")

    outs = {}

    def put(name_, res, shape):
        for kind, r in zip(("grad", "delta", "new_m", "new_v"), res):
            outs[kind + "_" + name_] = r.reshape(shape)

    def shards_update(tag, w_, m_, v_, groups):
        return _adamw_shards(w_, m_, v_, groups, chip, name="adamw_" + tag)

    g_conv = _split_wait("sibling", sb2, [grad_x], name="sibling_wait_conv")
    put("conv_w_out", shards_update("conv_out", conv_w_out, m_conv_w_out, v_conv_w_out, g_conv[0:1]), conv_w_out.shape)
    r_cin = shards_update("conv_in", conv_w_in, m_conv_w_in, v_conv_w_in, g_conv[1:2])
    put("conv_w_in", r_cin, conv_w_in.shape)
    g_mlp1 = _split_wait("sibling", sb1, [r_cin[0]], name="sibling_wait_mlp1")
    g_mlp0 = _split_wait("sibling", sb3, [r_cin[0]], name="sibling_wait_mlp0")
    put("mlp_w_down", shards_update("mlp_down", mlp_w_down, m_mlp_w_down, v_mlp_w_down, [g_mlp0[0], g_mlp1[0]]), mlp_w_down.shape)
    r_up = shards_update("mlp_up", mlp_w_up, m_mlp_w_up, v_mlp_w_up, [g_mlp0[1], g_mlp1[1]])
    put("mlp_w_up", r_up, mlp_w_up.shape)
    sb4 = start_sibling("fox", sc4, [r_up[0]])

    dmod_rows = []
    for sm, sl in ((sums_mix0, sums_mlp0), (sums_mix1, sums_mlp1)):
        dmod_rows += [sm[0:1], sm[1:2], sm[3:4], sl[0:1], sl[1:2], sl[3:4]]
    bf_row = jnp.pad(sums_bf[0:1], ((0, 0), (0, D - LANES)))
    small = jnp.concatenate([sums_mix0[2:3], sums_mix1[2:3], sums_mlp0[2:3], sums_mlp1[2:3], sums_f[0:1], sums_f[1:2], bf_row,
                             jnp.zeros((1, D), F32)] + dmod_rows + [dconv_w[0:3]], axis=0)
    small_all = _allgather8(_pad_rows(small, -(-small.shape[0] // 8) * 8), name="gather_small", deps=(sb4[3],))
    loss = _loss_sum(small_all, row=5, name="loss_sum")[0, 0]

    def rows_of(a_mix, a_mlp, a_fin, a_bf, a_ada):
        return jnp.concatenate([a_mix, a_mlp, a_fin.reshape(1, D), jnp.pad(a_bf, ((0, 0), (0, D - H))),
                                a_ada.reshape(L * NM, D)], axis=0)
    n_small = 2 * L + 2 + L * NM
    rw = -(-n_small // 8) * 8
    w_s = _pad_rows(rows_of(norm_mix, norm_mlp, final_norm, fox_b_f, ada_b), rw)
    m_s = _pad_rows(rows_of(m_norm_mix, m_norm_mlp, m_final_norm, m_fox_b_f, m_ada_b), rw)
    v_s = _pad_rows(rows_of(v_norm_mix, v_norm_mlp, v_final_norm, v_fox_b_f, v_ada_b), rw)
    src_rows = [0, 1, 2, 3, 4, 6] + [8 + r for r in range(L * NM)] + [7] * (rw - n_small)
    res_s = _adamw_small(w_s, m_s, v_s, small_all, rows=tuple(src_rows), name="adamw_small")
    for kind, r in zip(("grad", "delta", "new_m", "new_v"), res_s):
        outs[kind + "_norm_mix"] = r[0:L]
        outs[kind + "_norm_mlp"] = r[L:2 * L]
        outs[kind + "_final_norm"] = r[2 * L]
        outs[kind + "_fox_b_f"] = r[2 * L + 1:2 * L + 2, :H]
        outs[kind + "_ada_b"] = r[2 * L + 2:n_small].reshape(L, NM * D)

    dmod_all = small_all[:, 8:8 + L * NM, :].reshape(N_DEV, L, N_CHIPS, Ns_ada)
    dmod4 = jnp.transpose(dmod_all, (2, 1, 0, 3))
    act_t = jnp.transpose(c_act)
    res_a = _adamw_ada(ada_w, m_ada_w, v_ada_w, act_t, dmod4, name="adamw_ada")
    put("ada_w", res_a, ada_w.shape)

    r0 = 8 + L * NM
    dconv_all = jnp.pad(small_all[:, r0:r0 + 3, :], ((0, 0), (0, 5), (0, 0)))
    dconv4 = jnp.transpose(dconv_all.reshape(N_DEV, 8, N_CHIPS, Cs_conv), (2, 0, 1, 3))
    res_c = _adamw_conv_w(_pad_rows(conv_w[0], 8), _pad_rows(m_conv_w[0], 8), _pad_rows(v_conv_w[0], 8), dconv4,
                          name="adamw_conv_w")
    for kind, r in zip(("grad", "delta", "new_m", "new_v"), res_c):
        outs[kind + "_conv_w"] = r[:3].reshape(conv_w.shape)

    g_fox = _split_wait("sibling", sb4, [res_a[0], res_c[0], res_s[0]], name="sibling_wait_fox")
    put("fox_w_out", shards_update("fox_out", fox_w_out, m_fox_w_out, v_fox_w_out, g_fox[0:1]), fox_w_out.shape)
    t3 = lambda a: jnp.transpose(a, (0, 2, 1))
    for kind, r in zip(("grad", "delta", "new_m", "new_v"),
                       shards_update("fox_in", t3(fox_w_in), t3(m_fox_w_in), t3(v_fox_w_in), g_fox[1:2])):
        outs[kind + "_fox_w_in"] = t3(r)

    names = ["ada_w", "ada_b", "norm_mix", "norm_mlp", "fox_w_in", "fox_b_f", "fox_w_out", "conv_w_in", "conv_w", "conv_w_out",
             "mlp_w_up", "mlp_w_down", "final_norm"]
    return (loss, grad_x[None], *[outs["grad_" + n] for n in names], *[outs["delta_" + n] for n in names],
            *[outs["new_m_" + n] for n in names], *[outs["new_v_" + n] for n in names])
```

```python
import functools

import jax
import jax.numpy as jnp
from jax import lax
from jax.experimental import pallas as pl
from jax.experimental.pallas import tpu as pltpu

F32 = jnp.float32
BF16 = jnp.bfloat16
MESH = pl.DeviceIdType.MESH
ANY = pl.BlockSpec(memory_space=pl.ANY)
HBM = pl.BlockSpec(memory_space=pltpu.HBM)
SEM = pl.BlockSpec(memory_space=pltpu.SEMAPHORE)
EFFECT = pltpu.SideEffectType.DATAFLOW_SIDE_EFFECTING

RMS_EPS = 1e-6
ADAM_LR = 0.001
ADAM_B1 = 0.9
ADAM_B2 = 0.999
ADAM_EPS = 1e-08
ADAM_WD = 0.01
ADAM_STEP = 10
N_CHIPS = 4
N_DEV = 8
LANES = 128
VMEM_LIMIT = 56 * 1024 * 1024
NEG = -1e30


def _params(sems=None, vmem=VMEM_LIMIT):
    return pltpu.CompilerParams(dimension_semantics=sems, vmem_limit_bytes=vmem)


def _tile(n, pref, unit=LANES):
    if n <= pref:
        return n
    t = (pref // unit) * unit
    while n % t:
        t -= unit
    return t


def _me():
    return lax.axis_index("x"), lax.axis_index("y"), lax.axis_index("c")


def _call(body, deps, **kw):
    nd = len(deps)

    def wrapped(*refs):
        body(*refs[nd:])

    kw["in_specs"] = [ANY] * nd + list(kw["in_specs"])
    fn = pl.pallas_call(wrapped, **kw)
    return lambda *args: fn(*deps, *args)


def _allgather8(v, *, name, deps=()):
    R, C = v.shape

    def body(v_ref, out_ref, send_sems, recv_sems):
        x, y, c = _me()
        me = 4 * x + 2 * y + c
        out_ref[me] = v_ref[...]
        copies = []
        for k in range(1, N_DEV):
            px, py, pc = (x + (k >> 2)) % 2, (y + ((k >> 1) & 1)) % 2, (c + (k & 1)) % 2
            copies.append(pltpu.make_async_remote_copy(
                src_ref=v_ref, dst_ref=out_ref.at[me], send_sem=send_sems.at[k - 1], recv_sem=recv_sems.at[k - 1],
                device_id=(px, py, pc), device_id_type=MESH))
        for cp in copies:
            cp.start()
        for k in range(1, N_DEV):
            px, py, pc = (x + (k >> 2)) % 2, (y + ((k >> 1) & 1)) % 2, (c + (k & 1)) % 2
            peer = 4 * px + 2 * py + pc
            pltpu.make_async_remote_copy(
                src_ref=v_ref, dst_ref=out_ref.at[peer], send_sem=send_sems.at[k - 1], recv_sem=recv_sems.at[k - 1],
                device_id=(px, py, pc), device_id_type=MESH).wait_recv()
        for cp in copies:
            cp.wait_send()

    return _call(
        body, deps, name=name,
        out_shape=jax.ShapeDtypeStruct((N_DEV, R, C), v.dtype),
        in_specs=[pl.BlockSpec(memory_space=pltpu.VMEM)],
        out_specs=pl.BlockSpec(memory_space=pltpu.VMEM),
        scratch_shapes=[pltpu.SemaphoreType.DMA((N_DEV - 1,)), pltpu.SemaphoreType.DMA((N_DEV - 1,))],
        compiler_params=_params(),
    )(v)


def _chip_peers(x, y):
    return [((x + (k >> 1)) % 2, (y + (k & 1)) % 2) for k in range(1, N_CHIPS)]


def _slot(x, y, k):
    return 2 * ((x + (k >> 1)) % 2) + (y + (k & 1)) % 2


def _split_copies(kind, groups, send_sems, recv_sems):
    x, y, c = _me()
    j = 2 * x + y
    copies = []
    for a, g in enumerate(groups):
        if kind == "sibling":
            parts, land, sib = g
            for k in range(N_CHIPS):
                s = _slot(x, y, k)
                copies.append(pltpu.make_async_remote_copy(
                    src_ref=(parts if k == 0 else land).at[s], dst_ref=sib.at[s], send_sem=send_sems.at[N_CHIPS * a + k],
                    recv_sem=recv_sems.at[N_CHIPS * a + k], device_id=(x, y, 1 - c), device_id_type=MESH))
            continue
        land = g[-1]
        for k, (px, py) in enumerate(_chip_peers(x, y)):
            if kind == "gather1":
                src, dst, to = land.at[j, c], land.at[j, c], (px, py, c)
            elif kind == "gather2":
                src, dst, to = land.at[2 * px + py, c], land.at[2 * px + py, c], (x, y, 1 - c)
            else:
                src, dst, to = g[0].at[2 * px + py], land.at[j], (px, py, c)
            copies.append(pltpu.make_async_remote_copy(
                src_ref=src, dst_ref=dst, send_sem=send_sems.at[3 * a + k], recv_sem=recv_sems.at[3 * a + k],
                device_id=to, device_id_type=MESH))
    return copies


def _split_start(kind, groups, *, name, dep=None):
    flat = [a for g in groups for a in g]
    nf, per = len(flat), len(groups[0])
    ncp = len(groups) * (N_CHIPS if kind == "sibling" else 3)
    nd = 0 if dep is None else 1

    def body(*refs):
        ins = refs[nd:nd + nf]
        send_sems, recv_sems, token = refs[nd + nf], refs[nd + nf + 1], refs[-1]
        for cp in _split_copies(kind, [ins[i:i + per] for i in range(0, nf, per)], send_sems, recv_sems):
            cp.start()
        token[...] = jnp.zeros_like(token)

    outs = pl.pallas_call(
        body, name=name,
        out_shape=(pltpu.SemaphoreType.DMA((ncp,)), pltpu.SemaphoreType.DMA((ncp,)), *[pltpu.HBM(a.shape, a.dtype) for a in flat],
                   jax.ShapeDtypeStruct((8, LANES), F32)),
        in_specs=[ANY] * nd + [HBM] * nf,
        out_specs=(SEM, SEM, *[HBM] * nf, pl.BlockSpec(memory_space=pltpu.VMEM)),
        input_output_aliases={nd + i: 2 + i for i in range(nf)},
        compiler_params=pltpu.CompilerParams(has_side_effects=EFFECT),
    )(*([dep] if nd else []), *[pltpu.with_memory_space_constraint(a, pltpu.HBM) for a in flat])
    thru = list(outs[2:2 + nf])
    return outs[0], outs[1], [tuple(thru[i:i + per]) for i in range(0, nf, per)], outs[-1]


def _split_wait(kind, started, after, *, name):
    send_sems, recv_sems, groups, _ = started
    flat = [a for g in groups for a in g]
    nf, per = len(flat), len(groups[0])

    def body(*refs):
        ins = refs[:nf]
        for cp in _split_copies(kind, [ins[i:i + per] for i in range(0, nf, per)], refs[nf], refs[nf + 1]):
            cp.wait_send()
            cp.wait_recv()

    outs = pl.pallas_call(
        body, name=name,
        out_shape=tuple(pltpu.HBM(a.shape, a.dtype) for a in flat),
        in_specs=[HBM] * nf + [SEM, SEM] + [ANY] * len(after), out_specs=tuple([HBM] * nf),
        input_output_aliases={i: i for i in range(nf)},
        compiler_params=pltpu.CompilerParams(has_side_effects=EFFECT),
    )(*flat, send_sems, recv_sems, *after)
    outs = list(outs)
    return [tuple(outs[i:i + per]) for i in range(0, nf, per)]


def _place_cast(shards, layer, chip, *, halves, name, dep=None):
    _, R, C = shards.shape
    if halves == "rows":
        hr, hc = R // 2, C
    else:
        hr, hc = R, C // 2
    if hr % 16 == 0:
        tr, tc = _tile(hr, 512, 16), hc
    else:
        tr, tc = hr, _tile(hc, 256)
    nr, nc = hr // tr, hc // tc

    def body(chip_ref, x_ref, *rest):
        rest[-1][...] = x_ref[...].astype(BF16)

    if halves == "rows":
        o_map = lambda i, j, chip_ref: (chip_ref[0], i // nr, i % nr, j)
    else:
        o_map = lambda i, j, chip_ref: (chip_ref[0], j // nc, i, j % nc)
    return pl.pallas_call(
        body, name=name,
        grid_spec=pltpu.PrefetchScalarGridSpec(
            num_scalar_prefetch=1, grid=(R // tr, C // tc),
            in_specs=[pl.BlockSpec((None, tr, tc), lambda i, j, chip_ref: (layer, i, j))] + ([] if dep is None else [ANY]),
            out_specs=pl.BlockSpec((None, None, tr, tc), o_map)),
        out_shape=jax.ShapeDtypeStruct((N_CHIPS, 2, hr, hc), BF16), compiler_params=_params(("parallel", "parallel")),
    )(chip, shards, *([] if dep is None else [dep]))


def _accumulate(part, acc_ref, nk, finalize):
    if nk == 1:
        finalize(part)
        return
    k = pl.program_id(2)

    @pl.when(k == 0)
    def _():
        acc_ref[...] = part

    @pl.when(k > 0)
    def _():
        acc_ref[...] += part

    @pl.when(k == nk - 1)
    def _():
        finalize(acc_ref[...])


def _mm_nn(a, b, *, name, epilogue="plain", res=None, gate=None, norm=None, out_dtype=BF16, tm=1024, tn=1024, tk=2048, deps=()):
    if a.ndim == 3:
        Q, M, Kq = a.shape
        K = Q * Kq
    else:
        (M, K), Kq = a.shape, a.shape[1]
    tm, tk = _tile(M, tm, 16), _tile(Kq, tk)
    if a.ndim == 3:
        pa = Kq // tk
        a_spec = pl.BlockSpec((None, tm, tk), lambda i, j, k: (k // pa, i, k % pa))
    else:
        a_spec = pl.BlockSpec((tm, tk), lambda i, j, k: (i, k))
    if b.ndim == 3:
        P, _, Ns = b.shape
        N = P * Ns
        tn = _tile(Ns, tn)
        per = Ns // tn
        b_spec = pl.BlockSpec((None, tk, tn), lambda i, j, k: (j // per, k, j % per))
    else:
        N = b.shape[1]
        tn = _tile(N, tn)
        b_spec = pl.BlockSpec((tk, tn), lambda i, j, k: (k, j))
    nk = K // tk
    tile = pl.BlockSpec((tm, tn), lambda i, j, k: (i, j))

    def body(*refs):
        acc_ref = refs[-1] if nk > 1 else None
        a_ref, b_ref = refs[0], refs[1]
        part = jnp.dot(a_ref[...], b_ref[...], preferred_element_type=F32)
        if epilogue == "plain":
            def fin(acc):
                refs[2][...] = acc.astype(out_dtype)
        elif epilogue == "relu2":
            def fin(acc):
                refs[2][...] = acc.astype(BF16)
                refs[3][...] = jnp.square(jnp.maximum(acc, 0.0)).astype(BF16)
        elif epilogue == "add":
            def fin(acc):
                refs[3][...] = (acc + refs[2][...]).astype(out_dtype)
        elif epilogue == "resid":
            def fin(acc):
                refs[4][...] = refs[2][...] + refs[3][...] * acc
                refs[5][...] = acc.astype(BF16)
        else:
            def fin(acc):
                xv = refs[2][...] + refs[3][...] * acc
                refs[7][...] = xv
                refs[8][...] = acc.astype(BF16)
                r = lax.rsqrt(jnp.mean(xv * xv, axis=-1, keepdims=True) + RMS_EPS)
                refs[9][...] = (((xv * r) * refs[4][...]) * (1.0 + refs[5][...]) + refs[6][...]).astype(BF16)
        _accumulate(part, acc_ref, nk, fin)

    in_specs = [a_spec, b_spec]
    args = [a, b]
    if epilogue == "plain":
        out_shape, out_specs = jax.ShapeDtypeStruct((M, N), out_dtype), tile
    elif epilogue == "relu2":
        out_shape, out_specs = [jax.ShapeDtypeStruct((M, N), BF16)] * 2, [tile, tile]
    elif epilogue == "add":
        in_specs.append(tile)
        args.append(res)
        out_shape, out_specs = jax.ShapeDtypeStruct((M, N), out_dtype), tile
    else:
        row = pl.BlockSpec((1, tn), lambda i, j, k: (0, j))
        in_specs += [tile, row]
        args += [res, gate]
        out_shape, out_specs = [jax.ShapeDtypeStruct((M, N), F32), jax.ShapeDtypeStruct((M, N), BF16)], [tile, tile]
        if epilogue == "resid_norm":
            assert tn == N, "the next norm needs whole rows"
            in_specs += [row, row, row]
            args += list(norm)
            out_shape, out_specs = out_shape + [jax.ShapeDtypeStruct((M, N), BF16)], out_specs + [tile]
    return _call(
        body, deps, name=name, grid=(M // tm, N // tn, nk), in_specs=in_specs, out_specs=out_specs, out_shape=out_shape,
        scratch_shapes=[pltpu.VMEM((tm, tn), F32)] if nk > 1 else [],
        compiler_params=_params(("parallel", "parallel", "arbitrary")),
    )(*args)


def _mm_nt(a, b, *, name, n=None, epilogue="plain", extra=None, out_dtype=F32, tm=1024, tn=1024, tk=2048, deps=()):
    if a.ndim == 3:
        Q, M, Kq = a.shape
        K = Q * Kq
    else:
        (M, K), Kq = a.shape, a.shape[1]
    if b.ndim == 3:
        P, N, Ks = b.shape
    else:
        N, Ks = b.shape
    N = n or N
    tm, tn, tk = _tile(M, tm, 16), _tile(N, tn), _tile(min(Kq, Ks), tk)
    nk = K // tk
    if a.ndim == 3:
        pa = Kq // tk
        a_spec = pl.BlockSpec((None, tm, tk), lambda i, j, k: (k // pa, i, k % pa))
    else:
        a_spec = pl.BlockSpec((tm, tk), lambda i, j, k: (i, k))
    if b.ndim == 3:
        pb = Ks // tk
        b_spec = pl.BlockSpec((None, tn, tk), lambda i, j, k: (k // pb, j, k % pb))
    else:
        b_spec = pl.BlockSpec((tn, tk), lambda i, j, k: (j, k))
    tile = pl.BlockSpec((tm, tn), lambda i, j, k: (i, j))

    def body(*refs):
        acc_ref = refs[-1] if nk > 1 else None
        part = lax.dot_general(refs[0][...], refs[1][...], (((1,), (1,)), ((), ())), preferred_element_type=F32)
        if epilogue == "plain":
            def fin(acc):
                refs[2][...] = acc.astype(out_dtype)
        elif epilogue == "add":
            def fin(acc):
                refs[3][...] = (acc + refs[2][...]).astype(out_dtype)
        else:
            def fin(acc):
                refs[3][...] = (acc * (2.0 * jnp.maximum(refs[2][...].astype(F32), 0.0))).astype(out_dtype)
        _accumulate(part, acc_ref, nk, fin)

    in_specs, args = [a_spec, b_spec], [a, b]
    if epilogue != "plain":
        in_specs.append(tile)
        args.append(extra)
    return _call(
        body, deps, name=name, grid=(M // tm, N // tn, nk), in_specs=in_specs, out_specs=tile,
        out_shape=jax.ShapeDtypeStruct((M, N), out_dtype),
        scratch_shapes=[pltpu.VMEM((tm, tn), F32)] if nk > 1 else [],
        compiler_params=_params(("parallel", "parallel", "arbitrary")),
    )(*args)


def _mm_tn(a, b, *, name, out_parts=1, tm=1024, tn=1024, tk=4096, deps=()):
    if a.ndim == 3:
        Qa, M, Kq = a.shape
        Kd = Qa * Kq
    else:
        (M, Kd), Kq = a.shape, a.shape[1]
    if b.ndim == 3:
        Q, _, Nq = b.shape
        N = Q * Nq
    else:
        N, Nq = b.shape[1], b.shape[1]
    Ns = N // out_parts
    tn = _tile(Ns, tn)
    while Nq % tn or Ns % tn:
        tn -= LANES
    tm, tk = _tile(Kq, tm), _tile(M, tk, 16)
    nk = M // tk
    if a.ndim == 3:
        pa = Kq // tm
        a_spec = pl.BlockSpec((None, tk, tm), lambda i, j, k: (i // pa, k, i % pa))
    else:
        a_spec = pl.BlockSpec((tk, tm), lambda i, j, k: (k, i))
    if b.ndim == 3:
        pb = Nq // tn
        b_spec = pl.BlockSpec((None, tk, tn), lambda i, j, k: (j // pb, k, j % pb))
    else:
        b_spec = pl.BlockSpec((tk, tn), lambda i, j, k: (k, j))
    if out_parts > 1:
        po = Ns // tn
        o_spec = pl.BlockSpec((None, tm, tn), lambda i, j, k: (j // po, i, j % po))
        out_shape = jax.ShapeDtypeStruct((out_parts, Kd, Ns), BF16)
    else:
        o_spec = pl.BlockSpec((tm, tn), lambda i, j, k: (i, j))
        out_shape = jax.ShapeDtypeStruct((Kd, N), BF16)

    def body(*refs):
        acc_ref = refs[-1] if nk > 1 else None
        part = lax.dot_general(refs[0][...], refs[1][...], (((0,), (0,)), ((), ())), preferred_element_type=F32)

        def fin(acc):
            refs[2][...] = acc.astype(BF16)
        _accumulate(part, acc_ref, nk, fin)

    return _call(
        body, deps, name=name, grid=(Kd // tm, N // tn, nk),
        in_specs=[a_spec, b_spec], out_specs=o_spec, out_shape=out_shape,
        scratch_shapes=[pltpu.VMEM((tm, tn), F32)] if nk > 1 else [],
        compiler_params=_params(("parallel", "parallel", "arbitrary")),
    )(a, b)


def _rows(S, D, i_map=lambda i: (i, 0), ts=512):
    return pl.BlockSpec((ts, D), i_map)


def _norm_fwd(x, gain, sc, sh, *, name, deps=()):
    S, D = x.shape
    ts = _tile(S, 512, 16)
    vec = pl.BlockSpec((1, D), lambda i: (0, 0))

    def body(x_ref, g_ref, sc_ref, sh_ref, h_ref):
        xv = x_ref[...]
        r = lax.rsqrt(jnp.mean(xv * xv, axis=-1, keepdims=True) + RMS_EPS)
        h = (xv * r) * g_ref[...]
        h_ref[...] = (h * (1.0 + sc_ref[...]) + sh_ref[...]).astype(BF16)

    return _call(
        body, deps, name=name, grid=(S // ts,), in_specs=[_rows(S, D, ts=ts), vec, vec, vec], out_specs=_rows(S, D, ts=ts),
        out_shape=jax.ShapeDtypeStruct((S, D), BF16), compiler_params=_params(("parallel",)),
    )(x, gain, sc, sh)


def _loss_bwd(x, target, gain, gate_prev, *, name, deps=()):
    S, D = x.shape
    ts = _tile(S, 256, 16)
    vec = pl.BlockSpec((1, D), lambda i: (0, 0))

    def body(x_ref, t_ref, g_ref, gp_ref, dx_ref, dp_ref, sums_ref):
        @pl.when(pl.program_id(0) == 0)
        def _():
            sums_ref[...] = jnp.zeros_like(sums_ref)
        xv = x_ref[...]
        r = lax.rsqrt(jnp.mean(xv * xv, axis=-1, keepdims=True) + RMS_EPS)
        xn = xv * r
        err = xn * g_ref[...] - t_ref[...]
        loss = 0.5 * jnp.sum(jnp.mean(err * err, axis=-1, keepdims=True), axis=0, keepdims=True)
        dy = err * (1.0 / D)
        dxn = dy * g_ref[...]
        dx = r * (dxn - xn * jnp.mean(dxn * xn, axis=-1, keepdims=True))
        dx_ref[...] = dx
        dp_ref[...] = (gp_ref[...] * dx).astype(BF16)
        sums_ref[0:1, :] += jnp.sum(dy * xn, axis=0, keepdims=True)
        sums_ref[1:2, :] += jnp.broadcast_to(loss, (1, D))

    return _call(
        body, deps, name=name, grid=(S // ts,),
        in_specs=[_rows(S, D, ts=ts), _rows(S, D, ts=ts), vec, vec],
        out_specs=[_rows(S, D, ts=ts), _rows(S, D, ts=ts), pl.BlockSpec((8, D), lambda i: (0, 0))],
        out_shape=[jax.ShapeDtypeStruct((S, D), F32), jax.ShapeDtypeStruct((S, D), BF16), jax.ShapeDtypeStruct((8, D), F32)],
        compiler_params=_params(("arbitrary",)),
    )(x, target, gain, gate_prev)


def _norm_bwd(x, dh, dxp, mix, gain, sc, gate_prev, *, name, deps=()):
    S, D = x.shape
    ts = _tile(S, 256, 16)
    vec = pl.BlockSpec((1, D), lambda i: (0, 0))
    with_prev = gate_prev is not None

    def body(*refs):
        x_ref, dh_ref, dxp_ref, mix_ref, g_ref, sc_ref = refs[:6]
        outs = refs[7:] if with_prev else refs[6:]
        sums_ref = outs[-1]

        @pl.when(pl.program_id(0) == 0)
        def _():
            sums_ref[...] = jnp.zeros_like(sums_ref)
        xv, dhv, dxpv = x_ref[...], dh_ref[...].astype(F32), dxp_ref[...]
        r = lax.rsqrt(jnp.mean(xv * xv, axis=-1, keepdims=True) + RMS_EPS)
        xn = xv * r
        hn = xn * g_ref[...]
        dhn = dhv * (1.0 + sc_ref[...])
        dxn = dhn * g_ref[...]
        dx = dxpv + r * (dxn - xn * jnp.mean(dxn * xn, axis=-1, keepdims=True))
        outs[0][...] = dx
        if with_prev:
            outs[1][...] = (refs[6][...] * dx).astype(BF16)
        sums_ref[0:1, :] += jnp.sum(dhv, axis=0, keepdims=True)
        sums_ref[1:2, :] += jnp.sum(dhv * hn, axis=0, keepdims=True)
        sums_ref[2:3, :] += jnp.sum(dhn * xn, axis=0, keepdims=True)
        sums_ref[3:4, :] += jnp.sum(dxpv * mix_ref[...].astype(F32), axis=0, keepdims=True)

    tile = _rows(S, D, ts=ts)
    in_specs = [tile, tile, tile, tile, vec, vec] + ([vec] if with_prev else [])
    args = [x, dh, dxp, mix, gain, sc] + ([gate_prev] if with_prev else [])
    out_specs = [tile] + ([tile] if with_prev else []) + [pl.BlockSpec((8, D), lambda i: (0, 0))]
    out_shape = ([jax.ShapeDtypeStruct((S, D), F32)] + ([jax.ShapeDtypeStruct((S, D), BF16)] if with_prev else [])
                 + [jax.ShapeDtypeStruct((8, D), F32)])
    outs = _call(
        body, deps, name=name, grid=(S // ts,), in_specs=in_specs, out_specs=out_specs, out_shape=out_shape,
        compiler_params=_params(("arbitrary",)),
    )(*args)
    return (outs[0], outs[1], outs[2]) if with_prev else (outs[0], None, outs[1])


def _fgate_fwd(h, wf, bf, *, name, deps=()):
    S, D = h.shape
    ts = _tile(S, 256, 16)

    def body(h_ref, w_ref, b_ref, z_ref, f_ref, carry):
        @pl.when(pl.program_id(0) == 0)
        def _():
            carry[...] = jnp.zeros_like(carry)
        z = lax.dot_general(h_ref[...], w_ref[...], (((1,), (1,)), ((), ())), preferred_element_type=F32) + b_ref[...]
        logf = jnp.minimum(z, 0.0) - jnp.log(1.0 + jnp.exp(-jnp.abs(z)))
        row = lax.broadcasted_iota(jnp.int32, (ts, ts), 0)
        col = lax.broadcasted_iota(jnp.int32, (ts, ts), 1)
        tril = (col <= row).astype(F32)
        run = jnp.dot(tril, logf, preferred_element_type=F32, precision=lax.Precision.HIGHEST) + carry[0:1, :]
        z_ref[...] = z
        f_ref[...] = run
        carry[0:1, :] = run[ts - 1:ts, :]

    return _call(
        body, deps, name=name, grid=(S // ts,),
        in_specs=[pl.BlockSpec((ts, D), lambda i: (i, 0)), pl.BlockSpec((LANES, D), lambda i: (0, 0)),
                  pl.BlockSpec((1, LANES), lambda i: (0, 0))],
        out_specs=[pl.BlockSpec((ts, LANES), lambda i: (i, 0))] * 2,
        out_shape=[jax.ShapeDtypeStruct((S, LANES), F32)] * 2,
        scratch_shapes=[pltpu.VMEM((8, LANES), F32)],
        compiler_params=_params(("arbitrary",)),
    )(h, wf, bf)


def _fgate_bwd(dfq, dfk, z, *, name, deps=()):
    S = z.shape[0]
    ts = _tile(S, 256, 16)
    n = S // ts

    def body(dq_ref, dk_ref, z_ref, dz_ref, sums_ref, carry):
        @pl.when(pl.program_id(0) == 0)
        def _():
            carry[...] = jnp.zeros_like(carry)
            sums_ref[...] = jnp.zeros_like(sums_ref)
        df = dq_ref[...] - dk_ref[...]
        row = lax.broadcasted_iota(jnp.int32, (ts, ts), 0)
        col = lax.broadcasted_iota(jnp.int32, (ts, ts), 1)
        triu = (col >= row).astype(F32)
        run = jnp.dot(triu, df, preferred_element_type=F32, precision=lax.Precision.HIGHEST) + carry[0:1, :]
        zv = z_ref[...]
        dz = run * (1.0 / (1.0 + jnp.exp(zv)))
        dz_ref[...] = dz.astype(BF16)
        sums_ref[0:1, :] += jnp.sum(dz, axis=0, keepdims=True)
        carry[0:1, :] = run[0:1, :]

    rev = pl.BlockSpec((ts, LANES), lambda i: (n - 1 - i, 0))
    return _call(
        body, deps, name=name, grid=(n,), in_specs=[rev, rev, rev],
        out_specs=[rev, pl.BlockSpec((8, LANES), lambda i: (0, 0))],
        out_shape=[jax.ShapeDtypeStruct((S, LANES), BF16), jax.ShapeDtypeStruct((8, LANES), F32)],
        scratch_shapes=[pltpu.VMEM((8, LANES), F32)],
        compiler_params=_params(("arbitrary",)),
    )(dfq, dfk, z)


def _head_col(ref, rows, lane_mask):
    return jnp.sum(jnp.where(lane_mask, ref[rows, :], 0.0), axis=1, keepdims=True)


def _attn_fwd(qkv, fq, fk, *, heads, name, T=256, deps=()):
    S, D3 = qkv.shape
    D = D3 // 3
    dh = D // heads
    T = _tile(S, T, 16)
    nq = S // T
    scale = dh ** -0.5
    hp = fk.shape[1]

    def body(q_ref, k_ref, v_ref, fq_ref, fk_ref, o_ref, lse_ref):
        h = pl.program_id(0)

        @pl.when(h == 0)
        def _():
            lse_ref[...] = jnp.zeros_like(lse_ref)
        lane = lax.broadcasted_iota(jnp.int32, (1, LANES), 1) == h
        row = lax.broadcasted_iota(jnp.int32, (T, T), 0)
        col = lax.broadcasted_iota(jnp.int32, (T, T), 1)

        def q_block(qi, _):
            rows = pl.ds(pl.multiple_of(qi * T, T), T)
            q = q_ref[rows, :]
            fq_col = _head_col(fq_ref, rows, lane)

            def kv_block(kj, carry, diag):
                m, l, acc = carry
                cols = pl.ds(pl.multiple_of(kj * T, T), T)
                s = lax.dot_general(q, k_ref[cols, :], (((1,), (1,)), ((), ())), preferred_element_type=F32) * scale
                s = s + (fq_col - fk_ref[kj, pl.ds(h, 1), :])
                if diag:
                    s = jnp.where(col <= row, s, NEG)
                m_new = jnp.maximum(m, jnp.max(s, axis=1, keepdims=True))
                p = jnp.exp(s - m_new)
                alpha = jnp.exp(m - m_new)
                l = alpha * l + jnp.sum(p, axis=1, keepdims=True)
                acc = alpha * acc + jnp.dot(p.astype(BF16), v_ref[cols, :], preferred_element_type=F32)
                return m_new, l, acc

            init = (jnp.full((T, 1), NEG, F32), jnp.zeros((T, 1), F32), jnp.zeros((T, dh), F32))
            carry = lax.fori_loop(0, qi, lambda kj, cr: kv_block(kj, cr, False), init)
            m, l, acc = kv_block(qi, carry, True)
            o_ref[rows, :] = (acc / l).astype(BF16)
            lse_ref[rows, :] = jnp.where(lane, m + jnp.log(l), lse_ref[rows, :])
            return 0

        lax.fori_loop(0, nq, q_block, 0)

    head = lambda part: pl.BlockSpec((S, dh), lambda h: (0, part * heads + h))
    return _call(
        body, deps, name=name, grid=(heads,),
        in_specs=[head(0), head(1), head(2), pl.BlockSpec((S, LANES), lambda h: (0, 0)),
                  pl.BlockSpec((nq, hp, T), lambda h: (0, 0, 0))],
        out_specs=[pl.BlockSpec((S, dh), lambda h: (0, h)), pl.BlockSpec((S, LANES), lambda h: (0, 0))],
        out_shape=[jax.ShapeDtypeStruct((S, D), BF16), jax.ShapeDtypeStruct((S, LANES), F32)],
        compiler_params=_params(("arbitrary",)),
    )(qkv, qkv, qkv, fq, fk)


def _attn_bwd(qkv, o, do, fq, fk, lse, *, heads, name, T=256, deps=()):
    S, D3 = qkv.shape
    D = D3 // 3
    dh = D // heads
    T = _tile(S, T, 16)
    nq = S // T
    scale = dh ** -0.5
    hp = fk.shape[1]

    def body(q_ref, k_ref, v_ref, o_ref, do_ref, fq_ref, fk_ref, lse_ref, dqkv_ref, dfq_ref, dfk_ref,
             dq_acc, fq_col, lse_col, delta_col, dfq_col):
        h = pl.program_id(0)

        @pl.when(h == 0)
        def _():
            dfq_ref[...] = jnp.zeros_like(dfq_ref)
            dfk_ref[...] = jnp.zeros_like(dfk_ref)
        lane = lax.broadcasted_iota(jnp.int32, (1, LANES), 1) == h
        Th = T // 2
        dq_acc[...] = jnp.zeros_like(dq_acc)
        dfq_col[...] = jnp.zeros_like(dfq_col)

        def prep(qi, _):
            rows = pl.ds(pl.multiple_of(qi * T, T), T)
            fq_col[rows, :] = _head_col(fq_ref, rows, lane)
            lse_col[rows, :] = _head_col(lse_ref, rows, lane)
            delta_col[rows, :] = jnp.sum(do_ref[rows, :].astype(F32) * o_ref[rows, :].astype(F32), axis=1, keepdims=True)
            return 0

        lax.fori_loop(0, nq, prep, 0)

        def kv_block(kj, _):
            cols = pl.ds(pl.multiple_of(kj * T, T), T)
            k, v = k_ref[cols, :], v_ref[cols, :]
            fk_row = fk_ref[kj, pl.ds(h, 1), :]

            def pair_grad(rows, kk, vv, fk_r, masked):
                q, dov = q_ref[rows, :], do_ref[rows, :]
                s = lax.dot_general(q, kk, (((1,), (1,)), ((), ())), preferred_element_type=F32) * scale
                s = s + (fq_col[rows, :] - fk_r)
                p = jnp.exp(s - lse_col[rows, :])
                if masked:
                    p = jnp.where(lax.broadcasted_iota(jnp.int32, p.shape, 1) <= lax.broadcasted_iota(jnp.int32, p.shape, 0), p, 0.0)
                dp = lax.dot_general(dov, vv, (((1,), (1,)), ((), ())), preferred_element_type=F32)
                ds = p * (dp - delta_col[rows, :])
                dsb = ds.astype(BF16)
                dv = lax.dot_general(p.astype(BF16), dov, (((0,), (0,)), ((), ())), preferred_element_type=F32)
                dk = lax.dot_general(dsb, q, (((0,), (0,)), ((), ())), preferred_element_type=F32)
                dq_acc[rows, :] += jnp.dot(dsb, kk, preferred_element_type=F32)
                dfq_col[rows, :] += jnp.sum(ds, axis=1, keepdims=True)
                return dk, dv, jnp.sum(ds, axis=0, keepdims=True)

            def q_block(qi, carry):
                rows = pl.ds(pl.multiple_of(qi * T, T), T)
                return tuple(c + g for c, g in zip(carry, pair_grad(rows, k, v, fk_row, False)))

            top = pair_grad(cols, k[:Th], v[:Th], fk_row[:, :Th], True)
            bot = pair_grad(pl.ds(pl.multiple_of(kj * T + Th, Th), Th), k[Th:], v[Th:], fk_row[:, Th:], True)
            carry = (jnp.concatenate([top[0], bot[0]], axis=0), jnp.concatenate([top[1], bot[1]], axis=0),
                     jnp.concatenate([top[2], bot[2]], axis=1))
            dk, dv, dfk = lax.fori_loop(kj + 1, nq, q_block, carry)
            dqkv_ref[1, cols, :] = (dk * scale).astype(BF16)
            dqkv_ref[2, cols, :] = dv.astype(BF16)
            dfk_ref[kj, pl.ds(h, 1), :] = dfk
            return 0

        lax.fori_loop(0, nq, kv_block, 0)

        def finish(qi, _):
            rows = pl.ds(pl.multiple_of(qi * T, T), T)
            dqkv_ref[0, rows, :] = (dq_acc[rows, :] * scale).astype(BF16)
            dfq_ref[rows, :] = jnp.where(lane, dfq_col[rows, :], dfq_ref[rows, :])
            return 0

        lax.fori_loop(0, nq, finish, 0)

    head = lambda part: pl.BlockSpec((S, dh), lambda h: (0, part * heads + h))
    own = pl.BlockSpec((S, dh), lambda h: (0, h))
    full = pl.BlockSpec((S, LANES), lambda h: (0, 0))
    krow = pl.BlockSpec((nq, hp, T), lambda h: (0, 0, 0))
    return _call(
        body, deps, name=name, grid=(heads,),
        in_specs=[head(0), head(1), head(2), own, own, full, krow, full],
        out_specs=[pl.BlockSpec((3, S, dh), lambda h: (0, 0, h)), full, krow],
        out_shape=[jax.ShapeDtypeStruct((3, S, D), BF16), jax.ShapeDtypeStruct((S, LANES), F32),
                   jax.ShapeDtypeStruct((nq, hp, T), F32)],
        scratch_shapes=[pltpu.VMEM((S, dh), F32)] + [pltpu.VMEM((S, 1), F32)] * 4,
        compiler_params=_params(("arbitrary",)),
    )(qkv, qkv, qkv, o, do, fq, fk, lse)


def _shift_down(v, n):
    rows = lax.broadcasted_iota(jnp.int32, v.shape, 0)
    return jnp.where(rows >= n, pltpu.roll(v, n, axis=0), 0.0)


def _shift_up(v, n):
    S = v.shape[0]
    rows = lax.broadcasted_iota(jnp.int32, v.shape, 0)
    return jnp.where(rows < S - n, pltpu.roll(v, S - n, axis=0), 0.0)


def _conv_fwd(proj, conv_w, *, name, cb=LANES, deps=()):
    S, D3 = proj.shape
    D = D3 // 3
    nb = D // cb

    def body(bg_ref, cg_ref, u_ref, w_ref, z_ref):
        uc = cg_ref[...].astype(F32) * u_ref[...].astype(F32)
        w = w_ref[...]
        y = w[2:3, :] * uc + w[1:2, :] * _shift_down(uc, 1) + w[0:1, :] * _shift_down(uc, 2)
        z_ref[...] = (bg_ref[...].astype(F32) * y).astype(BF16)

    part = lambda g: pl.BlockSpec((S, cb), lambda j: (0, g * nb + j))
    return _call(
        body, deps, name=name, grid=(nb,),
        in_specs=[part(0), part(1), part(2), pl.BlockSpec((3, cb), lambda j: (0, j))],
        out_specs=pl.BlockSpec((S, cb), lambda j: (0, j)),
        out_shape=jax.ShapeDtypeStruct((S, D), BF16), compiler_params=_params(("parallel",)),
    )(proj, proj, proj, conv_w)


def _conv_bwd(proj, conv_w, dz, *, name, cb=LANES, deps=()):
    S, D3 = proj.shape
    D = D3 // 3
    nb = D // cb

    def body(bg_ref, cg_ref, u_ref, w_ref, dz_ref, dp_ref, dw_ref):
        cg, u = cg_ref[...].astype(F32), u_ref[...].astype(F32)
        uc = cg * u
        w = w_ref[...]
        uc1, uc2 = _shift_down(uc, 1), _shift_down(uc, 2)
        y = w[2:3, :] * uc + w[1:2, :] * uc1 + w[0:1, :] * uc2
        dz = dz_ref[...].astype(F32)
        dp_ref[0] = (dz * y).astype(BF16)
        dy = dz * bg_ref[...].astype(F32)
        duc = w[2:3, :] * dy + w[1:2, :] * _shift_up(dy, 1) + w[0:1, :] * _shift_up(dy, 2)
        dp_ref[1] = (duc * u).astype(BF16)
        dp_ref[2] = (duc * cg).astype(BF16)
        dw_ref[...] = jnp.zeros_like(dw_ref)
        dw_ref[0:1, :] = jnp.sum(dy * uc2, axis=0, keepdims=True)
        dw_ref[1:2, :] = jnp.sum(dy * uc1, axis=0, keepdims=True)
        dw_ref[2:3, :] = jnp.sum(dy * uc, axis=0, keepdims=True)

    part = lambda g: pl.BlockSpec((S, cb), lambda j: (0, g * nb + j))
    return _call(
        body, deps, name=name, grid=(nb,),
        in_specs=[part(0), part(1), part(2), pl.BlockSpec((3, cb), lambda j: (0, j)), pl.BlockSpec((S, cb), lambda j: (0, j))],
        out_specs=[pl.BlockSpec((3, S, cb), lambda j: (0, 0, j)), pl.BlockSpec((8, cb), lambda j: (0, j))],
        out_shape=[jax.ShapeDtypeStruct((3, S, D), BF16), jax.ShapeDtypeStruct((8, D), F32)],
        compiler_params=_params(("parallel",)),
    )(proj, proj, proj, conv_w, dz)


def _ada_fwd(c_all, ada_w, *, name, deps=()):
    L, D, Ns = ada_w.shape
    tn = _tile(Ns, 512)

    def body(c_ref, w_ref, o_ref, act_ref):
        cv = c_ref[...]
        act = cv * (1.0 / (1.0 + jnp.exp(-cv)))
        act_ref[...] = act
        o_ref[...] = jnp.dot(act.astype(BF16), w_ref[...].astype(BF16), preferred_element_type=F32)

    return _call(
        body, deps, name=name, grid=(L, Ns // tn),
        in_specs=[pl.BlockSpec((N_DEV, D), lambda l, j: (0, 0)), pl.BlockSpec((None, D, tn), lambda l, j: (l, 0, j))],
        out_specs=[pl.BlockSpec((None, N_DEV, tn), lambda l, j: (l, 0, j)), pl.BlockSpec((N_DEV, D), lambda l, j: (0, 0))],
        out_shape=[jax.ShapeDtypeStruct((L, N_DEV, Ns), F32), jax.ShapeDtypeStruct((N_DEV, D), F32)],
        compiler_params=_params(("arbitrary", "arbitrary")),
    )(c_all, ada_w)


def _select_mod(gathered, *, name, deps=()):
    _, LB, Ns = gathered.shape
    L = LB // N_DEV

    def body(g_ref, o_ref):
        x, y, c = _me()
        b = 4 * x + 2 * y + c
        for j in range(N_CHIPS):
            for l in range(L):
                o_ref[j, pl.ds(l, 1), :] = g_ref[2 * j + c, pl.ds(l * N_DEV + b, 1), :]

    return _call(
        body, deps, name=name, out_shape=jax.ShapeDtypeStruct((N_CHIPS, L, Ns), F32),
        in_specs=[pl.BlockSpec(memory_space=pltpu.VMEM)], out_specs=pl.BlockSpec(memory_space=pltpu.VMEM),
        compiler_params=_params(),
    )(gathered)


def _adamw_math(w, g, m, v):
    m = ADAM_B1 * m + (1.0 - ADAM_B1) * g
    v = ADAM_B2 * v + (1.0 - ADAM_B2) * jnp.square(g)
    m_hat = m / (1.0 - ADAM_B1 ** ADAM_STEP)
    v_hat = v / (1.0 - ADAM_B2 ** ADAM_STEP)
    delta = -ADAM_LR * (m_hat / (jnp.sqrt(v_hat) + ADAM_EPS) + ADAM_WD * w)
    return delta, m, v


def _adamw_shards(w, m, v, groups, chip, *, name):
    L, R, C = w.shape
    if R % 16 == 0:
        tr, tc = _tile(R, 128, 16), C
    else:
        tr, tc = R, _tile(C, 256)
    nr, nc = R // tr, C // tc

    def body(chip_ref, w_ref, m_ref, v_ref, *rest):
        srcs, (g_ref, d_ref, mo_ref, vo_ref) = rest[:2 * N_CHIPS * L], rest[2 * N_CHIPS * L:]
        for l in range(L):
            @pl.when(pl.program_id(0) == l)
            def _():
                s = srcs[2 * N_CHIPS * l:2 * N_CHIPS * (l + 1)]
                mine, other = s[0][...].astype(F32), s[N_CHIPS][...].astype(F32)
                for k in range(1, N_CHIPS):
                    mine = mine + s[k][...].astype(F32)
                    other = other + s[N_CHIPS + k][...].astype(F32)
                g = mine + other
                delta, mn, vn = _adamw_math(w_ref[...], g, m_ref[...], v_ref[...])
                g_ref[...] = g
                d_ref[...] = delta
                mo_ref[...] = mn
                vo_ref[...] = vn

    tile = pl.BlockSpec((None, tr, tc), lambda l, i, j, chip_ref: (l, i, j))

    def block(layer, k):
        def index(l, i, j, chip_ref):
            idle_i, idle_j = jnp.where(l < layer, 0, nr - 1), jnp.where(l < layer, 0, nc - 1)
            return (jnp.bitwise_xor(chip_ref[0], k), jnp.where(l == layer, i, idle_i), jnp.where(l == layer, j, idle_j))
        return pl.BlockSpec((None, tr, tc), index)

    in_specs, args = [tile] * 3, [w, m, v]
    for layer, (parts, land, sib) in enumerate(groups):
        in_specs += [block(layer, k) for k in range(N_CHIPS)] * 2
        args += [parts, land, land, land, sib, sib, sib, sib]
    return pl.pallas_call(
        body, name=name,
        grid_spec=pltpu.PrefetchScalarGridSpec(num_scalar_prefetch=1, grid=(L, nr, nc), in_specs=in_specs, out_specs=[tile] * 4),
        out_shape=[jax.ShapeDtypeStruct((L, R, C), F32)] * 4, compiler_params=_params(("arbitrary", "arbitrary", "arbitrary")),
    )(chip, *args)


def _adamw_ada(w, m, v, act_t, dmod, *, name, tr=256, deps=()):
    L, D, Ns = w.shape
    tr = _tile(D, tr, 8)

    def body(w_ref, m_ref, v_ref, a_ref, d_ref, g_ref, dl_ref, mo_ref, vo_ref):
        x, y, _ = _me()
        g = jnp.dot(a_ref[...], d_ref[2 * x + y], preferred_element_type=F32, precision=lax.Precision.HIGHEST)
        delta, mn, vn = _adamw_math(w_ref[...], g, m_ref[...], v_ref[...])
        g_ref[...] = g
        dl_ref[...] = delta
        mo_ref[...] = mn
        vo_ref[...] = vn

    tile = pl.BlockSpec((None, tr, Ns), lambda l, i: (l, i, 0))
    return _call(
        body, deps, name=name, grid=(L, D // tr),
        in_specs=[tile] * 3 + [pl.BlockSpec((tr, N_DEV), lambda l, i: (i, 0)),
                               pl.BlockSpec((N_CHIPS, None, N_DEV, Ns), lambda l, i: (0, l, 0, 0))],
        out_specs=[tile] * 4, out_shape=[jax.ShapeDtypeStruct((L, D, Ns), F32)] * 4,
        compiler_params=_params(("parallel", "parallel")),
    )(w, m, v, act_t, dmod)


def _adamw_small(w, m, v, gathered, *, rows, name, deps=()):
    n, D = w.shape

    def body(w_ref, m_ref, v_ref, s_ref, g_ref, d_ref, mo_ref, vo_ref):
        for r, src in enumerate(rows):
            g = s_ref[0, src:src + 1, :]
            for d in range(1, N_DEV):
                g = g + s_ref[d, src:src + 1, :]
            g_ref[r:r + 1, :] = g
        g = g_ref[...]
        delta, mn, vn = _adamw_math(w_ref[...], g, m_ref[...], v_ref[...])
        d_ref[...] = delta
        mo_ref[...] = mn
        vo_ref[...] = vn

    vm = pl.BlockSpec(memory_space=pltpu.VMEM)
    return _call(
        body, deps, name=name, in_specs=[vm] * 4, out_specs=[vm] * 4,
        out_shape=[jax.ShapeDtypeStruct((n, D), F32)] * 4, compiler_params=_params(),
    )(w, m, v, gathered)


def _adamw_conv_w(w, m, v, gathered4, *, name, deps=()):
    Cs = w.shape[1]

    def body(w_ref, m_ref, v_ref, s_ref, g_ref, d_ref, mo_ref, vo_ref):
        x, y, _ = _me()
        j = 2 * x + y
        g = s_ref[j, 0]
        for d in range(1, N_DEV):
            g = g + s_ref[j, d]
        delta, mn, vn = _adamw_math(w_ref[...], g, m_ref[...], v_ref[...])
        g_ref[...] = g
        d_ref[...] = delta
        mo_ref[...] = mn
        vo_ref[...] = vn

    vm = pl.BlockSpec(memory_space=pltpu.VMEM)
    return _call(
        body, deps, name=name, in_specs=[vm] * 4, out_specs=[vm] * 4,
        out_shape=[jax.ShapeDtypeStruct((8, Cs), F32)] * 4, compiler_params=_params(),
    )(w, m, v, gathered4)


def _loss_sum(gathered, *, row, name, deps=()):
    _, _, D = gathered.shape

    def body(s_ref, o_ref):
        t = s_ref[0, row:row + 1, :]
        for d in range(1, N_DEV):
            t = t + s_ref[d, row:row + 1, :]
        o_ref[...] = jnp.broadcast_to(t, (8, D))

    vm = pl.BlockSpec(memory_space=pltpu.VMEM)
    return pl.pallas_call(body, name=name, in_specs=[vm], out_specs=vm, out_shape=jax.ShapeDtypeStruct((8, D), F32),
                          compiler_params=_params())(gathered)


def _pad_rows(a, n):
    return jnp.pad(a, ((0, n - a.shape[0]), (0, 0)))


def kernel(x, c, ada_w, ada_b, norm_mix, norm_mlp, fox_w_in, fox_b_f, fox_w_out, conv_w_in, conv_w, conv_w_out, mlp_w_up, mlp_w_down, final_norm, loss_target, m_ada_w, m_ada_b, m_norm_mix, m_norm_mlp, m_fox_w_in, m_fox_b_f, m_fox_w_out, m_conv_w_in, m_conv_w, m_conv_w_out, m_mlp_w_up, m_mlp_w_down, m_final_norm, v_ada_w, v_ada_b, v_norm_mix, v_norm_mlp, v_fox_w_in, v_fox_b_f, v_fox_w_out, v_conv_w_in, v_conv_w, v_conv_w_out, v_mlp_w_up, v_mlp_w_down, v_final_norm):
    S, D = x.shape[1], x.shape[2]
    H = fox_b_f.shape[-1]
    L = ada_w.shape[0]
    NM = ada_b.shape[1] // D
    Ns_ada = ada_w.shape[2]
    Cs_fox = fox_w_in.shape[2]
    Cs_conv = conv_w.shape[2]
    x0 = x[0]
    target = loss_target[0]

    chip = (2 * lax.axis_index("x") + lax.axis_index("y")).astype(jnp.int32).reshape(1)

    fin_t = jnp.transpose(fox_w_in, (0, 2, 1))
    shards = dict(fin=(fin_t, 0), fout=(fox_w_out, 0), up0=(mlp_w_up, 0), dn0=(mlp_w_down, 0), cin=(conv_w_in, 0),
                  cout=(conv_w_out, 0), up1=(mlp_w_up, 1), dn1=(mlp_w_down, 1))
    halves = dict(fin="cols", fout="rows", up0="cols", dn0="rows", cin="cols", cout="rows", up1="cols", dn1="rows")
    gathers, placed = {}, {}

    def place(key, dep=None):
        placed[key] = _place_cast(*shards[key], chip, halves=halves[key], name="place_" + key, dep=dep)
        return placed[key]

    def start_gather(key, dep=None):
        gathers[key] = _split_start("gather1", [(placed[key],)], name="gather_start_" + key, dep=dep)
        return gathers[key][3]

    def pass_gather(key, after):
        landed = _split_wait("gather1", gathers[key], after, name="gather_landed_" + key)
        gathers[key] = _split_start("gather2", landed, name="gather_pass_" + key)
        return gathers[key][3]

    def gathered(key, after):
        return _split_wait("gather2", gathers[key], after, name="gather_wait_" + key)[0][0]

    place("fin")
    tok = start_gather("fin")
    for key in ("fout", "up0", "dn0", "cin", "cout", "up1", "dn1"):
        tok = place(key, tok)

    c_all = _allgather8(_pad_rows(c, 8), name="gather_c", deps=(tok,))[:, 0, :]
    mod_part, c_act = _ada_fwd(c_all, ada_w, name="ada_fwd")
    mod_all = _allgather8(mod_part.reshape(L * N_DEV, Ns_ada), name="gather_mod")
    mod = _select_mod(mod_all, name="select_mod")
    mod = jnp.transpose(mod, (1, 0, 2)).reshape(L, NM, 1, D) + ada_b.reshape(L, NM, 1, D)
    conv_w_all = _allgather8(_pad_rows(conv_w[0], 8), name="gather_conv_w")
    conv_w_full = jnp.transpose(conv_w_all[0::2, :3, :], (1, 0, 2)).reshape(3, D)

    def vec(a):
        return a.reshape(1, D)

    h0 = _norm_fwd(x0, vec(norm_mix[0]), mod[0, 1], mod[0, 0], name="norm_mix0", deps=(conv_w_all,))
    tok = h0
    for key in ("fout", "up0", "dn0", "cin", "cout", "up1", "dn1"):
        tok = start_gather(key, tok)
    tok = pass_gather("fin", [h0, tok])
    w_fin_t = jnp.transpose(gathered("fin", [tok]), (0, 2, 1, 3)).reshape(N_CHIPS * Cs_fox, D)
    w_f_t = _pad_rows(w_fin_t[3 * D:], LANES)
    tok = pass_gather("fout", [w_fin_t])
    qkv = _mm_nt(h0, w_fin_t, n=3 * D, name="fox_in", out_dtype=BF16, deps=(tok,))
    tok = pass_gather("up0", [qkv])
    b_f = jnp.pad(fox_b_f, ((0, 0), (0, LANES - H)))
    z_f, F_col = _fgate_fwd(h0, w_f_t, b_f, name="fgate_fwd", deps=(tok,))
    hp = max(8, H)
    at_f, at = _tile(S, 1024, 16), _tile(S, 1024, 16)
    F_rows = _pad_rows(jnp.transpose(F_col[:, :H]), hp)
    F_row = jnp.transpose(F_rows.reshape(hp, S // at, at), (1, 0, 2))
    o, lse = _attn_fwd(qkv, F_col, jnp.transpose(F_rows.reshape(hp, S // at_f, at_f), (1, 0, 2)), heads=H, name="attn_fwd", T=at_f)
    w_fout = gathered("fout", [o]).reshape(D, D)
    tok = pass_gather("dn0", [o])
    x1, mix0, h1 = _mm_nn(o, w_fout, name="fox_out", epilogue="resid_norm", res=x0, gate=mod[0, 2],
                          norm=(vec(norm_mlp[0]), mod[0, 4], mod[0, 3]), tm=512, tn=D, deps=(tok,))
    w_up0 = gathered("up0", [h1]).reshape(2 * N_CHIPS, D, -1)
    tok = pass_gather("cin", [h1])
    u0, a0 = _mm_nn(h1, w_up0, name="mlp_up0", epilogue="relu2", deps=(tok,))
    w_dn0 = gathered("dn0", [a0]).reshape(-1, D)
    tok = pass_gather("cout", [a0])
    x2, y0 = _mm_nn(a0, w_dn0, name="mlp_down0", epilogue="resid", res=x1, gate=mod[0, 5], deps=(tok,))
    h2 = _norm_fwd(x2, vec(norm_mix[1]), mod[1, 1], mod[1, 0], name="norm_mix1")
    g_cin = gathered("cin", [h2]).reshape(2 * N_CHIPS, D, -1)
    tok = pass_gather("up1", [h2])
    proj = _mm_nn(h2, g_cin, name="conv_in", deps=(tok,))
    w_cin = jnp.transpose(g_cin, (1, 0, 2)).reshape(D, 3 * D)
    zc = _conv_fwd(proj, conv_w_full, name="conv_fwd")
    w_cout = gathered("cout", [zc]).reshape(D, D)
    tok = pass_gather("dn1", [zc])
    x3, mix1, h3 = _mm_nn(zc, w_cout, name="conv_out", epilogue="resid_norm", res=x2, gate=mod[1, 2],
                          norm=(vec(norm_mlp[1]), mod[1, 4], mod[1, 3]), tm=512, tn=D, deps=(tok,))
    w_up1 = gathered("up1", [h3]).reshape(2 * N_CHIPS, D, -1)
    u1, a1 = _mm_nn(h3, w_up1, name="mlp_up1", epilogue="relu2")
    w_dn1 = gathered("dn1", [a1]).reshape(-1, D)
    x4, y1 = _mm_nn(a1, w_dn1, name="mlp_down1", epilogue="resid", res=x3, gate=mod[1, 5])
    w_up = [jnp.transpose(w_up0, (1, 0, 2)).reshape(D, -1), jnp.transpose(w_up1, (1, 0, 2)).reshape(D, -1)]
    w_dn = [w_dn0, w_dn1]

    dx4, dy1, sums_f = _loss_bwd(x4, target, vec(final_norm), mod[1, 5], name="loss_bwd")
    du1 = _mm_nt(dy1, w_dn[1], name="mlp_down1_dx", epilogue="drelu2", extra=u1, out_dtype=BF16)
    def start_scatter(tag, parts_list):
        groups = [(p, lax.empty(p.shape, p.dtype)) for p in parts_list]
        return _split_start("scatter", groups, name="scatter_start_" + tag)

    def start_sibling(tag, scatter, after):
        landed = _split_wait("scatter", scatter, after, name="scatter_wait_" + tag)
        groups = [(p, ld, lax.empty(p.shape, p.dtype)) for p, ld in landed]
        return _split_start("sibling", groups, name="sibling_start_" + tag)

    gw_dn1 = _mm_tn(a1, dy1, name="mlp_down1_dw")
    gw_up1 = _mm_tn(h3, du1, name="mlp_up1_dw", out_parts=N_CHIPS)
    sc1 = start_scatter("mlp1", [gw_dn1.reshape(N_CHIPS, -1, D), gw_up1])
    dh3 = _mm_nt(du1, w_up[1], name="mlp_up1_dx", out_dtype=BF16, tk=4096, deps=(sc1[3],))
    dx3, dmix1, sums_mlp1 = _norm_bwd(x3, dh3, dx4, y1, vec(norm_mlp[1]), mod[1, 4], mod[1, 2], name="norm_mlp1_bwd")
    dzc = _mm_nt(dmix1, w_cout, name="conv_out_dx", out_dtype=BF16)
    gw_cout = _mm_tn(zc, dmix1, name="conv_out_dw")
    dproj, dconv_w = _conv_bwd(proj, conv_w_full, dzc, name="conv_bwd")
    gw_cin = _mm_tn(h2, dproj, name="conv_in_dw", out_parts=N_CHIPS, tn=512)
    sc2 = start_scatter("conv", [gw_cout.reshape(N_CHIPS, -1, D), gw_cin])
    dh2 = _mm_nt(dproj, w_cin, name="conv_in_dx", out_dtype=BF16, deps=(sc2[3],))
    dx2, dy0, sums_mix1 = _norm_bwd(x2, dh2, dx3, mix1, vec(norm_mix[1]), mod[1, 1], mod[0, 5], name="norm_mix1_bwd")
    du0 = _mm_nt(dy0, w_dn[0], name="mlp_down0_dx", epilogue="drelu2", extra=u0, out_dtype=BF16)
    gw_dn0 = _mm_tn(a0, dy0, name="mlp_down0_dw")
    gw_up0 = _mm_tn(h1, du0, name="mlp_up0_dw", out_parts=N_CHIPS)
    sc3 = start_scatter("mlp0", [gw_dn0.reshape(N_CHIPS, -1, D), gw_up0])
    sb1 = start_sibling("mlp1", sc1, [sc3[3]])
    dh1 = _mm_nt(du0, w_up[0], name="mlp_up0_dx", out_dtype=BF16, tk=4096, deps=(sb1[3],))
    dx1, dmix0, sums_mlp0 = _norm_bwd(x1, dh1, dx2, y0, vec(norm_mlp[0]), mod[0, 4], mod[0, 2], name="norm_mlp0_bwd")
    do = _mm_nt(dmix0, w_fout, name="fox_out_dx", out_dtype=BF16)
    gw_fout = _mm_tn(o, dmix0, name="fox_out_dw")
    dqkv, dfq, dfk = _attn_bwd(qkv, o, do, F_col, F_row, lse, heads=H, name="attn_bwd", T=at)
    dfk_col = jnp.pad(jnp.transpose(jnp.transpose(dfk, (1, 0, 2)).reshape(hp, S)[:H]), ((0, 0), (0, LANES - H)))
    sb3 = start_sibling("mlp0", sc3, [dqkv])
    dz_f, sums_bf = _fgate_bwd(dfq, dfk_col, z_f, name="fgate_bwd", deps=(sb3[3],))
    gw_qkv_t = _mm_tn(dqkv, h0, name="fox_in_dw")
    gw_f_t = _mm_tn(dz_f, h0, name="fox_gate_dw")
    gw_fin_t = jnp.concatenate([gw_qkv_t, gw_f_t[:H]], axis=0).reshape(N_CHIPS, Cs_fox, D)
    sc4 = start_scatter("fox", [gw_fout.reshape(N_CHIPS, -1, D), gw_fin_t])
    sb2 = start_sibling("conv", sc2, [sc4[3]])
    dh0_f = _mm_nn(dz_f, w_f_t, name="fox_gate_dx", out_dtype=F32, deps=(sb2[3],))
    dh0 = _mm_nn(dqkv, w_fin_t, name="fox_in_dx", epilogue="add", res=dh0_f, out_dtype=BF16)
    grad_x, _, sums_mix0 = _norm_bwd(x0, dh0, dx1, mix0, vec(norm_mix[0]), mod[0, 1], None, name="norm_mix0_bwd")

    outs = {}

    def put(name_, res, shape):
        for kind, r in zip(("grad", "delta", "new_m", "new_v"), res):
            outs[kind + "_" + name_] = r.reshape(shape)

    def shards_update(tag, w_, m_, v_, groups):
        return _adamw_shards(w_, m_, v_, groups, chip, name="adamw_" + tag)

    g_conv = _split_wait("sibling", sb2, [grad_x], name="sibling_wait_conv")
    put("conv_w_out", shards_update("conv_out", conv_w_out, m_conv_w_out, v_conv_w_out, g_conv[0:1]), conv_w_out.shape)
    r_cin = shards_update("conv_in", conv_w_in, m_conv_w_in, v_conv_w_in, g_conv[1:2])
    put("conv_w_in", r_cin, conv_w_in.shape)
    g_mlp1 = _split_wait("sibling", sb1, [r_cin[0]], name="sibling_wait_mlp1")
    g_mlp0 = _split_wait("sibling", sb3, [r_cin[0]], name="sibling_wait_mlp0")
    put("mlp_w_down", shards_update("mlp_down", mlp_w_down, m_mlp_w_down, v_mlp_w_down, [g_mlp0[0], g_mlp1[0]]), mlp_w_down.shape)
    r_up = shards_update("mlp_up", mlp_w_up, m_mlp_w_up, v_mlp_w_up, [g_mlp0[1], g_mlp1[1]])
    put("mlp_w_up", r_up, mlp_w_up.shape)
    sb4 = start_sibling("fox", sc4, [r_up[0]])

    dmod_rows = []
    for sm, sl in ((sums_mix0, sums_mlp0), (sums_mix1, sums_mlp1)):
        dmod_rows += [sm[0:1], sm[1:2], sm[3:4], sl[0:1], sl[1:2], sl[3:4]]
    bf_row = jnp.pad(sums_bf[0:1], ((0, 0), (0, D - LANES)))
    small = jnp.concatenate([sums_mix0[2:3], sums_mix1[2:3], sums_mlp0[2:3], sums_mlp1[2:3], sums_f[0:1], sums_f[1:2], bf_row,
                             jnp.zeros((1, D), F32)] + dmod_rows + [dconv_w[0:3]], axis=0)
    small_all = _allgather8(_pad_rows(small, -(-small.shape[0] // 8) * 8), name="gather_small", deps=(sb4[3],))
    loss = _loss_sum(small_all, row=5, name="loss_sum")[0, 0]

    def rows_of(a_mix, a_mlp, a_fin, a_bf, a_ada):
        return jnp.concatenate([a_mix, a_mlp, a_fin.reshape(1, D), jnp.pad(a_bf, ((0, 0), (0, D - H))),
                                a_ada.reshape(L * NM, D)], axis=0)
    n_small = 2 * L + 2 + L * NM
    rw = -(-n_small // 8) * 8
    w_s = _pad_rows(rows_of(norm_mix, norm_mlp, final_norm, fox_b_f, ada_b), rw)
    m_s = _pad_rows(rows_of(m_norm_mix, m_norm_mlp, m_final_norm, m_fox_b_f, m_ada_b), rw)
    v_s = _pad_rows(rows_of(v_norm_mix, v_norm_mlp, v_final_norm, v_fox_b_f, v_ada_b), rw)
    src_rows = [0, 1, 2, 3, 4, 6] + [8 + r for r in range(L * NM)] + [7] * (rw - n_small)
    res_s = _adamw_small(w_s, m_s, v_s, small_all, rows=tuple(src_rows), name="adamw_small")
    for kind, r in zip(("grad", "delta", "new_m", "new_v"), res_s):
        outs[kind + "_norm_mix"] = r[0:L]
        outs[kind + "_norm_mlp"] = r[L:2 * L]
        outs[kind + "_final_norm"] = r[2 * L]
        outs[kind + "_fox_b_f"] = r[2 * L + 1:2 * L + 2, :H]
        outs[kind + "_ada_b"] = r[2 * L + 2:n_small].reshape(L, NM * D)

    dmod_all = small_all[:, 8:8 + L * NM, :].reshape(N_DEV, L, N_CHIPS, Ns_ada)
    dmod4 = jnp.transpose(dmod_all, (2, 1, 0, 3))
    act_t = jnp.transpose(c_act)
    res_a = _adamw_ada(ada_w, m_ada_w, v_ada_w, act_t, dmod4, name="adamw_ada")
    put("ada_w", res_a, ada_w.shape)

    r0 = 8 + L * NM
    dconv_all = jnp.pad(small_all[:, r0:r0 + 3, :], ((0, 0), (0, 5), (0, 0)))
    dconv4 = jnp.transpose(dconv_all.reshape(N_DEV, 8, N_CHIPS, Cs_conv), (2, 0, 1, 3))
    res_c = _adamw_conv_w(_pad_rows(conv_w[0], 8), _pad_rows(m_conv_w[0], 8), _pad_rows(v_conv_w[0], 8), dconv4,
                          name="adamw_conv_w")
    for kind, r in zip(("grad", "delta", "new_m", "new_v"), res_c):
        outs[kind + "_conv_w"] = r[:3].reshape(conv_w.shape)

    g_fox = _split_wait("sibling", sb4, [res_a[0], res_c[0], res_s[0]], name="sibling_wait_fox")
    put("fox_w_out", shards_update("fox_out", fox_w_out, m_fox_w_out, v_fox_w_out, g_fox[0:1]), fox_w_out.shape)
    t3 = lambda a: jnp.transpose(a, (0, 2, 1))
    for kind, r in zip(("grad", "delta", "new_m", "new_v"),
                       shards_update("fox_in", t3(fox_w_in), t3(m_fox_w_in), t3(v_fox_w_in), g_fox[1:2])):
        outs[kind + "_fox_w_in"] = t3(r)

    names = ["ada_w", "ada_b", "norm_mix", "norm_mlp", "fox_w_in", "fox_b_f", "fox_w_out", "conv_w_in", "conv_w", "conv_w_out",
             "mlp_w_up", "mlp_w_down", "final_norm"]
    return (loss, grad_x[None], *[outs["grad_" + n] for n in names], *[outs["delta_" + n] for n in names],
            *[outs["new_m_" + n] for n in names], *[outs["new_v_" + n] for n in names])
```

```python
import jax
import jax.numpy as jnp
from jax import lax
from jax.experimental import pallas as pl
from jax.experimental.pallas import tpu as pltpu

F32 = jnp.float32
BF16 = jnp.bfloat16
MESH = pl.DeviceIdType.MESH
ANY = pl.BlockSpec(memory_space=pl.ANY)
HBM = pl.BlockSpec(memory_space=pltpu.HBM)
SEM = pl.BlockSpec(memory_space=pltpu.SEMAPHORE)
EFFECT = pltpu.SideEffectType.DATAFLOW_SIDE_EFFECTING

RMS_EPS = 1e-6
ADAM_LR = 0.001
ADAM_B1 = 0.9
ADAM_B2 = 0.999
ADAM_EPS = 1e-08
ADAM_WD = 0.01
ADAM_STEP = 10
N_CHIPS = 4
N_DEV = 8
LANES = 128
VMEM_LIMIT = 56 * 1024 * 1024
NEG = -1e30


def _params(sems=None, vmem=VMEM_LIMIT):
    return pltpu.CompilerParams(dimension_semantics=sems, vmem_limit_bytes=vmem)


def _tile(n, pref, unit=LANES):
    if n <= pref:
        return n
    t = (pref // unit) * unit
    while n % t:
        t -= unit
    return t


def _me():
    return lax.axis_index("x"), lax.axis_index("y"), lax.axis_index("c")


def _call(body, deps, **kw):
    nd = len(deps)

    def wrapped(*refs):
        body(*refs[nd:])

    kw["in_specs"] = [ANY] * nd + list(kw["in_specs"])
    fn = pl.pallas_call(wrapped, **kw)
    return lambda *args: fn(*deps, *args)


def _allgather8(v, *, name, deps=()):
    R, C = v.shape

    def body(v_ref, out_ref, send_sems, recv_sems):
        x, y, c = _me()
        me = 4 * x + 2 * y + c
        out_ref[me] = v_ref[...]
        copies = []
        for k in range(1, N_DEV):
            px, py, pc = (x + (k >> 2)) % 2, (y + ((k >> 1) & 1)) % 2, (c + (k & 1)) % 2
            copies.append(pltpu.make_async_remote_copy(
                src_ref=v_ref, dst_ref=out_ref.at[me], send_sem=send_sems.at[k - 1], recv_sem=recv_sems.at[k - 1],
                device_id=(px, py, pc), device_id_type=MESH))
        for cp in copies:
            cp.start()
        for k in range(1, N_DEV):
            px, py, pc = (x + (k >> 2)) % 2, (y + ((k >> 1) & 1)) % 2, (c + (k & 1)) % 2
            peer = 4 * px + 2 * py + pc
            pltpu.make_async_remote_copy(
                src_ref=v_ref, dst_ref=out_ref.at[peer], send_sem=send_sems.at[k - 1], recv_sem=recv_sems.at[k - 1],
                device_id=(px, py, pc), device_id_type=MESH).wait_recv()
        for cp in copies:
            cp.wait_send()

    return _call(
        body, deps, name=name,
        out_shape=jax.ShapeDtypeStruct((N_DEV, R, C), v.dtype),
        in_specs=[pl.BlockSpec(memory_space=pltpu.VMEM)],
        out_specs=pl.BlockSpec(memory_space=pltpu.VMEM),
        scratch_shapes=[pltpu.SemaphoreType.DMA((N_DEV - 1,)), pltpu.SemaphoreType.DMA((N_DEV - 1,))],
        compiler_params=_params(),
    )(v)


def _chip_peers(x, y):
    return [((x + (k >> 1)) % 2, (y + (k & 1)) % 2) for k in range(1, N_CHIPS)]


def _slot(x, y, k):
    return 2 * ((x + (k >> 1)) % 2) + (y + (k & 1)) % 2


def _split_copies(kind, groups, send_sems, recv_sems):
    x, y, c = _me()
    j = 2 * x + y
    copies = []
    for a, g in enumerate(groups):
        if kind == "sibling":
            parts, land, sib = g
            for k in range(N_CHIPS):
                s = _slot(x, y, k)
                copies.append(pltpu.make_async_remote_copy(
                    src_ref=(parts if k == 0 else land).at[s], dst_ref=sib.at[s], send_sem=send_sems.at[N_CHIPS * a + k],
                    recv_sem=recv_sems.at[N_CHIPS * a + k], device_id=(x, y, 1 - c), device_id_type=MESH))
            continue
        land = g[-1]
        for k, (px, py) in enumerate(_chip_peers(x, y)):
            if kind == "gather1":
                src, dst, to = land.at[j, c], land.at[j, c], (px, py, c)
            elif kind == "gather2":
                src, dst, to = land.at[2 * px + py, c], land.at[2 * px + py, c], (x, y, 1 - c)
            else:
                src, dst, to = g[0].at[2 * px + py], land.at[j], (px, py, c)
            copies.append(pltpu.make_async_remote_copy(
                src_ref=src, dst_ref=dst, send_sem=send_sems.at[3 * a + k], recv_sem=recv_sems.at[3 * a + k],
                device_id=to, device_id_type=MESH))
    return copies


def _split_start(kind, groups, *, name, dep=None):
    flat = [a for g in groups for a in g]
    nf, per = len(flat), len(groups[0])
    ncp = len(groups) * (N_CHIPS if kind == "sibling" else 3)
    nd = 0 if dep is None else 1

    def body(*refs):
        ins = refs[nd:nd + nf]
        send_sems, recv_sems, token = refs[nd + nf], refs[nd + nf + 1], refs[-1]
        for cp in _split_copies(kind, [ins[i:i + per] for i in range(0, nf, per)], send_sems, recv_sems):
            cp.start()
        token[...] = jnp.zeros_like(token)

    outs = pl.pallas_call(
        body, name=name,
        out_shape=(pltpu.SemaphoreType.DMA((ncp,)), pltpu.SemaphoreType.DMA((ncp,)), *[pltpu.HBM(a.shape, a.dtype) for a in flat],
                   jax.ShapeDtypeStruct((8, LANES), F32)),
        in_specs=[ANY] * nd + [HBM] * nf,
        out_specs=(SEM, SEM, *[HBM] * nf, pl.BlockSpec(memory_space=pltpu.VMEM)),
        input_output_aliases={nd + i: 2 + i for i in range(nf)},
        compiler_params=pltpu.CompilerParams(has_side_effects=EFFECT),
    )(*([dep] if nd else []), *[pltpu.with_memory_space_constraint(a, pltpu.HBM) for a in flat])
    thru = list(outs[2:2 + nf])
    return outs[0], outs[1], [tuple(thru[i:i + per]) for i in range(0, nf, per)], outs[-1]


def _split_wait(kind, started, after, *, name):
    send_sems, recv_sems, groups, _ = started
    flat = [a for g in groups for a in g]
    nf, per = len(flat), len(groups[0])

    def body(*refs):
        ins = refs[:nf]
        for cp in _split_copies(kind, [ins[i:i + per] for i in range(0, nf, per)], refs[nf], refs[nf + 1]):
            cp.wait_send()
            cp.wait_recv()

    outs = pl.pallas_call(
        body, name=name,
        out_shape=tuple(pltpu.HBM(a.shape, a.dtype) for a in flat),
        in_specs=[HBM] * nf + [SEM, SEM] + [ANY] * len(after), out_specs=tuple([HBM] * nf),
        input_output_aliases={i: i for i in range(nf)},
        compiler_params=pltpu.CompilerParams(has_side_effects=EFFECT),
    )(*flat, send_sems, recv_sems, *after)
    outs = list(outs)
    return [tuple(outs[i:i + per]) for i in range(0, nf, per)]


def _place_cast(shards, layer, chip, *, halves, name, dep=None):
    _, R, C = shards.shape
    if halves == "rows":
        hr, hc = R // 2, C
    else:
        hr, hc = R, C // 2
    if hr % 16 == 0:
        tr, tc = _tile(hr, 512, 16), hc
    else:
        tr, tc = hr, _tile(hc, 256)
    nr, nc = hr // tr, hc // tc

    def body(chip_ref, x_ref, *rest):
        rest[-1][...] = x_ref[...].astype(BF16)

    if halves == "rows":
        o_map = lambda i, j, chip_ref: (chip_ref[0], i // nr, i % nr, j)
    else:
        o_map = lambda i, j, chip_ref: (chip_ref[0], j // nc, i, j % nc)
    return pl.pallas_call(
        body, name=name,
        grid_spec=pltpu.PrefetchScalarGridSpec(
            num_scalar_prefetch=1, grid=(R // tr, C // tc),
            in_specs=[pl.BlockSpec((None, tr, tc), lambda i, j, chip_ref: (layer, i, j))] + ([] if dep is None else [ANY]),
            out_specs=pl.BlockSpec((None, None, tr, tc), o_map)),
        out_shape=jax.ShapeDtypeStruct((N_CHIPS, 2, hr, hc), BF16), compiler_params=_params(("parallel", "parallel")),
    )(chip, shards, *([] if dep is None else [dep]))


def _accumulate(part, acc_ref, nk, finalize):
    if nk == 1:
        finalize(part)
        return
    k = pl.program_id(2)

    @pl.when(k == 0)
    def _():
        acc_ref[...] = part

    @pl.when(k > 0)
    def _():
        acc_ref[...] += part

    @pl.when(k == nk - 1)
    def _():
        finalize(acc_ref[...])


def _mm_nn(a, b, *, name, epilogue="plain", res=None, gate=None, norm=None, out_dtype=BF16, tm=1024, tn=1024, tk=2048, deps=()):
    if a.ndim == 3:
        Q, M, Kq = a.shape
        K = Q * Kq
    else:
        (M, K), Kq = a.shape, a.shape[1]
    tm, tk = _tile(M, tm, 16), _tile(Kq, tk)
    if a.ndim == 3:
        pa = Kq // tk
        a_spec = pl.BlockSpec((None, tm, tk), lambda i, j, k: (k // pa, i, k % pa))
    else:
        a_spec = pl.BlockSpec((tm, tk), lambda i, j, k: (i, k))
    if b.ndim == 3:
        P, _, Ns = b.shape
        N = P * Ns
        tn = _tile(Ns, tn)
        per = Ns // tn
        b_spec = pl.BlockSpec((None, tk, tn), lambda i, j, k: (j // per, k, j % per))
    else:
        N = b.shape[1]
        tn = _tile(N, tn)
        b_spec = pl.BlockSpec((tk, tn), lambda i, j, k: (k, j))
    nk = K // tk
    tile = pl.BlockSpec((tm, tn), lambda i, j, k: (i, j))

    def body(*refs):
        acc_ref = refs[-1] if nk > 1 else None
        a_ref, b_ref = refs[0], refs[1]
        part = jnp.dot(a_ref[...], b_ref[...], preferred_element_type=F32)
        if epilogue == "plain":
            def fin(acc):
                refs[2][...] = acc.astype(out_dtype)
        elif epilogue == "relu2":
            def fin(acc):
                refs[2][...] = acc.astype(BF16)
                refs[3][...] = jnp.square(jnp.maximum(acc, 0.0)).astype(BF16)
        elif epilogue == "add":
            def fin(acc):
                refs[3][...] = (acc + refs[2][...]).astype(out_dtype)
        elif epilogue == "resid":
            def fin(acc):
                refs[4][...] = refs[2][...] + refs[3][...] * acc
                refs[5][...] = acc.astype(BF16)
        else:
            def fin(acc):
                xv = refs[2][...] + refs[3][...] * acc
                refs[7][...] = xv
                refs[8][...] = acc.astype(BF16)
                r = lax.rsqrt(jnp.mean(xv * xv, axis=-1, keepdims=True) + RMS_EPS)
                refs[9][...] = (((xv * r) * refs[4][...]) * (1.0 + refs[5][...]) + refs[6][...]).astype(BF16)
        _accumulate(part, acc_ref, nk, fin)

    in_specs = [a_spec, b_spec]
    args = [a, b]
    if epilogue == "plain":
        out_shape, out_specs = jax.ShapeDtypeStruct((M, N), out_dtype), tile
    elif epilogue == "relu2":
        out_shape, out_specs = [jax.ShapeDtypeStruct((M, N), BF16)] * 2, [tile, tile]
    elif epilogue == "add":
        in_specs.append(tile)
        args.append(res)
        out_shape, out_specs = jax.ShapeDtypeStruct((M, N), out_dtype), tile
    else:
        row = pl.BlockSpec((1, tn), lambda i, j, k: (0, j))
        in_specs += [tile, row]
        args += [res, gate]
        out_shape, out_specs = [jax.ShapeDtypeStruct((M, N), F32), jax.ShapeDtypeStruct((M, N), BF16)], [tile, tile]
        if epilogue == "resid_norm":
            assert tn == N, "the next norm needs whole rows"
            in_specs += [row, row, row]
            args += list(norm)
            out_shape, out_specs = out_shape + [jax.ShapeDtypeStruct((M, N), BF16)], out_specs + [tile]
    return _call(
        body, deps, name=name, grid=(M // tm, N // tn, nk), in_specs=in_specs, out_specs=out_specs, out_shape=out_shape,
        scratch_shapes=[pltpu.VMEM((tm, tn), F32)] if nk > 1 else [],
        compiler_params=_params(("parallel", "parallel", "arbitrary")),
    )(*args)


def _mm_nt(a, b, *, name, n=None, epilogue="plain", extra=None, out_dtype=F32, tm=1024, tn=1024, tk=2048, deps=()):
    if a.ndim == 3:
        Q, M, Kq = a.shape
        K = Q * Kq
    else:
        (M, K), Kq = a.shape, a.shape[1]
    if b.ndim == 3:
        P, N, Ks = b.shape
    else:
        N, Ks = b.shape
    N = n or N
    tm, tn, tk = _tile(M, tm, 16), _tile(N, tn), _tile(min(Kq, Ks), tk)
    nk = K // tk
    if a.ndim == 3:
        pa = Kq // tk
        a_spec = pl.BlockSpec((None, tm, tk), lambda i, j, k: (k // pa, i, k % pa))
    else:
        a_spec = pl.BlockSpec((tm, tk), lambda i, j, k: (i, k))
    if b.ndim == 3:
        pb = Ks // tk
        b_spec = pl.BlockSpec((None, tn, tk), lambda i, j, k: (k // pb, j, k % pb))
    else:
        b_spec = pl.BlockSpec((tn, tk), lambda i, j, k: (j, k))
    tile = pl.BlockSpec((tm, tn), lambda i, j, k: (i, j))

    def body(*refs):
        acc_ref = refs[-1] if nk > 1 else None
        part = lax.dot_general(refs[0][...], refs[1][...], (((1,), (1,)), ((), ())), preferred_element_type=F32)
        if epilogue == "plain":
            def fin(acc):
                refs[2][...] = acc.astype(out_dtype)
        elif epilogue == "add":
            def fin(acc):
                refs[3][...] = (acc + refs[2][...]).astype(out_dtype)
        else:
            def fin(acc):
                refs[3][...] = (acc * (2.0 * jnp.maximum(refs[2][...].astype(F32), 0.0))).astype(out_dtype)
        _accumulate(part, acc_ref, nk, fin)

    in_specs, args = [a_spec, b_spec], [a, b]
    if epilogue != "plain":
        in_specs.append(tile)
        args.append(extra)
    return _call(
        body, deps, name=name, grid=(M // tm, N // tn, nk), in_specs=in_specs, out_specs=tile,
        out_shape=jax.ShapeDtypeStruct((M, N), out_dtype),
        scratch_shapes=[pltpu.VMEM((tm, tn), F32)] if nk > 1 else [],
        compiler_params=_params(("parallel", "parallel", "arbitrary")),
    )(*args)


def _mm_tn(a, b, *, name, out_parts=1, tm=1024, tn=1024, tk=4096, deps=()):
    if a.ndim == 3:
        Qa, M, Kq = a.shape
        Kd = Qa * Kq
    else:
        (M, Kd), Kq = a.shape, a.shape[1]
    if b.ndim == 3:
        Q, _, Nq = b.shape
        N = Q * Nq
    else:
        N, Nq = b.shape[1], b.shape[1]
    Ns = N // out_parts
    tn = _tile(Ns, tn)
    while Nq % tn or Ns % tn:
        tn -= LANES
    tm, tk = _tile(Kq, tm), _tile(M, tk, 16)
    nk = M // tk
    if a.ndim == 3:
        pa = Kq // tm
        a_spec = pl.BlockSpec((None, tk, tm), lambda i, j, k: (i // pa, k, i % pa))
    else:
        a_spec = pl.BlockSpec((tk, tm), lambda i, j, k: (k, i))
    if b.ndim == 3:
        pb = Nq // tn
        b_spec = pl.BlockSpec((None, tk, tn), lambda i, j, k: (j // pb, k, j % pb))
    else:
        b_spec = pl.BlockSpec((tk, tn), lambda i, j, k: (k, j))
    if out_parts > 1:
        po = Ns // tn
        o_spec = pl.BlockSpec((None, tm, tn), lambda i, j, k: (j // po, i, j % po))
        out_shape = jax.ShapeDtypeStruct((out_parts, Kd, Ns), BF16)
    else:
        o_spec = pl.BlockSpec((tm, tn), lambda i, j, k: (i, j))
        out_shape = jax.ShapeDtypeStruct((Kd, N), BF16)

    def body(*refs):
        acc_ref = refs[-1] if nk > 1 else None
        part = lax.dot_general(refs[0][...], refs[1][...], (((0,), (0,)), ((), ())), preferred_element_type=F32)

        def fin(acc):
            refs[2][...] = acc.astype(BF16)
        _accumulate(part, acc_ref, nk, fin)

    return _call(
        body, deps, name=name, grid=(Kd // tm, N // tn, nk),
        in_specs=[a_spec, b_spec], out_specs=o_spec, out_shape=out_shape,
        scratch_shapes=[pltpu.VMEM((tm, tn), F32)] if nk > 1 else [],
        compiler_params=_params(("parallel", "parallel", "arbitrary")),
    )(a, b)


def _rows(S, D, i_map=lambda i: (i, 0), ts=512):
    return pl.BlockSpec((ts, D), i_map)


def _norm_fwd(x, gain, sc, sh, *, name, deps=()):
    S, D = x.shape
    ts = _tile(S, 512, 16)
    vec = pl.BlockSpec((1, D), lambda i: (0, 0))

    def body(x_ref, g_ref, sc_ref, sh_ref, h_ref):
        xv = x_ref[...]
        r = lax.rsqrt(jnp.mean(xv * xv, axis=-1, keepdims=True) + RMS_EPS)
        h = (xv * r) * g_ref[...]
        h_ref[...] = (h * (1.0 + sc_ref[...]) + sh_ref[...]).astype(BF16)

    return _call(
        body, deps, name=name, grid=(S // ts,), in_specs=[_rows(S, D, ts=ts), vec, vec, vec], out_specs=_rows(S, D, ts=ts),
        out_shape=jax.ShapeDtypeStruct((S, D), BF16), compiler_params=_params(("parallel",)),
    )(x, gain, sc, sh)


def _loss_bwd(x, target, gain, gate_prev, *, name, deps=()):
    S, D = x.shape
    ts = _tile(S, 256, 16)
    vec = pl.BlockSpec((1, D), lambda i: (0, 0))

    def body(x_ref, t_ref, g_ref, gp_ref, dx_ref, dp_ref, sums_ref):
        @pl.when(pl.program_id(0) == 0)
        def _():
            sums_ref[...] = jnp.zeros_like(sums_ref)
        xv = x_ref[...]
        r = lax.rsqrt(jnp.mean(xv * xv, axis=-1, keepdims=True) + RMS_EPS)
        xn = xv * r
        err = xn * g_ref[...] - t_ref[...]
        loss = 0.5 * jnp.sum(jnp.mean(err * err, axis=-1, keepdims=True), axis=0, keepdims=True)
        dy = err * (1.0 / D)
        dxn = dy * g_ref[...]
        dx = r * (dxn - xn * jnp.mean(dxn * xn, axis=-1, keepdims=True))
        dx_ref[...] = dx
        dp_ref[...] = (gp_ref[...] * dx).astype(BF16)
        sums_ref[0:1, :] += jnp.sum(dy * xn, axis=0, keepdims=True)
        sums_ref[1:2, :] += jnp.broadcast_to(loss, (1, D))

    return _call(
        body, deps, name=name, grid=(S // ts,),
        in_specs=[_rows(S, D, ts=ts), _rows(S, D, ts=ts), vec, vec],
        out_specs=[_rows(S, D, ts=ts), _rows(S, D, ts=ts), pl.BlockSpec((8, D), lambda i: (0, 0))],
        out_shape=[jax.ShapeDtypeStruct((S, D), F32), jax.ShapeDtypeStruct((S, D), BF16), jax.ShapeDtypeStruct((8, D), F32)],
        compiler_params=_params(("arbitrary",)),
    )(x, target, gain, gate_prev)


def _norm_bwd(x, dh, dxp, mix, gain, sc, gate_prev, *, name, deps=()):
    S, D = x.shape
    ts = _tile(S, 256, 16)
    vec = pl.BlockSpec((1, D), lambda i: (0, 0))
    with_prev = gate_prev is not None

    def body(*refs):
        x_ref, dh_ref, dxp_ref, mix_ref, g_ref, sc_ref = refs[:6]
        outs = refs[7:] if with_prev else refs[6:]
        sums_ref = outs[-1]

        @pl.when(pl.program_id(0) == 0)
        def _():
            sums_ref[...] = jnp.zeros_like(sums_ref)
        xv, dhv, dxpv = x_ref[...], dh_ref[...].astype(F32), dxp_ref[...]
        r = lax.rsqrt(jnp.mean(xv * xv, axis=-1, keepdims=True) + RMS_EPS)
        xn = xv * r
        hn = xn * g_ref[...]
        dhn = dhv * (1.0 + sc_ref[...])
        dxn = dhn * g_ref[...]
        dx = dxpv + r * (dxn - xn * jnp.mean(dxn * xn, axis=-1, keepdims=True))
        outs[0][...] = dx
        if with_prev:
            outs[1][...] = (refs[6][...] * dx).astype(BF16)
        sums_ref[0:1, :] += jnp.sum(dhv, axis=0, keepdims=True)
        sums_ref[1:2, :] += jnp.sum(dhv * hn, axis=0, keepdims=True)
        sums_ref[2:3, :] += jnp.sum(dhn * xn, axis=0, keepdims=True)
        sums_ref[3:4, :] += jnp.sum(dxpv * mix_ref[...].astype(F32), axis=0, keepdims=True)

    tile = _rows(S, D, ts=ts)
    in_specs = [tile, tile, tile, tile, vec, vec] + ([vec] if with_prev else [])
    args = [x, dh, dxp, mix, gain, sc] + ([gate_prev] if with_prev else [])
    out_specs = [tile] + ([tile] if with_prev else []) + [pl.BlockSpec((8, D), lambda i: (0, 0))]
    out_shape = ([jax.ShapeDtypeStruct((S, D), F32)] + ([jax.ShapeDtypeStruct((S, D), BF16)] if with_prev else [])
                 + [jax.ShapeDtypeStruct((8, D), F32)])
    outs = _call(
        body, deps, name=name, grid=(S // ts,), in_specs=in_specs, out_specs=out_specs, out_shape=out_shape,
        compiler_params=_params(("arbitrary",)),
    )(*args)
    return (outs[0], outs[1], outs[2]) if with_prev else (outs[0], None, outs[1])


def _fgate_fwd(h, wf, bf, *, name, deps=()):
    S, D = h.shape
    ts = _tile(S, 256, 16)

    def body(h_ref, w_ref, b_ref, z_ref, f_ref, carry):
        @pl.when(pl.program_id(0) == 0)
        def _():
            carry[...] = jnp.zeros_like(carry)
        z = lax.dot_general(h_ref[...], w_ref[...], (((1,), (1,)), ((), ())), preferred_element_type=F32) + b_ref[...]
        logf = jnp.minimum(z, 0.0) - jnp.log(1.0 + jnp.exp(-jnp.abs(z)))
        row = lax.broadcasted_iota(jnp.int32, (ts, ts), 0)
        col = lax.broadcasted_iota(jnp.int32, (ts, ts), 1)
        tril = (col <= row).astype(F32)
        run = jnp.dot(tril, logf, preferred_element_type=F32, precision=lax.Precision.HIGHEST) + carry[0:1, :]
        z_ref[...] = z
        f_ref[...] = run
        carry[0:1, :] = run[ts - 1:ts, :]

    return _call(
        body, deps, name=name, grid=(S // ts,),
        in_specs=[pl.BlockSpec((ts, D), lambda i: (i, 0)), pl.BlockSpec((LANES, D), lambda i: (0, 0)),
                  pl.BlockSpec((1, LANES), lambda i: (0, 0))],
        out_specs=[pl.BlockSpec((ts, LANES), lambda i: (i, 0))] * 2,
        out_shape=[jax.ShapeDtypeStruct((S, LANES), F32)] * 2,
        scratch_shapes=[pltpu.VMEM((8, LANES), F32)],
        compiler_params=_params(("arbitrary",)),
    )(h, wf, bf)


def _fgate_bwd(dfq, dfk, z, *, name, deps=()):
    S = z.shape[0]
    ts = _tile(S, 256, 16)
    n = S // ts

    def body(dq_ref, dk_ref, z_ref, dz_ref, sums_ref, carry):
        @pl.when(pl.program_id(0) == 0)
        def _():
            carry[...] = jnp.zeros_like(carry)
            sums_ref[...] = jnp.zeros_like(sums_ref)
        df = dq_ref[...] - dk_ref[...]
        row = lax.broadcasted_iota(jnp.int32, (ts, ts), 0)
        col = lax.broadcasted_iota(jnp.int32, (ts, ts), 1)
        triu = (col >= row).astype(F32)
        run = jnp.dot(triu, df, preferred_element_type=F32, precision=lax.Precision.HIGHEST) + carry[0:1, :]
        zv = z_ref[...]
        dz = run * (1.0 / (1.0 + jnp.exp(zv)))
        dz_ref[...] = dz.astype(BF16)
        sums_ref[0:1, :] += jnp.sum(dz, axis=0, keepdims=True)
        carry[0:1, :] = run[0:1, :]

    rev = pl.BlockSpec((ts, LANES), lambda i: (n - 1 - i, 0))
    return _call(
        body, deps, name=name, grid=(n,), in_specs=[rev, rev, rev],
        out_specs=[rev, pl.BlockSpec((8, LANES), lambda i: (0, 0))],
        out_shape=[jax.ShapeDtypeStruct((S, LANES), BF16), jax.ShapeDtypeStruct((8, LANES), F32)],
        scratch_shapes=[pltpu.VMEM((8, LANES), F32)],
        compiler_params=_params(("arbitrary",)),
    )(dfq, dfk, z)


def _head_col(ref, rows, lane_mask):
    return jnp.sum(jnp.where(lane_mask, ref[rows, :], 0.0), axis=1, keepdims=True)


def _attn_fwd(qkv, fk, *, heads, name, T=256, deps=()):
    S, D3 = qkv.shape
    D = D3 // 3
    dh = D // heads
    T = _tile(S, T, 16)
    nq = S // T
    scale = dh ** -0.5
    hp = fk.shape[1]

    def body(q_ref, k_ref, v_ref, fk_ref, o_ref, lse_ref):
        h = pl.program_id(0)

        @pl.when(h == 0)
        def _():
            lse_ref[...] = jnp.zeros_like(lse_ref)
        lane = lax.broadcasted_iota(jnp.int32, (1, LANES), 1) == h
        row = lax.broadcasted_iota(jnp.int32, (T, T), 0)
        col = lax.broadcasted_iota(jnp.int32, (T, T), 1)

        def q_block(qi, _):
            rows = pl.ds(pl.multiple_of(qi * T, T), T)
            q = q_ref[rows, :]

            def kv_block(kj, carry, diag):
                m, l, acc = carry
                cols = pl.ds(pl.multiple_of(kj * T, T), T)
                s = lax.dot_general(q, k_ref[cols, :], (((1,), (1,)), ((), ())), preferred_element_type=F32) * scale
                s = s - fk_ref[kj, pl.ds(h, 1), :]
                if diag:
                    s = jnp.where(col <= row, s, NEG)
                m_new = jnp.maximum(m, jnp.max(s, axis=1, keepdims=True))
                p = jnp.exp(s - m_new)
                alpha = jnp.exp(m - m_new)
                l = alpha * l + jnp.sum(p, axis=1, keepdims=True)
                acc = alpha * acc + jnp.dot(p.astype(BF16), v_ref[cols, :], preferred_element_type=F32)
                return m_new, l, acc

            init = (jnp.full((T, 1), NEG, F32), jnp.zeros((T, 1), F32), jnp.zeros((T, dh), F32))
            carry = lax.fori_loop(0, qi, lambda kj, cr: kv_block(kj, cr, False), init)
            m, l, acc = kv_block(qi, carry, True)
            o_ref[rows, :] = (acc / l).astype(BF16)
            lse_ref[rows, :] = jnp.where(lane, m + jnp.log(l), lse_ref[rows, :])
            return 0

        lax.fori_loop(0, nq, q_block, 0)

    head = lambda part: pl.BlockSpec((S, dh), lambda h: (0, part * heads + h))
    return _call(
        body, deps, name=name, grid=(heads,),
        in_specs=[head(0), head(1), head(2), pl.BlockSpec((nq, hp, T), lambda h: (0, 0, 0))],
        out_specs=[pl.BlockSpec((S, dh), lambda h: (0, h)), pl.BlockSpec((S, LANES), lambda h: (0, 0))],
        out_shape=[jax.ShapeDtypeStruct((S, D), BF16), jax.ShapeDtypeStruct((S, LANES), F32)],
        compiler_params=_params(("arbitrary",)),
    )(qkv, qkv, qkv, fk)


def _attn_bwd(qkv, o, do, fk, lse, *, heads, name, T=256, deps=()):
    S, D3 = qkv.shape
    D = D3 // 3
    dh = D // heads
    T = _tile(S, T, 16)
    nq = S // T
    scale = dh ** -0.5
    hp = fk.shape[1]

    def body(q_ref, k_ref, v_ref, o_ref, do_ref, fk_ref, lse_ref, dqkv_ref, dfq_ref, dfk_ref,
             dq_acc, lse_col, delta_col, dfq_col):
        h = pl.program_id(0)

        @pl.when(h == 0)
        def _():
            dfq_ref[...] = jnp.zeros_like(dfq_ref)
            dfk_ref[...] = jnp.zeros_like(dfk_ref)
        lane = lax.broadcasted_iota(jnp.int32, (1, LANES), 1) == h
        Th = T // 2 if T % 32 == 0 else T
        dq_acc[...] = jnp.zeros_like(dq_acc)
        dfq_col[...] = jnp.zeros_like(dfq_col)

        def prep(qi, _):
            rows = pl.ds(pl.multiple_of(qi * T, T), T)
            lse_col[rows, :] = _head_col(lse_ref, rows, lane)
            delta_col[rows, :] = jnp.sum(do_ref[rows, :].astype(F32) * o_ref[rows, :].astype(F32), axis=1, keepdims=True)
            return 0

        lax.fori_loop(0, nq, prep, 0)

        def kv_block(kj, _):
            cols = pl.ds(pl.multiple_of(kj * T, T), T)
            k, v = k_ref[cols, :], v_ref[cols, :]
            fk_row = fk_ref[kj, pl.ds(h, 1), :]

            def pair_grad(rows, kk, vv, fk_r, masked):
                q, dov = q_ref[rows, :], do_ref[rows, :]
                s = lax.dot_general(q, kk, (((1,), (1,)), ((), ())), preferred_element_type=F32) * scale
                s = s - fk_r
                p = jnp.exp(s - lse_col[rows, :])
                if masked:
                    p = jnp.where(lax.broadcasted_iota(jnp.int32, p.shape, 1) <= lax.broadcasted_iota(jnp.int32, p.shape, 0), p, 0.0)
                dp = lax.dot_general(dov, vv, (((1,), (1,)), ((), ())), preferred_element_type=F32)
                ds = p * (dp - delta_col[rows, :])
                dsb = ds.astype(BF16)
                dv = lax.dot_general(p.astype(BF16), dov, (((0,), (0,)), ((), ())), preferred_element_type=F32)
                dk = lax.dot_general(dsb, q, (((0,), (0,)), ((), ())), preferred_element_type=F32)
                dq_acc[rows, :] += jnp.dot(dsb, kk, preferred_element_type=F32)
                dfq_col[rows, :] += jnp.sum(ds, axis=1, keepdims=True)
                return dk, dv, jnp.sum(ds, axis=0, keepdims=True)

            def q_block(qi, carry):
                rows = pl.ds(pl.multiple_of(qi * T, T), T)
                return tuple(c + g for c, g in zip(carry, pair_grad(rows, k, v, fk_row, False)))

            strips = [pair_grad(pl.ds(pl.multiple_of(kj * T + i * Th, Th), T - i * Th), k[i * Th:(i + 1) * Th],
                                v[i * Th:(i + 1) * Th], fk_row[:, i * Th:(i + 1) * Th], True) for i in range(T // Th)]
            carry = (jnp.concatenate([g[0] for g in strips], axis=0), jnp.concatenate([g[1] for g in strips], axis=0),
                     jnp.concatenate([g[2] for g in strips], axis=1))
            dk, dv, dfk = lax.fori_loop(kj + 1, nq, q_block, carry)
            dqkv_ref[1, cols, :] = (dk * scale).astype(BF16)
            dqkv_ref[2, cols, :] = dv.astype(BF16)
            dfk_ref[kj, pl.ds(h, 1), :] = dfk
            return 0

        lax.fori_loop(0, nq, kv_block, 0)

        def finish(qi, _):
            rows = pl.ds(pl.multiple_of(qi * T, T), T)
            dqkv_ref[0, rows, :] = (dq_acc[rows, :] * scale).astype(BF16)
            dfq_ref[rows, :] = jnp.where(lane, dfq_col[rows, :], dfq_ref[rows, :])
            return 0

        lax.fori_loop(0, nq, finish, 0)

    head = lambda part: pl.BlockSpec((S, dh), lambda h: (0, part * heads + h))
    own = pl.BlockSpec((S, dh), lambda h: (0, h))
    full = pl.BlockSpec((S, LANES), lambda h: (0, 0))
    krow = pl.BlockSpec((nq, hp, T), lambda h: (0, 0, 0))
    return _call(
        body, deps, name=name, grid=(heads,),
        in_specs=[head(0), head(1), head(2), own, own, krow, full],
        out_specs=[pl.BlockSpec((3, S, dh), lambda h: (0, 0, h)), full, krow],
        out_shape=[jax.ShapeDtypeStruct((3, S, D), BF16), jax.ShapeDtypeStruct((S, LANES), F32),
                   jax.ShapeDtypeStruct((nq, hp, T), F32)],
        scratch_shapes=[pltpu.VMEM((S, dh), F32)] + [pltpu.VMEM((S, 1), F32)] * 3,
        compiler_params=_params(("arbitrary",)),
    )(qkv, qkv, qkv, o, do, fk, lse)


def _shift_down(v, n):
    rows = lax.broadcasted_iota(jnp.int32, v.shape, 0)
    return jnp.where(rows >= n, pltpu.roll(v, n, axis=0), 0.0)


def _shift_up(v, n):
    S = v.shape[0]
    rows = lax.broadcasted_iota(jnp.int32, v.shape, 0)
    return jnp.where(rows < S - n, pltpu.roll(v, S - n, axis=0), 0.0)


def _conv_fwd(proj, conv_w, *, name, cb=LANES, deps=()):
    S, D3 = proj.shape
    D = D3 // 3
    nb = D // cb

    def body(bg_ref, cg_ref, u_ref, w_ref, z_ref):
        uc = cg_ref[...].astype(F32) * u_ref[...].astype(F32)
        w = w_ref[...]
        y = w[2:3, :] * uc + w[1:2, :] * _shift_down(uc, 1) + w[0:1, :] * _shift_down(uc, 2)
        z_ref[...] = (bg_ref[...].astype(F32) * y).astype(BF16)

    part = lambda g: pl.BlockSpec((S, cb), lambda j: (0, g * nb + j))
    return _call(
        body, deps, name=name, grid=(nb,),
        in_specs=[part(0), part(1), part(2), pl.BlockSpec((3, cb), lambda j: (0, j))],
        out_specs=pl.BlockSpec((S, cb), lambda j: (0, j)),
        out_shape=jax.ShapeDtypeStruct((S, D), BF16), compiler_params=_params(("parallel",)),
    )(proj, proj, proj, conv_w)


def _conv_bwd(proj, conv_w, dz, *, name, cb=LANES, deps=()):
    S, D3 = proj.shape
    D = D3 // 3
    nb = D // cb

    def body(bg_ref, cg_ref, u_ref, w_ref, dz_ref, dp_ref, dw_ref):
        cg, u = cg_ref[...].astype(F32), u_ref[...].astype(F32)
        uc = cg * u
        w = w_ref[...]
        uc1, uc2 = _shift_down(uc, 1), _shift_down(uc, 2)
        y = w[2:3, :] * uc + w[1:2, :] * uc1 + w[0:1, :] * uc2
        dz = dz_ref[...].astype(F32)
        dp_ref[0] = (dz * y).astype(BF16)
        dy = dz * bg_ref[...].astype(F32)
        duc = w[2:3, :] * dy + w[1:2, :] * _shift_up(dy, 1) + w[0:1, :] * _shift_up(dy, 2)
        dp_ref[1] = (duc * u).astype(BF16)
        dp_ref[2] = (duc * cg).astype(BF16)
        dw_ref[...] = jnp.zeros_like(dw_ref)
        dw_ref[0:1, :] = jnp.sum(dy * uc2, axis=0, keepdims=True)
        dw_ref[1:2, :] = jnp.sum(dy * uc1, axis=0, keepdims=True)
        dw_ref[2:3, :] = jnp.sum(dy * uc, axis=0, keepdims=True)

    part = lambda g: pl.BlockSpec((S, cb), lambda j: (0, g * nb + j))
    return _call(
        body, deps, name=name, grid=(nb,),
        in_specs=[part(0), part(1), part(2), pl.BlockSpec((3, cb), lambda j: (0, j)), pl.BlockSpec((S, cb), lambda j: (0, j))],
        out_specs=[pl.BlockSpec((3, S, cb), lambda j: (0, 0, j)), pl.BlockSpec((8, cb), lambda j: (0, j))],
        out_shape=[jax.ShapeDtypeStruct((3, S, D), BF16), jax.ShapeDtypeStruct((8, D), F32)],
        compiler_params=_params(("parallel",)),
    )(proj, proj, proj, conv_w, dz)


def _ada_fwd(c_all, ada_w, *, name, deps=()):
    L, D, Ns = ada_w.shape
    tn = _tile(Ns, 512)

    def body(c_ref, w_ref, o_ref, act_ref):
        cv = c_ref[...]
        act = cv * (1.0 / (1.0 + jnp.exp(-cv)))
        act_ref[...] = act
        o_ref[...] = jnp.dot(act.astype(BF16), w_ref[...].astype(BF16), preferred_element_type=F32)

    return _call(
        body, deps, name=name, grid=(L, Ns // tn),
        in_specs=[pl.BlockSpec((N_DEV, D), lambda l, j: (0, 0)), pl.BlockSpec((None, D, tn), lambda l, j: (l, 0, j))],
        out_specs=[pl.BlockSpec((None, N_DEV, tn), lambda l, j: (l, 0, j)), pl.BlockSpec((N_DEV, D), lambda l, j: (0, 0))],
        out_shape=[jax.ShapeDtypeStruct((L, N_DEV, Ns), F32), jax.ShapeDtypeStruct((N_DEV, D), F32)],
        compiler_params=_params(("arbitrary", "arbitrary")),
    )(c_all, ada_w)


def _select_mod(gathered, *, name, deps=()):
    _, LB, Ns = gathered.shape
    L = LB // N_DEV

    def body(g_ref, o_ref):
        x, y, c = _me()
        b = 4 * x + 2 * y + c
        for j in range(N_CHIPS):
            for l in range(L):
                o_ref[j, pl.ds(l, 1), :] = g_ref[2 * j + c, pl.ds(l * N_DEV + b, 1), :]

    return _call(
        body, deps, name=name, out_shape=jax.ShapeDtypeStruct((N_CHIPS, L, Ns), F32),
        in_specs=[pl.BlockSpec(memory_space=pltpu.VMEM)], out_specs=pl.BlockSpec(memory_space=pltpu.VMEM),
        compiler_params=_params(),
    )(gathered)


def _adamw_math(w, g, m, v):
    m = ADAM_B1 * m + (1.0 - ADAM_B1) * g
    v = ADAM_B2 * v + (1.0 - ADAM_B2) * jnp.square(g)
    m_hat = m / (1.0 - ADAM_B1 ** ADAM_STEP)
    v_hat = v / (1.0 - ADAM_B2 ** ADAM_STEP)
    delta = -ADAM_LR * (m_hat / (jnp.sqrt(v_hat) + ADAM_EPS) + ADAM_WD * w)
    return delta, m, v


def _adamw_shards(w, m, v, groups, chip, *, name):
    L, R, C = w.shape
    if R % 16 == 0:
        tr, tc = _tile(R, 128, 16), C
    else:
        tr, tc = R, _tile(C, 256)
    nr, nc = R // tr, C // tc

    def body(chip_ref, w_ref, m_ref, v_ref, *rest):
        srcs, (g_ref, d_ref, mo_ref, vo_ref) = rest[:2 * N_CHIPS * L], rest[2 * N_CHIPS * L:]
        for l in range(L):
            @pl.when(pl.program_id(0) == l)
            def _():
                s = srcs[2 * N_CHIPS * l:2 * N_CHIPS * (l + 1)]
                mine, other = s[0][...].astype(F32), s[N_CHIPS][...].astype(F32)
                for k in range(1, N_CHIPS):
                    mine = mine + s[k][...].astype(F32)
                    other = other + s[N_CHIPS + k][...].astype(F32)
                g = mine + other
                delta, mn, vn = _adamw_math(w_ref[...], g, m_ref[...], v_ref[...])
                g_ref[...] = g
                d_ref[...] = delta
                mo_ref[...] = mn
                vo_ref[...] = vn

    tile = pl.BlockSpec((None, tr, tc), lambda l, i, j, chip_ref: (l, i, j))

    def block(layer, k):
        def index(l, i, j, chip_ref):
            idle_i, idle_j = jnp.where(l < layer, 0, nr - 1), jnp.where(l < layer, 0, nc - 1)
            return (jnp.bitwise_xor(chip_ref[0], k), jnp.where(l == layer, i, idle_i), jnp.where(l == layer, j, idle_j))
        return pl.BlockSpec((None, tr, tc), index)

    in_specs, args = [tile] * 3, [w, m, v]
    for layer, (parts, land, sib) in enumerate(groups):
        in_specs += [block(layer, k) for k in range(N_CHIPS)] * 2
        args += [parts, land, land, land, sib, sib, sib, sib]
    return pl.pallas_call(
        body, name=name,
        grid_spec=pltpu.PrefetchScalarGridSpec(num_scalar_prefetch=1, grid=(L, nr, nc), in_specs=in_specs, out_specs=[tile] * 4),
        out_shape=[jax.ShapeDtypeStruct((L, R, C), F32)] * 4, compiler_params=_params(("arbitrary", "arbitrary", "arbitrary")),
    )(chip, *args)


def _adamw_ada(w, m, v, act_t, dmod, *, name, tr=256, deps=()):
    L, D, Ns = w.shape
    tr = _tile(D, tr, 8)

    def body(w_ref, m_ref, v_ref, a_ref, d_ref, g_ref, dl_ref, mo_ref, vo_ref):
        x, y, _ = _me()
        g = jnp.dot(a_ref[...], d_ref[2 * x + y], preferred_element_type=F32, precision=lax.Precision.HIGHEST)
        delta, mn, vn = _adamw_math(w_ref[...], g, m_ref[...], v_ref[...])
        g_ref[...] = g
        dl_ref[...] = delta
        mo_ref[...] = mn
        vo_ref[...] = vn

    tile = pl.BlockSpec((None, tr, Ns), lambda l, i: (l, i, 0))
    return _call(
        body, deps, name=name, grid=(L, D // tr),
        in_specs=[tile] * 3 + [pl.BlockSpec((tr, N_DEV), lambda l, i: (i, 0)),
                               pl.BlockSpec((N_CHIPS, None, N_DEV, Ns), lambda l, i: (0, l, 0, 0))],
        out_specs=[tile] * 4, out_shape=[jax.ShapeDtypeStruct((L, D, Ns), F32)] * 4,
        compiler_params=_params(("parallel", "parallel")),
    )(w, m, v, act_t, dmod)


def _adamw_small(w, m, v, gathered, *, rows, name, deps=()):
    n, D = w.shape

    def body(w_ref, m_ref, v_ref, s_ref, g_ref, d_ref, mo_ref, vo_ref):
        for r, src in enumerate(rows):
            g = s_ref[0, src:src + 1, :]
            for d in range(1, N_DEV):
                g = g + s_ref[d, src:src + 1, :]
            g_ref[r:r + 1, :] = g
        g = g_ref[...]
        delta, mn, vn = _adamw_math(w_ref[...], g, m_ref[...], v_ref[...])
        d_ref[...] = delta
        mo_ref[...] = mn
        vo_ref[...] = vn

    vm = pl.BlockSpec(memory_space=pltpu.VMEM)
    return _call(
        body, deps, name=name, in_specs=[vm] * 4, out_specs=[vm] * 4,
        out_shape=[jax.ShapeDtypeStruct((n, D), F32)] * 4, compiler_params=_params(),
    )(w, m, v, gathered)


def _adamw_conv_w(w, m, v, gathered4, *, name, deps=()):
    Cs = w.shape[1]

    def body(w_ref, m_ref, v_ref, s_ref, g_ref, d_ref, mo_ref, vo_ref):
        x, y, _ = _me()
        j = 2 * x + y
        g = s_ref[j, 0]
        for d in range(1, N_DEV):
            g = g + s_ref[j, d]
        delta, mn, vn = _adamw_math(w_ref[...], g, m_ref[...], v_ref[...])
        g_ref[...] = g
        d_ref[...] = delta
        mo_ref[...] = mn
        vo_ref[...] = vn

    vm = pl.BlockSpec(memory_space=pltpu.VMEM)
    return _call(
        body, deps, name=name, in_specs=[vm] * 4, out_specs=[vm] * 4,
        out_shape=[jax.ShapeDtypeStruct((8, Cs), F32)] * 4, compiler_params=_params(),
    )(w, m, v, gathered4)


def _loss_sum(gathered, *, row, name, deps=()):
    _, _, D = gathered.shape

    def body(s_ref, o_ref):
        t = s_ref[0, row:row + 1, :]
        for d in range(1, N_DEV):
            t = t + s_ref[d, row:row + 1, :]
        o_ref[...] = jnp.broadcast_to(t, (8, D))

    vm = pl.BlockSpec(memory_space=pltpu.VMEM)
    return pl.pallas_call(body, name=name, in_specs=[vm], out_specs=vm, out_shape=jax.ShapeDtypeStruct((8, D), F32),
                          compiler_params=_params())(gathered)


def _pad_rows(a, n):
    return jnp.pad(a, ((0, n - a.shape[0]), (0, 0)))


def kernel(x, c, ada_w, ada_b, norm_mix, norm_mlp, fox_w_in, fox_b_f, fox_w_out, conv_w_in, conv_w, conv_w_out, mlp_w_up, mlp_w_down, final_norm, loss_target, m_ada_w, m_ada_b, m_norm_mix, m_norm_mlp, m_fox_w_in, m_fox_b_f, m_fox_w_out, m_conv_w_in, m_conv_w, m_conv_w_out, m_mlp_w_up, m_mlp_w_down, m_final_norm, v_ada_w, v_ada_b, v_norm_mix, v_norm_mlp, v_fox_w_in, v_fox_b_f, v_fox_w_out, v_conv_w_in, v_conv_w, v_conv_w_out, v_mlp_w_up, v_mlp_w_down, v_final_norm):
    S, D = x.shape[1], x.shape[2]
    H = fox_b_f.shape[-1]
    L = ada_w.shape[0]
    NM = ada_b.shape[1] // D
    Ns_ada = ada_w.shape[2]
    Cs_fox = fox_w_in.shape[2]
    Cs_conv = conv_w.shape[2]
    x0 = x[0]
    target = loss_target[0]

    chip = (2 * lax.axis_index("x") + lax.axis_index("y")).astype(jnp.int32).reshape(1)

    fin_t = jnp.transpose(fox_w_in, (0, 2, 1))
    shards = dict(fin=(fin_t, 0), fout=(fox_w_out, 0), up0=(mlp_w_up, 0), dn0=(mlp_w_down, 0), cin=(conv_w_in, 0),
                  cout=(conv_w_out, 0), up1=(mlp_w_up, 1), dn1=(mlp_w_down, 1))
    halves = dict(fin="cols", fout="rows", up0="cols", dn0="rows", cin="cols", cout="rows", up1="cols", dn1="rows")
    gathers, placed = {}, {}

    def place(key, dep=None):
        placed[key] = _place_cast(*shards[key], chip, halves=halves[key], name="place_" + key, dep=dep)
        return placed[key]

    def start_gather(key, dep=None):
        gathers[key] = _split_start("gather1", [(placed[key],)], name="gather_start_" + key, dep=dep)
        return gathers[key][3]

    def pass_gather(key, after):
        landed = _split_wait("gather1", gathers[key], after, name="gather_landed_" + key)
        gathers[key] = _split_start("gather2", landed, name="gather_pass_" + key)
        return gathers[key][3]

    def gathered(key, after):
        return _split_wait("gather2", gathers[key], after, name="gather_wait_" + key)[0][0]

    place("fin")
    tok = start_gather("fin")
    for key in ("fout", "up0", "dn0", "cin", "cout", "up1", "dn1"):
        tok = place(key, tok)

    c_all = _allgather8(_pad_rows(c, 8), name="gather_c", deps=(tok,))[:, 0, :]
    mod_part, c_act = _ada_fwd(c_all, ada_w, name="ada_fwd")
    mod_all = _allgather8(mod_part.reshape(L * N_DEV, Ns_ada), name="gather_mod")
    mod = _select_mod(mod_all, name="select_mod")
    mod = jnp.transpose(mod, (1, 0, 2)).reshape(L, NM, 1, D) + ada_b.reshape(L, NM, 1, D)
    conv_w_all = _allgather8(_pad_rows(conv_w[0], 8), name="gather_conv_w")
    conv_w_full = jnp.transpose(conv_w_all[0::2, :3, :], (1, 0, 2)).reshape(3, D)

    def vec(a):
        return a.reshape(1, D)

    h0 = _norm_fwd(x0, vec(norm_mix[0]), mod[0, 1], mod[0, 0], name="norm_mix0", deps=(conv_w_all,))
    tok = h0
    for key in ("fout", "up0", "dn0", "cin", "cout", "up1", "dn1"):
        tok = start_gather(key, tok)
    tok = pass_gather("fin", [h0, tok])
    w_fin_t = jnp.transpose(gathered("fin", [tok]), (0, 2, 1, 3)).reshape(N_CHIPS * Cs_fox, D)
    w_f_t = _pad_rows(w_fin_t[3 * D:], LANES)
    tok = pass_gather("fout", [w_fin_t])
    qkv = _mm_nt(h0, w_fin_t, n=3 * D, name="fox_in", out_dtype=BF16, deps=(tok,))
    tok = pass_gather("up0", [qkv])
    b_f = jnp.pad(fox_b_f, ((0, 0), (0, LANES - H)))
    z_f, F_col = _fgate_fwd(h0, w_f_t, b_f, name="fgate_fwd", deps=(tok,))
    hp = max(8, H)
    at_f, at = _tile(S, 1024, 16), _tile(S, 1024, 16)
    F_rows = _pad_rows(jnp.transpose(F_col[:, :H]), hp)
    F_row = jnp.transpose(F_rows.reshape(hp, S // at, at), (1, 0, 2))
    o, lse = _attn_fwd(qkv, jnp.transpose(F_rows.reshape(hp, S // at_f, at_f), (1, 0, 2)), heads=H, name="attn_fwd", T=at_f)
    w_fout = gathered("fout", [o]).reshape(D, D)
    tok = pass_gather("dn0", [o])
    x1, mix0, h1 = _mm_nn(o, w_fout, name="fox_out", epilogue="resid_norm", res=x0, gate=mod[0, 2],
                          norm=(vec(norm_mlp[0]), mod[0, 4], mod[0, 3]), tm=512, tn=D, deps=(tok,))
    w_up0 = gathered("up0", [h1]).reshape(2 * N_CHIPS, D, -1)
    tok = pass_gather("cin", [h1])
    u0, a0 = _mm_nn(h1, w_up0, name="mlp_up0", epilogue="relu2", deps=(tok,))
    w_dn0 = gathered("dn0", [a0]).reshape(-1, D)
    tok = pass_gather("cout", [a0])
    x2, y0 = _mm_nn(a0, w_dn0, name="mlp_down0", epilogue="resid", res=x1, gate=mod[0, 5], deps=(tok,))
    h2 = _norm_fwd(x2, vec(norm_mix[1]), mod[1, 1], mod[1, 0], name="norm_mix1")
    g_cin = gathered("cin", [h2]).reshape(2 * N_CHIPS, D, -1)
    tok = pass_gather("up1", [h2])
    proj = _mm_nn(h2, g_cin, name="conv_in", deps=(tok,))
    w_cin = jnp.transpose(g_cin, (1, 0, 2)).reshape(D, 3 * D)
    zc = _conv_fwd(proj, conv_w_full, name="conv_fwd")
    w_cout = gathered("cout", [zc]).reshape(D, D)
    tok = pass_gather("dn1", [zc])
    x3, mix1, h3 = _mm_nn(zc, w_cout, name="conv_out", epilogue="resid_norm", res=x2, gate=mod[1, 2],
                          norm=(vec(norm_mlp[1]), mod[1, 4], mod[1, 3]), tm=512, tn=D, deps=(tok,))
    w_up1 = gathered("up1", [h3]).reshape(2 * N_CHIPS, D, -1)
    u1, a1 = _mm_nn(h3, w_up1, name="mlp_up1", epilogue="relu2")
    w_dn1 = gathered("dn1", [a1]).reshape(-1, D)
    x4, y1 = _mm_nn(a1, w_dn1, name="mlp_down1", epilogue="resid", res=x3, gate=mod[1, 5])
    w_up = [jnp.transpose(w_up0, (1, 0, 2)).reshape(D, -1), jnp.transpose(w_up1, (1, 0, 2)).reshape(D, -1)]
    w_dn = [w_dn0, w_dn1]

    dx4, dy1, sums_f = _loss_bwd(x4, target, vec(final_norm), mod[1, 5], name="loss_bwd")
    du1 = _mm_nt(dy1, w_dn[1], name="mlp_down1_dx", epilogue="drelu2", extra=u1, out_dtype=BF16)
    def start_scatter(tag, parts_list):
        groups = [(p, lax.empty(p.shape, p.dtype)) for p in parts_list]
        return _split_start("scatter", groups, name="scatter_start_" + tag)

    def start_sibling(tag, scatter, after):
        landed = _split_wait("scatter", scatter, after, name="scatter_wait_" + tag)
        groups = [(p, ld, lax.empty(p.shape, p.dtype)) for p, ld in landed]
        return _split_start("sibling", groups, name="sibling_start_" + tag)

    gw_dn1 = _mm_tn(a1, dy1, name="mlp_down1_dw")
    gw_up1 = _mm_tn(h3, du1, name="mlp_up1_dw", out_parts=N_CHIPS)
    sc1 = start_scatter("mlp1", [gw_dn1.reshape(N_CHIPS, -1, D), gw_up1])
    dh3 = _mm_nt(du1, w_up[1], name="mlp_up1_dx", out_dtype=BF16, tk=4096, deps=(sc1[3],))
    dx3, dmix1, sums_mlp1 = _norm_bwd(x3, dh3, dx4, y1, vec(norm_mlp[1]), mod[1, 4], mod[1, 2], name="norm_mlp1_bwd")
    dzc = _mm_nt(dmix1, w_cout, name="conv_out_dx", out_dtype=BF16)
    gw_cout = _mm_tn(zc, dmix1, name="conv_out_dw")
    dproj, dconv_w = _conv_bwd(proj, conv_w_full, dzc, name="conv_bwd")
    gw_cin = _mm_tn(h2, dproj, name="conv_in_dw", out_parts=N_CHIPS, tn=512)
    sc2 = start_scatter("conv", [gw_cout.reshape(N_CHIPS, -1, D), gw_cin])
    dh2 = _mm_nt(dproj, w_cin, name="conv_in_dx", out_dtype=BF16, deps=(sc2[3],))
    dx2, dy0, sums_mix1 = _norm_bwd(x2, dh2, dx3, mix1, vec(norm_mix[1]), mod[1, 1], mod[0, 5], name="norm_mix1_bwd")
    du0 = _mm_nt(dy0, w_dn[0], name="mlp_down0_dx", epilogue="drelu2", extra=u0, out_dtype=BF16)
    gw_dn0 = _mm_tn(a0, dy0, name="mlp_down0_dw")
    gw_up0 = _mm_tn(h1, du0, name="mlp_up0_dw", out_parts=N_CHIPS)
    sc3 = start_scatter("mlp0", [gw_dn0.reshape(N_CHIPS, -1, D), gw_up0])
    sb1 = start_sibling("mlp1", sc1, [sc3[3]])
    dh1 = _mm_nt(du0, w_up[0], name="mlp_up0_dx", out_dtype=BF16, tk=4096, deps=(sb1[3],))
    dx1, dmix0, sums_mlp0 = _norm_bwd(x1, dh1, dx2, y0, vec(norm_mlp[0]), mod[0, 4], mod[0, 2], name="norm_mlp0_bwd")
    do = _mm_nt(dmix0, w_fout, name="fox_out_dx", out_dtype=BF16)
    gw_fout = _mm_tn(o, dmix0, name="fox_out_dw")
    dqkv, dfq, dfk = _attn_bwd(qkv, o, do, F_row, lse, heads=H, name="attn_bwd", T=at)
    dfk_col = jnp.pad(jnp.transpose(jnp.transpose(dfk, (1, 0, 2)).reshape(hp, S)[:H]), ((0, 0), (0, LANES - H)))
    sb3 = start_sibling("mlp0", sc3, [dqkv])
    dz_f, sums_bf = _fgate_bwd(dfq, dfk_col, z_f, name="fgate_bwd", deps=(sb3[3],))
    gw_qkv_t = _mm_tn(dqkv, h0, name="fox_in_dw")
    gw_f_t = _mm_tn(dz_f, h0, name="fox_gate_dw")
    gw_fin_t = jnp.concatenate([gw_qkv_t, gw_f_t[:H]], axis=0).reshape(N_CHIPS, Cs_fox, D)
    sc4 = start_scatter("fox", [gw_fout.reshape(N_CHIPS, -1, D), gw_fin_t])
    sb2 = start_sibling("conv", sc2, [sc4[3]])
    dh0_f = _mm_nn(dz_f, w_f_t, name="fox_gate_dx", out_dtype=F32, deps=(sb2[3],))
    dh0 = _mm_nn(dqkv, w_fin_t, name="fox_in_dx", epilogue="add", res=dh0_f, out_dtype=BF16)
    grad_x, _, sums_mix0 = _norm_bwd(x0, dh0, dx1, mix0, vec(norm_mix[0]), mod[0, 1], None, name="norm_mix0_bwd")

    outs = {}

    def put(name_, res, shape):
        for kind, r in zip(("grad", "delta", "new_m", "new_v"), res):
            outs[kind + "_" + name_] = r.reshape(shape)

    def shards_update(tag, w_, m_, v_, groups):
        return _adamw_shards(w_, m_, v_, groups, chip, name="adamw_" + tag)

    g_conv = _split_wait("sibling", sb2, [grad_x], name="sibling_wait_conv")
    put("conv_w_out", shards_update("conv_out", conv_w_out, m_conv_w_out, v_conv_w_out, g_conv[0:1]), conv_w_out.shape)
    r_cin = shards_update("conv_in", conv_w_in, m_conv_w_in, v_conv_w_in, g_conv[1:2])
    put("conv_w_in", r_cin, conv_w_in.shape)
    g_mlp1 = _split_wait("sibling", sb1, [r_cin[0]], name="sibling_wait_mlp1")
    g_mlp0 = _split_wait("sibling", sb3, [r_cin[0]], name="sibling_wait_mlp0")
    put("mlp_w_down", shards_update("mlp_down", mlp_w_down, m_mlp_w_down, v_mlp_w_down, [g_mlp0[0], g_mlp1[0]]), mlp_w_down.shape)
    r_up = shards_update("mlp_up", mlp_w_up, m_mlp_w_up, v_mlp_w_up, [g_mlp0[1], g_mlp1[1]])
    put("mlp_w_up", r_up, mlp_w_up.shape)
    sb4 = start_sibling("fox", sc4, [r_up[0]])

    dmod_rows = []
    for sm, sl in ((sums_mix0, sums_mlp0), (sums_mix1, sums_mlp1)):
        dmod_rows += [sm[0:1], sm[1:2], sm[3:4], sl[0:1], sl[1:2], sl[3:4]]
    bf_row = jnp.pad(sums_bf[0:1], ((0, 0), (0, D - LANES)))
    small = jnp.concatenate([sums_mix0[2:3], sums_mix1[2:3], sums_mlp0[2:3], sums_mlp1[2:3], sums_f[0:1], sums_f[1:2], bf_row,
                             jnp.zeros((1, D), F32)] + dmod_rows + [dconv_w[0:3]], axis=0)
    small_all = _allgather8(_pad_rows(small, -(-small.shape[0] // 8) * 8), name="gather_small", deps=(sb4[3],))
    loss = _loss_sum(small_all, row=5, name="loss_sum")[0, 0]

    def rows_of(a_mix, a_mlp, a_fin, a_bf, a_ada):
        return jnp.concatenate([a_mix, a_mlp, a_fin.reshape(1, D), jnp.pad(a_bf, ((0, 0), (0, D - H))),
                                a_ada.reshape(L * NM, D)], axis=0)
    n_small = 2 * L + 2 + L * NM
    rw = -(-n_small // 8) * 8
    w_s = _pad_rows(rows_of(norm_mix, norm_mlp, final_norm, fox_b_f, ada_b), rw)
    m_s = _pad_rows(rows_of(m_norm_mix, m_norm_mlp, m_final_norm, m_fox_b_f, m_ada_b), rw)
    v_s = _pad_rows(rows_of(v_norm_mix, v_norm_mlp, v_final_norm, v_fox_b_f, v_ada_b), rw)
    src_rows = [0, 1, 2, 3, 4, 6] + [8 + r for r in range(L * NM)] + [7] * (rw - n_small)
    res_s = _adamw_small(w_s, m_s, v_s, small_all, rows=tuple(src_rows), name="adamw_small")
    for kind, r in zip(("grad", "delta", "new_m", "new_v"), res_s):
        outs[kind + "_norm_mix"] = r[0:L]
        outs[kind + "_norm_mlp"] = r[L:2 * L]
        outs[kind + "_final_norm"] = r[2 * L]
        outs[kind + "_fox_b_f"] = r[2 * L + 1:2 * L + 2, :H]
        outs[kind + "_ada_b"] = r[2 * L + 2:n_small].reshape(L, NM * D)

    dmod_all = small_all[:, 8:8 + L * NM, :].reshape(N_DEV, L, N_CHIPS, Ns_ada)
    dmod4 = jnp.transpose(dmod_all, (2, 1, 0, 3))
    act_t = jnp.transpose(c_act)
    res_a = _adamw_ada(ada_w, m_ada_w, v_ada_w, act_t, dmod4, name="adamw_ada")
    put("ada_w", res_a, ada_w.shape)

    r0 = 8 + L * NM
    dconv_all = jnp.pad(small_all[:, r0:r0 + 3, :], ((0, 0), (0, 5), (0, 0)))
    dconv4 = jnp.transpose(dconv_all.reshape(N_DEV, 8, N_CHIPS, Cs_conv), (2, 0, 1, 3))
    res_c = _adamw_conv_w(_pad_rows(conv_w[0], 8), _pad_rows(m_conv_w[0], 8), _pad_rows(v_conv_w[0], 8), dconv4,
                          name="adamw_conv_w")
    for kind, r in zip(("grad", "delta", "new_m", "new_v"), res_c):
        outs[kind + "_conv_w"] = r[:3].reshape(conv_w.shape)

    g_fox = _split_wait("sibling", sb4, [res_a[0], res_c[0], res_s[0]], name="sibling_wait_fox")
    put("fox_w_out", shards_update("fox_out", fox_w_out, m_fox_w_out, v_fox_w_out, g_fox[0:1]), fox_w_out.shape)
    t3 = lambda a: jnp.transpose(a, (0, 2, 1))
    for kind, r in zip(("grad", "delta", "new_m", "new_v"),
                       shards_update("fox_in", t3(fox_w_in), t3(m_fox_w_in), t3(v_fox_w_in), g_fox[1:2])):
        outs[kind + "_fox_w_in"] = t3(r)

    names = ["ada_w", "ada_b", "norm_mix", "norm_mlp", "fox_w_in", "fox_b_f", "fox_w_out", "conv_w_in", "conv_w", "conv_w_out",
             "mlp_w_up", "mlp_w_down", "final_norm"]
    return (loss, grad_x[None], *[outs["grad_" + n] for n in names], *[outs["delta_" + n] for n in names],
            *[outs["new_m_" + n] for n in names], *[outs["new_v_" + n] for n in names])
```

```python
import jax
import jax.numpy as jnp
from jax import lax
from jax.experimental import pallas as pl
from jax.experimental.pallas import tpu as pltpu

F32 = jnp.float32
BF16 = jnp.bfloat16
MESH = pl.DeviceIdType.MESH
ANY = pl.BlockSpec(memory_space=pl.ANY)
HBM = pl.BlockSpec(memory_space=pltpu.HBM)
SEM = pl.BlockSpec(memory_space=pltpu.SEMAPHORE)
EFFECT = pltpu.SideEffectType.DATAFLOW_SIDE_EFFECTING

RMS_EPS = 1e-6
ADAM_LR = 0.001
ADAM_B1 = 0.9
ADAM_B2 = 0.999
ADAM_EPS = 1e-08
ADAM_WD = 0.01
ADAM_STEP = 10
N_CHIPS = 4
N_DEV = 8
LANES = 128
VMEM_LIMIT = 56 * 1024 * 1024
NEG = -1e30


def _params(sems=None, vmem=VMEM_LIMIT):
    return pltpu.CompilerParams(dimension_semantics=sems, vmem_limit_bytes=vmem)


def _tile(n, pref, unit=LANES):
    if n <= pref:
        return n
    t = (pref // unit) * unit
    while n % t:
        t -= unit
    return t


def _me():
    return lax.axis_index("x"), lax.axis_index("y"), lax.axis_index("c")


def _call(body, deps, **kw):
    nd = len(deps)

    def wrapped(*refs):
        body(*refs[nd:])

    kw["in_specs"] = [ANY] * nd + list(kw["in_specs"])
    fn = pl.pallas_call(wrapped, **kw)
    return lambda *args: fn(*deps, *args)


def _allgather8(v, *, name, deps=()):
    R, C = v.shape

    def body(v_ref, out_ref, send_sems, recv_sems):
        x, y, c = _me()
        me = 4 * x + 2 * y + c
        out_ref[me] = v_ref[...]
        copies = []
        for k in range(1, N_DEV):
            px, py, pc = (x + (k >> 2)) % 2, (y + ((k >> 1) & 1)) % 2, (c + (k & 1)) % 2
            copies.append(pltpu.make_async_remote_copy(
                src_ref=v_ref, dst_ref=out_ref.at[me], send_sem=send_sems.at[k - 1], recv_sem=recv_sems.at[k - 1],
                device_id=(px, py, pc), device_id_type=MESH))
        for cp in copies:
            cp.start()
        for k in range(1, N_DEV):
            px, py, pc = (x + (k >> 2)) % 2, (y + ((k >> 1) & 1)) % 2, (c + (k & 1)) % 2
            peer = 4 * px + 2 * py + pc
            pltpu.make_async_remote_copy(
                src_ref=v_ref, dst_ref=out_ref.at[peer], send_sem=send_sems.at[k - 1], recv_sem=recv_sems.at[k - 1],
                device_id=(px, py, pc), device_id_type=MESH).wait_recv()
        for cp in copies:
            cp.wait_send()

    return _call(
        body, deps, name=name,
        out_shape=jax.ShapeDtypeStruct((N_DEV, R, C), v.dtype),
        in_specs=[pl.BlockSpec(memory_space=pltpu.VMEM)],
        out_specs=pl.BlockSpec(memory_space=pltpu.VMEM),
        scratch_shapes=[pltpu.SemaphoreType.DMA((N_DEV - 1,)), pltpu.SemaphoreType.DMA((N_DEV - 1,))],
        compiler_params=_params(),
    )(v)


def _chip_peers(x, y):
    return [((x + (k >> 1)) % 2, (y + (k & 1)) % 2) for k in range(1, N_CHIPS)]


def _slot(x, y, k):
    return 2 * ((x + (k >> 1)) % 2) + (y + (k & 1)) % 2


def _split_copies(kind, groups, send_sems, recv_sems):
    x, y, c = _me()
    j = 2 * x + y
    copies = []
    for a, g in enumerate(groups):
        if kind == "sibling":
            parts, land, sib = g
            for k in range(N_CHIPS):
                s = _slot(x, y, k)
                copies.append(pltpu.make_async_remote_copy(
                    src_ref=(parts if k == 0 else land).at[s], dst_ref=sib.at[s], send_sem=send_sems.at[N_CHIPS * a + k],
                    recv_sem=recv_sems.at[N_CHIPS * a + k], device_id=(x, y, 1 - c), device_id_type=MESH))
            continue
        land = g[-1]
        for k, (px, py) in enumerate(_chip_peers(x, y)):
            if kind == "gather1":
                src, dst, to = land.at[j, c], land.at[j, c], (px, py, c)
            elif kind == "gather2":
                src, dst, to = land.at[2 * px + py, c], land.at[2 * px + py, c], (x, y, 1 - c)
            else:
                src, dst, to = g[0].at[2 * px + py], land.at[j], (px, py, c)
            copies.append(pltpu.make_async_remote_copy(
                src_ref=src, dst_ref=dst, send_sem=send_sems.at[3 * a + k], recv_sem=recv_sems.at[3 * a + k],
                device_id=to, device_id_type=MESH))
    return copies


def _split_start(kind, groups, *, name, dep=None):
    flat = [a for g in groups for a in g]
    nf, per = len(flat), len(groups[0])
    ncp = len(groups) * (N_CHIPS if kind == "sibling" else 3)
    nd = 0 if dep is None else 1

    def body(*refs):
        ins = refs[nd:nd + nf]
        send_sems, recv_sems, token = refs[nd + nf], refs[nd + nf + 1], refs[-1]
        for cp in _split_copies(kind, [ins[i:i + per] for i in range(0, nf, per)], send_sems, recv_sems):
            cp.start()
        token[...] = jnp.zeros_like(token)

    outs = pl.pallas_call(
        body, name=name,
        out_shape=(pltpu.SemaphoreType.DMA((ncp,)), pltpu.SemaphoreType.DMA((ncp,)), *[pltpu.HBM(a.shape, a.dtype) for a in flat],
                   jax.ShapeDtypeStruct((8, LANES), F32)),
        in_specs=[ANY] * nd + [HBM] * nf,
        out_specs=(SEM, SEM, *[HBM] * nf, pl.BlockSpec(memory_space=pltpu.VMEM)),
        input_output_aliases={nd + i: 2 + i for i in range(nf)},
        compiler_params=pltpu.CompilerParams(has_side_effects=EFFECT),
    )(*([dep] if nd else []), *[pltpu.with_memory_space_constraint(a, pltpu.HBM) for a in flat])
    thru = list(outs[2:2 + nf])
    return outs[0], outs[1], [tuple(thru[i:i + per]) for i in range(0, nf, per)], outs[-1]


def _split_wait(kind, started, after, *, name):
    send_sems, recv_sems, groups, _ = started
    flat = [a for g in groups for a in g]
    nf, per = len(flat), len(groups[0])

    def body(*refs):
        ins = refs[:nf]
        for cp in _split_copies(kind, [ins[i:i + per] for i in range(0, nf, per)], refs[nf], refs[nf + 1]):
            cp.wait_send()
            cp.wait_recv()

    outs = pl.pallas_call(
        body, name=name,
        out_shape=tuple(pltpu.HBM(a.shape, a.dtype) for a in flat),
        in_specs=[HBM] * nf + [SEM, SEM] + [ANY] * len(after), out_specs=tuple([HBM] * nf),
        input_output_aliases={i: i for i in range(nf)},
        compiler_params=pltpu.CompilerParams(has_side_effects=EFFECT),
    )(*flat, send_sems, recv_sems, *after)
    outs = list(outs)
    return [tuple(outs[i:i + per]) for i in range(0, nf, per)]


def _place_cast(shards, layer, chip, *, halves, name, dep=None):
    _, R, C = shards.shape
    if halves == "rows":
        hr, hc = R // 2, C
    else:
        hr, hc = R, C // 2
    if hr % 16 == 0:
        tr, tc = _tile(hr, 512, 16), hc
    else:
        tr, tc = hr, _tile(hc, 256)
    nr, nc = hr // tr, hc // tc

    def body(chip_ref, x_ref, *rest):
        rest[-1][...] = x_ref[...].astype(BF16)

    if halves == "rows":
        o_map = lambda i, j, chip_ref: (chip_ref[0], i // nr, i % nr, j)
    else:
        o_map = lambda i, j, chip_ref: (chip_ref[0], j // nc, i, j % nc)
    return pl.pallas_call(
        body, name=name,
        grid_spec=pltpu.PrefetchScalarGridSpec(
            num_scalar_prefetch=1, grid=(R // tr, C // tc),
            in_specs=[pl.BlockSpec((None, tr, tc), lambda i, j, chip_ref: (layer, i, j))] + ([] if dep is None else [ANY]),
            out_specs=pl.BlockSpec((None, None, tr, tc), o_map)),
        out_shape=jax.ShapeDtypeStruct((N_CHIPS, 2, hr, hc), BF16), compiler_params=_params(("parallel", "parallel")),
    )(chip, shards, *([] if dep is None else [dep]))


def _accumulate(part, acc_ref, nk, finalize):
    if nk == 1:
        finalize(part)
        return
    k = pl.program_id(2)

    @pl.when(k == 0)
    def _():
        acc_ref[...] = part

    @pl.when(k > 0)
    def _():
        acc_ref[...] += part

    @pl.when(k == nk - 1)
    def _():
        finalize(acc_ref[...])


def _mm_nn(a, b, *, name, epilogue="plain", res=None, gate=None, norm=None, out_dtype=BF16, tm=1024, tn=1024, tk=2048, deps=()):
    if a.ndim == 3:
        Q, M, Kq = a.shape
        K = Q * Kq
    else:
        (M, K), Kq = a.shape, a.shape[1]
    tm, tk = _tile(M, tm, 16), _tile(Kq, tk)
    if a.ndim == 3:
        pa = Kq // tk
        a_spec = pl.BlockSpec((None, tm, tk), lambda i, j, k: (k // pa, i, k % pa))
    else:
        a_spec = pl.BlockSpec((tm, tk), lambda i, j, k: (i, k))
    if b.ndim == 3:
        P, _, Ns = b.shape
        N = P * Ns
        tn = _tile(Ns, tn)
        per = Ns // tn
        b_spec = pl.BlockSpec((None, tk, tn), lambda i, j, k: (j // per, k, j % per))
    else:
        N = b.shape[1]
        tn = _tile(N, tn)
        b_spec = pl.BlockSpec((tk, tn), lambda i, j, k: (k, j))
    nk = K // tk
    tile = pl.BlockSpec((tm, tn), lambda i, j, k: (i, j))

    def body(*refs):
        acc_ref = refs[-1] if nk > 1 else None
        a_ref, b_ref = refs[0], refs[1]
        part = jnp.dot(a_ref[...], b_ref[...], preferred_element_type=F32)
        if epilogue == "plain":
            def fin(acc):
                refs[2][...] = acc.astype(out_dtype)
        elif epilogue == "relu2":
            def fin(acc):
                refs[2][...] = acc.astype(BF16)
                refs[3][...] = jnp.square(jnp.maximum(acc, 0.0)).astype(BF16)
        elif epilogue == "add":
            def fin(acc):
                refs[3][...] = (acc + refs[2][...]).astype(out_dtype)
        elif epilogue == "resid":
            def fin(acc):
                refs[4][...] = refs[2][...] + refs[3][...] * acc
                refs[5][...] = acc.astype(BF16)
        else:
            def fin(acc):
                xv = refs[2][...] + refs[3][...] * acc
                refs[7][...] = xv
                refs[8][...] = acc.astype(BF16)
                r = lax.rsqrt(jnp.mean(xv * xv, axis=-1, keepdims=True) + RMS_EPS)
                refs[9][...] = (((xv * r) * refs[4][...]) * (1.0 + refs[5][...]) + refs[6][...]).astype(BF16)
        _accumulate(part, acc_ref, nk, fin)

    in_specs = [a_spec, b_spec]
    args = [a, b]
    if epilogue == "plain":
        out_shape, out_specs = jax.ShapeDtypeStruct((M, N), out_dtype), tile
    elif epilogue == "relu2":
        out_shape, out_specs = [jax.ShapeDtypeStruct((M, N), BF16)] * 2, [tile, tile]
    elif epilogue == "add":
        in_specs.append(tile)
        args.append(res)
        out_shape, out_specs = jax.ShapeDtypeStruct((M, N), out_dtype), tile
    else:
        row = pl.BlockSpec((1, tn), lambda i, j, k: (0, j))
        in_specs += [tile, row]
        args += [res, gate]
        out_shape, out_specs = [jax.ShapeDtypeStruct((M, N), F32), jax.ShapeDtypeStruct((M, N), BF16)], [tile, tile]
        if epilogue == "resid_norm":
            assert tn == N, "the next norm needs whole rows"
            in_specs += [row, row, row]
            args += list(norm)
            out_shape, out_specs = out_shape + [jax.ShapeDtypeStruct((M, N), BF16)], out_specs + [tile]
    return _call(
        body, deps, name=name, grid=(M // tm, N // tn, nk), in_specs=in_specs, out_specs=out_specs, out_shape=out_shape,
        scratch_shapes=[pltpu.VMEM((tm, tn), F32)] if nk > 1 else [],
        compiler_params=_params(("parallel", "parallel", "arbitrary")),
    )(*args)


def _mm_nt(a, b, *, name, n=None, epilogue="plain", extra=None, out_dtype=F32, tm=1024, tn=1024, tk=2048, deps=()):
    if a.ndim == 3:
        Q, M, Kq = a.shape
        K = Q * Kq
    else:
        (M, K), Kq = a.shape, a.shape[1]
    if b.ndim == 3:
        P, N, Ks = b.shape
    else:
        N, Ks = b.shape
    N = n or N
    tm, tn, tk = _tile(M, tm, 16), _tile(N, tn), _tile(min(Kq, Ks), tk)
    nk = K // tk
    if a.ndim == 3:
        pa = Kq // tk
        a_spec = pl.BlockSpec((None, tm, tk), lambda i, j, k: (k // pa, i, k % pa))
    else:
        a_spec = pl.BlockSpec((tm, tk), lambda i, j, k: (i, k))
    if b.ndim == 3:
        pb = Ks // tk
        b_spec = pl.BlockSpec((None, tn, tk), lambda i, j, k: (k // pb, j, k % pb))
    else:
        b_spec = pl.BlockSpec((tn, tk), lambda i, j, k: (j, k))
    tile = pl.BlockSpec((tm, tn), lambda i, j, k: (i, j))

    def body(*refs):
        acc_ref = refs[-1] if nk > 1 else None
        part = lax.dot_general(refs[0][...], refs[1][...], (((1,), (1,)), ((), ())), preferred_element_type=F32)
        if epilogue == "plain":
            def fin(acc):
                refs[2][...] = acc.astype(out_dtype)
        elif epilogue == "add":
            def fin(acc):
                refs[3][...] = (acc + refs[2][...]).astype(out_dtype)
        else:
            def fin(acc):
                refs[3][...] = (acc * (2.0 * jnp.maximum(refs[2][...].astype(F32), 0.0))).astype(out_dtype)
        _accumulate(part, acc_ref, nk, fin)

    in_specs, args = [a_spec, b_spec], [a, b]
    if epilogue != "plain":
        in_specs.append(tile)
        args.append(extra)
    return _call(
        body, deps, name=name, grid=(M // tm, N // tn, nk), in_specs=in_specs, out_specs=tile,
        out_shape=jax.ShapeDtypeStruct((M, N), out_dtype),
        scratch_shapes=[pltpu.VMEM((tm, tn), F32)] if nk > 1 else [],
        compiler_params=_params(("parallel", "parallel", "arbitrary")),
    )(*args)


def _mm_tn(a, b, *, name, out_parts=1, tm=1024, tn=1024, tk=4096, deps=()):
    if a.ndim == 3:
        Qa, M, Kq = a.shape
        Kd = Qa * Kq
    else:
        (M, Kd), Kq = a.shape, a.shape[1]
    if b.ndim == 3:
        Q, _, Nq = b.shape
        N = Q * Nq
    else:
        N, Nq = b.shape[1], b.shape[1]
    Ns = N // out_parts
    tn = _tile(Ns, tn)
    while Nq % tn or Ns % tn:
        tn -= LANES
    tm, tk = _tile(Kq, tm), _tile(M, tk, 16)
    nk = M // tk
    if a.ndim == 3:
        pa = Kq // tm
        a_spec = pl.BlockSpec((None, tk, tm), lambda i, j, k: (i // pa, k, i % pa))
    else:
        a_spec = pl.BlockSpec((tk, tm), lambda i, j, k: (k, i))
    if b.ndim == 3:
        pb = Nq // tn
        b_spec = pl.BlockSpec((None, tk, tn), lambda i, j, k: (j // pb, k, j % pb))
    else:
        b_spec = pl.BlockSpec((tk, tn), lambda i, j, k: (k, j))
    if out_parts > 1:
        po = Ns // tn
        o_spec = pl.BlockSpec((None, tm, tn), lambda i, j, k: (j // po, i, j % po))
        out_shape = jax.ShapeDtypeStruct((out_parts, Kd, Ns), BF16)
    else:
        o_spec = pl.BlockSpec((tm, tn), lambda i, j, k: (i, j))
        out_shape = jax.ShapeDtypeStruct((Kd, N), BF16)

    def body(*refs):
        acc_ref = refs[-1] if nk > 1 else None
        part = lax.dot_general(refs[0][...], refs[1][...], (((0,), (0,)), ((), ())), preferred_element_type=F32)

        def fin(acc):
            refs[2][...] = acc.astype(BF16)
        _accumulate(part, acc_ref, nk, fin)

    return _call(
        body, deps, name=name, grid=(Kd // tm, N // tn, nk),
        in_specs=[a_spec, b_spec], out_specs=o_spec, out_shape=out_shape,
        scratch_shapes=[pltpu.VMEM((tm, tn), F32)] if nk > 1 else [],
        compiler_params=_params(("parallel", "parallel", "arbitrary")),
    )(a, b)


def _rows(S, D, i_map=lambda i: (i, 0), ts=512):
    return pl.BlockSpec((ts, D), i_map)


def _norm_fwd(x, gain, sc, sh, *, name, deps=()):
    S, D = x.shape
    ts = _tile(S, 512, 16)
    vec = pl.BlockSpec((1, D), lambda i: (0, 0))

    def body(x_ref, g_ref, sc_ref, sh_ref, h_ref):
        xv = x_ref[...]
        r = lax.rsqrt(jnp.mean(xv * xv, axis=-1, keepdims=True) + RMS_EPS)
        h = (xv * r) * g_ref[...]
        h_ref[...] = (h * (1.0 + sc_ref[...]) + sh_ref[...]).astype(BF16)

    return _call(
        body, deps, name=name, grid=(S // ts,), in_specs=[_rows(S, D, ts=ts), vec, vec, vec], out_specs=_rows(S, D, ts=ts),
        out_shape=jax.ShapeDtypeStruct((S, D), BF16), compiler_params=_params(("parallel",)),
    )(x, gain, sc, sh)


def _loss_bwd(x, target, gain, gate_prev, *, name, deps=()):
    S, D = x.shape
    ts = _tile(S, 256, 16)
    vec = pl.BlockSpec((1, D), lambda i: (0, 0))

    def body(x_ref, t_ref, g_ref, gp_ref, dx_ref, dp_ref, sums_ref):
        @pl.when(pl.program_id(0) == 0)
        def _():
            sums_ref[...] = jnp.zeros_like(sums_ref)
        xv = x_ref[...]
        r = lax.rsqrt(jnp.mean(xv * xv, axis=-1, keepdims=True) + RMS_EPS)
        xn = xv * r
        err = xn * g_ref[...] - t_ref[...]
        loss = 0.5 * jnp.sum(jnp.mean(err * err, axis=-1, keepdims=True), axis=0, keepdims=True)
        dy = err * (1.0 / D)
        dxn = dy * g_ref[...]
        dx = r * (dxn - xn * jnp.mean(dxn * xn, axis=-1, keepdims=True))
        dx_ref[...] = dx
        dp_ref[...] = (gp_ref[...] * dx).astype(BF16)
        sums_ref[0:1, :] += jnp.sum(dy * xn, axis=0, keepdims=True)
        sums_ref[1:2, :] += jnp.broadcast_to(loss, (1, D))

    return _call(
        body, deps, name=name, grid=(S // ts,),
        in_specs=[_rows(S, D, ts=ts), _rows(S, D, ts=ts), vec, vec],
        out_specs=[_rows(S, D, ts=ts), _rows(S, D, ts=ts), pl.BlockSpec((8, D), lambda i: (0, 0))],
        out_shape=[jax.ShapeDtypeStruct((S, D), F32), jax.ShapeDtypeStruct((S, D), BF16), jax.ShapeDtypeStruct((8, D), F32)],
        compiler_params=_params(("arbitrary",)),
    )(x, target, gain, gate_prev)


def _norm_bwd(x, dh, dxp, mix, gain, sc, gate_prev, *, name, deps=()):
    S, D = x.shape
    ts = _tile(S, 256, 16)
    vec = pl.BlockSpec((1, D), lambda i: (0, 0))
    with_prev = gate_prev is not None

    def body(*refs):
        x_ref, dh_ref, dxp_ref, mix_ref, g_ref, sc_ref = refs[:6]
        outs = refs[7:] if with_prev else refs[6:]
        sums_ref = outs[-1]

        @pl.when(pl.program_id(0) == 0)
        def _():
            sums_ref[...] = jnp.zeros_like(sums_ref)
        xv, dhv, dxpv = x_ref[...], dh_ref[...].astype(F32), dxp_ref[...]
        r = lax.rsqrt(jnp.mean(xv * xv, axis=-1, keepdims=True) + RMS_EPS)
        xn = xv * r
        hn = xn * g_ref[...]
        dhn = dhv * (1.0 + sc_ref[...])
        dxn = dhn * g_ref[...]
        dx = dxpv + r * (dxn - xn * jnp.mean(dxn * xn, axis=-1, keepdims=True))
        outs[0][...] = dx
        if with_prev:
            outs[1][...] = (refs[6][...] * dx).astype(BF16)
        sums_ref[0:1, :] += jnp.sum(dhv, axis=0, keepdims=True)
        sums_ref[1:2, :] += jnp.sum(dhv * hn, axis=0, keepdims=True)
        sums_ref[2:3, :] += jnp.sum(dhn * xn, axis=0, keepdims=True)
        sums_ref[3:4, :] += jnp.sum(dxpv * mix_ref[...].astype(F32), axis=0, keepdims=True)

    tile = _rows(S, D, ts=ts)
    in_specs = [tile, tile, tile, tile, vec, vec] + ([vec] if with_prev else [])
    args = [x, dh, dxp, mix, gain, sc] + ([gate_prev] if with_prev else [])
    out_specs = [tile] + ([tile] if with_prev else []) + [pl.BlockSpec((8, D), lambda i: (0, 0))]
    out_shape = ([jax.ShapeDtypeStruct((S, D), F32)] + ([jax.ShapeDtypeStruct((S, D), BF16)] if with_prev else [])
                 + [jax.ShapeDtypeStruct((8, D), F32)])
    outs = _call(
        body, deps, name=name, grid=(S // ts,), in_specs=in_specs, out_specs=out_specs, out_shape=out_shape,
        compiler_params=_params(("arbitrary",)),
    )(*args)
    return (outs[0], outs[1], outs[2]) if with_prev else (outs[0], None, outs[1])


def _fgate_fwd(h, wf, bf, *, name, deps=()):
    S, D = h.shape
    ts = _tile(S, 256, 16)

    def body(h_ref, w_ref, b_ref, z_ref, f_ref, carry):
        @pl.when(pl.program_id(0) == 0)
        def _():
            carry[...] = jnp.zeros_like(carry)
        z = lax.dot_general(h_ref[...], w_ref[...], (((1,), (1,)), ((), ())), preferred_element_type=F32) + b_ref[...]
        logf = jnp.minimum(z, 0.0) - jnp.log(1.0 + jnp.exp(-jnp.abs(z)))
        row = lax.broadcasted_iota(jnp.int32, (ts, ts), 0)
        col = lax.broadcasted_iota(jnp.int32, (ts, ts), 1)
        tril = (col <= row).astype(F32)
        run = jnp.dot(tril, logf, preferred_element_type=F32, precision=lax.Precision.HIGHEST) + carry[0:1, :]
        z_ref[...] = z
        f_ref[...] = run
        carry[0:1, :] = run[ts - 1:ts, :]

    return _call(
        body, deps, name=name, grid=(S // ts,),
        in_specs=[pl.BlockSpec((ts, D), lambda i: (i, 0)), pl.BlockSpec((LANES, D), lambda i: (0, 0)),
                  pl.BlockSpec((1, LANES), lambda i: (0, 0))],
        out_specs=[pl.BlockSpec((ts, LANES), lambda i: (i, 0))] * 2,
        out_shape=[jax.ShapeDtypeStruct((S, LANES), F32)] * 2,
        scratch_shapes=[pltpu.VMEM((8, LANES), F32)],
        compiler_params=_params(("arbitrary",)),
    )(h, wf, bf)


def _fgate_bwd(dfq, dfk, z, *, name, deps=()):
    S = z.shape[0]
    ts = _tile(S, 256, 16)
    n = S // ts

    def body(dq_ref, dk_ref, z_ref, dz_ref, sums_ref, carry):
        @pl.when(pl.program_id(0) == 0)
        def _():
            carry[...] = jnp.zeros_like(carry)
            sums_ref[...] = jnp.zeros_like(sums_ref)
        df = dq_ref[...] - dk_ref[...]
        row = lax.broadcasted_iota(jnp.int32, (ts, ts), 0)
        col = lax.broadcasted_iota(jnp.int32, (ts, ts), 1)
        triu = (col >= row).astype(F32)
        run = jnp.dot(triu, df, preferred_element_type=F32, precision=lax.Precision.HIGHEST) + carry[0:1, :]
        zv = z_ref[...]
        dz = run * (1.0 / (1.0 + jnp.exp(zv)))
        dz_ref[...] = dz.astype(BF16)
        sums_ref[0:1, :] += jnp.sum(dz, axis=0, keepdims=True)
        carry[0:1, :] = run[0:1, :]

    rev = pl.BlockSpec((ts, LANES), lambda i: (n - 1 - i, 0))
    return _call(
        body, deps, name=name, grid=(n,), in_specs=[rev, rev, rev],
        out_specs=[rev, pl.BlockSpec((8, LANES), lambda i: (0, 0))],
        out_shape=[jax.ShapeDtypeStruct((S, LANES), BF16), jax.ShapeDtypeStruct((8, LANES), F32)],
        scratch_shapes=[pltpu.VMEM((8, LANES), F32)],
        compiler_params=_params(("arbitrary",)),
    )(dfq, dfk, z)


def _head_col(ref, rows, lane_mask):
    return jnp.sum(jnp.where(lane_mask, ref[rows, :], 0.0), axis=1, keepdims=True)


def _attn_fwd(qkv, fk, *, heads, name, T=256, deps=()):
    S, D3 = qkv.shape
    D = D3 // 3
    dh = D // heads
    T = _tile(S, T, 16)
    nq = S // T
    scale = dh ** -0.5
    hp = fk.shape[1]

    def body(q_ref, k_ref, v_ref, fk_ref, o_ref, lse_ref):
        h = pl.program_id(0)

        @pl.when(h == 0)
        def _():
            lse_ref[...] = jnp.zeros_like(lse_ref)
        lane = lax.broadcasted_iota(jnp.int32, (1, LANES), 1) == h
        row = lax.broadcasted_iota(jnp.int32, (T, T), 0)
        col = lax.broadcasted_iota(jnp.int32, (T, T), 1)

        def q_block(qi, _):
            rows = pl.ds(pl.multiple_of(qi * T, T), T)
            q = q_ref[rows, :]

            def kv_block(kj, carry, diag):
                m, l, acc = carry
                cols = pl.ds(pl.multiple_of(kj * T, T), T)
                s = lax.dot_general(q, k_ref[cols, :], (((1,), (1,)), ((), ())), preferred_element_type=F32) * scale
                s = s - fk_ref[kj, pl.ds(h, 1), :]
                if diag:
                    s = jnp.where(col <= row, s, NEG)
                m_new = jnp.maximum(m, jnp.max(s, axis=1, keepdims=True))
                p = jnp.exp(s - m_new)
                alpha = jnp.exp(m - m_new)
                l = alpha * l + jnp.sum(p, axis=1, keepdims=True)
                acc = alpha * acc + jnp.dot(p.astype(BF16), v_ref[cols, :], preferred_element_type=F32)
                return m_new, l, acc

            init = (jnp.full((T, 1), NEG, F32), jnp.zeros((T, 1), F32), jnp.zeros((T, dh), F32))
            carry = lax.fori_loop(0, qi, lambda kj, cr: kv_block(kj, cr, False), init)
            m, l, acc = kv_block(qi, carry, True)
            o_ref[rows, :] = (acc / l).astype(BF16)
            lse_ref[rows, :] = jnp.where(lane, m + jnp.log(l), lse_ref[rows, :])
            return 0

        lax.fori_loop(0, nq, q_block, 0)

    head = lambda part: pl.BlockSpec((S, dh), lambda h: (0, part * heads + h))
    return _call(
        body, deps, name=name, grid=(heads,),
        in_specs=[head(0), head(1), head(2), pl.BlockSpec((nq, hp, T), lambda h: (0, 0, 0))],
        out_specs=[pl.BlockSpec((S, dh), lambda h: (0, h)), pl.BlockSpec((S, LANES), lambda h: (0, 0))],
        out_shape=[jax.ShapeDtypeStruct((S, D), BF16), jax.ShapeDtypeStruct((S, LANES), F32)],
        compiler_params=_params(("arbitrary",)),
    )(qkv, qkv, qkv, fk)


def _attn_bwd(qkv, o, do, fk, lse, *, heads, name, T=256, deps=()):
    S, D3 = qkv.shape
    D = D3 // 3
    dh = D // heads
    T = _tile(S, T, 16)
    nq = S // T
    scale = dh ** -0.5
    hp = fk.shape[1]

    def body(q_ref, k_ref, v_ref, o_ref, do_ref, fk_ref, lse_ref, dqkv_ref, dfq_ref, dfk_ref,
             dq_acc, lse_col, delta_col, dfq_col):
        h = pl.program_id(0)

        @pl.when(h == 0)
        def _():
            dfq_ref[...] = jnp.zeros_like(dfq_ref)
            dfk_ref[...] = jnp.zeros_like(dfk_ref)
        lane = lax.broadcasted_iota(jnp.int32, (1, LANES), 1) == h
        Th = T // 2 if T % 32 == 0 else T
        dq_acc[...] = jnp.zeros_like(dq_acc)
        dfq_col[...] = jnp.zeros_like(dfq_col)

        def prep(qi, _):
            rows = pl.ds(pl.multiple_of(qi * T, T), T)
            lse_col[rows, :] = _head_col(lse_ref, rows, lane)
            delta_col[rows, :] = jnp.sum(do_ref[rows, :].astype(F32) * o_ref[rows, :].astype(F32), axis=1, keepdims=True)
            return 0

        lax.fori_loop(0, nq, prep, 0)

        def kv_block(kj, _):
            cols = pl.ds(pl.multiple_of(kj * T, T), T)
            k, v = k_ref[cols, :], v_ref[cols, :]
            fk_row = fk_ref[kj, pl.ds(h, 1), :]

            def pair_grad(rows, kk, vv, fk_r, masked):
                q, dov = q_ref[rows, :], do_ref[rows, :]
                s = lax.dot_general(q, kk, (((1,), (1,)), ((), ())), preferred_element_type=F32) * scale
                s = s - fk_r
                p = jnp.exp(s - lse_col[rows, :])
                if masked:
                    p = jnp.where(lax.broadcasted_iota(jnp.int32, p.shape, 1) <= lax.broadcasted_iota(jnp.int32, p.shape, 0), p, 0.0)
                dp = lax.dot_general(dov, vv, (((1,), (1,)), ((), ())), preferred_element_type=F32)
                ds = p * (dp - delta_col[rows, :])
                dsb = ds.astype(BF16)
                dv = lax.dot_general(p.astype(BF16), dov, (((0,), (0,)), ((), ())), preferred_element_type=F32)
                dk = lax.dot_general(dsb, q, (((0,), (0,)), ((), ())), preferred_element_type=F32)
                dq_acc[rows, :] += jnp.dot(dsb, kk, preferred_element_type=F32)
                dfq_col[rows, :] += jnp.sum(ds, axis=1, keepdims=True)
                return dk, dv, jnp.sum(ds, axis=0, keepdims=True)

            def q_block(qi, carry):
                rows = pl.ds(pl.multiple_of(qi * T, T), T)
                return tuple(c + g for c, g in zip(carry, pair_grad(rows, k, v, fk_row, False)))

            strips = [pair_grad(pl.ds(pl.multiple_of(kj * T + i * Th, Th), T - i * Th), k[i * Th:(i + 1) * Th],
                                v[i * Th:(i + 1) * Th], fk_row[:, i * Th:(i + 1) * Th], True) for i in range(T // Th)]
            carry = (jnp.concatenate([g[0] for g in strips], axis=0), jnp.concatenate([g[1] for g in strips], axis=0),
                     jnp.concatenate([g[2] for g in strips], axis=1))
            dk, dv, dfk = lax.fori_loop(kj + 1, nq, q_block, carry)
            dqkv_ref[1, cols, :] = (dk * scale).astype(BF16)
            dqkv_ref[2, cols, :] = dv.astype(BF16)
            dfk_ref[kj, pl.ds(h, 1), :] = dfk
            return 0

        lax.fori_loop(0, nq, kv_block, 0)

        def finish(qi, _):
            rows = pl.ds(pl.multiple_of(qi * T, T), T)
            dqkv_ref[0, rows, :] = (dq_acc[rows, :] * scale).astype(BF16)
            dfq_ref[rows, :] = jnp.where(lane, dfq_col[rows, :], dfq_ref[rows, :])
            return 0

        lax.fori_loop(0, nq, finish, 0)

    head = lambda part: pl.BlockSpec((S, dh), lambda h: (0, part * heads + h))
    own = pl.BlockSpec((S, dh), lambda h: (0, h))
    full = pl.BlockSpec((S, LANES), lambda h: (0, 0))
    krow = pl.BlockSpec((nq, hp, T), lambda h: (0, 0, 0))
    return _call(
        body, deps, name=name, grid=(heads,),
        in_specs=[head(0), head(1), head(2), own, own, krow, full],
        out_specs=[pl.BlockSpec((3, S, dh), lambda h: (0, 0, h)), full, krow],
        out_shape=[jax.ShapeDtypeStruct((3, S, D), BF16), jax.ShapeDtypeStruct((S, LANES), F32),
                   jax.ShapeDtypeStruct((nq, hp, T), F32)],
        scratch_shapes=[pltpu.VMEM((S, dh), F32)] + [pltpu.VMEM((S, 1), F32)] * 3,
        compiler_params=_params(("arbitrary",)),
    )(qkv, qkv, qkv, o, do, fk, lse)


def _shift_down(v, n):
    rows = lax.broadcasted_iota(jnp.int32, v.shape, 0)
    return jnp.where(rows >= n, pltpu.roll(v, n, axis=0), 0.0)


def _shift_up(v, n):
    S = v.shape[0]
    rows = lax.broadcasted_iota(jnp.int32, v.shape, 0)
    return jnp.where(rows < S - n, pltpu.roll(v, S - n, axis=0), 0.0)


def _conv_fwd(proj, conv_w, *, name, cb=LANES, deps=()):
    S, D3 = proj.shape
    D = D3 // 3
    nb = D // cb

    def body(bg_ref, cg_ref, u_ref, w_ref, z_ref):
        uc = cg_ref[...].astype(F32) * u_ref[...].astype(F32)
        w = w_ref[...]
        y = w[2:3, :] * uc + w[1:2, :] * _shift_down(uc, 1) + w[0:1, :] * _shift_down(uc, 2)
        z_ref[...] = (bg_ref[...].astype(F32) * y).astype(BF16)

    part = lambda g: pl.BlockSpec((S, cb), lambda j: (0, g * nb + j))
    return _call(
        body, deps, name=name, grid=(nb,),
        in_specs=[part(0), part(1), part(2), pl.BlockSpec((3, cb), lambda j: (0, j))],
        out_specs=pl.BlockSpec((S, cb), lambda j: (0, j)),
        out_shape=jax.ShapeDtypeStruct((S, D), BF16), compiler_params=_params(("parallel",)),
    )(proj, proj, proj, conv_w)


def _conv_bwd(proj, conv_w, dz, *, name, cb=LANES, deps=()):
    S, D3 = proj.shape
    D = D3 // 3
    nb = D // cb

    def body(bg_ref, cg_ref, u_ref, w_ref, dz_ref, dp_ref, dw_ref):
        cg, u = cg_ref[...].astype(F32), u_ref[...].astype(F32)
        uc = cg * u
        w = w_ref[...]
        uc1, uc2 = _shift_down(uc, 1), _shift_down(uc, 2)
        y = w[2:3, :] * uc + w[1:2, :] * uc1 + w[0:1, :] * uc2
        dz = dz_ref[...].astype(F32)
        dp_ref[0] = (dz * y).astype(BF16)
        dy = dz * bg_ref[...].astype(F32)
        duc = w[2:3, :] * dy + w[1:2, :] * _shift_up(dy, 1) + w[0:1, :] * _shift_up(dy, 2)
        dp_ref[1] = (duc * u).astype(BF16)
        dp_ref[2] = (duc * cg).astype(BF16)
        dw_ref[...] = jnp.zeros_like(dw_ref)
        dw_ref[0:1, :] = jnp.sum(dy * uc2, axis=0, keepdims=True)
        dw_ref[1:2, :] = jnp.sum(dy * uc1, axis=0, keepdims=True)
        dw_ref[2:3, :] = jnp.sum(dy * uc, axis=0, keepdims=True)

    part = lambda g: pl.BlockSpec((S, cb), lambda j: (0, g * nb + j))
    return _call(
        body, deps, name=name, grid=(nb,),
        in_specs=[part(0), part(1), part(2), pl.BlockSpec((3, cb), lambda j: (0, j)), pl.BlockSpec((S, cb), lambda j: (0, j))],
        out_specs=[pl.BlockSpec((3, S, cb), lambda j: (0, 0, j)), pl.BlockSpec((8, cb), lambda j: (0, j))],
        out_shape=[jax.ShapeDtypeStruct((3, S, D), BF16), jax.ShapeDtypeStruct((8, D), F32)],
        compiler_params=_params(("parallel",)),
    )(proj, proj, proj, conv_w, dz)


def _ada_fwd(c_all, ada_w, *, name, deps=()):
    L, D, Ns = ada_w.shape
    tn = _tile(Ns, 512)

    def body(c_ref, w_ref, o_ref, act_ref):
        cv = c_ref[...]
        act = cv * (1.0 / (1.0 + jnp.exp(-cv)))
        act_ref[...] = act
        o_ref[...] = jnp.dot(act.astype(BF16), w_ref[...].astype(BF16), preferred_element_type=F32)

    return _call(
        body, deps, name=name, grid=(L, Ns // tn),
        in_specs=[pl.BlockSpec((N_DEV, D), lambda l, j: (0, 0)), pl.BlockSpec((None, D, tn), lambda l, j: (l, 0, j))],
        out_specs=[pl.BlockSpec((None, N_DEV, tn), lambda l, j: (l, 0, j)), pl.BlockSpec((N_DEV, D), lambda l, j: (0, 0))],
        out_shape=[jax.ShapeDtypeStruct((L, N_DEV, Ns), F32), jax.ShapeDtypeStruct((N_DEV, D), F32)],
        compiler_params=_params(("arbitrary", "arbitrary")),
    )(c_all, ada_w)


def _select_mod(gathered, *, name, deps=()):
    _, LB, Ns = gathered.shape
    L = LB // N_DEV

    def body(g_ref, o_ref):
        x, y, c = _me()
        b = 4 * x + 2 * y + c
        for j in range(N_CHIPS):
            for l in range(L):
                o_ref[j, pl.ds(l, 1), :] = g_ref[2 * j + c, pl.ds(l * N_DEV + b, 1), :]

    return _call(
        body, deps, name=name, out_shape=jax.ShapeDtypeStruct((N_CHIPS, L, Ns), F32),
        in_specs=[pl.BlockSpec(memory_space=pltpu.VMEM)], out_specs=pl.BlockSpec(memory_space=pltpu.VMEM),
        compiler_params=_params(),
    )(gathered)


def _adamw_math(w, g, m, v):
    m = ADAM_B1 * m + (1.0 - ADAM_B1) * g
    v = ADAM_B2 * v + (1.0 - ADAM_B2) * jnp.square(g)
    m_hat = m / (1.0 - ADAM_B1 ** ADAM_STEP)
    v_hat = v / (1.0 - ADAM_B2 ** ADAM_STEP)
    delta = -ADAM_LR * (m_hat / (jnp.sqrt(v_hat) + ADAM_EPS) + ADAM_WD * w)
    return delta, m, v


def _adamw_shards(w, m, v, groups, chip, *, name):
    L, R, C = w.shape
    if R % 16 == 0:
        tr, tc = _tile(R, 128, 16), C
    else:
        tr, tc = R, _tile(C, 256)
    nr, nc = R // tr, C // tc

    def body(chip_ref, w_ref, m_ref, v_ref, *rest):
        srcs, (g_ref, d_ref, mo_ref, vo_ref) = rest[:2 * N_CHIPS * L], rest[2 * N_CHIPS * L:]
        for l in range(L):
            @pl.when(pl.program_id(0) == l)
            def _():
                s = srcs[2 * N_CHIPS * l:2 * N_CHIPS * (l + 1)]
                mine, other = s[0][...].astype(F32), s[N_CHIPS][...].astype(F32)
                for k in range(1, N_CHIPS):
                    mine = mine + s[k][...].astype(F32)
                    other = other + s[N_CHIPS + k][...].astype(F32)
                g = mine + other
                delta, mn, vn = _adamw_math(w_ref[...], g, m_ref[...], v_ref[...])
                g_ref[...] = g
                d_ref[...] = delta
                mo_ref[...] = mn
                vo_ref[...] = vn

    tile = pl.BlockSpec((None, tr, tc), lambda l, i, j, chip_ref: (l, i, j))

    def block(layer, k):
        def index(l, i, j, chip_ref):
            idle_i, idle_j = jnp.where(l < layer, 0, nr - 1), jnp.where(l < layer, 0, nc - 1)
            return (jnp.bitwise_xor(chip_ref[0], k), jnp.where(l == layer, i, idle_i), jnp.where(l == layer, j, idle_j))
        return pl.BlockSpec((None, tr, tc), index)

    in_specs, args = [tile] * 3, [w, m, v]
    for layer, (parts, land, sib) in enumerate(groups):
        in_specs += [block(layer, k) for k in range(N_CHIPS)] * 2
        args += [parts, land, land, land, sib, sib, sib, sib]
    return pl.pallas_call(
        body, name=name,
        grid_spec=pltpu.PrefetchScalarGridSpec(num_scalar_prefetch=1, grid=(L, nr, nc), in_specs=in_specs, out_specs=[tile] * 4),
        out_shape=[jax.ShapeDtypeStruct((L, R, C), F32)] * 4, compiler_params=_params(("arbitrary", "arbitrary", "arbitrary")),
    )(chip, *args)


def _adamw_ada(w, m, v, act_t, dmod, *, name, tr=256, deps=()):
    L, D, Ns = w.shape
    tr = _tile(D, tr, 8)

    def body(w_ref, m_ref, v_ref, a_ref, d_ref, g_ref, dl_ref, mo_ref, vo_ref):
        x, y, _ = _me()
        g = jnp.dot(a_ref[...], d_ref[2 * x + y], preferred_element_type=F32, precision=lax.Precision.HIGHEST)
        delta, mn, vn = _adamw_math(w_ref[...], g, m_ref[...], v_ref[...])
        g_ref[...] = g
        dl_ref[...] = delta
        mo_ref[...] = mn
        vo_ref[...] = vn

    tile = pl.BlockSpec((None, tr, Ns), lambda l, i: (l, i, 0))
    return _call(
        body, deps, name=name, grid=(L, D // tr),
        in_specs=[tile] * 3 + [pl.BlockSpec((tr, N_DEV), lambda l, i: (i, 0)),
                               pl.BlockSpec((N_CHIPS, None, N_DEV, Ns), lambda l, i: (0, l, 0, 0))],
        out_specs=[tile] * 4, out_shape=[jax.ShapeDtypeStruct((L, D, Ns), F32)] * 4,
        compiler_params=_params(("parallel", "parallel")),
    )(w, m, v, act_t, dmod)


def _adamw_small(w, m, v, gathered, *, rows, name, deps=()):
    n, D = w.shape

    def body(w_ref, m_ref, v_ref, s_ref, g_ref, d_ref, mo_ref, vo_ref):
        for r, src in enumerate(rows):
            g = s_ref[0, src:src + 1, :]
            for d in range(1, N_DEV):
                g = g + s_ref[d, src:src + 1, :]
            g_ref[r:r + 1, :] = g
        g = g_ref[...]
        delta, mn, vn = _adamw_math(w_ref[...], g, m_ref[...], v_ref[...])
        d_ref[...] = delta
        mo_ref[...] = mn
        vo_ref[...] = vn

    vm = pl.BlockSpec(memory_space=pltpu.VMEM)
    return _call(
        body, deps, name=name, in_specs=[vm] * 4, out_specs=[vm] * 4,
        out_shape=[jax.ShapeDtypeStruct((n, D), F32)] * 4, compiler_params=_params(),
    )(w, m, v, gathered)


def _adamw_conv_w(w, m, v, gathered4, *, name, deps=()):
    Cs = w.shape[1]

    def body(w_ref, m_ref, v_ref, s_ref, g_ref, d_ref, mo_ref, vo_ref):
        x, y, _ = _me()
        j = 2 * x + y
        g = s_ref[j, 0]
        for d in range(1, N_DEV):
            g = g + s_ref[j, d]
        delta, mn, vn = _adamw_math(w_ref[...], g, m_ref[...], v_ref[...])
        g_ref[...] = g
        d_ref[...] = delta
        mo_ref[...] = mn
        vo_ref[...] = vn

    vm = pl.BlockSpec(memory_space=pltpu.VMEM)
    return _call(
        body, deps, name=name, in_specs=[vm] * 4, out_specs=[vm] * 4,
        out_shape=[jax.ShapeDtypeStruct((8, Cs), F32)] * 4, compiler_params=_params(),
    )(w, m, v, gathered4)


def _loss_sum(gathered, *, row, name, deps=()):
    _, _, D = gathered.shape

    def body(s_ref, o_ref):
        t = s_ref[0, row:row + 1, :]
        for d in range(1, N_DEV):
            t = t + s_ref[d, row:row + 1, :]
        o_ref[...] = jnp.broadcast_to(t, (8, D))

    vm = pl.BlockSpec(memory_space=pltpu.VMEM)
    return pl.pallas_call(body, name=name, in_specs=[vm], out_specs=vm, out_shape=jax.ShapeDtypeStruct((8, D), F32),
                          compiler_params=_params())(gathered)


def _pad_rows(a, n):
    return jnp.pad(a, ((0, n - a.shape[0]), (0, 0)))


def kernel(x, c, ada_w, ada_b, norm_mix, norm_mlp, fox_w_in, fox_b_f, fox_w_out, conv_w_in, conv_w, conv_w_out, mlp_w_up, mlp_w_down, final_norm, loss_target, m_ada_w, m_ada_b, m_norm_mix, m_norm_mlp, m_fox_w_in, m_fox_b_f, m_fox_w_out, m_conv_w_in, m_conv_w, m_conv_w_out, m_mlp_w_up, m_mlp_w_down, m_final_norm, v_ada_w, v_ada_b, v_norm_mix, v_norm_mlp, v_fox_w_in, v_fox_b_f, v_fox_w_out, v_conv_w_in, v_conv_w, v_conv_w_out, v_mlp_w_up, v_mlp_w_down, v_final_norm):
    S, D = x.shape[1], x.shape[2]
    H = fox_b_f.shape[-1]
    L = ada_w.shape[0]
    NM = ada_b.shape[1] // D
    Ns_ada = ada_w.shape[2]
    Cs_fox = fox_w_in.shape[2]
    Cs_conv = conv_w.shape[2]
    x0 = x[0]
    target = loss_target[0]

    chip = (2 * lax.axis_index("x") + lax.axis_index("y")).astype(jnp.int32).reshape(1)

    fin_t = jnp.transpose(fox_w_in, (0, 2, 1))
    shards = dict(fin=(fin_t, 0), fout=(fox_w_out, 0), up0=(mlp_w_up, 0), dn0=(mlp_w_down, 0), cin=(conv_w_in, 0),
                  cout=(conv_w_out, 0), up1=(mlp_w_up, 1), dn1=(mlp_w_down, 1))
    halves = dict(fin="cols", fout="rows", up0="cols", dn0="rows", cin="cols", cout="rows", up1="cols", dn1="rows")
    gathers, placed = {}, {}

    def place(key, dep=None):
        placed[key] = _place_cast(*shards[key], chip, halves=halves[key], name="place_" + key, dep=dep)
        return placed[key]

    def start_gather(key, dep=None):
        gathers[key] = _split_start("gather1", [(placed[key],)], name="gather_start_" + key, dep=dep)
        return gathers[key][3]

    def pass_gather(key, after):
        landed = _split_wait("gather1", gathers[key], after, name="gather_landed_" + key)
        gathers[key] = _split_start("gather2", landed, name="gather_pass_" + key)
        return gathers[key][3]

    def gathered(key, after):
        return _split_wait("gather2", gathers[key], after, name="gather_wait_" + key)[0][0]

    fin_placed = place("fin")
    c_all = _allgather8(_pad_rows(c, 8), name="gather_c", deps=(fin_placed,))[:, 0, :]
    mod_part, c_act = _ada_fwd(c_all, ada_w, name="ada_fwd")
    mod_all = _allgather8(mod_part.reshape(L * N_DEV, Ns_ada), name="gather_mod")
    conv_w_all = _allgather8(_pad_rows(conv_w[0], 8), name="gather_conv_w", deps=(mod_all,))
    conv_w_full = jnp.transpose(conv_w_all[0::2, :3, :], (1, 0, 2)).reshape(3, D)

    tok = start_gather("fin", conv_w_all)
    for key in ("fout", "up0", "dn0", "cin", "cout", "up1", "dn1"):
        place(key, tok)
        tok = start_gather(key, tok)

    mod = _select_mod(mod_all, name="select_mod", deps=(tok,))
    mod = jnp.transpose(mod, (1, 0, 2)).reshape(L, NM, 1, D) + ada_b.reshape(L, NM, 1, D)

    def vec(a):
        return a.reshape(1, D)

    h0 = _norm_fwd(x0, vec(norm_mix[0]), mod[0, 1], mod[0, 0], name="norm_mix0")
    tok = pass_gather("fin", [h0])
    w_fin_t = jnp.transpose(gathered("fin", [tok]), (0, 2, 1, 3)).reshape(N_CHIPS * Cs_fox, D)
    w_f_t = _pad_rows(w_fin_t[3 * D:], LANES)
    tok = pass_gather("fout", [w_fin_t])
    qkv = _mm_nt(h0, w_fin_t, n=3 * D, name="fox_in", out_dtype=BF16, deps=(tok,))
    tok = pass_gather("up0", [qkv])
    b_f = jnp.pad(fox_b_f, ((0, 0), (0, LANES - H)))
    z_f, F_col = _fgate_fwd(h0, w_f_t, b_f, name="fgate_fwd", deps=(tok,))
    hp = max(8, H)
    at_f, at = _tile(S, 1024, 16), _tile(S, 1024, 16)
    F_rows = _pad_rows(jnp.transpose(F_col[:, :H]), hp)
    F_row = jnp.transpose(F_rows.reshape(hp, S // at, at), (1, 0, 2))
    o, lse = _attn_fwd(qkv, jnp.transpose(F_rows.reshape(hp, S // at_f, at_f), (1, 0, 2)), heads=H, name="attn_fwd", T=at_f)
    w_fout = gathered("fout", [o]).reshape(D, D)
    tok = pass_gather("dn0", [o])
    x1, mix0, h1 = _mm_nn(o, w_fout, name="fox_out", epilogue="resid_norm", res=x0, gate=mod[0, 2],
                          norm=(vec(norm_mlp[0]), mod[0, 4], mod[0, 3]), tm=512, tn=D, deps=(tok,))
    w_up0 = gathered("up0", [h1]).reshape(2 * N_CHIPS, D, -1)
    tok = pass_gather("cin", [h1])
    u0, a0 = _mm_nn(h1, w_up0, name="mlp_up0", epilogue="relu2", deps=(tok,))
    w_dn0 = gathered("dn0", [a0]).reshape(-1, D)
    tok = pass_gather("cout", [a0])
    x2, y0 = _mm_nn(a0, w_dn0, name="mlp_down0", epilogue="resid", res=x1, gate=mod[0, 5], deps=(tok,))
    h2 = _norm_fwd(x2, vec(norm_mix[1]), mod[1, 1], mod[1, 0], name="norm_mix1")
    g_cin = gathered("cin", [h2]).reshape(2 * N_CHIPS, D, -1)
    tok = pass_gather("up1", [h2])
    proj = _mm_nn(h2, g_cin, name="conv_in", deps=(tok,))
    w_cin = jnp.transpose(g_cin, (1, 0, 2)).reshape(D, 3 * D)
    zc = _conv_fwd(proj, conv_w_full, name="conv_fwd")
    w_cout = gathered("cout", [zc]).reshape(D, D)
    tok = pass_gather("dn1", [zc])
    x3, mix1, h3 = _mm_nn(zc, w_cout, name="conv_out", epilogue="resid_norm", res=x2, gate=mod[1, 2],
                          norm=(vec(norm_mlp[1]), mod[1, 4], mod[1, 3]), tm=512, tn=D, deps=(tok,))
    w_up1 = gathered("up1", [h3]).reshape(2 * N_CHIPS, D, -1)
    u1, a1 = _mm_nn(h3, w_up1, name="mlp_up1", epilogue="relu2")
    w_dn1 = gathered("dn1", [a1]).reshape(-1, D)
    x4, y1 = _mm_nn(a1, w_dn1, name="mlp_down1", epilogue="resid", res=x3, gate=mod[1, 5])
    w_up = [jnp.transpose(w_up0, (1, 0, 2)).reshape(D, -1), jnp.transpose(w_up1, (1, 0, 2)).reshape(D, -1)]
    w_dn = [w_dn0, w_dn1]

    dx4, dy1, sums_f = _loss_bwd(x4, target, vec(final_norm), mod[1, 5], name="loss_bwd")
    du1 = _mm_nt(dy1, w_dn[1], name="mlp_down1_dx", epilogue="drelu2", extra=u1, out_dtype=BF16)
    def start_scatter(tag, parts_list):
        groups = [(p, lax.empty(p.shape, p.dtype)) for p in parts_list]
        return _split_start("scatter", groups, name="scatter_start_" + tag)

    def start_sibling(tag, scatter, after):
        landed = _split_wait("scatter", scatter, after, name="scatter_wait_" + tag)
        groups = [(p, ld, lax.empty(p.shape, p.dtype)) for p, ld in landed]
        return _split_start("sibling", groups, name="sibling_start_" + tag)

    gw_dn1 = _mm_tn(a1, dy1, name="mlp_down1_dw")
    gw_up1 = _mm_tn(h3, du1, name="mlp_up1_dw", out_parts=N_CHIPS)
    sc1 = start_scatter("mlp1", [gw_dn1.reshape(N_CHIPS, -1, D), gw_up1])
    dh3 = _mm_nt(du1, w_up[1], name="mlp_up1_dx", out_dtype=BF16, tk=4096, deps=(sc1[3],))
    dx3, dmix1, sums_mlp1 = _norm_bwd(x3, dh3, dx4, y1, vec(norm_mlp[1]), mod[1, 4], mod[1, 2], name="norm_mlp1_bwd")
    dzc = _mm_nt(dmix1, w_cout, name="conv_out_dx", out_dtype=BF16)
    gw_cout = _mm_tn(zc, dmix1, name="conv_out_dw")
    dproj, dconv_w = _conv_bwd(proj, conv_w_full, dzc, name="conv_bwd")
    gw_cin = _mm_tn(h2, dproj, name="conv_in_dw", out_parts=N_CHIPS, tn=512)
    sc2 = start_scatter("conv", [gw_cout.reshape(N_CHIPS, -1, D), gw_cin])
    dh2 = _mm_nt(dproj, w_cin, name="conv_in_dx", out_dtype=BF16, deps=(sc2[3],))
    dx2, dy0, sums_mix1 = _norm_bwd(x2, dh2, dx3, mix1, vec(norm_mix[1]), mod[1, 1], mod[0, 5], name="norm_mix1_bwd")
    du0 = _mm_nt(dy0, w_dn[0], name="mlp_down0_dx", epilogue="drelu2", extra=u0, out_dtype=BF16)
    gw_dn0 = _mm_tn(a0, dy0, name="mlp_down0_dw")
    gw_up0 = _mm_tn(h1, du0, name="mlp_up0_dw", out_parts=N_CHIPS)
    sc3 = start_scatter("mlp0", [gw_dn0.reshape(N_CHIPS, -1, D), gw_up0])
    sb1 = start_sibling("mlp1", sc1, [sc3[3]])
    dh1 = _mm_nt(du0, w_up[0], name="mlp_up0_dx", out_dtype=BF16, tk=4096, deps=(sb1[3],))
    dx1, dmix0, sums_mlp0 = _norm_bwd(x1, dh1, dx2, y0, vec(norm_mlp[0]), mod[0, 4], mod[0, 2], name="norm_mlp0_bwd")
    do = _mm_nt(dmix0, w_fout, name="fox_out_dx", out_dtype=BF16)
    gw_fout = _mm_tn(o, dmix0, name="fox_out_dw")
    dqkv, dfq, dfk = _attn_bwd(qkv, o, do, F_row, lse, heads=H, name="attn_bwd", T=at)
    dfk_col = jnp.pad(jnp.transpose(jnp.transpose(dfk, (1, 0, 2)).reshape(hp, S)[:H]), ((0, 0), (0, LANES - H)))
    sb3 = start_sibling("mlp0", sc3, [dqkv])
    dz_f, sums_bf = _fgate_bwd(dfq, dfk_col, z_f, name="fgate_bwd", deps=(sb3[3],))
    gw_qkv_t = _mm_tn(dqkv, h0, name="fox_in_dw")
    gw_f_t = _mm_tn(dz_f, h0, name="fox_gate_dw")
    gw_fin_t = jnp.concatenate([gw_qkv_t, gw_f_t[:H]], axis=0).reshape(N_CHIPS, Cs_fox, D)
    sc4 = start_scatter("fox", [gw_fout.reshape(N_CHIPS, -1, D), gw_fin_t])
    sb2 = start_sibling("conv", sc2, [sc4[3]])
    dh0_f = _mm_nn(dz_f, w_f_t, name="fox_gate_dx", out_dtype=F32, deps=(sb2[3],))
    dh0 = _mm_nn(dqkv, w_fin_t, name="fox_in_dx", epilogue="add", res=dh0_f, out_dtype=BF16)
    grad_x, _, sums_mix0 = _norm_bwd(x0, dh0, dx1, mix0, vec(norm_mix[0]), mod[0, 1], None, name="norm_mix0_bwd")

    outs = {}

    def put(name_, res, shape):
        for kind, r in zip(("grad", "delta", "new_m", "new_v"), res):
            outs[kind + "_" + name_] = r.reshape(shape)

    def shards_update(tag, w_, m_, v_, groups):
        return _adamw_shards(w_, m_, v_, groups, chip, name="adamw_" + tag)

    g_conv = _split_wait("sibling", sb2, [grad_x], name="sibling_wait_conv")
    put("conv_w_out", shards_update("conv_out", conv_w_out, m_conv_w_out, v_conv_w_out, g_conv[0:1]), conv_w_out.shape)
    r_cin = shards_update("conv_in", conv_w_in, m_conv_w_in, v_conv_w_in, g_conv[1:2])
    put("conv_w_in", r_cin, conv_w_in.shape)
    g_mlp1 = _split_wait("sibling", sb1, [r_cin[0]], name="sibling_wait_mlp1")
    g_mlp0 = _split_wait("sibling", sb3, [r_cin[0]], name="sibling_wait_mlp0")
    put("mlp_w_down", shards_update("mlp_down", mlp_w_down, m_mlp_w_down, v_mlp_w_down, [g_mlp0[0], g_mlp1[0]]), mlp_w_down.shape)
    r_up = shards_update("mlp_up", mlp_w_up, m_mlp_w_up, v_mlp_w_up, [g_mlp0[1], g_mlp1[1]])
    put("mlp_w_up", r_up, mlp_w_up.shape)
    sb4 = start_sibling("fox", sc4, [r_up[0]])

    dmod_rows = []
    for sm, sl in ((sums_mix0, sums_mlp0), (sums_mix1, sums_mlp1)):
        dmod_rows += [sm[0:1], sm[1:2], sm[3:4], sl[0:1], sl[1:2], sl[3:4]]
    bf_row = jnp.pad(sums_bf[0:1], ((0, 0), (0, D - LANES)))
    small = jnp.concatenate([sums_mix0[2:3], sums_mix1[2:3], sums_mlp0[2:3], sums_mlp1[2:3], sums_f[0:1], sums_f[1:2], bf_row,
                             jnp.zeros((1, D), F32)] + dmod_rows + [dconv_w[0:3]], axis=0)
    small_all = _allgather8(_pad_rows(small, -(-small.shape[0] // 8) * 8), name="gather_small", deps=(sb4[3],))
    loss = _loss_sum(small_all, row=5, name="loss_sum")[0, 0]

    def rows_of(a_mix, a_mlp, a_fin, a_bf, a_ada):
        return jnp.concatenate([a_mix, a_mlp, a_fin.reshape(1, D), jnp.pad(a_bf, ((0, 0), (0, D - H))),
                                a_ada.reshape(L * NM, D)], axis=0)
    n_small = 2 * L + 2 + L * NM
    rw = -(-n_small // 8) * 8
    w_s = _pad_rows(rows_of(norm_mix, norm_mlp, final_norm, fox_b_f, ada_b), rw)
    m_s = _pad_rows(rows_of(m_norm_mix, m_norm_mlp, m_final_norm, m_fox_b_f, m_ada_b), rw)
    v_s = _pad_rows(rows_of(v_norm_mix, v_norm_mlp, v_final_norm, v_fox_b_f, v_ada_b), rw)
    src_rows = [0, 1, 2, 3, 4, 6] + [8 + r for r in range(L * NM)] + [7] * (rw - n_small)
    res_s = _adamw_small(w_s, m_s, v_s, small_all, rows=tuple(src_rows), name="adamw_small")
    for kind, r in zip(("grad", "delta", "new_m", "new_v"), res_s):
        outs[kind + "_norm_mix"] = r[0:L]
        outs[kind + "_norm_mlp"] = r[L:2 * L]
        outs[kind + "_final_norm"] = r[2 * L]
        outs[kind + "_fox_b_f"] = r[2 * L + 1:2 * L + 2, :H]
        outs[kind + "_ada_b"] = r[2 * L + 2:n_small].reshape(L, NM * D)

    dmod_all = small_all[:, 8:8 + L * NM, :].reshape(N_DEV, L, N_CHIPS, Ns_ada)
    dmod4 = jnp.transpose(dmod_all, (2, 1, 0, 3))
    act_t = jnp.transpose(c_act)
    res_a = _adamw_ada(ada_w, m_ada_w, v_ada_w, act_t, dmod4, name="adamw_ada")
    put("ada_w", res_a, ada_w.shape)

    r0 = 8 + L * NM
    dconv_all = jnp.pad(small_all[:, r0:r0 + 3, :], ((0, 0), (0, 5), (0, 0)))
    dconv4 = jnp.transpose(dconv_all.reshape(N_DEV, 8, N_CHIPS, Cs_conv), (2, 0, 1, 3))
    res_c = _adamw_conv_w(_pad_rows(conv_w[0], 8), _pad_rows(m_conv_w[0], 8), _pad_rows(v_conv_w[0], 8), dconv4,
                          name="adamw_conv_w")
    for kind, r in zip(("grad", "delta", "new_m", "new_v"), res_c):
        outs[kind + "_conv_w"] = r[:3].reshape(conv_w.shape)

    g_fox = _split_wait("sibling", sb4, [res_a[0], res_c[0], res_s[0]], name="sibling_wait_fox")
    put("fox_w_out", shards_update("fox_out", fox_w_out, m_fox_w_out, v_fox_w_out, g_fox[0:1]), fox_w_out.shape)
    t3 = lambda a: jnp.transpose(a, (0, 2, 1))
    for kind, r in zip(("grad", "delta", "new_m", "new_v"),
                       shards_update("fox_in", t3(fox_w_in), t3(m_fox_w_in), t3(v_fox_w_in), g_fox[1:2])):
        outs[kind + "_fox_w_in"] = t3(r)

    names = ["ada_w", "ada_b", "norm_mix", "norm_mlp", "fox_w_in", "fox_b_f", "fox_w_out", "conv_w_in", "conv_w", "conv_w_out",
             "mlp_w_up", "mlp_w_down", "final_norm"]
    return (loss, grad_x[None], *[outs["grad_" + n] for n in names], *[outs["delta_" + n] for n in names],
            *[outs["new_m_" + n] for n in names], *[outs["new_v_" + n] for n in names])
```

```python
import jax
import jax.numpy as jnp
from jax import lax
from jax.experimental import pallas as pl
from jax.experimental.pallas import tpu as pltpu

F32 = jnp.float32
BF16 = jnp.bfloat16
MESH = pl.DeviceIdType.MESH
ANY = pl.BlockSpec(memory_space=pl.ANY)
HBM = pl.BlockSpec(memory_space=pltpu.HBM)
SEM = pl.BlockSpec(memory_space=pltpu.SEMAPHORE)
EFFECT = pltpu.SideEffectType.DATAFLOW_SIDE_EFFECTING

RMS_EPS = 1e-6
ADAM_LR = 0.001
ADAM_B1 = 0.9
ADAM_B2 = 0.999
ADAM_EPS = 1e-08
ADAM_WD = 0.01
ADAM_STEP = 10
N_CHIPS = 4
N_DEV = 8
LANES = 128
VMEM_LIMIT = 56 * 1024 * 1024
NEG = -1e30


def _params(sems=None, vmem=VMEM_LIMIT):
    return pltpu.CompilerParams(dimension_semantics=sems, vmem_limit_bytes=vmem)


def _tile(n, pref, unit=LANES):
    if n <= pref:
        return n
    t = (pref // unit) * unit
    while n % t:
        t -= unit
    return t


def _me():
    return lax.axis_index("x"), lax.axis_index("y"), lax.axis_index("c")


def _call(body, deps, **kw):
    nd = len(deps)

    def wrapped(*refs):
        body(*refs[nd:])

    kw["in_specs"] = [ANY] * nd + list(kw["in_specs"])
    fn = pl.pallas_call(wrapped, **kw)
    return lambda *args: fn(*deps, *args)


def _allgather8(v, *, name, deps=()):
    R, C = v.shape

    def body(v_ref, out_ref, send_sems, recv_sems):
        x, y, c = _me()
        me = 4 * x + 2 * y + c
        out_ref[me] = v_ref[...]
        copies = []
        for k in range(1, N_DEV):
            px, py, pc = (x + (k >> 2)) % 2, (y + ((k >> 1) & 1)) % 2, (c + (k & 1)) % 2
            copies.append(pltpu.make_async_remote_copy(
                src_ref=v_ref, dst_ref=out_ref.at[me], send_sem=send_sems.at[k - 1], recv_sem=recv_sems.at[k - 1],
                device_id=(px, py, pc), device_id_type=MESH))
        for cp in copies:
            cp.start()
        for k in range(1, N_DEV):
            px, py, pc = (x + (k >> 2)) % 2, (y + ((k >> 1) & 1)) % 2, (c + (k & 1)) % 2
            peer = 4 * px + 2 * py + pc
            pltpu.make_async_remote_copy(
                src_ref=v_ref, dst_ref=out_ref.at[peer], send_sem=send_sems.at[k - 1], recv_sem=recv_sems.at[k - 1],
                device_id=(px, py, pc), device_id_type=MESH).wait_recv()
        for cp in copies:
            cp.wait_send()

    return _call(
        body, deps, name=name,
        out_shape=jax.ShapeDtypeStruct((N_DEV, R, C), v.dtype),
        in_specs=[pl.BlockSpec(memory_space=pltpu.VMEM)],
        out_specs=pl.BlockSpec(memory_space=pltpu.VMEM),
        scratch_shapes=[pltpu.SemaphoreType.DMA((N_DEV - 1,)), pltpu.SemaphoreType.DMA((N_DEV - 1,))],
        compiler_params=_params(),
    )(v)


def _chip_peers(x, y):
    return [((x + (k >> 1)) % 2, (y + (k & 1)) % 2) for k in range(1, N_CHIPS)]


def _slot(x, y, k):
    return 2 * ((x + (k >> 1)) % 2) + (y + (k & 1)) % 2


def _split_copies(kind, groups, send_sems, recv_sems):
    x, y, c = _me()
    j = 2 * x + y
    copies = []
    for a, g in enumerate(groups):
        if kind == "sibling":
            parts, land, sib = g
            for k in range(N_CHIPS):
                s = _slot(x, y, k)
                copies.append(pltpu.make_async_remote_copy(
                    src_ref=(parts if k == 0 else land).at[s], dst_ref=sib.at[s], send_sem=send_sems.at[N_CHIPS * a + k],
                    recv_sem=recv_sems.at[N_CHIPS * a + k], device_id=(x, y, 1 - c), device_id_type=MESH))
            continue
        land = g[-1]
        for k, (px, py) in enumerate(_chip_peers(x, y)):
            if kind == "gather1":
                src, dst, to = land.at[j, c], land.at[j, c], (px, py, c)
            elif kind == "gather2":
                src, dst, to = land.at[2 * px + py, c], land.at[2 * px + py, c], (x, y, 1 - c)
            else:
                src, dst, to = g[0].at[2 * px + py], land.at[j], (px, py, c)
            copies.append(pltpu.make_async_remote_copy(
                src_ref=src, dst_ref=dst, send_sem=send_sems.at[3 * a + k], recv_sem=recv_sems.at[3 * a + k],
                device_id=to, device_id_type=MESH))
    return copies


def _split_start(kind, groups, *, name, dep=None):
    flat = [a for g in groups for a in g]
    nf, per = len(flat), len(groups[0])
    ncp = len(groups) * (N_CHIPS if kind == "sibling" else 3)
    nd = 0 if dep is None else 1

    def body(*refs):
        ins = refs[nd:nd + nf]
        send_sems, recv_sems, token = refs[nd + nf], refs[nd + nf + 1], refs[-1]
        for cp in _split_copies(kind, [ins[i:i + per] for i in range(0, nf, per)], send_sems, recv_sems):
            cp.start()
        token[...] = jnp.zeros_like(token)

    outs = pl.pallas_call(
        body, name=name,
        out_shape=(pltpu.SemaphoreType.DMA((ncp,)), pltpu.SemaphoreType.DMA((ncp,)), *[pltpu.HBM(a.shape, a.dtype) for a in flat],
                   jax.ShapeDtypeStruct((8, LANES), F32)),
        in_specs=[ANY] * nd + [HBM] * nf,
        out_specs=(SEM, SEM, *[HBM] * nf, pl.BlockSpec(memory_space=pltpu.VMEM)),
        input_output_aliases={nd + i: 2 + i for i in range(nf)},
        compiler_params=pltpu.CompilerParams(has_side_effects=EFFECT),
    )(*([dep] if nd else []), *[pltpu.with_memory_space_constraint(a, pltpu.HBM) for a in flat])
    thru = list(outs[2:2 + nf])
    return outs[0], outs[1], [tuple(thru[i:i + per]) for i in range(0, nf, per)], outs[-1]


def _split_wait(kind, started, after, *, name):
    send_sems, recv_sems, groups, _ = started
    flat = [a for g in groups for a in g]
    nf, per = len(flat), len(groups[0])

    def body(*refs):
        ins = refs[:nf]
        for cp in _split_copies(kind, [ins[i:i + per] for i in range(0, nf, per)], refs[nf], refs[nf + 1]):
            cp.wait_send()
            cp.wait_recv()

    outs = pl.pallas_call(
        body, name=name,
        out_shape=tuple(pltpu.HBM(a.shape, a.dtype) for a in flat),
        in_specs=[HBM] * nf + [SEM, SEM] + [ANY] * len(after), out_specs=tuple([HBM] * nf),
        input_output_aliases={i: i for i in range(nf)},
        compiler_params=pltpu.CompilerParams(has_side_effects=EFFECT),
    )(*flat, send_sems, recv_sems, *after)
    outs = list(outs)
    return [tuple(outs[i:i + per]) for i in range(0, nf, per)]


def _place_cast(shards, layer, chip, *, halves, name, dep=None):
    _, R, C = shards.shape
    if halves == "rows":
        hr, hc = R // 2, C
    else:
        hr, hc = R, C // 2
    if hr % 16 == 0:
        tr, tc = _tile(hr, 512, 16), hc
    else:
        tr, tc = hr, _tile(hc, 256)
    nr, nc = hr // tr, hc // tc

    def body(chip_ref, x_ref, *rest):
        rest[-1][...] = x_ref[...].astype(BF16)

    if halves == "rows":
        o_map = lambda i, j, chip_ref: (chip_ref[0], i // nr, i % nr, j)
    else:
        o_map = lambda i, j, chip_ref: (chip_ref[0], j // nc, i, j % nc)
    return pl.pallas_call(
        body, name=name,
        grid_spec=pltpu.PrefetchScalarGridSpec(
            num_scalar_prefetch=1, grid=(R // tr, C // tc),
            in_specs=[pl.BlockSpec((None, tr, tc), lambda i, j, chip_ref: (layer, i, j))] + ([] if dep is None else [ANY]),
            out_specs=pl.BlockSpec((None, None, tr, tc), o_map)),
        out_shape=jax.ShapeDtypeStruct((N_CHIPS, 2, hr, hc), BF16), compiler_params=_params(("parallel", "parallel")),
    )(chip, shards, *([] if dep is None else [dep]))


def _accumulate(part, acc_ref, nk, finalize):
    if nk == 1:
        finalize(part)
        return
    k = pl.program_id(2)

    @pl.when(k == 0)
    def _():
        acc_ref[...] = part

    @pl.when(k > 0)
    def _():
        acc_ref[...] += part

    @pl.when(k == nk - 1)
    def _():
        finalize(acc_ref[...])


def _mm_nn(a, b, *, name, epilogue="plain", res=None, gate=None, norm=None, out_dtype=BF16, tm=1024, tn=1024, tk=2048, deps=()):
    if a.ndim == 3:
        Q, M, Kq = a.shape
        K = Q * Kq
    else:
        (M, K), Kq = a.shape, a.shape[1]
    tm, tk = _tile(M, tm, 16), _tile(Kq, tk)
    if a.ndim == 3:
        pa = Kq // tk
        a_spec = pl.BlockSpec((None, tm, tk), lambda i, j, k: (k // pa, i, k % pa))
    else:
        a_spec = pl.BlockSpec((tm, tk), lambda i, j, k: (i, k))
    if b.ndim == 3:
        P, _, Ns = b.shape
        N = P * Ns
        tn = _tile(Ns, tn)
        per = Ns // tn
        b_spec = pl.BlockSpec((None, tk, tn), lambda i, j, k: (j // per, k, j % per))
    else:
        N = b.shape[1]
        tn = _tile(N, tn)
        b_spec = pl.BlockSpec((tk, tn), lambda i, j, k: (k, j))
    nk = K // tk
    tile = pl.BlockSpec((tm, tn), lambda i, j, k: (i, j))

    def body(*refs):
        acc_ref = refs[-1] if nk > 1 else None
        a_ref, b_ref = refs[0], refs[1]
        part = jnp.dot(a_ref[...], b_ref[...], preferred_element_type=F32)
        if epilogue == "plain":
            def fin(acc):
                refs[2][...] = acc.astype(out_dtype)
        elif epilogue == "relu2":
            def fin(acc):
                refs[2][...] = acc.astype(BF16)
                refs[3][...] = jnp.square(jnp.maximum(acc, 0.0)).astype(BF16)
        elif epilogue == "add":
            def fin(acc):
                refs[3][...] = (acc + refs[2][...]).astype(out_dtype)
        elif epilogue == "resid":
            def fin(acc):
                refs[4][...] = refs[2][...] + refs[3][...] * acc
                refs[5][...] = acc.astype(BF16)
        else:
            def fin(acc):
                xv = refs[2][...] + refs[3][...] * acc
                refs[7][...] = xv
                refs[8][...] = acc.astype(BF16)
                r = lax.rsqrt(jnp.mean(xv * xv, axis=-1, keepdims=True) + RMS_EPS)
                refs[9][...] = (((xv * r) * refs[4][...]) * (1.0 + refs[5][...]) + refs[6][...]).astype(BF16)
        _accumulate(part, acc_ref, nk, fin)

    in_specs = [a_spec, b_spec]
    args = [a, b]
    if epilogue == "plain":
        out_shape, out_specs = jax.ShapeDtypeStruct((M, N), out_dtype), tile
    elif epilogue == "relu2":
        out_shape, out_specs = [jax.ShapeDtypeStruct((M, N), BF16)] * 2, [tile, tile]
    elif epilogue == "add":
        in_specs.append(tile)
        args.append(res)
        out_shape, out_specs = jax.ShapeDtypeStruct((M, N), out_dtype), tile
    else:
        row = pl.BlockSpec((1, tn), lambda i, j, k: (0, j))
        in_specs += [tile, row]
        args += [res, gate]
        out_shape, out_specs = [jax.ShapeDtypeStruct((M, N), F32), jax.ShapeDtypeStruct((M, N), BF16)], [tile, tile]
        if epilogue == "resid_norm":
            assert tn == N, "the next norm needs whole rows"
            in_specs += [row, row, row]
            args += list(norm)
            out_shape, out_specs = out_shape + [jax.ShapeDtypeStruct((M, N), BF16)], out_specs + [tile]
    return _call(
        body, deps, name=name, grid=(M // tm, N // tn, nk), in_specs=in_specs, out_specs=out_specs, out_shape=out_shape,
        scratch_shapes=[pltpu.VMEM((tm, tn), F32)] if nk > 1 else [],
        compiler_params=_params(("parallel", "parallel", "arbitrary")),
    )(*args)


def _mm_nt(a, b, *, name, n=None, epilogue="plain", extra=None, out_dtype=F32, tm=1024, tn=1024, tk=2048, deps=()):
    if a.ndim == 3:
        Q, M, Kq = a.shape
        K = Q * Kq
    else:
        (M, K), Kq = a.shape, a.shape[1]
    if b.ndim == 3:
        P, N, Ks = b.shape
    else:
        N, Ks = b.shape
    N = n or N
    tm, tn, tk = _tile(M, tm, 16), _tile(N, tn), _tile(min(Kq, Ks), tk)
    nk = K // tk
    if a.ndim == 3:
        pa = Kq // tk
        a_spec = pl.BlockSpec((None, tm, tk), lambda i, j, k: (k // pa, i, k % pa))
    else:
        a_spec = pl.BlockSpec((tm, tk), lambda i, j, k: (i, k))
    if b.ndim == 3:
        pb = Ks // tk
        b_spec = pl.BlockSpec((None, tn, tk), lambda i, j, k: (k // pb, j, k % pb))
    else:
        b_spec = pl.BlockSpec((tn, tk), lambda i, j, k: (j, k))
    tile = pl.BlockSpec((tm, tn), lambda i, j, k: (i, j))

    def body(*refs):
        acc_ref = refs[-1] if nk > 1 else None
        part = lax.dot_general(refs[0][...], refs[1][...], (((1,), (1,)), ((), ())), preferred_element_type=F32)
        if epilogue == "plain":
            def fin(acc):
                refs[2][...] = acc.astype(out_dtype)
        elif epilogue == "add":
            def fin(acc):
                refs[3][...] = (acc + refs[2][...]).astype(out_dtype)
        else:
            def fin(acc):
                refs[3][...] = (acc * (2.0 * jnp.maximum(refs[2][...].astype(F32), 0.0))).astype(out_dtype)
        _accumulate(part, acc_ref, nk, fin)

    in_specs, args = [a_spec, b_spec], [a, b]
    if epilogue != "plain":
        in_specs.append(tile)
        args.append(extra)
    return _call(
        body, deps, name=name, grid=(M // tm, N // tn, nk), in_specs=in_specs, out_specs=tile,
        out_shape=jax.ShapeDtypeStruct((M, N), out_dtype),
        scratch_shapes=[pltpu.VMEM((tm, tn), F32)] if nk > 1 else [],
        compiler_params=_params(("parallel", "parallel", "arbitrary")),
    )(*args)


def _mm_tn(a, b, *, name, out_parts=1, tm=1024, tn=1024, tk=4096, deps=()):
    if a.ndim == 3:
        Qa, M, Kq = a.shape
        Kd = Qa * Kq
    else:
        (M, Kd), Kq = a.shape, a.shape[1]
    if b.ndim == 3:
        Q, _, Nq = b.shape
        N = Q * Nq
    else:
        N, Nq = b.shape[1], b.shape[1]
    Ns = N // out_parts
    tn = _tile(Ns, tn)
    while Nq % tn or Ns % tn:
        tn -= LANES
    tm, tk = _tile(Kq, tm), _tile(M, tk, 16)
    nk = M // tk
    if a.ndim == 3:
        pa = Kq // tm
        a_spec = pl.BlockSpec((None, tk, tm), lambda i, j, k: (i // pa, k, i % pa))
    else:
        a_spec = pl.BlockSpec((tk, tm), lambda i, j, k: (k, i))
    if b.ndim == 3:
        pb = Nq // tn
        b_spec = pl.BlockSpec((None, tk, tn), lambda i, j, k: (j // pb, k, j % pb))
    else:
        b_spec = pl.BlockSpec((tk, tn), lambda i, j, k: (k, j))
    if out_parts > 1:
        po = Ns // tn
        o_spec = pl.BlockSpec((None, tm, tn), lambda i, j, k: (j // po, i, j % po))
        out_shape = jax.ShapeDtypeStruct((out_parts, Kd, Ns), BF16)
    else:
        o_spec = pl.BlockSpec((tm, tn), lambda i, j, k: (i, j))
        out_shape = jax.ShapeDtypeStruct((Kd, N), BF16)

    def body(*refs):
        acc_ref = refs[-1] if nk > 1 else None
        part = lax.dot_general(refs[0][...], refs[1][...], (((0,), (0,)), ((), ())), preferred_element_type=F32)

        def fin(acc):
            refs[2][...] = acc.astype(BF16)
        _accumulate(part, acc_ref, nk, fin)

    return _call(
        body, deps, name=name, grid=(Kd // tm, N // tn, nk),
        in_specs=[a_spec, b_spec], out_specs=o_spec, out_shape=out_shape,
        scratch_shapes=[pltpu.VMEM((tm, tn), F32)] if nk > 1 else [],
        compiler_params=_params(("parallel", "parallel", "arbitrary")),
    )(a, b)


def _rows(S, D, i_map=lambda i: (i, 0), ts=512):
    return pl.BlockSpec((ts, D), i_map)


def _norm_fwd(x, gain, sc, sh, *, name, deps=()):
    S, D = x.shape
    ts = _tile(S, 512, 16)
    vec = pl.BlockSpec((1, D), lambda i: (0, 0))

    def body(x_ref, g_ref, sc_ref, sh_ref, h_ref):
        xv = x_ref[...]
        r = lax.rsqrt(jnp.mean(xv * xv, axis=-1, keepdims=True) + RMS_EPS)
        h = (xv * r) * g_ref[...]
        h_ref[...] = (h * (1.0 + sc_ref[...]) + sh_ref[...]).astype(BF16)

    return _call(
        body, deps, name=name, grid=(S // ts,), in_specs=[_rows(S, D, ts=ts), vec, vec, vec], out_specs=_rows(S, D, ts=ts),
        out_shape=jax.ShapeDtypeStruct((S, D), BF16), compiler_params=_params(("parallel",)),
    )(x, gain, sc, sh)


def _loss_bwd(x, target, gain, gate_prev, *, name, deps=()):
    S, D = x.shape
    ts = _tile(S, 256, 16)
    vec = pl.BlockSpec((1, D), lambda i: (0, 0))

    def body(x_ref, t_ref, g_ref, gp_ref, dx_ref, dp_ref, sums_ref):
        @pl.when(pl.program_id(0) == 0)
        def _():
            sums_ref[...] = jnp.zeros_like(sums_ref)
        xv = x_ref[...]
        r = lax.rsqrt(jnp.mean(xv * xv, axis=-1, keepdims=True) + RMS_EPS)
        xn = xv * r
        err = xn * g_ref[...] - t_ref[...]
        loss = 0.5 * jnp.sum(jnp.mean(err * err, axis=-1, keepdims=True), axis=0, keepdims=True)
        dy = err * (1.0 / D)
        dxn = dy * g_ref[...]
        dx = r * (dxn - xn * jnp.mean(dxn * xn, axis=-1, keepdims=True))
        dx_ref[...] = dx
        dp_ref[...] = (gp_ref[...] * dx).astype(BF16)
        sums_ref[0:1, :] += jnp.sum(dy * xn, axis=0, keepdims=True)
        sums_ref[1:2, :] += jnp.broadcast_to(loss, (1, D))

    return _call(
        body, deps, name=name, grid=(S // ts,),
        in_specs=[_rows(S, D, ts=ts), _rows(S, D, ts=ts), vec, vec],
        out_specs=[_rows(S, D, ts=ts), _rows(S, D, ts=ts), pl.BlockSpec((8, D), lambda i: (0, 0))],
        out_shape=[jax.ShapeDtypeStruct((S, D), F32), jax.ShapeDtypeStruct((S, D), BF16), jax.ShapeDtypeStruct((8, D), F32)],
        compiler_params=_params(("arbitrary",)),
    )(x, target, gain, gate_prev)


def _norm_bwd(x, dh, dxp, mix, gain, sc, gate_prev, *, name, deps=()):
    S, D = x.shape
    ts = _tile(S, 256, 16)
    vec = pl.BlockSpec((1, D), lambda i: (0, 0))
    with_prev = gate_prev is not None

    def body(*refs):
        x_ref, dh_ref, dxp_ref, mix_ref, g_ref, sc_ref = refs[:6]
        outs = refs[7:] if with_prev else refs[6:]
        sums_ref = outs[-1]

        @pl.when(pl.program_id(0) == 0)
        def _():
            sums_ref[...] = jnp.zeros_like(sums_ref)
        xv, dhv, dxpv = x_ref[...], dh_ref[...].astype(F32), dxp_ref[...]
        r = lax.rsqrt(jnp.mean(xv * xv, axis=-1, keepdims=True) + RMS_EPS)
        xn = xv * r
        hn = xn * g_ref[...]
        dhn = dhv * (1.0 + sc_ref[...])
        dxn = dhn * g_ref[...]
        dx = dxpv + r * (dxn - xn * jnp.mean(dxn * xn, axis=-1, keepdims=True))
        outs[0][...] = dx
        if with_prev:
            outs[1][...] = (refs[6][...] * dx).astype(BF16)
        sums_ref[0:1, :] += jnp.sum(dhv, axis=0, keepdims=True)
        sums_ref[1:2, :] += jnp.sum(dhv * hn, axis=0, keepdims=True)
        sums_ref[2:3, :] += jnp.sum(dhn * xn, axis=0, keepdims=True)
        sums_ref[3:4, :] += jnp.sum(dxpv * mix_ref[...].astype(F32), axis=0, keepdims=True)

    tile = _rows(S, D, ts=ts)
    in_specs = [tile, tile, tile, tile, vec, vec] + ([vec] if with_prev else [])
    args = [x, dh, dxp, mix, gain, sc] + ([gate_prev] if with_prev else [])
    out_specs = [tile] + ([tile] if with_prev else []) + [pl.BlockSpec((8, D), lambda i: (0, 0))]
    out_shape = ([jax.ShapeDtypeStruct((S, D), F32)] + ([jax.ShapeDtypeStruct((S, D), BF16)] if with_prev else [])
                 + [jax.ShapeDtypeStruct((8, D), F32)])
    outs = _call(
        body, deps, name=name, grid=(S // ts,), in_specs=in_specs, out_specs=out_specs, out_shape=out_shape,
        compiler_params=_params(("arbitrary",)),
    )(*args)
    return (outs[0], outs[1], outs[2]) if with_prev else (outs[0], None, outs[1])


def _fgate_fwd(h, wf, bf, *, name, deps=()):
    S, D = h.shape
    ts = _tile(S, 256, 16)

    def body(h_ref, w_ref, b_ref, z_ref, f_ref, carry):
        @pl.when(pl.program_id(0) == 0)
        def _():
            carry[...] = jnp.zeros_like(carry)
        z = lax.dot_general(h_ref[...], w_ref[...], (((1,), (1,)), ((), ())), preferred_element_type=F32) + b_ref[...]
        logf = jnp.minimum(z, 0.0) - jnp.log(1.0 + jnp.exp(-jnp.abs(z)))
        row = lax.broadcasted_iota(jnp.int32, (ts, ts), 0)
        col = lax.broadcasted_iota(jnp.int32, (ts, ts), 1)
        tril = (col <= row).astype(F32)
        run = jnp.dot(tril, logf, preferred_element_type=F32, precision=lax.Precision.HIGHEST) + carry[0:1, :]
        z_ref[...] = z
        f_ref[...] = run
        carry[0:1, :] = run[ts - 1:ts, :]

    return _call(
        body, deps, name=name, grid=(S // ts,),
        in_specs=[pl.BlockSpec((ts, D), lambda i: (i, 0)), pl.BlockSpec((LANES, D), lambda i: (0, 0)),
                  pl.BlockSpec((1, LANES), lambda i: (0, 0))],
        out_specs=[pl.BlockSpec((ts, LANES), lambda i: (i, 0))] * 2,
        out_shape=[jax.ShapeDtypeStruct((S, LANES), F32)] * 2,
        scratch_shapes=[pltpu.VMEM((8, LANES), F32)],
        compiler_params=_params(("arbitrary",)),
    )(h, wf, bf)


def _fgate_bwd(dfq, dfk, z, *, name, deps=()):
    S = z.shape[0]
    ts = _tile(S, 256, 16)
    n = S // ts

    def body(dq_ref, dk_ref, z_ref, dz_ref, sums_ref, carry):
        @pl.when(pl.program_id(0) == 0)
        def _():
            carry[...] = jnp.zeros_like(carry)
            sums_ref[...] = jnp.zeros_like(sums_ref)
        df = dq_ref[...] - dk_ref[...]
        row = lax.broadcasted_iota(jnp.int32, (ts, ts), 0)
        col = lax.broadcasted_iota(jnp.int32, (ts, ts), 1)
        triu = (col >= row).astype(F32)
        run = jnp.dot(triu, df, preferred_element_type=F32, precision=lax.Precision.HIGHEST) + carry[0:1, :]
        zv = z_ref[...]
        dz = run * (1.0 / (1.0 + jnp.exp(zv)))
        dz_ref[...] = dz.astype(BF16)
        sums_ref[0:1, :] += jnp.sum(dz, axis=0, keepdims=True)
        carry[0:1, :] = run[0:1, :]

    rev = pl.BlockSpec((ts, LANES), lambda i: (n - 1 - i, 0))
    return _call(
        body, deps, name=name, grid=(n,), in_specs=[rev, rev, rev],
        out_specs=[rev, pl.BlockSpec((8, LANES), lambda i: (0, 0))],
        out_shape=[jax.ShapeDtypeStruct((S, LANES), BF16), jax.ShapeDtypeStruct((8, LANES), F32)],
        scratch_shapes=[pltpu.VMEM((8, LANES), F32)],
        compiler_params=_params(("arbitrary",)),
    )(dfq, dfk, z)


def _head_col(ref, rows, lane_mask):
    return jnp.sum(jnp.where(lane_mask, ref[rows, :], 0.0), axis=1, keepdims=True)


def _attn_fwd(qkv, fk, *, heads, name, T=256, deps=()):
    S, D3 = qkv.shape
    D = D3 // 3
    dh = D // heads
    T = _tile(S, T, 16)
    nq = S // T
    scale = dh ** -0.5
    hp = fk.shape[1]

    def body(q_ref, k_ref, v_ref, fk_ref, o_ref, lse_ref):
        h = pl.program_id(0)

        @pl.when(h == 0)
        def _():
            lse_ref[...] = jnp.zeros_like(lse_ref)
        lane = lax.broadcasted_iota(jnp.int32, (1, LANES), 1) == h
        row = lax.broadcasted_iota(jnp.int32, (T, T), 0)
        col = lax.broadcasted_iota(jnp.int32, (T, T), 1)

        def q_block(qi, _):
            rows = pl.ds(pl.multiple_of(qi * T, T), T)
            q = q_ref[rows, :]

            def kv_block(kj, carry, diag):
                m, l, acc = carry
                cols = pl.ds(pl.multiple_of(kj * T, T), T)
                s = lax.dot_general(q, k_ref[cols, :], (((1,), (1,)), ((), ())), preferred_element_type=F32) * scale
                s = s - fk_ref[kj, pl.ds(h, 1), :]
                if diag:
                    s = jnp.where(col <= row, s, NEG)
                m_new = jnp.maximum(m, jnp.max(s, axis=1, keepdims=True))
                p = jnp.exp(s - m_new)
                alpha = jnp.exp(m - m_new)
                l = alpha * l + jnp.sum(p, axis=1, keepdims=True)
                acc = alpha * acc + jnp.dot(p.astype(BF16), v_ref[cols, :], preferred_element_type=F32)
                return m_new, l, acc

            init = (jnp.full((T, 1), NEG, F32), jnp.zeros((T, 1), F32), jnp.zeros((T, dh), F32))
            carry = lax.fori_loop(0, qi, lambda kj, cr: kv_block(kj, cr, False), init)
            m, l, acc = kv_block(qi, carry, True)
            o_ref[rows, :] = (acc / l).astype(BF16)
            lse_ref[rows, :] = jnp.where(lane, m + jnp.log(l), lse_ref[rows, :])
            return 0

        lax.fori_loop(0, nq, q_block, 0)

    head = lambda part: pl.BlockSpec((S, dh), lambda h: (0, part * heads + h))
    return _call(
        body, deps, name=name, grid=(heads,),
        in_specs=[head(0), head(1), head(2), pl.BlockSpec((nq, hp, T), lambda h: (0, 0, 0))],
        out_specs=[pl.BlockSpec((S, dh), lambda h: (0, h)), pl.BlockSpec((S, LANES), lambda h: (0, 0))],
        out_shape=[jax.ShapeDtypeStruct((S, D), BF16), jax.ShapeDtypeStruct((S, LANES), F32)],
        compiler_params=_params(("arbitrary",)),
    )(qkv, qkv, qkv, fk)


def _attn_bwd(qkv, o, do, fk, lse, *, heads, name, T=256, deps=()):
    S, D3 = qkv.shape
    D = D3 // 3
    dh = D // heads
    T = _tile(S, T, 16)
    nq = S // T
    scale = dh ** -0.5
    hp = fk.shape[1]

    def body(q_ref, k_ref, v_ref, o_ref, do_ref, fk_ref, lse_ref, dqkv_ref, dfq_ref, dfk_ref,
             dq_acc, lse_col, delta_col, dfq_col):
        h = pl.program_id(0)

        @pl.when(h == 0)
        def _():
            dfq_ref[...] = jnp.zeros_like(dfq_ref)
            dfk_ref[...] = jnp.zeros_like(dfk_ref)
        lane = lax.broadcasted_iota(jnp.int32, (1, LANES), 1) == h
        Th = T // 2 if T % 32 == 0 else T
        dq_acc[...] = jnp.zeros_like(dq_acc)
        dfq_col[...] = jnp.zeros_like(dfq_col)

        def prep(qi, _):
            rows = pl.ds(pl.multiple_of(qi * T, T), T)
            lse_col[rows, :] = _head_col(lse_ref, rows, lane)
            delta_col[rows, :] = jnp.sum(do_ref[rows, :].astype(F32) * o_ref[rows, :].astype(F32), axis=1, keepdims=True)
            return 0

        lax.fori_loop(0, nq, prep, 0)

        def kv_block(kj, _):
            cols = pl.ds(pl.multiple_of(kj * T, T), T)
            k, v = k_ref[cols, :], v_ref[cols, :]
            fk_row = fk_ref[kj, pl.ds(h, 1), :]

            def pair_grad(rows, kk, vv, fk_r, masked):
                q, dov = q_ref[rows, :], do_ref[rows, :]
                s = lax.dot_general(q, kk, (((1,), (1,)), ((), ())), preferred_element_type=F32) * scale
                s = s - fk_r
                p = jnp.exp(s - lse_col[rows, :])
                if masked:
                    p = jnp.where(lax.broadcasted_iota(jnp.int32, p.shape, 1) <= lax.broadcasted_iota(jnp.int32, p.shape, 0), p, 0.0)
                dp = lax.dot_general(dov, vv, (((1,), (1,)), ((), ())), preferred_element_type=F32)
                ds = p * (dp - delta_col[rows, :])
                dsb = ds.astype(BF16)
                dv = lax.dot_general(p.astype(BF16), dov, (((0,), (0,)), ((), ())), preferred_element_type=F32)
                dk = lax.dot_general(dsb, q, (((0,), (0,)), ((), ())), preferred_element_type=F32)
                dq_acc[rows, :] += jnp.dot(dsb, kk, preferred_element_type=F32)
                dfq_col[rows, :] += jnp.sum(ds, axis=1, keepdims=True)
                return dk, dv, jnp.sum(ds, axis=0, keepdims=True)

            def q_block(qi, carry):
                rows = pl.ds(pl.multiple_of(qi * T, T), T)
                return tuple(c + g for c, g in zip(carry, pair_grad(rows, k, v, fk_row, False)))

            strips = [pair_grad(pl.ds(pl.multiple_of(kj * T + i * Th, Th), T - i * Th), k[i * Th:(i + 1) * Th],
                                v[i * Th:(i + 1) * Th], fk_row[:, i * Th:(i + 1) * Th], True) for i in range(T // Th)]
            carry = (jnp.concatenate([g[0] for g in strips], axis=0), jnp.concatenate([g[1] for g in strips], axis=0),
                     jnp.concatenate([g[2] for g in strips], axis=1))
            dk, dv, dfk = lax.fori_loop(kj + 1, nq, q_block, carry)
            dqkv_ref[1, cols, :] = (dk * scale).astype(BF16)
            dqkv_ref[2, cols, :] = dv.astype(BF16)
            dfk_ref[kj, pl.ds(h, 1), :] = dfk
            return 0

        lax.fori_loop(0, nq, kv_block, 0)

        def finish(qi, _):
            rows = pl.ds(pl.multiple_of(qi * T, T), T)
            dqkv_ref[0, rows, :] = (dq_acc[rows, :] * scale).astype(BF16)
            dfq_ref[rows, :] = jnp.where(lane, dfq_col[rows, :], dfq_ref[rows, :])
            return 0

        lax.fori_loop(0, nq, finish, 0)

    head = lambda part: pl.BlockSpec((S, dh), lambda h: (0, part * heads + h))
    own = pl.BlockSpec((S, dh), lambda h: (0, h))
    full = pl.BlockSpec((S, LANES), lambda h: (0, 0))
    krow = pl.BlockSpec((nq, hp, T), lambda h: (0, 0, 0))
    return _call(
        body, deps, name=name, grid=(heads,),
        in_specs=[head(0), head(1), head(2), own, own, krow, full],
        out_specs=[pl.BlockSpec((3, S, dh), lambda h: (0, 0, h)), full, krow],
        out_shape=[jax.ShapeDtypeStruct((3, S, D), BF16), jax.ShapeDtypeStruct((S, LANES), F32),
                   jax.ShapeDtypeStruct((nq, hp, T), F32)],
        scratch_shapes=[pltpu.VMEM((S, dh), F32)] + [pltpu.VMEM((S, 1), F32)] * 3,
        compiler_params=_params(("arbitrary",)),
    )(qkv, qkv, qkv, o, do, fk, lse)


def _shift_down(v, n):
    rows = lax.broadcasted_iota(jnp.int32, v.shape, 0)
    return jnp.where(rows >= n, pltpu.roll(v, n, axis=0), 0.0)


def _shift_up(v, n):
    S = v.shape[0]
    rows = lax.broadcasted_iota(jnp.int32, v.shape, 0)
    return jnp.where(rows < S - n, pltpu.roll(v, S - n, axis=0), 0.0)


def _conv_fwd(proj, conv_w, *, name, cb=LANES, deps=()):
    S, D3 = proj.shape
    D = D3 // 3
    nb = D // cb

    def body(bg_ref, cg_ref, u_ref, w_ref, z_ref):
        uc = cg_ref[...].astype(F32) * u_ref[...].astype(F32)
        w = w_ref[...]
        y = w[2:3, :] * uc + w[1:2, :] * _shift_down(uc, 1) + w[0:1, :] * _shift_down(uc, 2)
        z_ref[...] = (bg_ref[...].astype(F32) * y).astype(BF16)

    part = lambda g: pl.BlockSpec((S, cb), lambda j: (0, g * nb + j))
    return _call(
        body, deps, name=name, grid=(nb,),
        in_specs=[part(0), part(1), part(2), pl.BlockSpec((3, cb), lambda j: (0, j))],
        out_specs=pl.BlockSpec((S, cb), lambda j: (0, j)),
        out_shape=jax.ShapeDtypeStruct((S, D), BF16), compiler_params=_params(("parallel",)),
    )(proj, proj, proj, conv_w)


def _conv_bwd(proj, conv_w, dz, *, name, cb=LANES, deps=()):
    S, D3 = proj.shape
    D = D3 // 3
    nb = D // cb

    def body(bg_ref, cg_ref, u_ref, w_ref, dz_ref, dp_ref, dw_ref):
        cg, u = cg_ref[...].astype(F32), u_ref[...].astype(F32)
        uc = cg * u
        w = w_ref[...]
        uc1, uc2 = _shift_down(uc, 1), _shift_down(uc, 2)
        y = w[2:3, :] * uc + w[1:2, :] * uc1 + w[0:1, :] * uc2
        dz = dz_ref[...].astype(F32)
        dp_ref[0] = (dz * y).astype(BF16)
        dy = dz * bg_ref[...].astype(F32)
        duc = w[2:3, :] * dy + w[1:2, :] * _shift_up(dy, 1) + w[0:1, :] * _shift_up(dy, 2)
        dp_ref[1] = (duc * u).astype(BF16)
        dp_ref[2] = (duc * cg).astype(BF16)
        dw_ref[...] = jnp.zeros_like(dw_ref)
        dw_ref[0:1, :] = jnp.sum(dy * uc2, axis=0, keepdims=True)
        dw_ref[1:2, :] = jnp.sum(dy * uc1, axis=0, keepdims=True)
        dw_ref[2:3, :] = jnp.sum(dy * uc, axis=0, keepdims=True)

    part = lambda g: pl.BlockSpec((S, cb), lambda j: (0, g * nb + j))
    return _call(
        body, deps, name=name, grid=(nb,),
        in_specs=[part(0), part(1), part(2), pl.BlockSpec((3, cb), lambda j: (0, j)), pl.BlockSpec((S, cb), lambda j: (0, j))],
        out_specs=[pl.BlockSpec((3, S, cb), lambda j: (0, 0, j)), pl.BlockSpec((8, cb), lambda j: (0, j))],
        out_shape=[jax.ShapeDtypeStruct((3, S, D), BF16), jax.ShapeDtypeStruct((8, D), F32)],
        compiler_params=_params(("parallel",)),
    )(proj, proj, proj, conv_w, dz)


def _ada_fwd(c_all, ada_w, *, name, deps=()):
    L, D, Ns = ada_w.shape
    tn = _tile(Ns, 512)

    def body(c_ref, w_ref, o_ref, act_ref):
        cv = c_ref[...]
        act = cv * (1.0 / (1.0 + jnp.exp(-cv)))
        act_ref[...] = act
        o_ref[...] = jnp.dot(act.astype(BF16), w_ref[...].astype(BF16), preferred_element_type=F32)

    return _call(
        body, deps, name=name, grid=(L, Ns // tn),
        in_specs=[pl.BlockSpec((N_DEV, D), lambda l, j: (0, 0)), pl.BlockSpec((None, D, tn), lambda l, j: (l, 0, j))],
        out_specs=[pl.BlockSpec((None, N_DEV, tn), lambda l, j: (l, 0, j)), pl.BlockSpec((N_DEV, D), lambda l, j: (0, 0))],
        out_shape=[jax.ShapeDtypeStruct((L, N_DEV, Ns), F32), jax.ShapeDtypeStruct((N_DEV, D), F32)],
        compiler_params=_params(("arbitrary", "arbitrary")),
    )(c_all, ada_w)


def _select_mod(gathered, *, name, deps=()):
    _, LB, Ns = gathered.shape
    L = LB // N_DEV

    def body(g_ref, o_ref):
        x, y, c = _me()
        b = 4 * x + 2 * y + c
        for j in range(N_CHIPS):
            for l in range(L):
                o_ref[j, pl.ds(l, 1), :] = g_ref[2 * j + c, pl.ds(l * N_DEV + b, 1), :]

    return _call(
        body, deps, name=name, out_shape=jax.ShapeDtypeStruct((N_CHIPS, L, Ns), F32),
        in_specs=[pl.BlockSpec(memory_space=pltpu.VMEM)], out_specs=pl.BlockSpec(memory_space=pltpu.VMEM),
        compiler_params=_params(),
    )(gathered)


def _adamw_math(w, g, m, v):
    m = ADAM_B1 * m + (1.0 - ADAM_B1) * g
    v = ADAM_B2 * v + (1.0 - ADAM_B2) * jnp.square(g)
    m_hat = m / (1.0 - ADAM_B1 ** ADAM_STEP)
    v_hat = v / (1.0 - ADAM_B2 ** ADAM_STEP)
    delta = -ADAM_LR * (m_hat / (jnp.sqrt(v_hat) + ADAM_EPS) + ADAM_WD * w)
    return delta, m, v


def _adamw_shards(w, m, v, groups, chip, *, name):
    L, R, C = w.shape
    if R % 16 == 0:
        tr, tc = _tile(R, 128, 16), C
    else:
        tr, tc = R, _tile(C, 256)
    nr, nc = R // tr, C // tc

    def body(chip_ref, w_ref, m_ref, v_ref, *rest):
        srcs, (g_ref, d_ref, mo_ref, vo_ref) = rest[:2 * N_CHIPS * L], rest[2 * N_CHIPS * L:]
        for l in range(L):
            @pl.when(pl.program_id(0) == l)
            def _():
                s = srcs[2 * N_CHIPS * l:2 * N_CHIPS * (l + 1)]
                mine, other = s[0][...].astype(F32), s[N_CHIPS][...].astype(F32)
                for k in range(1, N_CHIPS):
                    mine = mine + s[k][...].astype(F32)
                    other = other + s[N_CHIPS + k][...].astype(F32)
                g = mine + other
                delta, mn, vn = _adamw_math(w_ref[...], g, m_ref[...], v_ref[...])
                g_ref[...] = g
                d_ref[...] = delta
                mo_ref[...] = mn
                vo_ref[...] = vn

    tile = pl.BlockSpec((None, tr, tc), lambda l, i, j, chip_ref: (l, i, j))

    def block(layer, k):
        def index(l, i, j, chip_ref):
            idle_i, idle_j = jnp.where(l < layer, 0, nr - 1), jnp.where(l < layer, 0, nc - 1)
            return (jnp.bitwise_xor(chip_ref[0], k), jnp.where(l == layer, i, idle_i), jnp.where(l == layer, j, idle_j))
        return pl.BlockSpec((None, tr, tc), index)

    in_specs, args = [tile] * 3, [w, m, v]
    for layer, (parts, land, sib) in enumerate(groups):
        in_specs += [block(layer, k) for k in range(N_CHIPS)] * 2
        args += [parts, land, land, land, sib, sib, sib, sib]
    return pl.pallas_call(
        body, name=name,
        grid_spec=pltpu.PrefetchScalarGridSpec(num_scalar_prefetch=1, grid=(L, nr, nc), in_specs=in_specs, out_specs=[tile] * 4),
        out_shape=[jax.ShapeDtypeStruct((L, R, C), F32)] * 4, compiler_params=_params(("arbitrary", "arbitrary", "arbitrary")),
    )(chip, *args)


def _adamw_ada(w, m, v, act_t, dmod, *, name, tr=256, deps=()):
    L, D, Ns = w.shape
    tr = _tile(D, tr, 8)

    def body(w_ref, m_ref, v_ref, a_ref, d_ref, g_ref, dl_ref, mo_ref, vo_ref):
        x, y, _ = _me()
        g = jnp.dot(a_ref[...], d_ref[2 * x + y], preferred_element_type=F32, precision=lax.Precision.HIGHEST)
        delta, mn, vn = _adamw_math(w_ref[...], g, m_ref[...], v_ref[...])
        g_ref[...] = g
        dl_ref[...] = delta
        mo_ref[...] = mn
        vo_ref[...] = vn

    tile = pl.BlockSpec((None, tr, Ns), lambda l, i: (l, i, 0))
    return _call(
        body, deps, name=name, grid=(L, D // tr),
        in_specs=[tile] * 3 + [pl.BlockSpec((tr, N_DEV), lambda l, i: (i, 0)),
                               pl.BlockSpec((N_CHIPS, None, N_DEV, Ns), lambda l, i: (0, l, 0, 0))],
        out_specs=[tile] * 4, out_shape=[jax.ShapeDtypeStruct((L, D, Ns), F32)] * 4,
        compiler_params=_params(("parallel", "parallel")),
    )(w, m, v, act_t, dmod)


def _adamw_small(w, m, v, gathered, *, rows, name, deps=()):
    n, D = w.shape

    def body(w_ref, m_ref, v_ref, s_ref, g_ref, d_ref, mo_ref, vo_ref):
        for r, src in enumerate(rows):
            g = s_ref[0, src:src + 1, :]
            for d in range(1, N_DEV):
                g = g + s_ref[d, src:src + 1, :]
            g_ref[r:r + 1, :] = g
        g = g_ref[...]
        delta, mn, vn = _adamw_math(w_ref[...], g, m_ref[...], v_ref[...])
        d_ref[...] = delta
        mo_ref[...] = mn
        vo_ref[...] = vn

    vm = pl.BlockSpec(memory_space=pltpu.VMEM)
    return _call(
        body, deps, name=name, in_specs=[vm] * 4, out_specs=[vm] * 4,
        out_shape=[jax.ShapeDtypeStruct((n, D), F32)] * 4, compiler_params=_params(),
    )(w, m, v, gathered)


def _adamw_conv_w(w, m, v, gathered4, *, name, deps=()):
    Cs = w.shape[1]

    def body(w_ref, m_ref, v_ref, s_ref, g_ref, d_ref, mo_ref, vo_ref):
        x, y, _ = _me()
        j = 2 * x + y
        g = s_ref[j, 0]
        for d in range(1, N_DEV):
            g = g + s_ref[j, d]
        delta, mn, vn = _adamw_math(w_ref[...], g, m_ref[...], v_ref[...])
        g_ref[...] = g
        d_ref[...] = delta
        mo_ref[...] = mn
        vo_ref[...] = vn

    vm = pl.BlockSpec(memory_space=pltpu.VMEM)
    return _call(
        body, deps, name=name, in_specs=[vm] * 4, out_specs=[vm] * 4,
        out_shape=[jax.ShapeDtypeStruct((8, Cs), F32)] * 4, compiler_params=_params(),
    )(w, m, v, gathered4)


def _loss_sum(gathered, *, row, name, deps=()):
    _, _, D = gathered.shape

    def body(s_ref, o_ref):
        t = s_ref[0, row:row + 1, :]
        for d in range(1, N_DEV):
            t = t + s_ref[d, row:row + 1, :]
        o_ref[...] = jnp.broadcast_to(t, (8, D))

    vm = pl.BlockSpec(memory_space=pltpu.VMEM)
    return pl.pallas_call(body, name=name, in_specs=[vm], out_specs=vm, out_shape=jax.ShapeDtypeStruct((8, D), F32),
                          compiler_params=_params())(gathered)


def _pad_rows(a, n):
    return jnp.pad(a, ((0, n - a.shape[0]), (0, 0)))


def kernel(x, c, ada_w, ada_b, norm_mix, norm_mlp, fox_w_in, fox_b_f, fox_w_out, conv_w_in, conv_w, conv_w_out, mlp_w_up, mlp_w_down, final_norm, loss_target, m_ada_w, m_ada_b, m_norm_mix, m_norm_mlp, m_fox_w_in, m_fox_b_f, m_fox_w_out, m_conv_w_in, m_conv_w, m_conv_w_out, m_mlp_w_up, m_mlp_w_down, m_final_norm, v_ada_w, v_ada_b, v_norm_mix, v_norm_mlp, v_fox_w_in, v_fox_b_f, v_fox_w_out, v_conv_w_in, v_conv_w, v_conv_w_out, v_mlp_w_up, v_mlp_w_down, v_final_norm):
    S, D = x.shape[1], x.shape[2]
    H = fox_b_f.shape[-1]
    L = ada_w.shape[0]
    NM = ada_b.shape[1] // D
    Ns_ada = ada_w.shape[2]
    Cs_fox = fox_w_in.shape[2]
    Cs_conv = conv_w.shape[2]
    x0 = x[0]
    target = loss_target[0]

    chip = (2 * lax.axis_index("x") + lax.axis_index("y")).astype(jnp.int32).reshape(1)

    fin_t = jnp.transpose(fox_w_in, (0, 2, 1))
    shards = dict(fin=(fin_t, 0), fout=(fox_w_out, 0), up0=(mlp_w_up, 0), dn0=(mlp_w_down, 0), cin=(conv_w_in, 0),
                  cout=(conv_w_out, 0), up1=(mlp_w_up, 1), dn1=(mlp_w_down, 1))
    halves = dict(fin="cols", fout="rows", up0="cols", dn0="rows", cin="cols", cout="rows", up1="cols", dn1="rows")
    gathers, placed = {}, {}

    def place(key, dep=None):
        placed[key] = _place_cast(*shards[key], chip, halves=halves[key], name="place_" + key, dep=dep)
        return placed[key]

    def start_gather(key, dep=None):
        gathers[key] = _split_start("gather1", [(placed[key],)], name="gather_start_" + key, dep=dep)
        return gathers[key][3]

    def pass_gather(key, after):
        landed = _split_wait("gather1", gathers[key], after, name="gather_landed_" + key)
        gathers[key] = _split_start("gather2", landed, name="gather_pass_" + key)
        return gathers[key][3]

    def gathered(key, after):
        return _split_wait("gather2", gathers[key], after, name="gather_wait_" + key)[0][0]

    fin_placed = place("fin")
    c_all = _allgather8(_pad_rows(c, 8), name="gather_c", deps=(fin_placed,))[:, 0, :]
    mod_part, c_act = _ada_fwd(c_all, ada_w, name="ada_fwd")
    mod_all = _allgather8(mod_part.reshape(L * N_DEV, Ns_ada), name="gather_mod")
    conv_w_all = _allgather8(_pad_rows(conv_w[0], 8), name="gather_conv_w", deps=(mod_all,))
    conv_w_full = jnp.transpose(conv_w_all[0::2, :3, :], (1, 0, 2)).reshape(3, D)

    tok = start_gather("fin", conv_w_all)
    for key in ("fout", "up0", "dn0", "cin", "cout", "up1", "dn1"):
        place(key, tok)
        tok = start_gather(key, tok)

    mod = _select_mod(mod_all, name="select_mod", deps=(tok,))
    mod = jnp.transpose(mod, (1, 0, 2)).reshape(L, NM, 1, D) + ada_b.reshape(L, NM, 1, D)

    def vec(a):
        return a.reshape(1, D)

    h0 = _norm_fwd(x0, vec(norm_mix[0]), mod[0, 1], mod[0, 0], name="norm_mix0")
    tok = pass_gather("fin", [h0])
    w_fin_t = jnp.transpose(gathered("fin", [tok]), (0, 2, 1, 3)).reshape(N_CHIPS * Cs_fox, D)
    w_f_t = _pad_rows(w_fin_t[3 * D:], LANES)
    tok = pass_gather("fout", [w_fin_t])
    qkv = _mm_nt(h0, w_fin_t, n=3 * D, name="fox_in", out_dtype=BF16, deps=(tok,))
    b_f = jnp.pad(fox_b_f, ((0, 0), (0, LANES - H)))
    z_f, F_col = _fgate_fwd(h0, w_f_t, b_f, name="fgate_fwd")
    hp = max(8, H)
    at = _tile(S, 1024, 16)
    F_rows = _pad_rows(jnp.transpose(F_col[:, :H]), hp)
    F_row = jnp.transpose(F_rows.reshape(hp, S // at, at), (1, 0, 2))
    o, lse = _attn_fwd(qkv, F_row, heads=H, name="attn_fwd", T=at)
    w_fout = gathered("fout", [o]).reshape(D, D)
    tok = pass_gather("up0", [o])
    tok = pass_gather("dn0", [tok])
    x1, mix0, h1 = _mm_nn(o, w_fout, name="fox_out", epilogue="resid_norm", res=x0, gate=mod[0, 2],
                          norm=(vec(norm_mlp[0]), mod[0, 4], mod[0, 3]), tm=512, tn=D, deps=(tok,))
    w_up0 = gathered("up0", [h1]).reshape(2 * N_CHIPS, D, -1)
    tok = pass_gather("cin", [h1])
    u0, a0 = _mm_nn(h1, w_up0, name="mlp_up0", epilogue="relu2", deps=(tok,))
    w_dn0 = gathered("dn0", [a0]).reshape(-1, D)
    tok = pass_gather("cout", [a0])
    x2, y0 = _mm_nn(a0, w_dn0, name="mlp_down0", epilogue="resid", res=x1, gate=mod[0, 5], deps=(tok,))
    h2 = _norm_fwd(x2, vec(norm_mix[1]), mod[1, 1], mod[1, 0], name="norm_mix1")
    g_cin = gathered("cin", [h2]).reshape(2 * N_CHIPS, D, -1)
    tok = pass_gather("up1", [h2])
    proj = _mm_nn(h2, g_cin, name="conv_in", deps=(tok,))
    w_cin = jnp.transpose(g_cin, (1, 0, 2)).reshape(D, 3 * D)
    zc = _conv_fwd(proj, conv_w_full, name="conv_fwd")
    w_cout = gathered("cout", [zc]).reshape(D, D)
    tok = pass_gather("dn1", [zc])
    x3, mix1, h3 = _mm_nn(zc, w_cout, name="conv_out", epilogue="resid_norm", res=x2, gate=mod[1, 2],
                          norm=(vec(norm_mlp[1]), mod[1, 4], mod[1, 3]), tm=512, tn=D, deps=(tok,))
    w_up1 = gathered("up1", [h3]).reshape(2 * N_CHIPS, D, -1)
    u1, a1 = _mm_nn(h3, w_up1, name="mlp_up1", epilogue="relu2")
    w_dn1 = gathered("dn1", [a1]).reshape(-1, D)
    x4, y1 = _mm_nn(a1, w_dn1, name="mlp_down1", epilogue="resid", res=x3, gate=mod[1, 5])
    w_up = [jnp.transpose(w_up0, (1, 0, 2)).reshape(D, -1), jnp.transpose(w_up1, (1, 0, 2)).reshape(D, -1)]
    w_dn = [w_dn0, w_dn1]

    dx4, dy1, sums_f = _loss_bwd(x4, target, vec(final_norm), mod[1, 5], name="loss_bwd")
    du1 = _mm_nt(dy1, w_dn[1], name="mlp_down1_dx", epilogue="drelu2", extra=u1, out_dtype=BF16)
    def start_scatter(tag, parts_list):
        groups = [(p, lax.empty(p.shape, p.dtype)) for p in parts_list]
        return _split_start("scatter", groups, name="scatter_start_" + tag)

    def start_sibling(tag, scatter, after):
        landed = _split_wait("scatter", scatter, after, name="scatter_wait_" + tag)
        groups = [(p, ld, lax.empty(p.shape, p.dtype)) for p, ld in landed]
        return _split_start("sibling", groups, name="sibling_start_" + tag)

    gw_dn1 = _mm_tn(a1, dy1, name="mlp_down1_dw")
    gw_up1 = _mm_tn(h3, du1, name="mlp_up1_dw", out_parts=N_CHIPS)
    sc1 = start_scatter("mlp1", [gw_dn1.reshape(N_CHIPS, -1, D), gw_up1])
    dh3 = _mm_nt(du1, w_up[1], name="mlp_up1_dx", out_dtype=BF16, tk=4096, deps=(sc1[3],))
    dx3, dmix1, sums_mlp1 = _norm_bwd(x3, dh3, dx4, y1, vec(norm_mlp[1]), mod[1, 4], mod[1, 2], name="norm_mlp1_bwd")
    dzc = _mm_nt(dmix1, w_cout, name="conv_out_dx", out_dtype=BF16)
    gw_cout = _mm_tn(zc, dmix1, name="conv_out_dw")
    dproj, dconv_w = _conv_bwd(proj, conv_w_full, dzc, name="conv_bwd")
    gw_cin = _mm_tn(h2, dproj, name="conv_in_dw", out_parts=N_CHIPS, tn=512)
    sc2 = start_scatter("conv", [gw_cout.reshape(N_CHIPS, -1, D), gw_cin])
    dh2 = _mm_nt(dproj, w_cin, name="conv_in_dx", out_dtype=BF16, deps=(sc2[3],))
    dx2, dy0, sums_mix1 = _norm_bwd(x2, dh2, dx3, mix1, vec(norm_mix[1]), mod[1, 1], mod[0, 5], name="norm_mix1_bwd")
    du0 = _mm_nt(dy0, w_dn[0], name="mlp_down0_dx", epilogue="drelu2", extra=u0, out_dtype=BF16)
    gw_dn0 = _mm_tn(a0, dy0, name="mlp_down0_dw")
    gw_up0 = _mm_tn(h1, du0, name="mlp_up0_dw", out_parts=N_CHIPS)
    sc3 = start_scatter("mlp0", [gw_dn0.reshape(N_CHIPS, -1, D), gw_up0])
    sb1 = start_sibling("mlp1", sc1, [sc3[3]])
    dh1 = _mm_nt(du0, w_up[0], name="mlp_up0_dx", out_dtype=BF16, tk=4096, deps=(sb1[3],))
    dx1, dmix0, sums_mlp0 = _norm_bwd(x1, dh1, dx2, y0, vec(norm_mlp[0]), mod[0, 4], mod[0, 2], name="norm_mlp0_bwd")
    do = _mm_nt(dmix0, w_fout, name="fox_out_dx", out_dtype=BF16)
    gw_fout = _mm_tn(o, dmix0, name="fox_out_dw")
    dqkv, dfq, dfk = _attn_bwd(qkv, o, do, F_row, lse, heads=H, name="attn_bwd", T=at)
    dfk_col = jnp.pad(jnp.transpose(jnp.transpose(dfk, (1, 0, 2)).reshape(hp, S)[:H]), ((0, 0), (0, LANES - H)))
    sb3 = start_sibling("mlp0", sc3, [dqkv])
    dz_f, sums_bf = _fgate_bwd(dfq, dfk_col, z_f, name="fgate_bwd", deps=(sb3[3],))
    gw_qkv_t = _mm_tn(dqkv, h0, name="fox_in_dw")
    gw_f_t = _mm_tn(dz_f, h0, name="fox_gate_dw")
    gw_fin_t = jnp.concatenate([gw_qkv_t, gw_f_t[:H]], axis=0).reshape(N_CHIPS, Cs_fox, D)
    sc4 = start_scatter("fox", [gw_fout.reshape(N_CHIPS, -1, D), gw_fin_t])
    sb2 = start_sibling("conv", sc2, [sc4[3]])
    dh0_f = _mm_nn(dz_f, w_f_t, name="fox_gate_dx", out_dtype=F32, deps=(sb2[3],))
    dh0 = _mm_nn(dqkv, w_fin_t, name="fox_in_dx", epilogue="add", res=dh0_f, out_dtype=BF16)
    grad_x, _, sums_mix0 = _norm_bwd(x0, dh0, dx1, mix0, vec(norm_mix[0]), mod[0, 1], None, name="norm_mix0_bwd")

    outs = {}

    def put(name_, res, shape):
        for kind, r in zip(("grad", "delta", "new_m", "new_v"), res):
            outs[kind + "_" + name_] = r.reshape(shape)

    def shards_update(tag, w_, m_, v_, groups):
        return _adamw_shards(w_, m_, v_, groups, chip, name="adamw_" + tag)

    g_conv = _split_wait("sibling", sb2, [grad_x], name="sibling_wait_conv")
    put("conv_w_out", shards_update("conv_out", conv_w_out, m_conv_w_out, v_conv_w_out, g_conv[0:1]), conv_w_out.shape)
    r_cin = shards_update("conv_in", conv_w_in, m_conv_w_in, v_conv_w_in, g_conv[1:2])
    put("conv_w_in", r_cin, conv_w_in.shape)
    g_mlp1 = _split_wait("sibling", sb1, [r_cin[0]], name="sibling_wait_mlp1")
    g_mlp0 = _split_wait("sibling", sb3, [r_cin[0]], name="sibling_wait_mlp0")
    put("mlp_w_down", shards_update("mlp_down", mlp_w_down, m_mlp_w_down, v_mlp_w_down, [g_mlp0[0], g_mlp1[0]]), mlp_w_down.shape)
    r_up = shards_update("mlp_up", mlp_w_up, m_mlp_w_up, v_mlp_w_up, [g_mlp0[1], g_mlp1[1]])
    put("mlp_w_up", r_up, mlp_w_up.shape)
    sb4 = start_sibling("fox", sc4, [r_up[0]])

    dmod_rows = []
    for sm, sl in ((sums_mix0, sums_mlp0), (sums_mix1, sums_mlp1)):
        dmod_rows += [sm[0:1], sm[1:2], sm[3:4], sl[0:1], sl[1:2], sl[3:4]]
    bf_row = jnp.pad(sums_bf[0:1], ((0, 0), (0, D - LANES)))
    small = jnp.concatenate([sums_mix0[2:3], sums_mix1[2:3], sums_mlp0[2:3], sums_mlp1[2:3], sums_f[0:1], sums_f[1:2], bf_row,
                             jnp.zeros((1, D), F32)] + dmod_rows + [dconv_w[0:3]], axis=0)
    small_all = _allgather8(_pad_rows(small, -(-small.shape[0] // 8) * 8), name="gather_small", deps=(sb4[3],))
    loss = _loss_sum(small_all, row=5, name="loss_sum")[0, 0]

    def rows_of(a_mix, a_mlp, a_fin, a_bf, a_ada):
        return jnp.concatenate([a_mix, a_mlp, a_fin.reshape(1, D), jnp.pad(a_bf, ((0, 0), (0, D - H))),
                                a_ada.reshape(L * NM, D)], axis=0)
    n_small = 2 * L + 2 + L * NM
    rw = -(-n_small // 8) * 8
    w_s = _pad_rows(rows_of(norm_mix, norm_mlp, final_norm, fox_b_f, ada_b), rw)
    m_s = _pad_rows(rows_of(m_norm_mix, m_norm_mlp, m_final_norm, m_fox_b_f, m_ada_b), rw)
    v_s = _pad_rows(rows_of(v_norm_mix, v_norm_mlp, v_final_norm, v_fox_b_f, v_ada_b), rw)
    src_rows = [0, 1, 2, 3, 4, 6] + [8 + r for r in range(L * NM)] + [7] * (rw - n_small)
    res_s = _adamw_small(w_s, m_s, v_s, small_all, rows=tuple(src_rows), name="adamw_small")
    for kind, r in zip(("grad", "delta", "new_m", "new_v"), res_s):
        outs[kind + "_norm_mix"] = r[0:L]
        outs[kind + "_norm_mlp"] = r[L:2 * L]
        outs[kind + "_final_norm"] = r[2 * L]
        outs[kind + "_fox_b_f"] = r[2 * L + 1:2 * L + 2, :H]
        outs[kind + "_ada_b"] = r[2 * L + 2:n_small].reshape(L, NM * D)

    dmod_all = small_all[:, 8:8 + L * NM, :].reshape(N_DEV, L, N_CHIPS, Ns_ada)
    dmod4 = jnp.transpose(dmod_all, (2, 1, 0, 3))
    act_t = jnp.transpose(c_act)
    res_a = _adamw_ada(ada_w, m_ada_w, v_ada_w, act_t, dmod4, name="adamw_ada")
    put("ada_w", res_a, ada_w.shape)

    r0 = 8 + L * NM
    dconv_all = jnp.pad(small_all[:, r0:r0 + 3, :], ((0, 0), (0, 5), (0, 0)))
    dconv4 = jnp.transpose(dconv_all.reshape(N_DEV, 8, N_CHIPS, Cs_conv), (2, 0, 1, 3))
    res_c = _adamw_conv_w(_pad_rows(conv_w[0], 8), _pad_rows(m_conv_w[0], 8), _pad_rows(v_conv_w[0], 8), dconv4,
                          name="adamw_conv_w")
    for kind, r in zip(("grad", "delta", "new_m", "new_v"), res_c):
        outs[kind + "_conv_w"] = r[:3].reshape(conv_w.shape)

    g_fox = _split_wait("sibling", sb4, [res_a[0], res_c[0], res_s[0]], name="sibling_wait_fox")
    put("fox_w_out", shards_update("fox_out", fox_w_out, m_fox_w_out, v_fox_w_out, g_fox[0:1]), fox_w_out.shape)
    t3 = lambda a: jnp.transpose(a, (0, 2, 1))
    for kind, r in zip(("grad", "delta", "new_m", "new_v"),
                       shards_update("fox_in", t3(fox_w_in), t3(m_fox_w_in), t3(v_fox_w_in), g_fox[1:2])):
        outs[kind + "_fox_w_in"] = t3(r)

    names = ["ada_w", "ada_b", "norm_mix", "norm_mlp", "fox_w_in", "fox_b_f", "fox_w_out", "conv_w_in", "conv_w", "conv_w_out",
             "mlp_w_up", "mlp_w_down", "final_norm"]
    return (loss, grad_x[None], *[outs["grad_" + n] for n in names], *[outs["delta_" + n] for n in names],
            *[outs["new_m_" + n] for n in names], *[outs["new_v_" + n] for n in names])
```

```python
import jax
import jax.numpy as jnp
from jax import lax
from jax.experimental import pallas as pl
from jax.experimental.pallas import tpu as pltpu

F32 = jnp.float32
BF16 = jnp.bfloat16
MESH = pl.DeviceIdType.MESH
ANY = pl.BlockSpec(memory_space=pl.ANY)
HBM = pl.BlockSpec(memory_space=pltpu.HBM)
SEM = pl.BlockSpec(memory_space=pltpu.SEMAPHORE)
EFFECT = pltpu.SideEffectType.DATAFLOW_SIDE_EFFECTING

RMS_EPS = 1e-6
ADAM_LR = 0.001
ADAM_B1 = 0.9
ADAM_B2 = 0.999
ADAM_EPS = 1e-08
ADAM_WD = 0.01
ADAM_STEP = 10
N_CHIPS = 4
N_DEV = 8
LANES = 128
VMEM_LIMIT = 56 * 1024 * 1024
NEG = -1e30


def _params(sems=None, vmem=VMEM_LIMIT):
    return pltpu.CompilerParams(dimension_semantics=sems, vmem_limit_bytes=vmem)


def _tile(n, pref, unit=LANES):
    if n <= pref:
        return n
    t = (pref // unit) * unit
    while n % t:
        t -= unit
    return t


def _me():
    return lax.axis_index("x"), lax.axis_index("y"), lax.axis_index("c")


def _call(body, deps, **kw):
    nd = len(deps)

    def wrapped(*refs):
        body(*refs[nd:])

    kw["in_specs"] = [ANY] * nd + list(kw["in_specs"])
    fn = pl.pallas_call(wrapped, **kw)
    return lambda *args: fn(*deps, *args)


def _allgather8(v, *, name, deps=()):
    R, C = v.shape

    def body(v_ref, out_ref, send_sems, recv_sems):
        x, y, c = _me()
        me = 4 * x + 2 * y + c
        out_ref[me] = v_ref[...]
        copies = []
        for k in range(1, N_DEV):
            px, py, pc = (x + (k >> 2)) % 2, (y + ((k >> 1) & 1)) % 2, (c + (k & 1)) % 2
            copies.append(pltpu.make_async_remote_copy(
                src_ref=v_ref, dst_ref=out_ref.at[me], send_sem=send_sems.at[k - 1], recv_sem=recv_sems.at[k - 1],
                device_id=(px, py, pc), device_id_type=MESH))
        for cp in copies:
            cp.start()
        for k in range(1, N_DEV):
            px, py, pc = (x + (k >> 2)) % 2, (y + ((k >> 1) & 1)) % 2, (c + (k & 1)) % 2
            peer = 4 * px + 2 * py + pc
            pltpu.make_async_remote_copy(
                src_ref=v_ref, dst_ref=out_ref.at[peer], send_sem=send_sems.at[k - 1], recv_sem=recv_sems.at[k - 1],
                device_id=(px, py, pc), device_id_type=MESH).wait_recv()
        for cp in copies:
            cp.wait_send()

    return _call(
        body, deps, name=name,
        out_shape=jax.ShapeDtypeStruct((N_DEV, R, C), v.dtype),
        in_specs=[pl.BlockSpec(memory_space=pltpu.VMEM)],
        out_specs=pl.BlockSpec(memory_space=pltpu.VMEM),
        scratch_shapes=[pltpu.SemaphoreType.DMA((N_DEV - 1,)), pltpu.SemaphoreType.DMA((N_DEV - 1,))],
        compiler_params=_params(),
    )(v)


def _chip_peers(x, y):
    return [((x + (k >> 1)) % 2, (y + (k & 1)) % 2) for k in range(1, N_CHIPS)]


def _slot(x, y, k):
    return 2 * ((x + (k >> 1)) % 2) + (y + (k & 1)) % 2


def _split_copies(kind, groups, send_sems, recv_sems):
    x, y, c = _me()
    j = 2 * x + y
    copies = []
    for a, g in enumerate(groups):
        if kind == "sibling":
            parts, land, sib = g
            for k in range(N_CHIPS):
                s = _slot(x, y, k)
                copies.append(pltpu.make_async_remote_copy(
                    src_ref=(parts if k == 0 else land).at[s], dst_ref=sib.at[s], send_sem=send_sems.at[N_CHIPS * a + k],
                    recv_sem=recv_sems.at[N_CHIPS * a + k], device_id=(x, y, 1 - c), device_id_type=MESH))
            continue
        land = g[-1]
        for k, (px, py) in enumerate(_chip_peers(x, y)):
            if kind == "gather1":
                src, dst, to = land.at[j, c], land.at[j, c], (px, py, c)
            elif kind == "gather2":
                src, dst, to = land.at[2 * px + py, c], land.at[2 * px + py, c], (x, y, 1 - c)
            else:
                src, dst, to = g[0].at[2 * px + py], land.at[j], (px, py, c)
            copies.append(pltpu.make_async_remote_copy(
                src_ref=src, dst_ref=dst, send_sem=send_sems.at[3 * a + k], recv_sem=recv_sems.at[3 * a + k],
                device_id=to, device_id_type=MESH))
    return copies


def _split_start(kind, groups, *, name, dep=None):
    flat = [a for g in groups for a in g]
    nf, per = len(flat), len(groups[0])
    ncp = len(groups) * (N_CHIPS if kind == "sibling" else 3)
    nd = 0 if dep is None else 1

    def body(*refs):
        ins = refs[nd:nd + nf]
        send_sems, recv_sems, token = refs[nd + nf], refs[nd + nf + 1], refs[-1]
        for cp in _split_copies(kind, [ins[i:i + per] for i in range(0, nf, per)], send_sems, recv_sems):
            cp.start()
        token[...] = jnp.zeros_like(token)

    outs = pl.pallas_call(
        body, name=name,
        out_shape=(pltpu.SemaphoreType.DMA((ncp,)), pltpu.SemaphoreType.DMA((ncp,)), *[pltpu.HBM(a.shape, a.dtype) for a in flat],
                   jax.ShapeDtypeStruct((8, LANES), F32)),
        in_specs=[ANY] * nd + [HBM] * nf,
        out_specs=(SEM, SEM, *[HBM] * nf, pl.BlockSpec(memory_space=pltpu.VMEM)),
        input_output_aliases={nd + i: 2 + i for i in range(nf)},
        compiler_params=pltpu.CompilerParams(has_side_effects=EFFECT),
    )(*([dep] if nd else []), *[pltpu.with_memory_space_constraint(a, pltpu.HBM) for a in flat])
    thru = list(outs[2:2 + nf])
    return outs[0], outs[1], [tuple(thru[i:i + per]) for i in range(0, nf, per)], outs[-1]


def _split_wait(kind, started, after, *, name):
    send_sems, recv_sems, groups, _ = started
    flat = [a for g in groups for a in g]
    nf, per = len(flat), len(groups[0])

    def body(*refs):
        ins = refs[:nf]
        for cp in _split_copies(kind, [ins[i:i + per] for i in range(0, nf, per)], refs[nf], refs[nf + 1]):
            cp.wait_send()
            cp.wait_recv()

    outs = pl.pallas_call(
        body, name=name,
        out_shape=tuple(pltpu.HBM(a.shape, a.dtype) for a in flat),
        in_specs=[HBM] * nf + [SEM, SEM] + [ANY] * len(after), out_specs=tuple([HBM] * nf),
        input_output_aliases={i: i for i in range(nf)},
        compiler_params=pltpu.CompilerParams(has_side_effects=EFFECT),
    )(*flat, send_sems, recv_sems, *after)
    outs = list(outs)
    return [tuple(outs[i:i + per]) for i in range(0, nf, per)]


def _place_cast(shards, layer, chip, *, halves, name, dep=None):
    _, R, C = shards.shape
    if halves == "rows":
        hr, hc = R // 2, C
    else:
        hr, hc = R, C // 2
    if hr % 16 == 0:
        tr, tc = _tile(hr, 512, 16), hc
    else:
        tr, tc = hr, _tile(hc, 256)
    nr, nc = hr // tr, hc // tc

    def body(chip_ref, x_ref, *rest):
        rest[-1][...] = x_ref[...].astype(BF16)

    if halves == "rows":
        o_map = lambda i, j, chip_ref: (chip_ref[0], i // nr, i % nr, j)
    else:
        o_map = lambda i, j, chip_ref: (chip_ref[0], j // nc, i, j % nc)
    return pl.pallas_call(
        body, name=name,
        grid_spec=pltpu.PrefetchScalarGridSpec(
            num_scalar_prefetch=1, grid=(R // tr, C // tc),
            in_specs=[pl.BlockSpec((None, tr, tc), lambda i, j, chip_ref: (layer, i, j))] + ([] if dep is None else [ANY]),
            out_specs=pl.BlockSpec((None, None, tr, tc), o_map)),
        out_shape=jax.ShapeDtypeStruct((N_CHIPS, 2, hr, hc), BF16), compiler_params=_params(("parallel", "parallel")),
    )(chip, shards, *([] if dep is None else [dep]))


def _accumulate(part, acc_ref, nk, finalize):
    if nk == 1:
        finalize(part)
        return
    k = pl.program_id(2)

    @pl.when(k == 0)
    def _():
        acc_ref[...] = part

    @pl.when(k > 0)
    def _():
        acc_ref[...] += part

    @pl.when(k == nk - 1)
    def _():
        finalize(acc_ref[...])


def _mm_nn(a, b, *, name, epilogue="plain", res=None, gate=None, norm=None, out_dtype=BF16, tm=1024, tn=1024, tk=2048, deps=()):
    if a.ndim == 3:
        Q, M, Kq = a.shape
        K = Q * Kq
    else:
        (M, K), Kq = a.shape, a.shape[1]
    tm, tk = _tile(M, tm, 16), _tile(Kq, tk)
    if a.ndim == 3:
        pa = Kq // tk
        a_spec = pl.BlockSpec((None, tm, tk), lambda i, j, k: (k // pa, i, k % pa))
    else:
        a_spec = pl.BlockSpec((tm, tk), lambda i, j, k: (i, k))
    if b.ndim == 3:
        P, _, Ns = b.shape
        N = P * Ns
        tn = _tile(Ns, tn)
        per = Ns // tn
        b_spec = pl.BlockSpec((None, tk, tn), lambda i, j, k: (j // per, k, j % per))
    else:
        N = b.shape[1]
        tn = _tile(N, tn)
        b_spec = pl.BlockSpec((tk, tn), lambda i, j, k: (k, j))
    nk = K // tk
    tile = pl.BlockSpec((tm, tn), lambda i, j, k: (i, j))

    def body(*refs):
        acc_ref = refs[-1] if nk > 1 else None
        a_ref, b_ref = refs[0], refs[1]
        part = jnp.dot(a_ref[...], b_ref[...], preferred_element_type=F32)
        if epilogue == "plain":
            def fin(acc):
                refs[2][...] = acc.astype(out_dtype)
        elif epilogue == "relu2":
            def fin(acc):
                refs[2][...] = acc.astype(BF16)
                refs[3][...] = jnp.square(jnp.maximum(acc, 0.0)).astype(BF16)
        elif epilogue == "add":
            def fin(acc):
                refs[3][...] = (acc + refs[2][...]).astype(out_dtype)
        elif epilogue == "resid":
            def fin(acc):
                refs[4][...] = refs[2][...] + refs[3][...] * acc
                refs[5][...] = acc.astype(BF16)
        else:
            def fin(acc):
                xv = refs[2][...] + refs[3][...] * acc
                refs[7][...] = xv
                refs[8][...] = acc.astype(BF16)
                r = lax.rsqrt(jnp.mean(xv * xv, axis=-1, keepdims=True) + RMS_EPS)
                refs[9][...] = (((xv * r) * refs[4][...]) * (1.0 + refs[5][...]) + refs[6][...]).astype(BF16)
        _accumulate(part, acc_ref, nk, fin)

    in_specs = [a_spec, b_spec]
    args = [a, b]
    if epilogue == "plain":
        out_shape, out_specs = jax.ShapeDtypeStruct((M, N), out_dtype), tile
    elif epilogue == "relu2":
        out_shape, out_specs = [jax.ShapeDtypeStruct((M, N), BF16)] * 2, [tile, tile]
    elif epilogue == "add":
        in_specs.append(tile)
        args.append(res)
        out_shape, out_specs = jax.ShapeDtypeStruct((M, N), out_dtype), tile
    else:
        row = pl.BlockSpec((1, tn), lambda i, j, k: (0, j))
        in_specs += [tile, row]
        args += [res, gate]
        out_shape, out_specs = [jax.ShapeDtypeStruct((M, N), F32), jax.ShapeDtypeStruct((M, N), BF16)], [tile, tile]
        if epilogue == "resid_norm":
            assert tn == N, "the next norm needs whole rows"
            in_specs += [row, row, row]
            args += list(norm)
            out_shape, out_specs = out_shape + [jax.ShapeDtypeStruct((M, N), BF16)], out_specs + [tile]
    return _call(
        body, deps, name=name, grid=(M // tm, N // tn, nk), in_specs=in_specs, out_specs=out_specs, out_shape=out_shape,
        scratch_shapes=[pltpu.VMEM((tm, tn), F32)] if nk > 1 else [],
        compiler_params=_params(("parallel", "parallel", "arbitrary")),
    )(*args)


def _mm_nt(a, b, *, name, n=None, epilogue="plain", extra=None, out_dtype=F32, tm=1024, tn=1024, tk=2048, deps=()):
    if a.ndim == 3:
        Q, M, Kq = a.shape
        K = Q * Kq
    else:
        (M, K), Kq = a.shape, a.shape[1]
    if b.ndim == 3:
        P, N, Ks = b.shape
    else:
        N, Ks = b.shape
    N = n or N
    tm, tn, tk = _tile(M, tm, 16), _tile(N, tn), _tile(min(Kq, Ks), tk)
    nk = K // tk
    if a.ndim == 3:
        pa = Kq // tk
        a_spec = pl.BlockSpec((None, tm, tk), lambda i, j, k: (k // pa, i, k % pa))
    else:
        a_spec = pl.BlockSpec((tm, tk), lambda i, j, k: (i, k))
    if b.ndim == 3:
        pb = Ks // tk
        b_spec = pl.BlockSpec((None, tn, tk), lambda i, j, k: (k // pb, j, k % pb))
    else:
        b_spec = pl.BlockSpec((tn, tk), lambda i, j, k: (j, k))
    tile = pl.BlockSpec((tm, tn), lambda i, j, k: (i, j))

    def body(*refs):
        acc_ref = refs[-1] if nk > 1 else None
        part = lax.dot_general(refs[0][...], refs[1][...], (((1,), (1,)), ((), ())), preferred_element_type=F32)
        if epilogue == "plain":
            def fin(acc):
                refs[2][...] = acc.astype(out_dtype)
        elif epilogue == "add":
            def fin(acc):
                refs[3][...] = (acc + refs[2][...]).astype(out_dtype)
        else:
            def fin(acc):
                refs[3][...] = (acc * (2.0 * jnp.maximum(refs[2][...].astype(F32), 0.0))).astype(out_dtype)
        _accumulate(part, acc_ref, nk, fin)

    in_specs, args = [a_spec, b_spec], [a, b]
    if epilogue != "plain":
        in_specs.append(tile)
        args.append(extra)
    return _call(
        body, deps, name=name, grid=(M // tm, N // tn, nk), in_specs=in_specs, out_specs=tile,
        out_shape=jax.ShapeDtypeStruct((M, N), out_dtype),
        scratch_shapes=[pltpu.VMEM((tm, tn), F32)] if nk > 1 else [],
        compiler_params=_params(("parallel", "parallel", "arbitrary")),
    )(*args)


def _put_rows(buf, src, row0, n, *, name):
    def body(buf_ref, src_ref, out_ref):
        out_ref[...] = src_ref[0:n, :]

    return pl.pallas_call(
        body, name=name, grid=(1,),
        in_specs=[ANY, pl.BlockSpec(src.shape, lambda i: (0, 0))],
        out_specs=pl.BlockSpec((n, buf.shape[1]), lambda i: (row0 // n, 0)),
        out_shape=jax.ShapeDtypeStruct(buf.shape, buf.dtype), input_output_aliases={0: 0},
        compiler_params=_params(("arbitrary",)),
    )(buf, src)


def _mm_tn(a, b, *, name, out_parts=1, out_rows=None, tm=1024, tn=1024, tk=4096, deps=()):
    if a.ndim == 3:
        Qa, M, Kq = a.shape
        Kd = Qa * Kq
    else:
        (M, Kd), Kq = a.shape, a.shape[1]
    if b.ndim == 3:
        Q, _, Nq = b.shape
        N = Q * Nq
    else:
        N, Nq = b.shape[1], b.shape[1]
    Ns = N // out_parts
    tn = _tile(Ns, tn)
    while Nq % tn or Ns % tn:
        tn -= LANES
    tm, tk = _tile(Kq, tm), _tile(M, tk, 16)
    nk = M // tk
    if a.ndim == 3:
        pa = Kq // tm
        a_spec = pl.BlockSpec((None, tk, tm), lambda i, j, k: (i // pa, k, i % pa))
    else:
        a_spec = pl.BlockSpec((tk, tm), lambda i, j, k: (k, i))
    if b.ndim == 3:
        pb = Nq // tn
        b_spec = pl.BlockSpec((None, tk, tn), lambda i, j, k: (j // pb, k, j % pb))
    else:
        b_spec = pl.BlockSpec((tk, tn), lambda i, j, k: (k, j))
    if out_parts > 1:
        po = Ns // tn
        o_spec = pl.BlockSpec((None, tm, tn), lambda i, j, k: (j // po, i, j % po))
        out_shape = jax.ShapeDtypeStruct((out_parts, Kd, Ns), BF16)
    else:
        o_spec = pl.BlockSpec((tm, tn), lambda i, j, k: (i, j))
        out_shape = jax.ShapeDtypeStruct((out_rows or Kd, N), BF16)

    def body(*refs):
        acc_ref = refs[-1] if nk > 1 else None
        part = lax.dot_general(refs[0][...], refs[1][...], (((0,), (0,)), ((), ())), preferred_element_type=F32)

        def fin(acc):
            refs[2][...] = acc.astype(BF16)
        _accumulate(part, acc_ref, nk, fin)

    return _call(
        body, deps, name=name, grid=(Kd // tm, N // tn, nk),
        in_specs=[a_spec, b_spec], out_specs=o_spec, out_shape=out_shape,
        scratch_shapes=[pltpu.VMEM((tm, tn), F32)] if nk > 1 else [],
        compiler_params=_params(("parallel", "parallel", "arbitrary")),
    )(a, b)


def _rows(S, D, i_map=lambda i: (i, 0), ts=512):
    return pl.BlockSpec((ts, D), i_map)


def _norm_fwd(x, gain, sc, sh, *, name, deps=()):
    S, D = x.shape
    ts = _tile(S, 512, 16)
    vec = pl.BlockSpec((1, D), lambda i: (0, 0))

    def body(x_ref, g_ref, sc_ref, sh_ref, h_ref):
        xv = x_ref[...]
        r = lax.rsqrt(jnp.mean(xv * xv, axis=-1, keepdims=True) + RMS_EPS)
        h = (xv * r) * g_ref[...]
        h_ref[...] = (h * (1.0 + sc_ref[...]) + sh_ref[...]).astype(BF16)

    return _call(
        body, deps, name=name, grid=(S // ts,), in_specs=[_rows(S, D, ts=ts), vec, vec, vec], out_specs=_rows(S, D, ts=ts),
        out_shape=jax.ShapeDtypeStruct((S, D), BF16), compiler_params=_params(("parallel",)),
    )(x, gain, sc, sh)


def _loss_bwd(x, target, gain, gate_prev, *, name, deps=()):
    S, D = x.shape
    ts = _tile(S, 256, 16)
    vec = pl.BlockSpec((1, D), lambda i: (0, 0))

    def body(x_ref, t_ref, g_ref, gp_ref, dx_ref, dp_ref, sums_ref):
        @pl.when(pl.program_id(0) == 0)
        def _():
            sums_ref[...] = jnp.zeros_like(sums_ref)
        xv = x_ref[...]
        r = lax.rsqrt(jnp.mean(xv * xv, axis=-1, keepdims=True) + RMS_EPS)
        xn = xv * r
        err = xn * g_ref[...] - t_ref[...]
        loss = 0.5 * jnp.sum(jnp.mean(err * err, axis=-1, keepdims=True), axis=0, keepdims=True)
        dy = err * (1.0 / D)
        dxn = dy * g_ref[...]
        dx = r * (dxn - xn * jnp.mean(dxn * xn, axis=-1, keepdims=True))
        dx_ref[...] = dx
        dp_ref[...] = (gp_ref[...] * dx).astype(BF16)
        sums_ref[0:1, :] += jnp.sum(dy * xn, axis=0, keepdims=True)
        sums_ref[1:2, :] += jnp.broadcast_to(loss, (1, D))

    return _call(
        body, deps, name=name, grid=(S // ts,),
        in_specs=[_rows(S, D, ts=ts), _rows(S, D, ts=ts), vec, vec],
        out_specs=[_rows(S, D, ts=ts), _rows(S, D, ts=ts), pl.BlockSpec((8, D), lambda i: (0, 0))],
        out_shape=[jax.ShapeDtypeStruct((S, D), F32), jax.ShapeDtypeStruct((S, D), BF16), jax.ShapeDtypeStruct((8, D), F32)],
        compiler_params=_params(("arbitrary",)),
    )(x, target, gain, gate_prev)


def _norm_bwd(x, dh, dxp, mix, gain, sc, gate_prev, *, name, deps=()):
    S, D = x.shape
    ts = _tile(S, 256, 16)
    vec = pl.BlockSpec((1, D), lambda i: (0, 0))
    with_prev = gate_prev is not None

    def body(*refs):
        x_ref, dh_ref, dxp_ref, mix_ref, g_ref, sc_ref = refs[:6]
        outs = refs[7:] if with_prev else refs[6:]
        sums_ref = outs[-1]

        @pl.when(pl.program_id(0) == 0)
        def _():
            sums_ref[...] = jnp.zeros_like(sums_ref)
        xv, dhv, dxpv = x_ref[...], dh_ref[...].astype(F32), dxp_ref[...]
        r = lax.rsqrt(jnp.mean(xv * xv, axis=-1, keepdims=True) + RMS_EPS)
        xn = xv * r
        hn = xn * g_ref[...]
        dhn = dhv * (1.0 + sc_ref[...])
        dxn = dhn * g_ref[...]
        dx = dxpv + r * (dxn - xn * jnp.mean(dxn * xn, axis=-1, keepdims=True))
        outs[0][...] = dx
        if with_prev:
            outs[1][...] = (refs[6][...] * dx).astype(BF16)
        sums_ref[0:1, :] += jnp.sum(dhv, axis=0, keepdims=True)
        sums_ref[1:2, :] += jnp.sum(dhv * hn, axis=0, keepdims=True)
        sums_ref[2:3, :] += jnp.sum(dhn * xn, axis=0, keepdims=True)
        sums_ref[3:4, :] += jnp.sum(dxpv * mix_ref[...].astype(F32), axis=0, keepdims=True)

    tile = _rows(S, D, ts=ts)
    in_specs = [tile, tile, tile, tile, vec, vec] + ([vec] if with_prev else [])
    args = [x, dh, dxp, mix, gain, sc] + ([gate_prev] if with_prev else [])
    out_specs = [tile] + ([tile] if with_prev else []) + [pl.BlockSpec((8, D), lambda i: (0, 0))]
    out_shape = ([jax.ShapeDtypeStruct((S, D), F32)] + ([jax.ShapeDtypeStruct((S, D), BF16)] if with_prev else [])
                 + [jax.ShapeDtypeStruct((8, D), F32)])
    outs = _call(
        body, deps, name=name, grid=(S // ts,), in_specs=in_specs, out_specs=out_specs, out_shape=out_shape,
        compiler_params=_params(("arbitrary",)),
    )(*args)
    return (outs[0], outs[1], outs[2]) if with_prev else (outs[0], None, outs[1])


def _fgate_fwd(h, wf, bf, *, name, deps=()):
    S, D = h.shape
    ts = _tile(S, 256, 16)

    def body(h_ref, w_ref, b_ref, z_ref, f_ref, carry):
        @pl.when(pl.program_id(0) == 0)
        def _():
            carry[...] = jnp.zeros_like(carry)
        z = lax.dot_general(h_ref[...], w_ref[...], (((1,), (1,)), ((), ())), preferred_element_type=F32) + b_ref[...]
        logf = jnp.minimum(z, 0.0) - jnp.log(1.0 + jnp.exp(-jnp.abs(z)))
        row = lax.broadcasted_iota(jnp.int32, (ts, ts), 0)
        col = lax.broadcasted_iota(jnp.int32, (ts, ts), 1)
        tril = (col <= row).astype(F32)
        run = jnp.dot(tril, logf, preferred_element_type=F32, precision=lax.Precision.HIGHEST) + carry[0:1, :]
        z_ref[...] = z
        f_ref[...] = run
        carry[0:1, :] = run[ts - 1:ts, :]

    return _call(
        body, deps, name=name, grid=(S // ts,),
        in_specs=[pl.BlockSpec((ts, D), lambda i: (i, 0)), pl.BlockSpec((LANES, D), lambda i: (0, 0)),
                  pl.BlockSpec((1, LANES), lambda i: (0, 0))],
        out_specs=[pl.BlockSpec((ts, LANES), lambda i: (i, 0))] * 2,
        out_shape=[jax.ShapeDtypeStruct((S, LANES), F32)] * 2,
        scratch_shapes=[pltpu.VMEM((8, LANES), F32)],
        compiler_params=_params(("arbitrary",)),
    )(h, wf, bf)


def _fgate_bwd(dfq, dfk, z, *, name, deps=()):
    S = z.shape[0]
    ts = _tile(S, 256, 16)
    n = S // ts

    def body(dq_ref, dk_ref, z_ref, dz_ref, sums_ref, carry):
        @pl.when(pl.program_id(0) == 0)
        def _():
            carry[...] = jnp.zeros_like(carry)
            sums_ref[...] = jnp.zeros_like(sums_ref)
        df = dq_ref[...] - dk_ref[...]
        row = lax.broadcasted_iota(jnp.int32, (ts, ts), 0)
        col = lax.broadcasted_iota(jnp.int32, (ts, ts), 1)
        triu = (col >= row).astype(F32)
        run = jnp.dot(triu, df, preferred_element_type=F32, precision=lax.Precision.HIGHEST) + carry[0:1, :]
        zv = z_ref[...]
        dz = run * (1.0 / (1.0 + jnp.exp(zv)))
        dz_ref[...] = dz.astype(BF16)
        sums_ref[0:1, :] += jnp.sum(dz, axis=0, keepdims=True)
        carry[0:1, :] = run[0:1, :]

    rev = pl.BlockSpec((ts, LANES), lambda i: (n - 1 - i, 0))
    return _call(
        body, deps, name=name, grid=(n,), in_specs=[rev, rev, rev],
        out_specs=[rev, pl.BlockSpec((8, LANES), lambda i: (0, 0))],
        out_shape=[jax.ShapeDtypeStruct((S, LANES), BF16), jax.ShapeDtypeStruct((8, LANES), F32)],
        scratch_shapes=[pltpu.VMEM((8, LANES), F32)],
        compiler_params=_params(("arbitrary",)),
    )(dfq, dfk, z)


def _head_col(ref, rows, lane_mask):
    return jnp.sum(jnp.where(lane_mask, ref[rows, :], 0.0), axis=1, keepdims=True)


def _attn_fwd(qkv, fk, *, heads, name, T=256, deps=()):
    S, D3 = qkv.shape
    D = D3 // 3
    dh = D // heads
    T = _tile(S, T, 16)
    nq = S // T
    scale = dh ** -0.5
    hp = fk.shape[1]

    def body(q_ref, k_ref, v_ref, fk_ref, o_ref, lse_ref):
        h = pl.program_id(0)

        @pl.when(h == 0)
        def _():
            lse_ref[...] = jnp.zeros_like(lse_ref)
        lane = lax.broadcasted_iota(jnp.int32, (1, LANES), 1) == h
        row = lax.broadcasted_iota(jnp.int32, (T, T), 0)
        col = lax.broadcasted_iota(jnp.int32, (T, T), 1)

        def q_block(qi, _):
            rows = pl.ds(pl.multiple_of(qi * T, T), T)
            q = q_ref[rows, :]

            def kv_block(kj, carry, diag):
                m, l, acc = carry
                cols = pl.ds(pl.multiple_of(kj * T, T), T)
                s = lax.dot_general(q, k_ref[cols, :], (((1,), (1,)), ((), ())), preferred_element_type=F32) * scale
                s = s - fk_ref[kj, pl.ds(h, 1), :]
                if diag:
                    s = jnp.where(col <= row, s, NEG)
                m_new = jnp.maximum(m, jnp.max(s, axis=1, keepdims=True))
                p = jnp.exp(s - m_new)
                alpha = jnp.exp(m - m_new)
                l = alpha * l + jnp.sum(p, axis=1, keepdims=True)
                acc = alpha * acc + jnp.dot(p.astype(BF16), v_ref[cols, :], preferred_element_type=F32)
                return m_new, l, acc

            init = (jnp.full((T, 1), NEG, F32), jnp.zeros((T, 1), F32), jnp.zeros((T, dh), F32))
            carry = lax.fori_loop(0, qi, lambda kj, cr: kv_block(kj, cr, False), init)
            m, l, acc = kv_block(qi, carry, True)
            o_ref[rows, :] = (acc / l).astype(BF16)
            lse_ref[rows, :] = jnp.where(lane, m + jnp.log(l), lse_ref[rows, :])
            return 0

        lax.fori_loop(0, nq, q_block, 0)

    head = lambda part: pl.BlockSpec((S, dh), lambda h: (0, part * heads + h))
    return _call(
        body, deps, name=name, grid=(heads,),
        in_specs=[head(0), head(1), head(2), pl.BlockSpec((nq, hp, T), lambda h: (0, 0, 0))],
        out_specs=[pl.BlockSpec((S, dh), lambda h: (0, h)), pl.BlockSpec((S, LANES), lambda h: (0, 0))],
        out_shape=[jax.ShapeDtypeStruct((S, D), BF16), jax.ShapeDtypeStruct((S, LANES), F32)],
        compiler_params=_params(("arbitrary",)),
    )(qkv, qkv, qkv, fk)


def _attn_bwd(qkv, o, do, fk, lse, *, heads, name, T=256, deps=()):
    S, D3 = qkv.shape
    D = D3 // 3
    dh = D // heads
    T = _tile(S, T, 16)
    nq = S // T
    scale = dh ** -0.5
    hp = fk.shape[1]

    def body(q_ref, k_ref, v_ref, o_ref, do_ref, fk_ref, lse_ref, dqkv_ref, dfq_ref, dfk_ref,
             dq_acc, lse_col, delta_col, dfq_col):
        h = pl.program_id(0)

        @pl.when(h == 0)
        def _():
            dfq_ref[...] = jnp.zeros_like(dfq_ref)
            dfk_ref[...] = jnp.zeros_like(dfk_ref)
        lane = lax.broadcasted_iota(jnp.int32, (1, LANES), 1) == h
        Th = T // 2 if T % 32 == 0 else T
        dq_acc[...] = jnp.zeros_like(dq_acc)
        dfq_col[...] = jnp.zeros_like(dfq_col)

        def prep(qi, _):
            rows = pl.ds(pl.multiple_of(qi * T, T), T)
            lse_col[rows, :] = _head_col(lse_ref, rows, lane)
            delta_col[rows, :] = jnp.sum(do_ref[rows, :].astype(F32) * o_ref[rows, :].astype(F32), axis=1, keepdims=True)
            return 0

        lax.fori_loop(0, nq, prep, 0)

        def kv_block(kj, _):
            cols = pl.ds(pl.multiple_of(kj * T, T), T)
            k, v = k_ref[cols, :], v_ref[cols, :]
            fk_row = fk_ref[kj, pl.ds(h, 1), :]

            def pair_grad(rows, kk, vv, fk_r, masked):
                q, dov = q_ref[rows, :], do_ref[rows, :]
                s = lax.dot_general(q, kk, (((1,), (1,)), ((), ())), preferred_element_type=F32) * scale
                s = s - fk_r
                p = jnp.exp(s - lse_col[rows, :])
                if masked:
                    p = jnp.where(lax.broadcasted_iota(jnp.int32, p.shape, 1) <= lax.broadcasted_iota(jnp.int32, p.shape, 0), p, 0.0)
                dp = lax.dot_general(dov, vv, (((1,), (1,)), ((), ())), preferred_element_type=F32)
                ds = p * (dp - delta_col[rows, :])
                dsb = ds.astype(BF16)
                dv = lax.dot_general(p.astype(BF16), dov, (((0,), (0,)), ((), ())), preferred_element_type=F32)
                dk = lax.dot_general(dsb, q, (((0,), (0,)), ((), ())), preferred_element_type=F32)
                dq_acc[rows, :] += jnp.dot(dsb, kk, preferred_element_type=F32)
                dfq_col[rows, :] += jnp.sum(ds, axis=1, keepdims=True)
                return dk, dv, jnp.sum(ds, axis=0, keepdims=True)

            def q_block(qi, carry):
                rows = pl.ds(pl.multiple_of(qi * T, T), T)
                return tuple(c + g for c, g in zip(carry, pair_grad(rows, k, v, fk_row, False)))

            strips = [pair_grad(pl.ds(pl.multiple_of(kj * T + i * Th, Th), T - i * Th), k[i * Th:(i + 1) * Th],
                                v[i * Th:(i + 1) * Th], fk_row[:, i * Th:(i + 1) * Th], True) for i in range(T // Th)]
            carry = (jnp.concatenate([g[0] for g in strips], axis=0), jnp.concatenate([g[1] for g in strips], axis=0),
                     jnp.concatenate([g[2] for g in strips], axis=1))
            dk, dv, dfk = lax.fori_loop(kj + 1, nq, q_block, carry)
            dqkv_ref[1, cols, :] = (dk * scale).astype(BF16)
            dqkv_ref[2, cols, :] = dv.astype(BF16)
            dfk_ref[kj, pl.ds(h, 1), :] = dfk
            return 0

        lax.fori_loop(0, nq, kv_block, 0)

        def finish(qi, _):
            rows = pl.ds(pl.multiple_of(qi * T, T), T)
            dqkv_ref[0, rows, :] = (dq_acc[rows, :] * scale).astype(BF16)
            dfq_ref[rows, :] = jnp.where(lane, dfq_col[rows, :], dfq_ref[rows, :])
            return 0

        lax.fori_loop(0, nq, finish, 0)

    head = lambda part: pl.BlockSpec((S, dh), lambda h: (0, part * heads + h))
    own = pl.BlockSpec((S, dh), lambda h: (0, h))
    full = pl.BlockSpec((S, LANES), lambda h: (0, 0))
    krow = pl.BlockSpec((nq, hp, T), lambda h: (0, 0, 0))
    return _call(
        body, deps, name=name, grid=(heads,),
        in_specs=[head(0), head(1), head(2), own, own, krow, full],
        out_specs=[pl.BlockSpec((3, S, dh), lambda h: (0, 0, h)), full, krow],
        out_shape=[jax.ShapeDtypeStruct((3, S, D), BF16), jax.ShapeDtypeStruct((S, LANES), F32),
                   jax.ShapeDtypeStruct((nq, hp, T), F32)],
        scratch_shapes=[pltpu.VMEM((S, dh), F32)] + [pltpu.VMEM((S, 1), F32)] * 3,
        compiler_params=_params(("arbitrary",)),
    )(qkv, qkv, qkv, o, do, fk, lse)


def _shift_down(v, n):
    rows = lax.broadcasted_iota(jnp.int32, v.shape, 0)
    return jnp.where(rows >= n, pltpu.roll(v, n, axis=0), 0.0)


def _shift_up(v, n):
    S = v.shape[0]
    rows = lax.broadcasted_iota(jnp.int32, v.shape, 0)
    return jnp.where(rows < S - n, pltpu.roll(v, S - n, axis=0), 0.0)


def _conv_fwd(proj, conv_w, *, name, cb=LANES, deps=()):
    S, D3 = proj.shape
    D = D3 // 3
    nb = D // cb

    def body(bg_ref, cg_ref, u_ref, w_ref, z_ref):
        uc = cg_ref[...].astype(F32) * u_ref[...].astype(F32)
        w = w_ref[...]
        y = w[2:3, :] * uc + w[1:2, :] * _shift_down(uc, 1) + w[0:1, :] * _shift_down(uc, 2)
        z_ref[...] = (bg_ref[...].astype(F32) * y).astype(BF16)

    part = lambda g: pl.BlockSpec((S, cb), lambda j: (0, g * nb + j))
    return _call(
        body, deps, name=name, grid=(nb,),
        in_specs=[part(0), part(1), part(2), pl.BlockSpec((3, cb), lambda j: (0, j))],
        out_specs=pl.BlockSpec((S, cb), lambda j: (0, j)),
        out_shape=jax.ShapeDtypeStruct((S, D), BF16), compiler_params=_params(("parallel",)),
    )(proj, proj, proj, conv_w)


def _conv_bwd(proj, conv_w, dz, *, name, cb=LANES, deps=()):
    S, D3 = proj.shape
    D = D3 // 3
    nb = D // cb

    def body(bg_ref, cg_ref, u_ref, w_ref, dz_ref, dp_ref, dw_ref):
        cg, u = cg_ref[...].astype(F32), u_ref[...].astype(F32)
        uc = cg * u
        w = w_ref[...]
        uc1, uc2 = _shift_down(uc, 1), _shift_down(uc, 2)
        y = w[2:3, :] * uc + w[1:2, :] * uc1 + w[0:1, :] * uc2
        dz = dz_ref[...].astype(F32)
        dp_ref[0] = (dz * y).astype(BF16)
        dy = dz * bg_ref[...].astype(F32)
        duc = w[2:3, :] * dy + w[1:2, :] * _shift_up(dy, 1) + w[0:1, :] * _shift_up(dy, 2)
        dp_ref[1] = (duc * u).astype(BF16)
        dp_ref[2] = (duc * cg).astype(BF16)
        dw_ref[...] = jnp.zeros_like(dw_ref)
        dw_ref[0:1, :] = jnp.sum(dy * uc2, axis=0, keepdims=True)
        dw_ref[1:2, :] = jnp.sum(dy * uc1, axis=0, keepdims=True)
        dw_ref[2:3, :] = jnp.sum(dy * uc, axis=0, keepdims=True)

    part = lambda g: pl.BlockSpec((S, cb), lambda j: (0, g * nb + j))
    return _call(
        body, deps, name=name, grid=(nb,),
        in_specs=[part(0), part(1), part(2), pl.BlockSpec((3, cb), lambda j: (0, j)), pl.BlockSpec((S, cb), lambda j: (0, j))],
        out_specs=[pl.BlockSpec((3, S, cb), lambda j: (0, 0, j)), pl.BlockSpec((8, cb), lambda j: (0, j))],
        out_shape=[jax.ShapeDtypeStruct((3, S, D), BF16), jax.ShapeDtypeStruct((8, D), F32)],
        compiler_params=_params(("parallel",)),
    )(proj, proj, proj, conv_w, dz)


def _ada_fwd(c_all, ada_w, *, name, deps=()):
    L, D, Ns = ada_w.shape
    tn = _tile(Ns, 512)

    def body(c_ref, w_ref, o_ref, act_ref):
        cv = c_ref[...]
        act = cv * (1.0 / (1.0 + jnp.exp(-cv)))
        act_ref[...] = act
        o_ref[...] = jnp.dot(act.astype(BF16), w_ref[...].astype(BF16), preferred_element_type=F32)

    return _call(
        body, deps, name=name, grid=(L, Ns // tn),
        in_specs=[pl.BlockSpec((N_DEV, D), lambda l, j: (0, 0)), pl.BlockSpec((None, D, tn), lambda l, j: (l, 0, j))],
        out_specs=[pl.BlockSpec((None, N_DEV, tn), lambda l, j: (l, 0, j)), pl.BlockSpec((N_DEV, D), lambda l, j: (0, 0))],
        out_shape=[jax.ShapeDtypeStruct((L, N_DEV, Ns), F32), jax.ShapeDtypeStruct((N_DEV, D), F32)],
        compiler_params=_params(("arbitrary", "arbitrary")),
    )(c_all, ada_w)


def _select_mod(gathered, *, name, deps=()):
    _, LB, Ns = gathered.shape
    L = LB // N_DEV

    def body(g_ref, o_ref):
        x, y, c = _me()
        b = 4 * x + 2 * y + c
        for j in range(N_CHIPS):
            for l in range(L):
                o_ref[j, pl.ds(l, 1), :] = g_ref[2 * j + c, pl.ds(l * N_DEV + b, 1), :]

    return _call(
        body, deps, name=name, out_shape=jax.ShapeDtypeStruct((N_CHIPS, L, Ns), F32),
        in_specs=[pl.BlockSpec(memory_space=pltpu.VMEM)], out_specs=pl.BlockSpec(memory_space=pltpu.VMEM),
        compiler_params=_params(),
    )(gathered)


def _adamw_math(w, g, m, v):
    m = ADAM_B1 * m + (1.0 - ADAM_B1) * g
    v = ADAM_B2 * v + (1.0 - ADAM_B2) * jnp.square(g)
    m_hat = m / (1.0 - ADAM_B1 ** ADAM_STEP)
    v_hat = v / (1.0 - ADAM_B2 ** ADAM_STEP)
    delta = -ADAM_LR * (m_hat / (jnp.sqrt(v_hat) + ADAM_EPS) + ADAM_WD * w)
    return delta, m, v


def _adamw_shards(w, m, v, groups, chip, *, name):
    L, R, C = w.shape
    if R % 16 == 0:
        tr, tc = _tile(R, 128, 16), C
    else:
        tr, tc = R, _tile(C, 256)
    nr, nc = R // tr, C // tc

    def body(chip_ref, w_ref, m_ref, v_ref, *rest):
        srcs, (g_ref, d_ref, mo_ref, vo_ref) = rest[:2 * N_CHIPS * L], rest[2 * N_CHIPS * L:]
        for l in range(L):
            @pl.when(pl.program_id(0) == l)
            def _():
                s = srcs[2 * N_CHIPS * l:2 * N_CHIPS * (l + 1)]
                mine, other = s[0][...].astype(F32), s[N_CHIPS][...].astype(F32)
                for k in range(1, N_CHIPS):
                    mine = mine + s[k][...].astype(F32)
                    other = other + s[N_CHIPS + k][...].astype(F32)
                g = mine + other
                delta, mn, vn = _adamw_math(w_ref[...], g, m_ref[...], v_ref[...])
                g_ref[...] = g
                d_ref[...] = delta
                mo_ref[...] = mn
                vo_ref[...] = vn

    tile = pl.BlockSpec((None, tr, tc), lambda l, i, j, chip_ref: (l, i, j))

    def block(layer, k):
        def index(l, i, j, chip_ref):
            idle_i, idle_j = jnp.where(l < layer, 0, nr - 1), jnp.where(l < layer, 0, nc - 1)
            return (jnp.bitwise_xor(chip_ref[0], k), jnp.where(l == layer, i, idle_i), jnp.where(l == layer, j, idle_j))
        return pl.BlockSpec((None, tr, tc), index)

    in_specs, args = [tile] * 3, [w, m, v]
    for layer, (parts, land, sib) in enumerate(groups):
        in_specs += [block(layer, k) for k in range(N_CHIPS)] * 2
        args += [parts, land, land, land, sib, sib, sib, sib]
    return pl.pallas_call(
        body, name=name,
        grid_spec=pltpu.PrefetchScalarGridSpec(num_scalar_prefetch=1, grid=(L, nr, nc), in_specs=in_specs, out_specs=[tile] * 4),
        out_shape=[jax.ShapeDtypeStruct((L, R, C), F32)] * 4, compiler_params=_params(("arbitrary", "arbitrary", "arbitrary")),
    )(chip, *args)


def _adamw_ada(w, m, v, act_t, dmod, *, name, tr=256, deps=()):
    L, D, Ns = w.shape
    tr = _tile(D, tr, 8)

    def body(w_ref, m_ref, v_ref, a_ref, d_ref, g_ref, dl_ref, mo_ref, vo_ref):
        x, y, _ = _me()
        g = jnp.dot(a_ref[...], d_ref[2 * x + y], preferred_element_type=F32, precision=lax.Precision.HIGHEST)
        delta, mn, vn = _adamw_math(w_ref[...], g, m_ref[...], v_ref[...])
        g_ref[...] = g
        dl_ref[...] = delta
        mo_ref[...] = mn
        vo_ref[...] = vn

    tile = pl.BlockSpec((None, tr, Ns), lambda l, i: (l, i, 0))
    return _call(
        body, deps, name=name, grid=(L, D // tr),
        in_specs=[tile] * 3 + [pl.BlockSpec((tr, N_DEV), lambda l, i: (i, 0)),
                               pl.BlockSpec((N_CHIPS, None, N_DEV, Ns), lambda l, i: (0, l, 0, 0))],
        out_specs=[tile] * 4, out_shape=[jax.ShapeDtypeStruct((L, D, Ns), F32)] * 4,
        compiler_params=_params(("parallel", "parallel")),
    )(w, m, v, act_t, dmod)


def _adamw_small(w, m, v, gathered, *, rows, name, deps=()):
    n, D = w.shape

    def body(w_ref, m_ref, v_ref, s_ref, g_ref, d_ref, mo_ref, vo_ref):
        for r, src in enumerate(rows):
            g = s_ref[0, src:src + 1, :]
            for d in range(1, N_DEV):
                g = g + s_ref[d, src:src + 1, :]
            g_ref[r:r + 1, :] = g
        g = g_ref[...]
        delta, mn, vn = _adamw_math(w_ref[...], g, m_ref[...], v_ref[...])
        d_ref[...] = delta
        mo_ref[...] = mn
        vo_ref[...] = vn

    vm = pl.BlockSpec(memory_space=pltpu.VMEM)
    return _call(
        body, deps, name=name, in_specs=[vm] * 4, out_specs=[vm] * 4,
        out_shape=[jax.ShapeDtypeStruct((n, D), F32)] * 4, compiler_params=_params(),
    )(w, m, v, gathered)


def _adamw_conv_w(w, m, v, gathered4, *, name, deps=()):
    Cs = w.shape[1]

    def body(w_ref, m_ref, v_ref, s_ref, g_ref, d_ref, mo_ref, vo_ref):
        x, y, _ = _me()
        j = 2 * x + y
        g = s_ref[j, 0]
        for d in range(1, N_DEV):
            g = g + s_ref[j, d]
        delta, mn, vn = _adamw_math(w_ref[...], g, m_ref[...], v_ref[...])
        g_ref[...] = g
        d_ref[...] = delta
        mo_ref[...] = mn
        vo_ref[...] = vn

    vm = pl.BlockSpec(memory_space=pltpu.VMEM)
    return _call(
        body, deps, name=name, in_specs=[vm] * 4, out_specs=[vm] * 4,
        out_shape=[jax.ShapeDtypeStruct((8, Cs), F32)] * 4, compiler_params=_params(),
    )(w, m, v, gathered4)


def _loss_sum(gathered, *, row, name, deps=()):
    _, _, D = gathered.shape

    def body(s_ref, o_ref):
        t = s_ref[0, row:row + 1, :]
        for d in range(1, N_DEV):
            t = t + s_ref[d, row:row + 1, :]
        o_ref[...] = jnp.broadcast_to(t, (8, D))

    vm = pl.BlockSpec(memory_space=pltpu.VMEM)
    return pl.pallas_call(body, name=name, in_specs=[vm], out_specs=vm, out_shape=jax.ShapeDtypeStruct((8, D), F32),
                          compiler_params=_params())(gathered)


def _pad_rows(a, n):
    return jnp.pad(a, ((0, n - a.shape[0]), (0, 0)))


def kernel(x, c, ada_w, ada_b, norm_mix, norm_mlp, fox_w_in, fox_b_f, fox_w_out, conv_w_in, conv_w, conv_w_out, mlp_w_up, mlp_w_down, final_norm, loss_target, m_ada_w, m_ada_b, m_norm_mix, m_norm_mlp, m_fox_w_in, m_fox_b_f, m_fox_w_out, m_conv_w_in, m_conv_w, m_conv_w_out, m_mlp_w_up, m_mlp_w_down, m_final_norm, v_ada_w, v_ada_b, v_norm_mix, v_norm_mlp, v_fox_w_in, v_fox_b_f, v_fox_w_out, v_conv_w_in, v_conv_w, v_conv_w_out, v_mlp_w_up, v_mlp_w_down, v_final_norm):
    S, D = x.shape[1], x.shape[2]
    H = fox_b_f.shape[-1]
    L = ada_w.shape[0]
    NM = ada_b.shape[1] // D
    Ns_ada = ada_w.shape[2]
    Cs_fox = fox_w_in.shape[2]
    Cs_conv = conv_w.shape[2]
    x0 = x[0]
    target = loss_target[0]

    chip = (2 * lax.axis_index("x") + lax.axis_index("y")).astype(jnp.int32).reshape(1)

    fin_t = jnp.transpose(fox_w_in, (0, 2, 1))
    shards = dict(fin=(fin_t, 0), fout=(fox_w_out, 0), up0=(mlp_w_up, 0), dn0=(mlp_w_down, 0), cin=(conv_w_in, 0),
                  cout=(conv_w_out, 0), up1=(mlp_w_up, 1), dn1=(mlp_w_down, 1))
    halves = dict(fin="cols", fout="rows", up0="cols", dn0="rows", cin="cols", cout="rows", up1="cols", dn1="rows")
    gathers, placed = {}, {}

    def place(key, dep=None):
        placed[key] = _place_cast(*shards[key], chip, halves=halves[key], name="place_" + key, dep=dep)
        return placed[key]

    def start_gather(key, dep=None):
        gathers[key] = _split_start("gather1", [(placed[key],)], name="gather_start_" + key, dep=dep)
        return gathers[key][3]

    def pass_gather(key, after):
        landed = _split_wait("gather1", gathers[key], after, name="gather_landed_" + key)
        gathers[key] = _split_start("gather2", landed, name="gather_pass_" + key)
        return gathers[key][3]

    def gathered(key, after):
        return _split_wait("gather2", gathers[key], after, name="gather_wait_" + key)[0][0]

    fin_placed = place("fin")
    c_all = _allgather8(_pad_rows(c, 8), name="gather_c", deps=(fin_placed,))[:, 0, :]
    mod_part, c_act = _ada_fwd(c_all, ada_w, name="ada_fwd")
    mod_all = _allgather8(mod_part.reshape(L * N_DEV, Ns_ada), name="gather_mod")
    conv_w_all = _allgather8(_pad_rows(conv_w[0], 8), name="gather_conv_w", deps=(mod_all,))
    conv_w_full = jnp.transpose(conv_w_all[0::2, :3, :], (1, 0, 2)).reshape(3, D)

    tok = start_gather("fin", conv_w_all)
    for key in ("fout", "up0", "dn0", "cin", "cout", "up1", "dn1"):
        place(key, tok)
        tok = start_gather(key, tok)

    mod = _select_mod(mod_all, name="select_mod", deps=(tok,))
    mod = jnp.transpose(mod, (1, 0, 2)).reshape(L, NM, 1, D) + ada_b.reshape(L, NM, 1, D)

    def vec(a):
        return a.reshape(1, D)

    h0 = _norm_fwd(x0, vec(norm_mix[0]), mod[0, 1], mod[0, 0], name="norm_mix0")
    tok = pass_gather("fin", [h0])
    w_fin_t = jnp.transpose(gathered("fin", [tok]), (0, 2, 1, 3)).reshape(N_CHIPS * Cs_fox, D)
    w_f_t = _pad_rows(w_fin_t[3 * D:], LANES)
    tok = pass_gather("fout", [w_fin_t])
    qkv = _mm_nt(h0, w_fin_t, n=3 * D, name="fox_in", out_dtype=BF16, deps=(tok,))
    b_f = jnp.pad(fox_b_f, ((0, 0), (0, LANES - H)))
    z_f, F_col = _fgate_fwd(h0, w_f_t, b_f, name="fgate_fwd")
    hp = max(8, H)
    at = _tile(S, 1024, 16)
    F_rows = _pad_rows(jnp.transpose(F_col[:, :H]), hp)
    F_row = jnp.transpose(F_rows.reshape(hp, S // at, at), (1, 0, 2))
    o, lse = _attn_fwd(qkv, F_row, heads=H, name="attn_fwd", T=at)
    w_fout = gathered("fout", [o]).reshape(D, D)
    tok = pass_gather("up0", [o])
    tok = pass_gather("dn0", [tok])
    x1, mix0, h1 = _mm_nn(o, w_fout, name="fox_out", epilogue="resid_norm", res=x0, gate=mod[0, 2],
                          norm=(vec(norm_mlp[0]), mod[0, 4], mod[0, 3]), tm=512, tn=D, deps=(tok,))
    w_up0 = gathered("up0", [h1]).reshape(2 * N_CHIPS, D, -1)
    tok = pass_gather("cin", [h1])
    u0, a0 = _mm_nn(h1, w_up0, name="mlp_up0", epilogue="relu2", deps=(tok,))
    w_dn0 = gathered("dn0", [a0]).reshape(-1, D)
    tok = pass_gather("cout", [a0])
    x2, y0 = _mm_nn(a0, w_dn0, name="mlp_down0", epilogue="resid", res=x1, gate=mod[0, 5], deps=(tok,))
    h2 = _norm_fwd(x2, vec(norm_mix[1]), mod[1, 1], mod[1, 0], name="norm_mix1")
    g_cin = gathered("cin", [h2]).reshape(2 * N_CHIPS, D, -1)
    tok = pass_gather("up1", [h2])
    proj = _mm_nn(h2, g_cin, name="conv_in", deps=(tok,))
    w_cin = jnp.transpose(g_cin, (1, 0, 2)).reshape(D, 3 * D)
    zc = _conv_fwd(proj, conv_w_full, name="conv_fwd")
    w_cout = gathered("cout", [zc]).reshape(D, D)
    tok = pass_gather("dn1", [zc])
    x3, mix1, h3 = _mm_nn(zc, w_cout, name="conv_out", epilogue="resid_norm", res=x2, gate=mod[1, 2],
                          norm=(vec(norm_mlp[1]), mod[1, 4], mod[1, 3]), tm=512, tn=D, deps=(tok,))
    w_up1 = gathered("up1", [h3]).reshape(2 * N_CHIPS, D, -1)
    u1, a1 = _mm_nn(h3, w_up1, name="mlp_up1", epilogue="relu2")
    w_dn1 = gathered("dn1", [a1]).reshape(-1, D)
    x4, y1 = _mm_nn(a1, w_dn1, name="mlp_down1", epilogue="resid", res=x3, gate=mod[1, 5])
    w_up = [jnp.transpose(w_up0, (1, 0, 2)).reshape(D, -1), jnp.transpose(w_up1, (1, 0, 2)).reshape(D, -1)]
    w_dn = [w_dn0, w_dn1]

    dx4, dy1, sums_f = _loss_bwd(x4, target, vec(final_norm), mod[1, 5], name="loss_bwd")
    du1 = _mm_nt(dy1, w_dn[1], name="mlp_down1_dx", epilogue="drelu2", extra=u1, out_dtype=BF16)
    def start_scatter(tag, parts_list):
        groups = [(p, lax.empty(p.shape, p.dtype)) for p in parts_list]
        return _split_start("scatter", groups, name="scatter_start_" + tag)

    def start_sibling(tag, scatter, after):
        landed = _split_wait("scatter", scatter, after, name="scatter_wait_" + tag)
        groups = [(p, ld, lax.empty(p.shape, p.dtype)) for p, ld in landed]
        return _split_start("sibling", groups, name="sibling_start_" + tag)

    gw_dn1 = _mm_tn(a1, dy1, name="mlp_down1_dw")
    gw_up1 = _mm_tn(h3, du1, name="mlp_up1_dw", out_parts=N_CHIPS)
    sc1 = start_scatter("mlp1", [gw_dn1.reshape(N_CHIPS, -1, D), gw_up1])
    dh3 = _mm_nt(du1, w_up[1], name="mlp_up1_dx", out_dtype=BF16, tk=4096, deps=(sc1[3],))
    dx3, dmix1, sums_mlp1 = _norm_bwd(x3, dh3, dx4, y1, vec(norm_mlp[1]), mod[1, 4], mod[1, 2], name="norm_mlp1_bwd")
    dzc = _mm_nt(dmix1, w_cout, name="conv_out_dx", out_dtype=BF16)
    gw_cout = _mm_tn(zc, dmix1, name="conv_out_dw")
    dproj, dconv_w = _conv_bwd(proj, conv_w_full, dzc, name="conv_bwd")
    gw_cin = _mm_tn(h2, dproj, name="conv_in_dw", out_parts=N_CHIPS, tn=512)
    sc2 = start_scatter("conv", [gw_cout.reshape(N_CHIPS, -1, D), gw_cin])
    dh2 = _mm_nt(dproj, w_cin, name="conv_in_dx", out_dtype=BF16, deps=(sc2[3],))
    dx2, dy0, sums_mix1 = _norm_bwd(x2, dh2, dx3, mix1, vec(norm_mix[1]), mod[1, 1], mod[0, 5], name="norm_mix1_bwd")
    du0 = _mm_nt(dy0, w_dn[0], name="mlp_down0_dx", epilogue="drelu2", extra=u0, out_dtype=BF16)
    gw_dn0 = _mm_tn(a0, dy0, name="mlp_down0_dw")
    gw_up0 = _mm_tn(h1, du0, name="mlp_up0_dw", out_parts=N_CHIPS)
    sc3 = start_scatter("mlp0", [gw_dn0.reshape(N_CHIPS, -1, D), gw_up0])
    sb1 = start_sibling("mlp1", sc1, [sc3[3]])
    dh1 = _mm_nt(du0, w_up[0], name="mlp_up0_dx", out_dtype=BF16, tk=4096, deps=(sb1[3],))
    dx1, dmix0, sums_mlp0 = _norm_bwd(x1, dh1, dx2, y0, vec(norm_mlp[0]), mod[0, 4], mod[0, 2], name="norm_mlp0_bwd")
    do = _mm_nt(dmix0, w_fout, name="fox_out_dx", out_dtype=BF16)
    gw_fout = _mm_tn(o, dmix0, name="fox_out_dw")
    dqkv, dfq, dfk = _attn_bwd(qkv, o, do, F_row, lse, heads=H, name="attn_bwd", T=at)
    dfk_col = jnp.pad(jnp.transpose(jnp.transpose(dfk, (1, 0, 2)).reshape(hp, S)[:H]), ((0, 0), (0, LANES - H)))
    sb3 = start_sibling("mlp0", sc3, [dqkv])
    dz_f, sums_bf = _fgate_bwd(dfq, dfk_col, z_f, name="fgate_bwd", deps=(sb3[3],))
    gw_qkv_t = _mm_tn(dqkv, h0, name="fox_in_dw", out_rows=3 * D + H)
    gw_f_t = _mm_tn(dz_f, h0, name="fox_gate_dw")
    gw_fin_t = _put_rows(gw_qkv_t, gw_f_t, 3 * D, H, name="fox_gate_dw_rows").reshape(N_CHIPS, Cs_fox, D)
    sc4 = start_scatter("fox", [gw_fout.reshape(N_CHIPS, -1, D), gw_fin_t])
    sb2 = start_sibling("conv", sc2, [sc4[3]])
    dh0_f = _mm_nn(dz_f, w_f_t, name="fox_gate_dx", out_dtype=F32, deps=(sb2[3],))
    dh0 = _mm_nn(dqkv, w_fin_t, name="fox_in_dx", epilogue="add", res=dh0_f, out_dtype=BF16)
    grad_x, _, sums_mix0 = _norm_bwd(x0, dh0, dx1, mix0, vec(norm_mix[0]), mod[0, 1], None, name="norm_mix0_bwd")

    outs = {}

    def put(name_, res, shape):
        for kind, r in zip(("grad", "delta", "new_m", "new_v"), res):
            outs[kind + "_" + name_] = r.reshape(shape)

    def shards_update(tag, w_, m_, v_, groups):
        return _adamw_shards(w_, m_, v_, groups, chip, name="adamw_" + tag)

    g_conv = _split_wait("sibling", sb2, [grad_x], name="sibling_wait_conv")
    put("conv_w_out", shards_update("conv_out", conv_w_out, m_conv_w_out, v_conv_w_out, g_conv[0:1]), conv_w_out.shape)
    r_cin = shards_update("conv_in", conv_w_in, m_conv_w_in, v_conv_w_in, g_conv[1:2])
    put("conv_w_in", r_cin, conv_w_in.shape)
    g_mlp1 = _split_wait("sibling", sb1, [r_cin[0]], name="sibling_wait_mlp1")
    g_mlp0 = _split_wait("sibling", sb3, [r_cin[0]], name="sibling_wait_mlp0")
    put("mlp_w_down", shards_update("mlp_down", mlp_w_down, m_mlp_w_down, v_mlp_w_down, [g_mlp0[0], g_mlp1[0]]), mlp_w_down.shape)
    r_up = shards_update("mlp_up", mlp_w_up, m_mlp_w_up, v_mlp_w_up, [g_mlp0[1], g_mlp1[1]])
    put("mlp_w_up", r_up, mlp_w_up.shape)
    sb4 = start_sibling("fox", sc4, [r_up[0]])

    dmod_rows = []
    for sm, sl in ((sums_mix0, sums_mlp0), (sums_mix1, sums_mlp1)):
        dmod_rows += [sm[0:1], sm[1:2], sm[3:4], sl[0:1], sl[1:2], sl[3:4]]
    bf_row = jnp.pad(sums_bf[0:1], ((0, 0), (0, D - LANES)))
    small = jnp.concatenate([sums_mix0[2:3], sums_mix1[2:3], sums_mlp0[2:3], sums_mlp1[2:3], sums_f[0:1], sums_f[1:2], bf_row,
                             jnp.zeros((1, D), F32)] + dmod_rows + [dconv_w[0:3]], axis=0)
    small_all = _allgather8(_pad_rows(small, -(-small.shape[0] // 8) * 8), name="gather_small", deps=(sb4[3],))
    loss = _loss_sum(small_all, row=5, name="loss_sum")[0, 0]

    def rows_of(a_mix, a_mlp, a_fin, a_bf, a_ada):
        return jnp.concatenate([a_mix, a_mlp, a_fin.reshape(1, D), jnp.pad(a_bf, ((0, 0), (0, D - H))),
                                a_ada.reshape(L * NM, D)], axis=0)
    n_small = 2 * L + 2 + L * NM
    rw = -(-n_small // 8) * 8
    w_s = _pad_rows(rows_of(norm_mix, norm_mlp, final_norm, fox_b_f, ada_b), rw)
    m_s = _pad_rows(rows_of(m_norm_mix, m_norm_mlp, m_final_norm, m_fox_b_f, m_ada_b), rw)
    v_s = _pad_rows(rows_of(v_norm_mix, v_norm_mlp, v_final_norm, v_fox_b_f, v_ada_b), rw)
    src_rows = [0, 1, 2, 3, 4, 6] + [8 + r for r in range(L * NM)] + [7] * (rw - n_small)
    res_s = _adamw_small(w_s, m_s, v_s, small_all, rows=tuple(src_rows), name="adamw_small")
    for kind, r in zip(("grad", "delta", "new_m", "new_v"), res_s):
        outs[kind + "_norm_mix"] = r[0:L]
        outs[kind + "_norm_mlp"] = r[L:2 * L]
        outs[kind + "_final_norm"] = r[2 * L]
        outs[kind + "_fox_b_f"] = r[2 * L + 1:2 * L + 2, :H]
        outs[kind + "_ada_b"] = r[2 * L + 2:n_small].reshape(L, NM * D)

    dmod_all = small_all[:, 8:8 + L * NM, :].reshape(N_DEV, L, N_CHIPS, Ns_ada)
    dmod4 = jnp.transpose(dmod_all, (2, 1, 0, 3))
    act_t = jnp.transpose(c_act)
    res_a = _adamw_ada(ada_w, m_ada_w, v_ada_w, act_t, dmod4, name="adamw_ada")
    put("ada_w", res_a, ada_w.shape)

    r0 = 8 + L * NM
    dconv_all = jnp.pad(small_all[:, r0:r0 + 3, :], ((0, 0), (0, 5), (0, 0)))
    dconv4 = jnp.transpose(dconv_all.reshape(N_DEV, 8, N_CHIPS, Cs_conv), (2, 0, 1, 3))
    res_c = _adamw_conv_w(_pad_rows(conv_w[0], 8), _pad_rows(m_conv_w[0], 8), _pad_rows(v_conv_w[0], 8), dconv4,
                          name="adamw_conv_w")
    for kind, r in zip(("grad", "delta", "new_m", "new_v"), res_c):
        outs[kind + "_conv_w"] = r[:3].reshape(conv_w.shape)

    g_fox = _split_wait("sibling", sb4, [res_a[0], res_c[0], res_s[0]], name="sibling_wait_fox")
    put("fox_w_out", shards_update("fox_out", fox_w_out, m_fox_w_out, v_fox_w_out, g_fox[0:1]), fox_w_out.shape)
    t3 = lambda a: jnp.transpose(a, (0, 2, 1))
    for kind, r in zip(("grad", "delta", "new_m", "new_v"),
                       shards_update("fox_in", t3(fox_w_in), t3(m_fox_w_in), t3(v_fox_w_in), g_fox[1:2])):
        outs[kind + "_fox_w_in"] = t3(r)

    names = ["ada_w", "ada_b", "norm_mix", "norm_mlp", "fox_w_in", "fox_b_f", "fox_w_out", "conv_w_in", "conv_w", "conv_w_out",
             "mlp_w_up", "mlp_w_down", "final_norm"]
    return (loss, grad_x[None], *[outs["grad_" + n] for n in names], *[outs["delta_" + n] for n in names],
            *[outs["new_m_" + n] for n in names], *[outs["new_v_" + n] for n in names])
```

```python
import jax
import jax.numpy as jnp
from jax import lax
from jax.experimental import pallas as pl
from jax.experimental.pallas import tpu as pltpu

F32 = jnp.float32
BF16 = jnp.bfloat16
MESH = pl.DeviceIdType.MESH
ANY = pl.BlockSpec(memory_space=pl.ANY)
HBM = pl.BlockSpec(memory_space=pltpu.HBM)
SEM = pl.BlockSpec(memory_space=pltpu.SEMAPHORE)
EFFECT = pltpu.SideEffectType.DATAFLOW_SIDE_EFFECTING

RMS_EPS = 1e-6
ADAM_LR = 0.001
ADAM_B1 = 0.9
ADAM_B2 = 0.999
ADAM_EPS = 1e-08
ADAM_WD = 0.01
ADAM_STEP = 10
N_CHIPS = 4
N_DEV = 8
LANES = 128
VMEM_LIMIT = 56 * 1024 * 1024
NEG = -1e30


def _params(sems=None, vmem=VMEM_LIMIT):
    return pltpu.CompilerParams(dimension_semantics=sems, vmem_limit_bytes=vmem)


def _tile(n, pref, unit=LANES):
    if n <= pref:
        return n
    t = (pref // unit) * unit
    while n % t:
        t -= unit
    return t


def _me():
    return lax.axis_index("x"), lax.axis_index("y"), lax.axis_index("c")


def _call(body, deps, **kw):
    nd = len(deps)

    def wrapped(*refs):
        body(*refs[nd:])

    kw["in_specs"] = [ANY] * nd + list(kw["in_specs"])
    fn = pl.pallas_call(wrapped, **kw)
    return lambda *args: fn(*deps, *args)


def _allgather8(v, *, name, deps=()):
    R, C = v.shape

    def body(v_ref, out_ref, send_sems, recv_sems):
        x, y, c = _me()
        me = 4 * x + 2 * y + c
        out_ref[me] = v_ref[...]
        copies = []
        for k in range(1, N_DEV):
            px, py, pc = (x + (k >> 2)) % 2, (y + ((k >> 1) & 1)) % 2, (c + (k & 1)) % 2
            copies.append(pltpu.make_async_remote_copy(
                src_ref=v_ref, dst_ref=out_ref.at[me], send_sem=send_sems.at[k - 1], recv_sem=recv_sems.at[k - 1],
                device_id=(px, py, pc), device_id_type=MESH))
        for cp in copies:
            cp.start()
        for k in range(1, N_DEV):
            px, py, pc = (x + (k >> 2)) % 2, (y + ((k >> 1) & 1)) % 2, (c + (k & 1)) % 2
            peer = 4 * px + 2 * py + pc
            pltpu.make_async_remote_copy(
                src_ref=v_ref, dst_ref=out_ref.at[peer], send_sem=send_sems.at[k - 1], recv_sem=recv_sems.at[k - 1],
                device_id=(px, py, pc), device_id_type=MESH).wait_recv()
        for cp in copies:
            cp.wait_send()

    return _call(
        body, deps, name=name,
        out_shape=jax.ShapeDtypeStruct((N_DEV, R, C), v.dtype),
        in_specs=[pl.BlockSpec(memory_space=pltpu.VMEM)],
        out_specs=pl.BlockSpec(memory_space=pltpu.VMEM),
        scratch_shapes=[pltpu.SemaphoreType.DMA((N_DEV - 1,)), pltpu.SemaphoreType.DMA((N_DEV - 1,))],
        compiler_params=_params(),
    )(v)


def _chip_peers(x, y):
    return [((x + (k >> 1)) % 2, (y + (k & 1)) % 2) for k in range(1, N_CHIPS)]


def _slot(x, y, k):
    return 2 * ((x + (k >> 1)) % 2) + (y + (k & 1)) % 2


def _split_copies(kind, groups, send_sems, recv_sems):
    x, y, c = _me()
    j = 2 * x + y
    copies = []
    for a, g in enumerate(groups):
        if kind == "sibling":
            parts, land, sib = g
            for k in range(N_CHIPS):
                s = _slot(x, y, k)
                copies.append(pltpu.make_async_remote_copy(
                    src_ref=(parts if k == 0 else land).at[s], dst_ref=sib.at[s], send_sem=send_sems.at[N_CHIPS * a + k],
                    recv_sem=recv_sems.at[N_CHIPS * a + k], device_id=(x, y, 1 - c), device_id_type=MESH))
            continue
        land = g[-1]
        for k, (px, py) in enumerate(_chip_peers(x, y)):
            if kind == "gather1":
                src, dst, to = land.at[j, c], land.at[j, c], (px, py, c)
            elif kind == "gather2":
                src, dst, to = land.at[2 * px + py, c], land.at[2 * px + py, c], (x, y, 1 - c)
            else:
                src, dst, to = g[0].at[2 * px + py], land.at[j], (px, py, c)
            copies.append(pltpu.make_async_remote_copy(
                src_ref=src, dst_ref=dst, send_sem=send_sems.at[3 * a + k], recv_sem=recv_sems.at[3 * a + k],
                device_id=to, device_id_type=MESH))
    return copies


def _split_start(kind, groups, *, name, dep=None):
    flat = [a for g in groups for a in g]
    nf, per = len(flat), len(groups[0])
    ncp = len(groups) * (N_CHIPS if kind == "sibling" else 3)
    nd = 0 if dep is None else 1

    def body(*refs):
        ins = refs[nd:nd + nf]
        send_sems, recv_sems, token = refs[nd + nf], refs[nd + nf + 1], refs[-1]
        for cp in _split_copies(kind, [ins[i:i + per] for i in range(0, nf, per)], send_sems, recv_sems):
            cp.start()
        token[...] = jnp.zeros_like(token)

    outs = pl.pallas_call(
        body, name=name,
        out_shape=(pltpu.SemaphoreType.DMA((ncp,)), pltpu.SemaphoreType.DMA((ncp,)), *[pltpu.HBM(a.shape, a.dtype) for a in flat],
                   jax.ShapeDtypeStruct((8, LANES), F32)),
        in_specs=[ANY] * nd + [HBM] * nf,
        out_specs=(SEM, SEM, *[HBM] * nf, pl.BlockSpec(memory_space=pltpu.VMEM)),
        input_output_aliases={nd + i: 2 + i for i in range(nf)},
        compiler_params=pltpu.CompilerParams(has_side_effects=EFFECT),
    )(*([dep] if nd else []), *[pltpu.with_memory_space_constraint(a, pltpu.HBM) for a in flat])
    thru = list(outs[2:2 + nf])
    return outs[0], outs[1], [tuple(thru[i:i + per]) for i in range(0, nf, per)], outs[-1]


def _split_wait(kind, started, after, *, name):
    send_sems, recv_sems, groups, _ = started
    flat = [a for g in groups for a in g]
    nf, per = len(flat), len(groups[0])

    def body(*refs):
        ins = refs[:nf]
        for cp in _split_copies(kind, [ins[i:i + per] for i in range(0, nf, per)], refs[nf], refs[nf + 1]):
            cp.wait_send()
            cp.wait_recv()

    outs = pl.pallas_call(
        body, name=name,
        out_shape=tuple(pltpu.HBM(a.shape, a.dtype) for a in flat),
        in_specs=[HBM] * nf + [SEM, SEM] + [ANY] * len(after), out_specs=tuple([HBM] * nf),
        input_output_aliases={i: i for i in range(nf)},
        compiler_params=pltpu.CompilerParams(has_side_effects=EFFECT),
    )(*flat, send_sems, recv_sems, *after)
    outs = list(outs)
    return [tuple(outs[i:i + per]) for i in range(0, nf, per)]


def _place_cast(shards, layer, chip, *, halves, name, dep=None):
    _, R, C = shards.shape
    if halves == "rows":
        hr, hc = R // 2, C
    else:
        hr, hc = R, C // 2
    if hr % 16 == 0:
        tr, tc = _tile(hr, 512, 16), hc
    else:
        tr, tc = hr, _tile(hc, 256)
    nr, nc = hr // tr, hc // tc

    def body(chip_ref, x_ref, *rest):
        rest[-1][...] = x_ref[...].astype(BF16)

    if halves == "rows":
        o_map = lambda i, j, chip_ref: (chip_ref[0], i // nr, i % nr, j)
    else:
        o_map = lambda i, j, chip_ref: (chip_ref[0], j // nc, i, j % nc)
    return pl.pallas_call(
        body, name=name,
        grid_spec=pltpu.PrefetchScalarGridSpec(
            num_scalar_prefetch=1, grid=(R // tr, C // tc),
            in_specs=[pl.BlockSpec((None, tr, tc), lambda i, j, chip_ref: (layer, i, j))] + ([] if dep is None else [ANY]),
            out_specs=pl.BlockSpec((None, None, tr, tc), o_map)),
        out_shape=jax.ShapeDtypeStruct((N_CHIPS, 2, hr, hc), BF16), compiler_params=_params(("parallel", "parallel")),
    )(chip, shards, *([] if dep is None else [dep]))


def _accumulate(part, acc_ref, nk, finalize):
    if nk == 1:
        finalize(part)
        return
    k = pl.program_id(2)

    @pl.when(k == 0)
    def _():
        acc_ref[...] = part

    @pl.when(k > 0)
    def _():
        acc_ref[...] += part

    @pl.when(k == nk - 1)
    def _():
        finalize(acc_ref[...])


def _mm_nn(a, b, *, name, epilogue="plain", res=None, gate=None, norm=None, out_dtype=BF16, tm=1024, tn=1024, tk=2048, deps=()):
    if a.ndim == 3:
        Q, M, Kq = a.shape
        K = Q * Kq
    else:
        (M, K), Kq = a.shape, a.shape[1]
    tm, tk = _tile(M, tm, 16), _tile(Kq, tk)
    if a.ndim == 3:
        pa = Kq // tk
        a_spec = pl.BlockSpec((None, tm, tk), lambda i, j, k: (k // pa, i, k % pa))
    else:
        a_spec = pl.BlockSpec((tm, tk), lambda i, j, k: (i, k))
    if b.ndim == 3:
        P, _, Ns = b.shape
        N = P * Ns
        tn = _tile(Ns, tn)
        per = Ns // tn
        b_spec = pl.BlockSpec((None, tk, tn), lambda i, j, k: (j // per, k, j % per))
    else:
        N = b.shape[1]
        tn = _tile(N, tn)
        b_spec = pl.BlockSpec((tk, tn), lambda i, j, k: (k, j))
    nk = K // tk
    tile = pl.BlockSpec((tm, tn), lambda i, j, k: (i, j))

    def body(*refs):
        acc_ref = refs[-1] if nk > 1 else None
        a_ref, b_ref = refs[0], refs[1]
        part = jnp.dot(a_ref[...], b_ref[...], preferred_element_type=F32)
        if epilogue == "plain":
            def fin(acc):
                refs[2][...] = acc.astype(out_dtype)
        elif epilogue == "relu2":
            def fin(acc):
                refs[2][...] = acc.astype(BF16)
                refs[3][...] = jnp.square(jnp.maximum(acc, 0.0)).astype(BF16)
        elif epilogue == "add":
            def fin(acc):
                refs[3][...] = (acc + refs[2][...]).astype(out_dtype)
        elif epilogue == "resid":
            def fin(acc):
                refs[4][...] = refs[2][...] + refs[3][...] * acc
                refs[5][...] = acc.astype(BF16)
        else:
            def fin(acc):
                xv = refs[2][...] + refs[3][...] * acc
                refs[7][...] = xv
                refs[8][...] = acc.astype(BF16)
                r = lax.rsqrt(jnp.mean(xv * xv, axis=-1, keepdims=True) + RMS_EPS)
                refs[9][...] = (((xv * r) * refs[4][...]) * (1.0 + refs[5][...]) + refs[6][...]).astype(BF16)
        _accumulate(part, acc_ref, nk, fin)

    in_specs = [a_spec, b_spec]
    args = [a, b]
    if epilogue == "plain":
        out_shape, out_specs = jax.ShapeDtypeStruct((M, N), out_dtype), tile
    elif epilogue == "relu2":
        out_shape, out_specs = [jax.ShapeDtypeStruct((M, N), BF16)] * 2, [tile, tile]
    elif epilogue == "add":
        in_specs.append(tile)
        args.append(res)
        out_shape, out_specs = jax.ShapeDtypeStruct((M, N), out_dtype), tile
    else:
        row = pl.BlockSpec((1, tn), lambda i, j, k: (0, j))
        in_specs += [tile, row]
        args += [res, gate]
        out_shape, out_specs = [jax.ShapeDtypeStruct((M, N), F32), jax.ShapeDtypeStruct((M, N), BF16)], [tile, tile]
        if epilogue == "resid_norm":
            assert tn == N, "the next norm needs whole rows"
            in_specs += [row, row, row]
            args += list(norm)
            out_shape, out_specs = out_shape + [jax.ShapeDtypeStruct((M, N), BF16)], out_specs + [tile]
    return _call(
        body, deps, name=name, grid=(M // tm, N // tn, nk), in_specs=in_specs, out_specs=out_specs, out_shape=out_shape,
        scratch_shapes=[pltpu.VMEM((tm, tn), F32)] if nk > 1 else [],
        compiler_params=_params(("parallel", "parallel", "arbitrary")),
    )(*args)


def _mm_nt(a, b, *, name, n=None, epilogue="plain", extra=None, out_dtype=F32, tm=1024, tn=1024, tk=2048, deps=()):
    if a.ndim == 3:
        Q, M, Kq = a.shape
        K = Q * Kq
    else:
        (M, K), Kq = a.shape, a.shape[1]
    if b.ndim == 3:
        P, N, Ks = b.shape
    else:
        N, Ks = b.shape
    N = n or N
    tm, tn, tk = _tile(M, tm, 16), _tile(N, tn), _tile(min(Kq, Ks), tk)
    nk = K // tk
    if a.ndim == 3:
        pa = Kq // tk
        a_spec = pl.BlockSpec((None, tm, tk), lambda i, j, k: (k // pa, i, k % pa))
    else:
        a_spec = pl.BlockSpec((tm, tk), lambda i, j, k: (i, k))
    if b.ndim == 3:
        pb = Ks // tk
        b_spec = pl.BlockSpec((None, tn, tk), lambda i, j, k: (k // pb, j, k % pb))
    else:
        b_spec = pl.BlockSpec((tn, tk), lambda i, j, k: (j, k))
    tile = pl.BlockSpec((tm, tn), lambda i, j, k: (i, j))

    def body(*refs):
        acc_ref = refs[-1] if nk > 1 else None
        part = lax.dot_general(refs[0][...], refs[1][...], (((1,), (1,)), ((), ())), preferred_element_type=F32)
        if epilogue == "plain":
            def fin(acc):
                refs[2][...] = acc.astype(out_dtype)
        elif epilogue == "add":
            def fin(acc):
                refs[3][...] = (acc + refs[2][...]).astype(out_dtype)
        else:
            def fin(acc):
                refs[3][...] = (acc * (2.0 * jnp.maximum(refs[2][...].astype(F32), 0.0))).astype(out_dtype)
        _accumulate(part, acc_ref, nk, fin)

    in_specs, args = [a_spec, b_spec], [a, b]
    if epilogue != "plain":
        in_specs.append(tile)
        args.append(extra)
    return _call(
        body, deps, name=name, grid=(M // tm, N // tn, nk), in_specs=in_specs, out_specs=tile,
        out_shape=jax.ShapeDtypeStruct((M, N), out_dtype),
        scratch_shapes=[pltpu.VMEM((tm, tn), F32)] if nk > 1 else [],
        compiler_params=_params(("parallel", "parallel", "arbitrary")),
    )(*args)


def _put_rows(buf, src, row0, n, *, name):
    def body(buf_ref, src_ref, out_ref):
        out_ref[...] = src_ref[0:n, :]

    return pl.pallas_call(
        body, name=name, grid=(1,),
        in_specs=[ANY, pl.BlockSpec(src.shape, lambda i: (0, 0))],
        out_specs=pl.BlockSpec((n, buf.shape[1]), lambda i: (row0 // n, 0)),
        out_shape=jax.ShapeDtypeStruct(buf.shape, buf.dtype), input_output_aliases={0: 0},
        compiler_params=_params(("arbitrary",)),
    )(buf, src)


def _mm_tn(a, b, *, name, out_parts=1, out_rows=None, tm=1024, tn=1024, tk=4096, deps=()):
    if a.ndim == 3:
        Qa, M, Kq = a.shape
        Kd = Qa * Kq
    else:
        (M, Kd), Kq = a.shape, a.shape[1]
    if b.ndim == 3:
        Q, _, Nq = b.shape
        N = Q * Nq
    else:
        N, Nq = b.shape[1], b.shape[1]
    Ns = N // out_parts
    tn = _tile(Ns, tn)
    while Nq % tn or Ns % tn:
        tn -= LANES
    tm, tk = _tile(Kq, tm), _tile(M, tk, 16)
    nk = M // tk
    if a.ndim == 3:
        pa = Kq // tm
        a_spec = pl.BlockSpec((None, tk, tm), lambda i, j, k: (i // pa, k, i % pa))
    else:
        a_spec = pl.BlockSpec((tk, tm), lambda i, j, k: (k, i))
    if b.ndim == 3:
        pb = Nq // tn
        b_spec = pl.BlockSpec((None, tk, tn), lambda i, j, k: (j // pb, k, j % pb))
    else:
        b_spec = pl.BlockSpec((tk, tn), lambda i, j, k: (k, j))
    if out_parts > 1:
        po = Ns // tn
        o_spec = pl.BlockSpec((None, tm, tn), lambda i, j, k: (j // po, i, j % po))
        out_shape = jax.ShapeDtypeStruct((out_parts, Kd, Ns), BF16)
    else:
        o_spec = pl.BlockSpec((tm, tn), lambda i, j, k: (i, j))
        out_shape = jax.ShapeDtypeStruct((out_rows or Kd, N), BF16)

    def body(*refs):
        acc_ref = refs[-1] if nk > 1 else None
        part = lax.dot_general(refs[0][...], refs[1][...], (((0,), (0,)), ((), ())), preferred_element_type=F32)

        def fin(acc):
            refs[2][...] = acc.astype(BF16)
        _accumulate(part, acc_ref, nk, fin)

    return _call(
        body, deps, name=name, grid=(Kd // tm, N // tn, nk),
        in_specs=[a_spec, b_spec], out_specs=o_spec, out_shape=out_shape,
        scratch_shapes=[pltpu.VMEM((tm, tn), F32)] if nk > 1 else [],
        compiler_params=_params(("parallel", "parallel", "arbitrary")),
    )(a, b)


def _rows(S, D, i_map=lambda i: (i, 0), ts=512):
    return pl.BlockSpec((ts, D), i_map)


def _norm_fwd(x, gain, sc, sh, *, name, deps=()):
    S, D = x.shape
    ts = _tile(S, 512, 16)
    vec = pl.BlockSpec((1, D), lambda i: (0, 0))

    def body(x_ref, g_ref, sc_ref, sh_ref, h_ref):
        xv = x_ref[...]
        r = lax.rsqrt(jnp.mean(xv * xv, axis=-1, keepdims=True) + RMS_EPS)
        h = (xv * r) * g_ref[...]
        h_ref[...] = (h * (1.0 + sc_ref[...]) + sh_ref[...]).astype(BF16)

    return _call(
        body, deps, name=name, grid=(S // ts,), in_specs=[_rows(S, D, ts=ts), vec, vec, vec], out_specs=_rows(S, D, ts=ts),
        out_shape=jax.ShapeDtypeStruct((S, D), BF16), compiler_params=_params(("parallel",)),
    )(x, gain, sc, sh)


def _loss_bwd(x, target, gain, gate_prev, *, name, deps=()):
    S, D = x.shape
    ts = _tile(S, 256, 16)
    vec = pl.BlockSpec((1, D), lambda i: (0, 0))

    def body(x_ref, t_ref, g_ref, gp_ref, dx_ref, dp_ref, sums_ref):
        @pl.when(pl.program_id(0) == 0)
        def _():
            sums_ref[...] = jnp.zeros_like(sums_ref)
        xv = x_ref[...]
        r = lax.rsqrt(jnp.mean(xv * xv, axis=-1, keepdims=True) + RMS_EPS)
        xn = xv * r
        err = xn * g_ref[...] - t_ref[...]
        loss = 0.5 * jnp.sum(jnp.mean(err * err, axis=-1, keepdims=True), axis=0, keepdims=True)
        dy = err * (1.0 / D)
        dxn = dy * g_ref[...]
        dx = r * (dxn - xn * jnp.mean(dxn * xn, axis=-1, keepdims=True))
        dx_ref[...] = dx
        dp_ref[...] = (gp_ref[...] * dx).astype(BF16)
        sums_ref[0:1, :] += jnp.sum(dy * xn, axis=0, keepdims=True)
        sums_ref[1:2, :] += jnp.broadcast_to(loss, (1, D))

    return _call(
        body, deps, name=name, grid=(S // ts,),
        in_specs=[_rows(S, D, ts=ts), _rows(S, D, ts=ts), vec, vec],
        out_specs=[_rows(S, D, ts=ts), _rows(S, D, ts=ts), pl.BlockSpec((8, D), lambda i: (0, 0))],
        out_shape=[jax.ShapeDtypeStruct((S, D), F32), jax.ShapeDtypeStruct((S, D), BF16), jax.ShapeDtypeStruct((8, D), F32)],
        compiler_params=_params(("arbitrary",)),
    )(x, target, gain, gate_prev)


def _norm_bwd(x, dh, dxp, mix, gain, sc, gate_prev, *, name, deps=()):
    S, D = x.shape
    ts = _tile(S, 256, 16)
    vec = pl.BlockSpec((1, D), lambda i: (0, 0))
    with_prev = gate_prev is not None

    def body(*refs):
        x_ref, dh_ref, dxp_ref, mix_ref, g_ref, sc_ref = refs[:6]
        outs = refs[7:] if with_prev else refs[6:]
        sums_ref = outs[-1]

        @pl.when(pl.program_id(0) == 0)
        def _():
            sums_ref[...] = jnp.zeros_like(sums_ref)
        xv, dhv, dxpv = x_ref[...], dh_ref[...].astype(F32), dxp_ref[...]
        r = lax.rsqrt(jnp.mean(xv * xv, axis=-1, keepdims=True) + RMS_EPS)
        xn = xv * r
        hn = xn * g_ref[...]
        dhn = dhv * (1.0 + sc_ref[...])
        dxn = dhn * g_ref[...]
        dx = dxpv + r * (dxn - xn * jnp.mean(dxn * xn, axis=-1, keepdims=True))
        outs[0][...] = dx
        if with_prev:
            outs[1][...] = (refs[6][...] * dx).astype(BF16)
        sums_ref[0:1, :] += jnp.sum(dhv, axis=0, keepdims=True)
        sums_ref[1:2, :] += jnp.sum(dhv * hn, axis=0, keepdims=True)
        sums_ref[2:3, :] += jnp.sum(dhn * xn, axis=0, keepdims=True)
        sums_ref[3:4, :] += jnp.sum(dxpv * mix_ref[...].astype(F32), axis=0, keepdims=True)

    tile = _rows(S, D, ts=ts)
    in_specs = [tile, tile, tile, tile, vec, vec] + ([vec] if with_prev else [])
    args = [x, dh, dxp, mix, gain, sc] + ([gate_prev] if with_prev else [])
    out_specs = [tile] + ([tile] if with_prev else []) + [pl.BlockSpec((8, D), lambda i: (0, 0))]
    out_shape = ([jax.ShapeDtypeStruct((S, D), F32)] + ([jax.ShapeDtypeStruct((S, D), BF16)] if with_prev else [])
                 + [jax.ShapeDtypeStruct((8, D), F32)])
    outs = _call(
        body, deps, name=name, grid=(S // ts,), in_specs=in_specs, out_specs=out_specs, out_shape=out_shape,
        compiler_params=_params(("arbitrary",)),
    )(*args)
    return (outs[0], outs[1], outs[2]) if with_prev else (outs[0], None, outs[1])


def _fgate_fwd(h, wf, bf, *, name, deps=()):
    S, D = h.shape
    ts = _tile(S, 256, 16)

    def body(h_ref, w_ref, b_ref, z_ref, f_ref, carry):
        @pl.when(pl.program_id(0) == 0)
        def _():
            carry[...] = jnp.zeros_like(carry)
        z = lax.dot_general(h_ref[...], w_ref[...], (((1,), (1,)), ((), ())), preferred_element_type=F32) + b_ref[...]
        logf = jnp.minimum(z, 0.0) - jnp.log(1.0 + jnp.exp(-jnp.abs(z)))
        row = lax.broadcasted_iota(jnp.int32, (ts, ts), 0)
        col = lax.broadcasted_iota(jnp.int32, (ts, ts), 1)
        tril = (col <= row).astype(F32)
        run = jnp.dot(tril, logf, preferred_element_type=F32, precision=lax.Precision.HIGHEST) + carry[0:1, :]
        z_ref[...] = z
        f_ref[...] = run
        carry[0:1, :] = run[ts - 1:ts, :]

    return _call(
        body, deps, name=name, grid=(S // ts,),
        in_specs=[pl.BlockSpec((ts, D), lambda i: (i, 0)), pl.BlockSpec((LANES, D), lambda i: (0, 0)),
                  pl.BlockSpec((1, LANES), lambda i: (0, 0))],
        out_specs=[pl.BlockSpec((ts, LANES), lambda i: (i, 0))] * 2,
        out_shape=[jax.ShapeDtypeStruct((S, LANES), F32)] * 2,
        scratch_shapes=[pltpu.VMEM((8, LANES), F32)],
        compiler_params=_params(("arbitrary",)),
    )(h, wf, bf)


def _fgate_bwd(dfq, dfk, z, *, name, deps=()):
    S = z.shape[0]
    ts = _tile(S, 256, 16)
    n = S // ts

    def body(dq_ref, dk_ref, z_ref, dz_ref, sums_ref, carry):
        @pl.when(pl.program_id(0) == 0)
        def _():
            carry[...] = jnp.zeros_like(carry)
            sums_ref[...] = jnp.zeros_like(sums_ref)
        df = dq_ref[...] - dk_ref[...]
        row = lax.broadcasted_iota(jnp.int32, (ts, ts), 0)
        col = lax.broadcasted_iota(jnp.int32, (ts, ts), 1)
        triu = (col >= row).astype(F32)
        run = jnp.dot(triu, df, preferred_element_type=F32, precision=lax.Precision.HIGHEST) + carry[0:1, :]
        zv = z_ref[...]
        dz = run * (1.0 / (1.0 + jnp.exp(zv)))
        dz_ref[...] = dz.astype(BF16)
        sums_ref[0:1, :] += jnp.sum(dz, axis=0, keepdims=True)
        carry[0:1, :] = run[0:1, :]

    rev = pl.BlockSpec((ts, LANES), lambda i: (n - 1 - i, 0))
    return _call(
        body, deps, name=name, grid=(n,), in_specs=[rev, rev, rev],
        out_specs=[rev, pl.BlockSpec((8, LANES), lambda i: (0, 0))],
        out_shape=[jax.ShapeDtypeStruct((S, LANES), BF16), jax.ShapeDtypeStruct((8, LANES), F32)],
        scratch_shapes=[pltpu.VMEM((8, LANES), F32)],
        compiler_params=_params(("arbitrary",)),
    )(dfq, dfk, z)


def _head_col(ref, rows, lane_mask):
    return jnp.sum(jnp.where(lane_mask, ref[rows, :], 0.0), axis=1, keepdims=True)


def _attn_fwd(qkv, fk, *, heads, name, T=256, deps=()):
    S, D3 = qkv.shape
    D = D3 // 3
    dh = D // heads
    T = _tile(S, T, 16)
    nq = S // T
    scale = dh ** -0.5
    hp = fk.shape[1]

    def body(q_ref, k_ref, v_ref, fk_ref, o_ref, lse_ref):
        h = pl.program_id(0)

        @pl.when(h == 0)
        def _():
            lse_ref[...] = jnp.zeros_like(lse_ref)
        lane = lax.broadcasted_iota(jnp.int32, (1, LANES), 1) == h
        row = lax.broadcasted_iota(jnp.int32, (T, T), 0)
        col = lax.broadcasted_iota(jnp.int32, (T, T), 1)

        def q_block(qi, _):
            rows = pl.ds(pl.multiple_of(qi * T, T), T)
            q = q_ref[rows, :]

            def kv_block(kj, carry, diag):
                m, l, acc = carry
                cols = pl.ds(pl.multiple_of(kj * T, T), T)
                s = lax.dot_general(q, k_ref[cols, :], (((1,), (1,)), ((), ())), preferred_element_type=F32) * scale
                s = s - fk_ref[kj, pl.ds(h, 1), :]
                if diag:
                    s = jnp.where(col <= row, s, NEG)
                m_new = jnp.maximum(m, jnp.max(s, axis=1, keepdims=True))
                p = jnp.exp(s - m_new)
                alpha = jnp.exp(m - m_new)
                l = alpha * l + jnp.sum(p, axis=1, keepdims=True)
                acc = alpha * acc + jnp.dot(p.astype(BF16), v_ref[cols, :], preferred_element_type=F32)
                return m_new, l, acc

            init = (jnp.full((T, 1), NEG, F32), jnp.zeros((T, 1), F32), jnp.zeros((T, dh), F32))
            carry = lax.fori_loop(0, qi, lambda kj, cr: kv_block(kj, cr, False), init)
            m, l, acc = kv_block(qi, carry, True)
            o_ref[rows, :] = (acc / l).astype(BF16)
            lse_ref[rows, :] = jnp.where(lane, m + jnp.log(l), lse_ref[rows, :])
            return 0

        lax.fori_loop(0, nq, q_block, 0)

    head = lambda part: pl.BlockSpec((S, dh), lambda h: (0, part * heads + h))
    return _call(
        body, deps, name=name, grid=(heads,),
        in_specs=[head(0), head(1), head(2), pl.BlockSpec((nq, hp, T), lambda h: (0, 0, 0))],
        out_specs=[pl.BlockSpec((S, dh), lambda h: (0, h)), pl.BlockSpec((S, LANES), lambda h: (0, 0))],
        out_shape=[jax.ShapeDtypeStruct((S, D), BF16), jax.ShapeDtypeStruct((S, LANES), F32)],
        compiler_params=_params(("arbitrary",)),
    )(qkv, qkv, qkv, fk)


def _attn_bwd(qkv, o, do, fk, lse, *, heads, name, T=256, deps=()):
    S, D3 = qkv.shape
    D = D3 // 3
    dh = D // heads
    T = _tile(S, T, 16)
    nq = S // T
    scale = dh ** -0.5
    hp = fk.shape[1]

    def body(q_ref, k_ref, v_ref, o_ref, do_ref, fk_ref, lse_ref, dqkv_ref, dfq_ref, dfk_ref,
             dq_acc, lse_col, delta_col, dfq_col):
        h = pl.program_id(0)

        @pl.when(h == 0)
        def _():
            dfq_ref[...] = jnp.zeros_like(dfq_ref)
            dfk_ref[...] = jnp.zeros_like(dfk_ref)
        lane = lax.broadcasted_iota(jnp.int32, (1, LANES), 1) == h
        Th = T // 2 if T % 32 == 0 else T
        dq_acc[...] = jnp.zeros_like(dq_acc)
        dfq_col[...] = jnp.zeros_like(dfq_col)

        def prep(qi, _):
            rows = pl.ds(pl.multiple_of(qi * T, T), T)
            lse_col[rows, :] = _head_col(lse_ref, rows, lane)
            delta_col[rows, :] = jnp.sum(do_ref[rows, :].astype(F32) * o_ref[rows, :].astype(F32), axis=1, keepdims=True)
            return 0

        lax.fori_loop(0, nq, prep, 0)

        def kv_block(kj, _):
            cols = pl.ds(pl.multiple_of(kj * T, T), T)
            k, v = k_ref[cols, :], v_ref[cols, :]
            fk_row = fk_ref[kj, pl.ds(h, 1), :]

            def pair_grad(rows, kk, vv, fk_r, masked):
                q, dov = q_ref[rows, :], do_ref[rows, :]
                s = lax.dot_general(q, kk, (((1,), (1,)), ((), ())), preferred_element_type=F32) * scale
                s = s - fk_r
                p = jnp.exp(s - lse_col[rows, :])
                if masked:
                    p = jnp.where(lax.broadcasted_iota(jnp.int32, p.shape, 1) <= lax.broadcasted_iota(jnp.int32, p.shape, 0), p, 0.0)
                dp = lax.dot_general(dov, vv, (((1,), (1,)), ((), ())), preferred_element_type=F32)
                ds = p * (dp - delta_col[rows, :])
                dsb = ds.astype(BF16)
                dv = lax.dot_general(p.astype(BF16), dov, (((0,), (0,)), ((), ())), preferred_element_type=F32)
                dk = lax.dot_general(dsb, q, (((0,), (0,)), ((), ())), preferred_element_type=F32)
                dq_acc[rows, :] += jnp.dot(dsb, kk, preferred_element_type=F32)
                dfq_col[rows, :] += jnp.sum(ds, axis=1, keepdims=True)
                return dk, dv, jnp.sum(ds, axis=0, keepdims=True)

            def q_block(qi, carry):
                rows = pl.ds(pl.multiple_of(qi * T, T), T)
                return tuple(c + g for c, g in zip(carry, pair_grad(rows, k, v, fk_row, False)))

            strips = [pair_grad(pl.ds(pl.multiple_of(kj * T + i * Th, Th), T - i * Th), k[i * Th:(i + 1) * Th],
                                v[i * Th:(i + 1) * Th], fk_row[:, i * Th:(i + 1) * Th], True) for i in range(T // Th)]
            carry = (jnp.concatenate([g[0] for g in strips], axis=0), jnp.concatenate([g[1] for g in strips], axis=0),
                     jnp.concatenate([g[2] for g in strips], axis=1))
            dk, dv, dfk = lax.fori_loop(kj + 1, nq, q_block, carry)
            dqkv_ref[1, cols, :] = (dk * scale).astype(BF16)
            dqkv_ref[2, cols, :] = dv.astype(BF16)
            dfk_ref[kj, pl.ds(h, 1), :] = dfk
            return 0

        lax.fori_loop(0, nq, kv_block, 0)

        def finish(qi, _):
            rows = pl.ds(pl.multiple_of(qi * T, T), T)
            dqkv_ref[0, rows, :] = (dq_acc[rows, :] * scale).astype(BF16)
            dfq_ref[rows, :] = jnp.where(lane, dfq_col[rows, :], dfq_ref[rows, :])
            return 0

        lax.fori_loop(0, nq, finish, 0)

    head = lambda part: pl.BlockSpec((S, dh), lambda h: (0, part * heads + h))
    own = pl.BlockSpec((S, dh), lambda h: (0, h))
    full = pl.BlockSpec((S, LANES), lambda h: (0, 0))
    krow = pl.BlockSpec((nq, hp, T), lambda h: (0, 0, 0))
    return _call(
        body, deps, name=name, grid=(heads,),
        in_specs=[head(0), head(1), head(2), own, own, krow, full],
        out_specs=[pl.BlockSpec((3, S, dh), lambda h: (0, 0, h)), full, krow],
        out_shape=[jax.ShapeDtypeStruct((3, S, D), BF16), jax.ShapeDtypeStruct((S, LANES), F32),
                   jax.ShapeDtypeStruct((nq, hp, T), F32)],
        scratch_shapes=[pltpu.VMEM((S, dh), F32)] + [pltpu.VMEM((S, 1), F32)] * 3,
        compiler_params=_params(("arbitrary",)),
    )(qkv, qkv, qkv, o, do, fk, lse)


def _shift_down(v, n):
    rows = lax.broadcasted_iota(jnp.int32, v.shape, 0)
    return jnp.where(rows >= n, pltpu.roll(v, n, axis=0), 0.0)


def _shift_up(v, n):
    S = v.shape[0]
    rows = lax.broadcasted_iota(jnp.int32, v.shape, 0)
    return jnp.where(rows < S - n, pltpu.roll(v, S - n, axis=0), 0.0)


def _conv_fwd(proj, conv_w, *, name, cb=LANES, deps=()):
    S, D3 = proj.shape
    D = D3 // 3
    nb = D // cb

    def body(bg_ref, cg_ref, u_ref, w_ref, z_ref):
        uc = cg_ref[...].astype(F32) * u_ref[...].astype(F32)
        w = w_ref[...]
        y = w[2:3, :] * uc + w[1:2, :] * _shift_down(uc, 1) + w[0:1, :] * _shift_down(uc, 2)
        z_ref[...] = (bg_ref[...].astype(F32) * y).astype(BF16)

    part = lambda g: pl.BlockSpec((S, cb), lambda j: (0, g * nb + j))
    return _call(
        body, deps, name=name, grid=(nb,),
        in_specs=[part(0), part(1), part(2), pl.BlockSpec((3, cb), lambda j: (0, j))],
        out_specs=pl.BlockSpec((S, cb), lambda j: (0, j)),
        out_shape=jax.ShapeDtypeStruct((S, D), BF16), compiler_params=_params(("parallel",)),
    )(proj, proj, proj, conv_w)


def _conv_bwd(proj, conv_w, dz, *, name, cb=LANES, deps=()):
    S, D3 = proj.shape
    D = D3 // 3
    nb = D // cb

    def body(bg_ref, cg_ref, u_ref, w_ref, dz_ref, dp_ref, dw_ref):
        cg, u = cg_ref[...].astype(F32), u_ref[...].astype(F32)
        uc = cg * u
        w = w_ref[...]
        uc1, uc2 = _shift_down(uc, 1), _shift_down(uc, 2)
        y = w[2:3, :] * uc + w[1:2, :] * uc1 + w[0:1, :] * uc2
        dz = dz_ref[...].astype(F32)
        dp_ref[0] = (dz * y).astype(BF16)
        dy = dz * bg_ref[...].astype(F32)
        duc = w[2:3, :] * dy + w[1:2, :] * _shift_up(dy, 1) + w[0:1, :] * _shift_up(dy, 2)
        dp_ref[1] = (duc * u).astype(BF16)
        dp_ref[2] = (duc * cg).astype(BF16)
        dw_ref[...] = jnp.zeros_like(dw_ref)
        dw_ref[0:1, :] = jnp.sum(dy * uc2, axis=0, keepdims=True)
        dw_ref[1:2, :] = jnp.sum(dy * uc1, axis=0, keepdims=True)
        dw_ref[2:3, :] = jnp.sum(dy * uc, axis=0, keepdims=True)

    part = lambda g: pl.BlockSpec((S, cb), lambda j: (0, g * nb + j))
    return _call(
        body, deps, name=name, grid=(nb,),
        in_specs=[part(0), part(1), part(2), pl.BlockSpec((3, cb), lambda j: (0, j)), pl.BlockSpec((S, cb), lambda j: (0, j))],
        out_specs=[pl.BlockSpec((3, S, cb), lambda j: (0, 0, j)), pl.BlockSpec((8, cb), lambda j: (0, j))],
        out_shape=[jax.ShapeDtypeStruct((3, S, D), BF16), jax.ShapeDtypeStruct((8, D), F32)],
        compiler_params=_params(("parallel",)),
    )(proj, proj, proj, conv_w, dz)


def _ada_fwd(c_all, ada_w, *, name, deps=()):
    L, D, Ns = ada_w.shape
    tn = _tile(Ns, 512)

    def body(c_ref, w_ref, o_ref, act_ref):
        cv = c_ref[...]
        act = cv * (1.0 / (1.0 + jnp.exp(-cv)))
        act_ref[...] = act
        o_ref[...] = jnp.dot(act.astype(BF16), w_ref[...].astype(BF16), preferred_element_type=F32)

    return _call(
        body, deps, name=name, grid=(L, Ns // tn),
        in_specs=[pl.BlockSpec((N_DEV, D), lambda l, j: (0, 0)), pl.BlockSpec((None, D, tn), lambda l, j: (l, 0, j))],
        out_specs=[pl.BlockSpec((None, N_DEV, tn), lambda l, j: (l, 0, j)), pl.BlockSpec((N_DEV, D), lambda l, j: (0, 0))],
        out_shape=[jax.ShapeDtypeStruct((L, N_DEV, Ns), F32), jax.ShapeDtypeStruct((N_DEV, D), F32)],
        compiler_params=_params(("arbitrary", "arbitrary")),
    )(c_all, ada_w)


def _select_mod(gathered, *, name, deps=()):
    _, LB, Ns = gathered.shape
    L = LB // N_DEV

    def body(g_ref, o_ref):
        x, y, c = _me()
        b = 4 * x + 2 * y + c
        for j in range(N_CHIPS):
            for l in range(L):
                o_ref[j, pl.ds(l, 1), :] = g_ref[2 * j + c, pl.ds(l * N_DEV + b, 1), :]

    return _call(
        body, deps, name=name, out_shape=jax.ShapeDtypeStruct((N_CHIPS, L, Ns), F32),
        in_specs=[pl.BlockSpec(memory_space=pltpu.VMEM)], out_specs=pl.BlockSpec(memory_space=pltpu.VMEM),
        compiler_params=_params(),
    )(gathered)


def _adamw_math(w, g, m, v):
    m = ADAM_B1 * m + (1.0 - ADAM_B1) * g
    v = ADAM_B2 * v + (1.0 - ADAM_B2) * jnp.square(g)
    m_hat = m / (1.0 - ADAM_B1 ** ADAM_STEP)
    v_hat = v / (1.0 - ADAM_B2 ** ADAM_STEP)
    delta = -ADAM_LR * (m_hat / (jnp.sqrt(v_hat) + ADAM_EPS) + ADAM_WD * w)
    return delta, m, v


def _adamw_shards(w, m, v, groups, chip, *, name):
    L, R, C = w.shape
    if R % 16 == 0:
        tr, tc = _tile(R, 128 if L > 1 else 256, 16), C
    else:
        tr, tc = R, _tile(C, 256)
    nr, nc = R // tr, C // tc

    def body(chip_ref, w_ref, m_ref, v_ref, *rest):
        srcs, (g_ref, d_ref, mo_ref, vo_ref) = rest[:2 * N_CHIPS * L], rest[2 * N_CHIPS * L:]
        for l in range(L):
            @pl.when(pl.program_id(0) == l)
            def _():
                s = srcs[2 * N_CHIPS * l:2 * N_CHIPS * (l + 1)]
                mine, other = s[0][...].astype(F32), s[N_CHIPS][...].astype(F32)
                for k in range(1, N_CHIPS):
                    mine = mine + s[k][...].astype(F32)
                    other = other + s[N_CHIPS + k][...].astype(F32)
                g = mine + other
                delta, mn, vn = _adamw_math(w_ref[...], g, m_ref[...], v_ref[...])
                g_ref[...] = g
                d_ref[...] = delta
                mo_ref[...] = mn
                vo_ref[...] = vn

    tile = pl.BlockSpec((None, tr, tc), lambda l, i, j, chip_ref: (l, i, j))

    def block(layer, k):
        def index(l, i, j, chip_ref):
            idle_i, idle_j = jnp.where(l < layer, 0, nr - 1), jnp.where(l < layer, 0, nc - 1)
            return (jnp.bitwise_xor(chip_ref[0], k), jnp.where(l == layer, i, idle_i), jnp.where(l == layer, j, idle_j))
        return pl.BlockSpec((None, tr, tc), index)

    in_specs, args = [tile] * 3, [w, m, v]
    for layer, (parts, land, sib) in enumerate(groups):
        in_specs += [block(layer, k) for k in range(N_CHIPS)] * 2
        args += [parts, land, land, land, sib, sib, sib, sib]
    return pl.pallas_call(
        body, name=name,
        grid_spec=pltpu.PrefetchScalarGridSpec(num_scalar_prefetch=1, grid=(L, nr, nc), in_specs=in_specs, out_specs=[tile] * 4),
        out_shape=[jax.ShapeDtypeStruct((L, R, C), F32)] * 4, compiler_params=_params(("arbitrary", "arbitrary", "arbitrary")),
    )(chip, *args)


def _adamw_ada(w, m, v, act_t, dmod, *, name, tr=256, deps=()):
    L, D, Ns = w.shape
    tr = _tile(D, tr, 8)

    def body(w_ref, m_ref, v_ref, a_ref, d_ref, g_ref, dl_ref, mo_ref, vo_ref):
        x, y, _ = _me()
        g = jnp.dot(a_ref[...], d_ref[2 * x + y], preferred_element_type=F32, precision=lax.Precision.HIGHEST)
        delta, mn, vn = _adamw_math(w_ref[...], g, m_ref[...], v_ref[...])
        g_ref[...] = g
        dl_ref[...] = delta
        mo_ref[...] = mn
        vo_ref[...] = vn

    tile = pl.BlockSpec((None, tr, Ns), lambda l, i: (l, i, 0))
    return _call(
        body, deps, name=name, grid=(L, D // tr),
        in_specs=[tile] * 3 + [pl.BlockSpec((tr, N_DEV), lambda l, i: (i, 0)),
                               pl.BlockSpec((N_CHIPS, None, N_DEV, Ns), lambda l, i: (0, l, 0, 0))],
        out_specs=[tile] * 4, out_shape=[jax.ShapeDtypeStruct((L, D, Ns), F32)] * 4,
        compiler_params=_params(("parallel", "parallel")),
    )(w, m, v, act_t, dmod)


def _adamw_small(w, m, v, gathered, *, rows, name, deps=()):
    n, D = w.shape

    def body(w_ref, m_ref, v_ref, s_ref, g_ref, d_ref, mo_ref, vo_ref):
        for r, src in enumerate(rows):
            g = s_ref[0, src:src + 1, :]
            for d in range(1, N_DEV):
                g = g + s_ref[d, src:src + 1, :]
            g_ref[r:r + 1, :] = g
        g = g_ref[...]
        delta, mn, vn = _adamw_math(w_ref[...], g, m_ref[...], v_ref[...])
        d_ref[...] = delta
        mo_ref[...] = mn
        vo_ref[...] = vn

    vm = pl.BlockSpec(memory_space=pltpu.VMEM)
    return _call(
        body, deps, name=name, in_specs=[vm] * 4, out_specs=[vm] * 4,
        out_shape=[jax.ShapeDtypeStruct((n, D), F32)] * 4, compiler_params=_params(),
    )(w, m, v, gathered)


def _adamw_conv_w(w, m, v, gathered4, *, name, deps=()):
    Cs = w.shape[1]

    def body(w_ref, m_ref, v_ref, s_ref, g_ref, d_ref, mo_ref, vo_ref):
        x, y, _ = _me()
        j = 2 * x + y
        g = s_ref[j, 0]
        for d in range(1, N_DEV):
            g = g + s_ref[j, d]
        delta, mn, vn = _adamw_math(w_ref[...], g, m_ref[...], v_ref[...])
        g_ref[...] = g
        d_ref[...] = delta
        mo_ref[...] = mn
        vo_ref[...] = vn

    vm = pl.BlockSpec(memory_space=pltpu.VMEM)
    return _call(
        body, deps, name=name, in_specs=[vm] * 4, out_specs=[vm] * 4,
        out_shape=[jax.ShapeDtypeStruct((8, Cs), F32)] * 4, compiler_params=_params(),
    )(w, m, v, gathered4)


def _loss_sum(gathered, *, row, name, deps=()):
    _, _, D = gathered.shape

    def body(s_ref, o_ref):
        t = s_ref[0, row:row + 1, :]
        for d in range(1, N_DEV):
            t = t + s_ref[d, row:row + 1, :]
        o_ref[...] = jnp.broadcast_to(t, (8, D))

    vm = pl.BlockSpec(memory_space=pltpu.VMEM)
    return pl.pallas_call(body, name=name, in_specs=[vm], out_specs=vm, out_shape=jax.ShapeDtypeStruct((8, D), F32),
                          compiler_params=_params())(gathered)


def _pad_rows(a, n):
    return jnp.pad(a, ((0, n - a.shape[0]), (0, 0)))


def kernel(x, c, ada_w, ada_b, norm_mix, norm_mlp, fox_w_in, fox_b_f, fox_w_out, conv_w_in, conv_w, conv_w_out, mlp_w_up, mlp_w_down, final_norm, loss_target, m_ada_w, m_ada_b, m_norm_mix, m_norm_mlp, m_fox_w_in, m_fox_b_f, m_fox_w_out, m_conv_w_in, m_conv_w, m_conv_w_out, m_mlp_w_up, m_mlp_w_down, m_final_norm, v_ada_w, v_ada_b, v_norm_mix, v_norm_mlp, v_fox_w_in, v_fox_b_f, v_fox_w_out, v_conv_w_in, v_conv_w, v_conv_w_out, v_mlp_w_up, v_mlp_w_down, v_final_norm):
    S, D = x.shape[1], x.shape[2]
    H = fox_b_f.shape[-1]
    L = ada_w.shape[0]
    NM = ada_b.shape[1] // D
    Ns_ada = ada_w.shape[2]
    Cs_fox = fox_w_in.shape[2]
    Cs_conv = conv_w.shape[2]
    x0 = x[0]
    target = loss_target[0]

    chip = (2 * lax.axis_index("x") + lax.axis_index("y")).astype(jnp.int32).reshape(1)

    fin_t = jnp.transpose(fox_w_in, (0, 2, 1))
    shards = dict(fin=(fin_t, 0), fout=(fox_w_out, 0), up0=(mlp_w_up, 0), dn0=(mlp_w_down, 0), cin=(conv_w_in, 0),
                  cout=(conv_w_out, 0), up1=(mlp_w_up, 1), dn1=(mlp_w_down, 1))
    halves = dict(fin="cols", fout="rows", up0="cols", dn0="rows", cin="cols", cout="rows", up1="cols", dn1="rows")
    gathers, placed = {}, {}

    def place(key, dep=None):
        placed[key] = _place_cast(*shards[key], chip, halves=halves[key], name="place_" + key, dep=dep)
        return placed[key]

    def start_gather(key, dep=None):
        gathers[key] = _split_start("gather1", [(placed[key],)], name="gather_start_" + key, dep=dep)
        return gathers[key][3]

    def pass_gather(key, after):
        landed = _split_wait("gather1", gathers[key], after, name="gather_landed_" + key)
        gathers[key] = _split_start("gather2", landed, name="gather_pass_" + key)
        return gathers[key][3]

    def gathered(key, after):
        return _split_wait("gather2", gathers[key], after, name="gather_wait_" + key)[0][0]

    fin_placed = place("fin")
    c_all = _allgather8(_pad_rows(c, 8), name="gather_c", deps=(fin_placed,))[:, 0, :]
    mod_part, c_act = _ada_fwd(c_all, ada_w, name="ada_fwd")
    mod_all = _allgather8(mod_part.reshape(L * N_DEV, Ns_ada), name="gather_mod")
    conv_w_all = _allgather8(_pad_rows(conv_w[0], 8), name="gather_conv_w", deps=(mod_all,))
    conv_w_full = jnp.transpose(conv_w_all[0::2, :3, :], (1, 0, 2)).reshape(3, D)

    tok = start_gather("fin", conv_w_all)
    for key in ("fout", "up0", "dn0", "cin", "cout", "up1", "dn1"):
        place(key, tok)
        tok = start_gather(key, tok)

    mod = _select_mod(mod_all, name="select_mod", deps=(tok,))
    mod = jnp.transpose(mod, (1, 0, 2)).reshape(L, NM, 1, D) + ada_b.reshape(L, NM, 1, D)

    def vec(a):
        return a.reshape(1, D)

    h0 = _norm_fwd(x0, vec(norm_mix[0]), mod[0, 1], mod[0, 0], name="norm_mix0")
    tok = pass_gather("fin", [h0])
    w_fin_t = jnp.transpose(gathered("fin", [tok]), (0, 2, 1, 3)).reshape(N_CHIPS * Cs_fox, D)
    w_f_t = _pad_rows(w_fin_t[3 * D:], LANES)
    tok = pass_gather("fout", [w_fin_t])
    qkv = _mm_nt(h0, w_fin_t, n=3 * D, name="fox_in", out_dtype=BF16, deps=(tok,))
    b_f = jnp.pad(fox_b_f, ((0, 0), (0, LANES - H)))
    z_f, F_col = _fgate_fwd(h0, w_f_t, b_f, name="fgate_fwd")
    hp = max(8, H)
    at = _tile(S, 1024, 16)
    F_rows = _pad_rows(jnp.transpose(F_col[:, :H]), hp)
    F_row = jnp.transpose(F_rows.reshape(hp, S // at, at), (1, 0, 2))
    o, lse = _attn_fwd(qkv, F_row, heads=H, name="attn_fwd", T=at)
    w_fout = gathered("fout", [o]).reshape(D, D)
    tok = pass_gather("up0", [o])
    tok = pass_gather("dn0", [tok])
    x1, mix0, h1 = _mm_nn(o, w_fout, name="fox_out", epilogue="resid_norm", res=x0, gate=mod[0, 2],
                          norm=(vec(norm_mlp[0]), mod[0, 4], mod[0, 3]), tm=512, tn=D, deps=(tok,))
    w_up0 = gathered("up0", [h1]).reshape(2 * N_CHIPS, D, -1)
    tok = pass_gather("cin", [h1])
    u0, a0 = _mm_nn(h1, w_up0, name="mlp_up0", epilogue="relu2", deps=(tok,))
    w_dn0 = gathered("dn0", [a0]).reshape(-1, D)
    tok = pass_gather("cout", [a0])
    x2, y0 = _mm_nn(a0, w_dn0, name="mlp_down0", epilogue="resid", res=x1, gate=mod[0, 5], deps=(tok,))
    h2 = _norm_fwd(x2, vec(norm_mix[1]), mod[1, 1], mod[1, 0], name="norm_mix1")
    g_cin = gathered("cin", [h2]).reshape(2 * N_CHIPS, D, -1)
    tok = pass_gather("up1", [h2])
    proj = _mm_nn(h2, g_cin, name="conv_in", deps=(tok,))
    w_cin = jnp.transpose(g_cin, (1, 0, 2)).reshape(D, 3 * D)
    zc = _conv_fwd(proj, conv_w_full, name="conv_fwd")
    w_cout = gathered("cout", [zc]).reshape(D, D)
    tok = pass_gather("dn1", [zc])
    x3, mix1, h3 = _mm_nn(zc, w_cout, name="conv_out", epilogue="resid_norm", res=x2, gate=mod[1, 2],
                          norm=(vec(norm_mlp[1]), mod[1, 4], mod[1, 3]), tm=512, tn=D, deps=(tok,))
    w_up1 = gathered("up1", [h3]).reshape(2 * N_CHIPS, D, -1)
    u1, a1 = _mm_nn(h3, w_up1, name="mlp_up1", epilogue="relu2")
    w_dn1 = gathered("dn1", [a1]).reshape(-1, D)
    x4, y1 = _mm_nn(a1, w_dn1, name="mlp_down1", epilogue="resid", res=x3, gate=mod[1, 5])
    w_up = [jnp.transpose(w_up0, (1, 0, 2)).reshape(D, -1), jnp.transpose(w_up1, (1, 0, 2)).reshape(D, -1)]
    w_dn = [w_dn0, w_dn1]

    dx4, dy1, sums_f = _loss_bwd(x4, target, vec(final_norm), mod[1, 5], name="loss_bwd")
    du1 = _mm_nt(dy1, w_dn[1], name="mlp_down1_dx", epilogue="drelu2", extra=u1, out_dtype=BF16)
    def start_scatter(tag, parts_list):
        groups = [(p, lax.empty(p.shape, p.dtype)) for p in parts_list]
        return _split_start("scatter", groups, name="scatter_start_" + tag)

    def start_sibling(tag, scatter, after):
        landed = _split_wait("scatter", scatter, after, name="scatter_wait_" + tag)
        groups = [(p, ld, lax.empty(p.shape, p.dtype)) for p, ld in landed]
        return _split_start("sibling", groups, name="sibling_start_" + tag)

    gw_dn1 = _mm_tn(a1, dy1, name="mlp_down1_dw")
    gw_up1 = _mm_tn(h3, du1, name="mlp_up1_dw", out_parts=N_CHIPS)
    sc1 = start_scatter("mlp1", [gw_dn1.reshape(N_CHIPS, -1, D), gw_up1])
    dh3 = _mm_nt(du1, w_up[1], name="mlp_up1_dx", out_dtype=BF16, tk=4096, deps=(sc1[3],))
    dx3, dmix1, sums_mlp1 = _norm_bwd(x3, dh3, dx4, y1, vec(norm_mlp[1]), mod[1, 4], mod[1, 2], name="norm_mlp1_bwd")
    dzc = _mm_nt(dmix1, w_cout, name="conv_out_dx", out_dtype=BF16)
    gw_cout = _mm_tn(zc, dmix1, name="conv_out_dw")
    dproj, dconv_w = _conv_bwd(proj, conv_w_full, dzc, name="conv_bwd")
    gw_cin = _mm_tn(h2, dproj, name="conv_in_dw", out_parts=N_CHIPS, tn=512)
    sc2 = start_scatter("conv", [gw_cout.reshape(N_CHIPS, -1, D), gw_cin])
    dh2 = _mm_nt(dproj, w_cin, name="conv_in_dx", out_dtype=BF16, deps=(sc2[3],))
    dx2, dy0, sums_mix1 = _norm_bwd(x2, dh2, dx3, mix1, vec(norm_mix[1]), mod[1, 1], mod[0, 5], name="norm_mix1_bwd")
    du0 = _mm_nt(dy0, w_dn[0], name="mlp_down0_dx", epilogue="drelu2", extra=u0, out_dtype=BF16)
    gw_dn0 = _mm_tn(a0, dy0, name="mlp_down0_dw")
    gw_up0 = _mm_tn(h1, du0, name="mlp_up0_dw", out_parts=N_CHIPS)
    sc3 = start_scatter("mlp0", [gw_dn0.reshape(N_CHIPS, -1, D), gw_up0])
    sb1 = start_sibling("mlp1", sc1, [sc3[3]])
    dh1 = _mm_nt(du0, w_up[0], name="mlp_up0_dx", out_dtype=BF16, tk=4096, deps=(sb1[3],))
    dx1, dmix0, sums_mlp0 = _norm_bwd(x1, dh1, dx2, y0, vec(norm_mlp[0]), mod[0, 4], mod[0, 2], name="norm_mlp0_bwd")
    do = _mm_nt(dmix0, w_fout, name="fox_out_dx", out_dtype=BF16)
    gw_fout = _mm_tn(o, dmix0, name="fox_out_dw")
    dqkv, dfq, dfk = _attn_bwd(qkv, o, do, F_row, lse, heads=H, name="attn_bwd", T=at)
    dfk_col = jnp.pad(jnp.transpose(jnp.transpose(dfk, (1, 0, 2)).reshape(hp, S)[:H]), ((0, 0), (0, LANES - H)))
    sb3 = start_sibling("mlp0", sc3, [dqkv])
    dz_f, sums_bf = _fgate_bwd(dfq, dfk_col, z_f, name="fgate_bwd", deps=(sb3[3],))
    gw_qkv_t = _mm_tn(dqkv, h0, name="fox_in_dw", out_rows=3 * D + H)
    gw_f_t = _mm_tn(dz_f, h0, name="fox_gate_dw")
    gw_fin_t = _put_rows(gw_qkv_t, gw_f_t, 3 * D, H, name="fox_gate_dw_rows").reshape(N_CHIPS, Cs_fox, D)
    sc4 = start_scatter("fox", [gw_fout.reshape(N_CHIPS, -1, D), gw_fin_t])
    sb2 = start_sibling("conv", sc2, [sc4[3]])
    dh0_f = _mm_nn(dz_f, w_f_t, name="fox_gate_dx", out_dtype=F32, deps=(sb2[3],))
    dh0 = _mm_nn(dqkv, w_fin_t, name="fox_in_dx", epilogue="add", res=dh0_f, out_dtype=BF16)
    grad_x, _, sums_mix0 = _norm_bwd(x0, dh0, dx1, mix0, vec(norm_mix[0]), mod[0, 1], None, name="norm_mix0_bwd")

    outs = {}

    def put(name_, res, shape):
        for kind, r in zip(("grad", "delta", "new_m", "new_v"), res):
            outs[kind + "_" + name_] = r.reshape(shape)

    def shards_update(tag, w_, m_, v_, groups):
        return _adamw_shards(w_, m_, v_, groups, chip, name="adamw_" + tag)

    g_conv = _split_wait("sibling", sb2, [grad_x], name="sibling_wait_conv")
    put("conv_w_out", shards_update("conv_out", conv_w_out, m_conv_w_out, v_conv_w_out, g_conv[0:1]), conv_w_out.shape)
    r_cin = shards_update("conv_in", conv_w_in, m_conv_w_in, v_conv_w_in, g_conv[1:2])
    put("conv_w_in", r_cin, conv_w_in.shape)
    g_mlp1 = _split_wait("sibling", sb1, [r_cin[0]], name="sibling_wait_mlp1")
    g_mlp0 = _split_wait("sibling", sb3, [r_cin[0]], name="sibling_wait_mlp0")
    put("mlp_w_down", shards_update("mlp_down", mlp_w_down, m_mlp_w_down, v_mlp_w_down, [g_mlp0[0], g_mlp1[0]]), mlp_w_down.shape)
    r_up = shards_update("mlp_up", mlp_w_up, m_mlp_w_up, v_mlp_w_up, [g_mlp0[1], g_mlp1[1]])
    put("mlp_w_up", r_up, mlp_w_up.shape)
    sb4 = start_sibling("fox", sc4, [r_up[0]])

    dmod_rows = []
    for sm, sl in ((sums_mix0, sums_mlp0), (sums_mix1, sums_mlp1)):
        dmod_rows += [sm[0:1], sm[1:2], sm[3:4], sl[0:1], sl[1:2], sl[3:4]]
    bf_row = jnp.pad(sums_bf[0:1], ((0, 0), (0, D - LANES)))
    small = jnp.concatenate([sums_mix0[2:3], sums_mix1[2:3], sums_mlp0[2:3], sums_mlp1[2:3], sums_f[0:1], sums_f[1:2], bf_row,
                             jnp.zeros((1, D), F32)] + dmod_rows + [dconv_w[0:3]], axis=0)
    small_all = _allgather8(_pad_rows(small, -(-small.shape[0] // 8) * 8), name="gather_small", deps=(sb4[3],))
    loss = _loss_sum(small_all, row=5, name="loss_sum")[0, 0]

    def rows_of(a_mix, a_mlp, a_fin, a_bf, a_ada):
        return jnp.concatenate([a_mix, a_mlp, a_fin.reshape(1, D), jnp.pad(a_bf, ((0, 0), (0, D - H))),
                                a_ada.reshape(L * NM, D)], axis=0)
    n_small = 2 * L + 2 + L * NM
    rw = -(-n_small // 8) * 8
    w_s = _pad_rows(rows_of(norm_mix, norm_mlp, final_norm, fox_b_f, ada_b), rw)
    m_s = _pad_rows(rows_of(m_norm_mix, m_norm_mlp, m_final_norm, m_fox_b_f, m_ada_b), rw)
    v_s = _pad_rows(rows_of(v_norm_mix, v_norm_mlp, v_final_norm, v_fox_b_f, v_ada_b), rw)
    src_rows = [0, 1, 2, 3, 4, 6] + [8 + r for r in range(L * NM)] + [7] * (rw - n_small)
    res_s = _adamw_small(w_s, m_s, v_s, small_all, rows=tuple(src_rows), name="adamw_small")
    for kind, r in zip(("grad", "delta", "new_m", "new_v"), res_s):
        outs[kind + "_norm_mix"] = r[0:L]
        outs[kind + "_norm_mlp"] = r[L:2 * L]
        outs[kind + "_final_norm"] = r[2 * L]
        outs[kind + "_fox_b_f"] = r[2 * L + 1:2 * L + 2, :H]
        outs[kind + "_ada_b"] = r[2 * L + 2:n_small].reshape(L, NM * D)

    dmod_all = small_all[:, 8:8 + L * NM, :].reshape(N_DEV, L, N_CHIPS, Ns_ada)
    dmod4 = jnp.transpose(dmod_all, (2, 1, 0, 3))
    act_t = jnp.transpose(c_act)
    res_a = _adamw_ada(ada_w, m_ada_w, v_ada_w, act_t, dmod4, name="adamw_ada")
    put("ada_w", res_a, ada_w.shape)

    r0 = 8 + L * NM
    dconv_all = jnp.pad(small_all[:, r0:r0 + 3, :], ((0, 0), (0, 5), (0, 0)))
    dconv4 = jnp.transpose(dconv_all.reshape(N_DEV, 8, N_CHIPS, Cs_conv), (2, 0, 1, 3))
    res_c = _adamw_conv_w(_pad_rows(conv_w[0], 8), _pad_rows(m_conv_w[0], 8), _pad_rows(v_conv_w[0], 8), dconv4,
                          name="adamw_conv_w")
    for kind, r in zip(("grad", "delta", "new_m", "new_v"), res_c):
        outs[kind + "_conv_w"] = r[:3].reshape(conv_w.shape)

    g_fox = _split_wait("sibling", sb4, [res_a[0], res_c[0], res_s[0]], name="sibling_wait_fox")
    put("fox_w_out", shards_update("fox_out", fox_w_out, m_fox_w_out, v_fox_w_out, g_fox[0:1]), fox_w_out.shape)
    t3 = lambda a: jnp.transpose(a, (0, 2, 1))
    for kind, r in zip(("grad", "delta", "new_m", "new_v"),
                       shards_update("fox_in", t3(fox_w_in), t3(m_fox_w_in), t3(v_fox_w_in), g_fox[1:2])):
        outs[kind + "_fox_w_in"] = t3(r)

    names = ["ada_w", "ada_b", "norm_mix", "norm_mlp", "fox_w_in", "fox_b_f", "fox_w_out", "conv_w_in", "conv_w", "conv_w_out",
             "mlp_w_up", "mlp_w_down", "final_norm"]
    return (loss, grad_x[None], *[outs["grad_" + n] for n in names], *[outs["delta_" + n] for n in names],
            *[outs["new_m_" + n] for n in names], *[outs["new_v_" + n] for n in names])
```
